```python
import jax, jax.numpy as jnp
from jax import lax
import numpy as np

D_MODEL = 2048
BATCH = 2
SEQ = 4096
DEPTH = 2

GRID_W = 64
CTX_LEN = 256
HEAD_DIM = 128
SWA_HEADS = 4
SWA_KV_HEADS = 2
SWA_WINDOW = 128
SWA_BLOCK = 128
RET_HEADS = 4
RET_DK = 128
RET_DV = 128
RET_CHUNK = 128
MLA_HEADS = 4
MLA_Q_RANK = 384
MLA_KV_RANK = 128
MLA_NOPE = 128
MLA_ROPE = 64
MLA_V = 128
MLA_QBLOCK = 128
HGRN_HEADS = 4
HGRN_DK = 128
HGRN_DV = 128
HGRN_CHUNK = 16
MIX_WIDTH = SWA_HEADS * HEAD_DIM + RET_HEADS * RET_DV + MLA_HEADS * MLA_V + HGRN_HEADS * HGRN_DV
IN_WIDTHS = (SWA_HEADS * HEAD_DIM, SWA_KV_HEADS * HEAD_DIM, SWA_KV_HEADS * HEAD_DIM,
             RET_HEADS * RET_DK, RET_HEADS * RET_DK, RET_HEADS * RET_DV, RET_HEADS * RET_DV,
             MLA_Q_RANK, MLA_KV_RANK, MLA_ROPE,
             HGRN_HEADS * HGRN_DK, HGRN_HEADS * HGRN_DK, HGRN_HEADS * HGRN_DK, HGRN_HEADS * HGRN_DV, HGRN_HEADS * HGRN_DV)
IN_COLS = sum(IN_WIDTHS)
N_GROUPS = 4
EXPERTS_PER_GROUP = 8
N_EXPERTS = N_GROUPS * EXPERTS_PER_GROUP
TOP_K = 2
EXPERT_FF = 512
MOE_BLOCK = 128
ROPE_BASE = 10000.0
EPS = 1e-6
DEEPNORM_ALPHA = (2 * DEPTH) ** 0.25
DEEPNORM_BETA = (8 * DEPTH) ** -0.25

kernel_name = "hybrid_parallel_heads_dit_moe"

F32 = jnp.float32


def _layer_norm(x, g=None, b=None):
    xf = x.astype(F32)
    xc = xf - jnp.mean(xf, -1, keepdims=True)
    y = xc * lax.rsqrt(jnp.mean(xc * xc, -1, keepdims=True) + EPS)
    if g is not None:
        y = y * g.astype(F32) + b.astype(F32)
    return y.astype(x.dtype)


def _rms_norm(x, g):
    xf = x.astype(F32)
    y = xf * lax.rsqrt(jnp.mean(xf * xf, -1, keepdims=True) + EPS)
    return (y * g.astype(F32)).astype(x.dtype)


def _modulate(x, shift, scale):
    return _layer_norm(x) * (1.0 + scale) + shift


def _heads(a, h):
    return a.reshape(a.shape[:-1] + (h, a.shape[-1] // h))


def _bhtd(a, h):
    return jnp.swapaxes(_heads(a, h), 1, 2)


def _axial_angles(row, col, rot_dim):
    n_freq = rot_dim // 4
    inv = ROPE_BASE ** (-jnp.arange(n_freq, dtype=F32) / n_freq)
    return jnp.concatenate([row.astype(F32)[:, None] * inv, col.astype(F32)[:, None] * inv], -1)


def _rope(x, ang):
    half = x.shape[-1] // 2
    x1, x2 = x[..., :half], x[..., half:]
    cos, sin = jnp.cos(ang).astype(x.dtype), jnp.sin(ang).astype(x.dtype)
    return jnp.concatenate([x1 * cos - x2 * sin, x2 * cos + x1 * sin], -1)


def _swa(q, k, v, qc, kc, vc, sink, need_ctx):
    B, T, _, dh = q.shape
    L = kc.shape[1]
    nb = T // SWA_BLOCK
    g = SWA_HEADS // SWA_KV_HEADS
    scale = dh ** -0.5
    qb = q.reshape(B, nb, SWA_BLOCK, SWA_KV_HEADS, g, dh)
    pad = ((0, 0), (SWA_BLOCK, SWA_BLOCK), (0, 0), (0, 0))
    kb = jnp.pad(k, pad).reshape(B, nb + 2, SWA_BLOCK, SWA_KV_HEADS, dh)
    vb = jnp.pad(v, pad).reshape(B, nb + 2, SWA_BLOCK, SWA_KV_HEADS, dh)
    band = lambda a: jnp.concatenate([a[:, :-2], a[:, 1:-1], a[:, 2:]], axis=2)
    kband, vband = band(kb), band(vb)
    s_loc = jnp.einsum('bnqhgd,bnkhd->bnhgqk', qb, kband).astype(F32) * scale
    qi = jnp.arange(SWA_BLOCK)[:, None]
    ki = jnp.arange(3 * SWA_BLOCK)[None, :] - SWA_BLOCK
    kabs = jnp.arange(nb)[:, None, None] * SWA_BLOCK + ki[None]
    mask = (jnp.abs(ki - qi) <= SWA_WINDOW)[None] & (kabs >= 0) & (kabs < T)
    s_loc = jnp.where(mask[None, :, None, None], s_loc, -jnp.inf)
    s_ctx = jnp.einsum('bnqhgd,blhd->bnhgql', qb, kc).astype(F32) * scale
    sink_f = sink.astype(F32).reshape(1, 1, SWA_KV_HEADS, g, 1, 1)
    s_sink = jnp.broadcast_to(sink_f, s_loc.shape[:-1] + (1,))
    p = jax.nn.softmax(jnp.concatenate([s_loc, s_ctx, s_sink], -1), axis=-1)
    nk = 3 * SWA_BLOCK
    o = (jnp.einsum('bnhgqk,bnkhd->bnqhgd', p[..., :nk].astype(v.dtype), vband)
         + jnp.einsum('bnhgql,blhd->bnqhgd', p[..., nk:nk + L].astype(v.dtype), vc))
    o_lat = o.reshape(B, T, SWA_HEADS * dh)
    o_ctx = None
    if need_ctx:
        qcg = qc.reshape(B, L, SWA_KV_HEADS, g, dh)
        sc = jnp.einsum('blhgd,bmhd->bhglm', qcg, kc).astype(F32) * scale
        sc_sink = jnp.broadcast_to(sink.astype(F32).reshape(1, SWA_KV_HEADS, g, 1, 1), sc.shape[:-1] + (1,))
        pc = jax.nn.softmax(jnp.concatenate([sc, sc_sink], -1), axis=-1)
        o_ctx = jnp.einsum('bhglm,bmhd->blhgd', pc[..., :L].astype(vc.dtype), vc).reshape(B, L, SWA_HEADS * dh)
    return o_lat, o_ctx


def _chunk_scan(q, k, v, g, s0, chunk, per_dim):
    dt = v.dtype
    q, k, v, g, s0 = (a.astype(F32) for a in (q, k, v, g, s0))
    B, H, T, _ = q.shape
    nc = T // chunk
    rs = lambda a: a.reshape(B, H, nc, chunk, a.shape[-1])
    q, k, v, g = rs(q), rs(k), rs(v), rs(g)
    b = jnp.cumsum(g, axis=3)
    b_last = b[:, :, :, -1:, :]
    causal = jnp.tril(jnp.ones((chunk, chunk), bool))
    if per_dim:
        diff = b[:, :, :, :, None, :] - b[:, :, :, None, :, :]
        dec = jnp.exp(jnp.where(causal[:, :, None], diff, -jnp.inf))
        scores = jnp.einsum('bhcnmd,bhcmd->bhcnm', q[:, :, :, :, None, :] * dec, k)
    else:
        bs = b[..., 0]
        diff = bs[..., :, None] - bs[..., None, :]
        dec = jnp.exp(jnp.where(causal, diff, -jnp.inf))
        scores = jnp.einsum('bhcnd,bhcmd->bhcnm', q, k) * dec
    o_intra = jnp.einsum('bhcnm,bhcmv->bhcnv', scores, v)
    q_in = q * jnp.exp(b)
    k_st = k * jnp.exp(b_last - b)
    a_last = jnp.exp(b_last[:, :, :, 0, :])

    def step(S, xs):
        qi, ki, vi, ai = xs
        o = jnp.einsum('bhnd,bhdv->bhnv', qi, S)
        S = ai[..., :, None] * S + jnp.einsum('bhmd,bhmv->bhdv', ki, vi)
        return S, o

    xs = tuple(jnp.moveaxis(a, 2, 0) for a in (q_in, k_st, v, a_last))
    s_T, o_inter = lax.scan(step, s0, xs)
    o = o_intra + jnp.moveaxis(o_inter, 0, 2)
    return o.reshape(B, H, T, -1).astype(dt), s_T


def _bidir_scan(q, v, k_dirs, g_dirs, qc, vc, kc_dirs, gc_dirs, chunk, per_dim):
    B, H, _, dk = q.shape
    s0 = jnp.zeros((B, H, dk, v.shape[-1]), F32)
    ident = lambda a: a
    flip = lambda a: jnp.flip(a, axis=2)
    o_lat, o_ctx = [], []
    for d, tf in enumerate((ident, flip)):
        oc, sc = _chunk_scan(tf(qc), tf(kc_dirs[d]), tf(vc), tf(gc_dirs[d]), s0, chunk, per_dim)
        o, _ = _chunk_scan(tf(q), tf(k_dirs[d]), tf(v), tf(g_dirs[d]), sc, chunk, per_dim)
        o_lat.append(tf(o))
        o_ctx.append(tf(oc))
    return o_lat[0] + o_lat[1], o_ctx[0] + o_ctx[1]


def _mla(qn, qr, kn, kr, v, qnc, qrc, knc, krc, vc, need_ctx):
    B, T, H, _ = qn.shape
    L = knc.shape[1]
    nb = T // MLA_QBLOCK
    scale = (MLA_NOPE + MLA_ROPE) ** -0.5
    k_n = jnp.concatenate([knc, kn], 1)
    k_r = jnp.concatenate([krc, kr], 1)
    v_all = jnp.concatenate([vc, v], 1)

    def block(args):
        qbn, qbr = args
        s = (jnp.einsum('bqhd,bkhd->bhqk', qbn, k_n) + jnp.einsum('bqhr,bkr->bhqk', qbr, k_r)).astype(F32) * scale
        p = jax.nn.softmax(s, axis=-1).astype(v_all.dtype)
        return jnp.einsum('bhqk,bkhd->bqhd', p, v_all)

    to_blocks = lambda a: jnp.moveaxis(a.reshape((B, nb, MLA_QBLOCK) + a.shape[2:]), 1, 0)
    o = lax.map(block, (to_blocks(qn), to_blocks(qr)))
    o_lat = jnp.moveaxis(o, 0, 1).reshape(B, T, H * MLA_V)
    o_ctx = None
    if need_ctx:
        s = (jnp.einsum('blhd,bmhd->bhlm', qnc, knc) + jnp.einsum('blhr,bmr->bhlm', qrc, krc)).astype(F32) * scale
        pc = jax.nn.softmax(s, axis=-1).astype(vc.dtype)
        o_ctx = jnp.einsum('bhlm,bmhd->blhd', pc, vc).reshape(B, L, H * MLA_V)
    return o_lat, o_ctx


def _ret_out(o, gate, gn):
    y = _layer_norm(jnp.swapaxes(o, 1, 2))
    y = y.reshape(y.shape[:2] + (-1,)) * gn
    return jax.nn.silu(gate) * y


def _hgrn_out(o, gate, gn):
    y = _rms_norm(jnp.swapaxes(o, 1, 2), gn.reshape(HGRN_HEADS, HGRN_DV))
    return jax.nn.silu(gate) * y.reshape(y.shape[:2] + (-1,))


def _token_mixers(h_lat, h_ctx, w_in, w_out, swa_sink, ret_s, ret_gn, q_norm, kv_norm, w_uq, w_ukv,
                  lb, hgrn_gn, ang_swa, ang_mla, need_ctx):
    B, T, _ = h_lat.shape
    L = h_ctx.shape[1]
    cuts = np.cumsum(IN_WIDTHS)[:-1].tolist()
    (sq, sk, sv, rq, rk, rv, rg, cq, ckv, kr, hq, hff, hfb, hi, hg) = jnp.split(h_lat @ w_in, cuts, axis=-1)
    (sqc, skc, svc, rqc, rkc, rvc, rgc, cqc, ckvc, krc, hqc, hffc, hfbc, hic, hgc) = jnp.split(h_ctx @ w_in, cuts, axis=-1)

    a_lat, a_ctx = _swa(_rope(_heads(sq, SWA_HEADS), ang_swa[:, None]), _rope(_heads(sk, SWA_KV_HEADS), ang_swa[:, None]),
                        _heads(sv, SWA_KV_HEADS), _heads(sqc, SWA_HEADS), _heads(skc, SWA_KV_HEADS),
                        _heads(svc, SWA_KV_HEADS), swa_sink, need_ctx)

    log_gamma = jnp.log1p(-jnp.exp2(-ret_s.astype(F32)))
    ret_g = lambda n: [jnp.broadcast_to(log_gamma[d][None, :, None, None], (B, RET_HEADS, n, 1)) for d in range(2)]
    ksc = RET_DK ** -0.5
    k_r, kc_r = _bhtd(rk, RET_HEADS) * ksc, _bhtd(rkc, RET_HEADS) * ksc
    b_lat, b_ctx = _bidir_scan(_bhtd(rq, RET_HEADS), _bhtd(rv, RET_HEADS), [k_r, k_r], ret_g(T),
                               _bhtd(rqc, RET_HEADS), _bhtd(rvc, RET_HEADS), [kc_r, kc_r], ret_g(L), RET_CHUNK, False)

    def mla_proj(cq_, ckv_):
        qq = _heads(_rms_norm(cq_, q_norm) @ w_uq, MLA_HEADS)
        kv = _heads(_rms_norm(ckv_, kv_norm) @ w_ukv, MLA_HEADS)
        return qq[..., :MLA_NOPE], qq[..., MLA_NOPE:], kv[..., :MLA_NOPE], kv[..., MLA_NOPE:]
    qn, qr, kn, vv = mla_proj(cq, ckv)
    qnc, qrc, knc, vvc = mla_proj(cqc, ckvc)
    c_lat, c_ctx = _mla(qn, _rope(qr, ang_mla[:, None]), kn, _rope(kr, ang_mla), vv,
                        qnc, qrc, knc, krc, vvc, need_ctx)

    def hgrn_dirs(zf, zb):
        ks_, gs_ = [], []
        for d, z in enumerate((zf, zb)):
            lbd = lb[d].reshape(HGRN_HEADS, 1, HGRN_DK)
            f = lbd + (1.0 - lbd) * jax.nn.sigmoid(_bhtd(z, HGRN_HEADS).astype(F32))
            ks_.append(1.0 - f)
            gs_.append(jnp.log(f))
        return ks_, gs_
    hk, hgd = hgrn_dirs(hff, hfb)
    hkc, hgdc = hgrn_dirs(hffc, hfbc)
    d_lat, d_ctx = _bidir_scan(jax.nn.silu(_bhtd(hq, HGRN_HEADS)), _bhtd(hi, HGRN_HEADS), hk, hgd,
                               jax.nn.silu(_bhtd(hqc, HGRN_HEADS)), _bhtd(hic, HGRN_HEADS), hkc, hgdc, HGRN_CHUNK, True)

    o_lat = jnp.concatenate([a_lat, _ret_out(b_lat, rg, ret_gn), c_lat, _hgrn_out(d_lat, hg, hgrn_gn)], -1) @ w_out
    o_ctx = None
    if need_ctx:
        o_ctx = jnp.concatenate([a_ctx, _ret_out(b_ctx, rgc, ret_gn), c_ctx, _hgrn_out(d_ctx, hgc, hgrn_gn)], -1) @ w_out
    return o_lat, o_ctx


def _hier_moe(h, w_rg, w_re, w1, w3, w2):
    n, d = h.shape
    pg = jax.nn.softmax((h @ w_rg).astype(F32), axis=-1)
    grp = jnp.argmax(pg, axis=-1)
    p_grp = jnp.take_along_axis(pg, grp[:, None], axis=-1)
    le = (h @ w_re).astype(F32).reshape(n, N_GROUPS, EXPERTS_PER_GROUP)
    le = jnp.take_along_axis(le, grp[:, None, None], axis=1)[:, 0]
    top_p, top_i = lax.top_k(jax.nn.softmax(le, axis=-1), TOP_K)
    gate = p_grp * top_p / jnp.sum(top_p, -1, keepdims=True)
    eid = (grp[:, None] * EXPERTS_PER_GROUP + top_i).reshape(-1).astype(jnp.int32)
    tok = jnp.repeat(jnp.arange(n, dtype=jnp.int32), TOP_K)
    wgt = gate.reshape(-1)
    order = jnp.argsort(eid)
    eid, tok, wgt = eid[order], tok[order], wgt[order]
    counts = jnp.zeros((N_EXPERTS,), jnp.int32).at[eid].add(1)
    padded = (counts + MOE_BLOCK - 1) // MOE_BLOCK * MOE_BLOCK
    start = jnp.cumsum(counts) - counts
    pend = jnp.cumsum(padded)
    pstart = pend - padded
    dest = pstart[eid] + jnp.arange(eid.shape[0], dtype=jnp.int32) - start[eid]
    n_blocks = (n * TOP_K + N_EXPERTS * (MOE_BLOCK - 1) + MOE_BLOCK - 1) // MOE_BLOCK
    xbuf = jnp.zeros((n_blocks * MOE_BLOCK, d), h.dtype).at[dest].set(h[tok])
    blk_e = jnp.minimum(jnp.searchsorted(pend, jnp.arange(n_blocks, dtype=jnp.int32) * MOE_BLOCK, side='right'),
                        N_EXPERTS - 1)

    def expert_block(args):
        xb, e = args
        return (jax.nn.silu(xb @ w1[e]) * (xb @ w3[e])) @ w2[e]

    ybuf = lax.map(expert_block, (xbuf.reshape(n_blocks, MOE_BLOCK, d), blk_e)).reshape(-1, d)
    y = jax.ops.segment_sum(ybuf[dest].astype(F32) * wgt[:, None], tok, num_segments=n)
    return y.astype(h.dtype)


def setup_inputs(seed: int = 0) -> dict:
    key = jax.random.key(seed)
    ks = jax.random.split(key, 28)
    nrm = lambda k, shape, s: jax.random.normal(k, shape, F32) * s
    D = D_MODEL
    return {
        "x": nrm(ks[0], (BATCH, SEQ, D), 1.0),
        "c": nrm(ks[1], (BATCH, D), 1.0),
        "ctx": nrm(ks[2], (BATCH, CTX_LEN, D), 1.0),
        "c_ctx": nrm(ks[3], (D,), 1.0),
        "w_ada": nrm(ks[4], (DEPTH, D, 6 * D), 0.5 * D ** -0.5),
        "b_ada": nrm(ks[5], (DEPTH, 6 * D), 0.02),
        "w_in": nrm(ks[6], (DEPTH, D, IN_COLS), D ** -0.5),
        "swa_sink": nrm(ks[7], (DEPTH, SWA_HEADS), 1.0),
        "ret_decay_exp": 5.0 + jnp.arange(RET_HEADS, dtype=F32) + nrm(ks[8], (DEPTH, 2, RET_HEADS), 0.2),
        "ret_gn": 1.0 + nrm(ks[9], (DEPTH, RET_HEADS * RET_DV), 0.1),
        "mla_q_norm": 1.0 + nrm(ks[10], (DEPTH, MLA_Q_RANK), 0.1),
        "mla_kv_norm": 1.0 + nrm(ks[11], (DEPTH, MLA_KV_RANK), 0.1),
        "mla_w_uq": nrm(ks[12], (DEPTH, MLA_Q_RANK, MLA_HEADS * (MLA_NOPE + MLA_ROPE)), MLA_Q_RANK ** -0.5),
        "mla_w_ukv": nrm(ks[13], (DEPTH, MLA_KV_RANK, MLA_HEADS * (MLA_NOPE + MLA_V)), MLA_KV_RANK ** -0.5),
        "hgrn_lb_logits": nrm(ks[14], (DEPTH, 2, HGRN_HEADS * HGRN_DK), 0.5),
        "hgrn_gn": 1.0 + nrm(ks[15], (DEPTH, HGRN_HEADS * HGRN_DV), 0.1),
        "w_out": nrm(ks[16], (DEPTH, MIX_WIDTH, D), DEEPNORM_BETA * MIX_WIDTH ** -0.5),
        "ln1_g": 1.0 + nrm(ks[17], (DEPTH, D), 0.1),
        "ln1_b": nrm(ks[18], (DEPTH, D), 0.02),
        "router_group": nrm(ks[19], (DEPTH, D, N_GROUPS), D ** -0.5),
        "router_expert": nrm(ks[20], (DEPTH, D, N_EXPERTS), D ** -0.5),
        "moe_w1": nrm(ks[21], (DEPTH, N_EXPERTS, D, EXPERT_FF), D ** -0.5),
        "moe_w3": nrm(ks[22], (DEPTH, N_EXPERTS, D, EXPERT_FF), D ** -0.5),
        "moe_w2": nrm(ks[23], (DEPTH, N_EXPERTS, EXPERT_FF, D), DEEPNORM_BETA * EXPERT_FF ** -0.5),
        "ln2_g": 1.0 + nrm(ks[24], (DEPTH, D), 0.1),
        "ln2_b": nrm(ks[25], (DEPTH, D), 0.02),
    }


def reference(x, c, ctx, c_ctx, w_ada, b_ada, w_in, swa_sink, ret_decay_exp, ret_gn, mla_q_norm, mla_kv_norm,
              mla_w_uq, mla_w_ukv, hgrn_lb_logits, hgrn_gn, w_out, ln1_g, ln1_b, router_group, router_expert,
              moe_w1, moe_w3, moe_w2, ln2_g, ln2_b):
    B, T, D = x.shape
    L = ctx.shape[1]
    rows = T // GRID_W
    row = jnp.repeat(jnp.arange(rows, dtype=jnp.int32), GRID_W)
    col = jnp.tile(jnp.arange(GRID_W, dtype=jnp.int32), rows)
    ang_swa = _axial_angles(row, col, HEAD_DIM)
    ang_mla = _axial_angles(row, col, MLA_ROPE)
    lbp = jax.nn.softmax(hgrn_lb_logits.astype(F32), axis=0)
    lower_bounds = jnp.cumsum(lbp, axis=0) - lbp[0]

    for l in range(DEPTH):
        need_ctx = l < DEPTH - 1
        ml = (jax.nn.silu(c) @ w_ada[l] + b_ada[l])[:, None, :]
        mc = jax.nn.silu(c_ctx) @ w_ada[l] + b_ada[l]
        sh1, sc1, g1, sh2, sc2, g2 = jnp.split(ml, 6, axis=-1)
        csh1, csc1, cg1, csh2, csc2, cg2 = jnp.split(mc, 6, axis=-1)

        o_lat, o_ctx = _token_mixers(_modulate(x, sh1, sc1), _modulate(ctx, csh1, csc1), w_in[l], w_out[l],
                                     swa_sink[l], ret_decay_exp[l], ret_gn[l], mla_q_norm[l], mla_kv_norm[l],
                                     mla_w_uq[l], mla_w_ukv[l], lower_bounds[l], hgrn_gn[l], ang_swa, ang_mla, need_ctx)
        x = _layer_norm(DEEPNORM_ALPHA * x + g1 * o_lat, ln1_g[l], ln1_b[l])
        h2 = _modulate(x, sh2, sc2).reshape(B * T, D)
        if need_ctx:
            ctx = _layer_norm(DEEPNORM_ALPHA * ctx + cg1 * o_ctx, ln1_g[l], ln1_b[l])
            h2c = _modulate(ctx, csh2, csc2).reshape(B * L, D)
            y = _hier_moe(jnp.concatenate([h2, h2c], 0), router_group[l], router_expert[l],
                          moe_w1[l], moe_w3[l], moe_w2[l])
            y_lat, y_ctx = y[:B * T].reshape(B, T, D), y[B * T:].reshape(B, L, D)
            ctx = _layer_norm(DEEPNORM_ALPHA * ctx + cg2 * y_ctx, ln2_g[l], ln2_b[l])
        else:
            y_lat = _hier_moe(h2, router_group[l], router_expert[l], moe_w1[l], moe_w3[l], moe_w2[l]).reshape(B, T, D)
        x = _layer_norm(DEEPNORM_ALPHA * x + g2 * y_lat, ln2_g[l], ln2_b[l])
    return x
```

```python
import functools
import math

import numpy as np
import jax
import jax.numpy as jnp
from jax import lax
from jax.experimental import pallas as pl
from jax.experimental.pallas import tpu as pltpu

F32 = jnp.float32
BF16 = jnp.bfloat16

GRID_W = 64
HEAD_DIM = 128
SWA_HEADS = 4
SWA_KV_HEADS = 2
SWA_BLOCK = 128
RET_HEADS = 4
RET_DK = 128
MLA_HEADS = 4
MLA_Q_RANK = 384
MLA_KV_RANK = 128
MLA_NOPE = 128
MLA_ROPE = 64
MLA_V = 128
HGRN_HEADS = 4
N_GROUPS = 4
EXPERTS_PER_GROUP = 8
N_EXPERTS = N_GROUPS * EXPERTS_PER_GROUP
TOP_K = 2
ROPE_BASE = 10000.0
EPS = 1e-6

LANES = 128
SUBLANES = 8
VMEM_LIMIT_BYTES = 56 * 1024 * 1024

CHUNK = 128
ROW_TILE = 256
MOE_ROWS = 256
WIDE = 512

SEG_A = 1024
SEG_B = 2048
SEG_C = 640
SEG_D = 2560


def _cparams(sem, vmem=VMEM_LIMIT_BYTES):
    return pltpu.CompilerParams(dimension_semantics=sem, vmem_limit_bytes=vmem)


def _ln_rows(x):
    mu = jnp.mean(x, axis=-1, keepdims=True)
    xc = x - mu
    var = jnp.mean(xc * xc, axis=-1, keepdims=True)
    return xc * lax.rsqrt(var + EPS)


def _silu(x):
    return x * (1.0 / (1.0 + jnp.exp(-x)))


def _dot(a, b):
    return jnp.dot(a, b, preferred_element_type=F32)


def _dot_nt(a, b):
    return lax.dot_general(a, b, (((1,), (1,)), ((), ())), preferred_element_type=F32)


def _dot_tn(a, b):
    return lax.dot_general(a, b, (((0,), (0,)), ((), ())), preferred_element_type=F32)


def _ada_kernel(c_ref, w_ref, b_ref, o_ref):
    s = _silu(c_ref[...]).astype(BF16)
    o_ref[0] = _dot(s, w_ref[0].astype(BF16)) + b_ref[0]


def _ada(cvec, w_ada, b_ada):
    depth, d, n6 = w_ada.shape
    tn = n6 // 8
    return pl.pallas_call(
        _ada_kernel,
        out_shape=jax.ShapeDtypeStruct((depth, SUBLANES, n6), F32),
        grid=(depth, n6 // tn),
        in_specs=[
            pl.BlockSpec((SUBLANES, d), lambda l, j: (0, 0)),
            pl.BlockSpec((1, d, tn), lambda l, j: (l, 0, j)),
            pl.BlockSpec((1, 1, tn), lambda l, j: (l, 0, j)),
        ],
        out_specs=pl.BlockSpec((1, SUBLANES, tn), lambda l, j: (l, 0, j)),
        compiler_params=_cparams(("arbitrary", "arbitrary")),
        name="ada_mod",
    )(cvec, w_ada, b_ada.reshape(depth, 1, n6))


def _proj_kernel(x_ref, sh_ref, sc_ref, wa_ref, wb_ref, wc_ref, wd_ref, oa, ob, oc, od):
    y = _ln_rows(x_ref[...])
    h = (y * (1.0 + sc_ref[0]) + sh_ref[0]).astype(BF16)
    for w_ref, o_ref in ((wa_ref, oa), (wb_ref, ob), (wc_ref, oc), (wd_ref, od)):
        width = o_ref.shape[1]
        for j in range(0, width, WIDE):
            cw = min(WIDE, width - j)
            o_ref[:, j:j + cw] = _dot(h, w_ref[:, j:j + cw])


def _group_of_tile(i, tiles_per_batch):
    return jnp.where(i % tiles_per_batch == 0, 2, i // tiles_per_batch)


def _proj(stream, shift, scale, w_segs, tiles_per_batch):
    n, d = stream.shape
    tm = ROW_TILE
    gmap = lambda i: (_group_of_tile(i, tiles_per_batch), 0, 0)
    resident = pl.BlockSpec(memory_space=pltpu.VMEM)
    widths = [w.shape[1] for w in w_segs]
    return pl.pallas_call(
        _proj_kernel,
        out_shape=[jax.ShapeDtypeStruct((n, w), F32) for w in widths],
        grid=(n // tm,),
        in_specs=[
            pl.BlockSpec((tm, d), lambda i: (i, 0)),
            pl.BlockSpec((1, 1, d), gmap),
            pl.BlockSpec((1, 1, d), gmap),
            resident, resident, resident, resident,
        ],
        out_specs=[pl.BlockSpec((tm, w), lambda i: (i, 0)) for w in widths],
        compiler_params=_cparams(("arbitrary",)),
        name="ln_mod_inproj",
    )(stream, shift, scale, *w_segs)


def _rope128(x, cos, sin):
    return x * cos + pltpu.roll(x, 64, 1) * sin


def _rope64(x, cos, sin):
    lane = lax.broadcasted_iota(jnp.int32, x.shape, 1)
    rot = jnp.where((lane % 64) < 32, pltpu.roll(x, 96, 1), pltpu.roll(x, 32, 1))
    return x * cos + rot * sin


def _rope_tables(t_len, l_len, batch):
    rows = t_len // GRID_W
    row = np.repeat(np.arange(rows), GRID_W).astype(np.float32)
    col = np.tile(np.arange(GRID_W), rows).astype(np.float32)

    def angles(rot_dim):
        n_freq = rot_dim // 4
        inv = (ROPE_BASE ** (-np.arange(n_freq, dtype=np.float32) / n_freq)).astype(np.float32)
        return np.concatenate([row[:, None] * inv, col[:, None] * inv], -1).astype(np.float32)

    a_swa = angles(HEAD_DIM)
    cos_swa = np.concatenate([np.cos(a_swa), np.cos(a_swa)], -1)
    sin_swa = np.concatenate([-np.sin(a_swa), np.sin(a_swa)], -1)
    a_mla = angles(MLA_ROPE)
    cos_m = np.concatenate([np.cos(a_mla), np.cos(a_mla), np.ones((t_len, 64), np.float32)], -1)
    sin_m = np.concatenate([-np.sin(a_mla), np.sin(a_mla), np.zeros((t_len, 64), np.float32)], -1)
    ones = np.ones((l_len, LANES), np.float32)
    zeros = np.zeros((l_len, LANES), np.float32)
    cos_rows = np.concatenate([np.concatenate([ones, cos_m], 0)] * batch, 0)
    sin_rows = np.concatenate([np.concatenate([zeros, sin_m], 0)] * batch, 0)
    return (jnp.asarray(cos_swa, F32), jnp.asarray(sin_swa, F32),
            jnp.asarray(cos_rows, F32), jnp.asarray(sin_rows, F32))


def _mla_up_kernel(pc_ref, qn_ref, kvn_ref, wq_ref, wk_ref, wv_ref, cos_ref, sin_ref,
                   q_out, k_out, v_out, *, scale):
    pc = pc_ref[...]
    cq = pc[:, :MLA_Q_RANK]
    ckv = pc[:, MLA_Q_RANK:MLA_Q_RANK + MLA_KV_RANK]
    kr = pc[:, MLA_Q_RANK + MLA_KV_RANK:]
    cos = cos_ref[...]
    sin = sin_ref[...]

    def rms(x, g):
        return x * lax.rsqrt(jnp.mean(x * x, axis=-1, keepdims=True) + EPS) * g

    qh = _dot(rms(cq, qn_ref[...]).astype(BF16), wq_ref[...])
    ckn = rms(ckv, kvn_ref[...]).astype(BF16)
    kh = _dot(ckn, wk_ref[...])
    v_out[...] = _dot(ckn, wv_ref[...]).astype(BF16)
    kr_rot = _rope64(kr, cos, sin).astype(BF16)
    for h in range(MLA_HEADS):
        base = h * 2 * LANES
        q_out[:, base:base + LANES] = (qh[:, base:base + LANES] * scale).astype(BF16)
        q_out[:, base + LANES:base + 2 * LANES] = (
            _rope64(qh[:, base + LANES:base + 2 * LANES], cos, sin) * scale).astype(BF16)
        k_out[:, base:base + LANES] = kh[:, h * LANES:(h + 1) * LANES].astype(BF16)
        k_out[:, base + LANES:base + 2 * LANES] = kr_rot


def _mla_up(pc, q_norm, kv_norm, wq, wk, wv, cos_rows, sin_rows):
    n = pc.shape[0]
    tm = ROW_TILE
    scale = float((MLA_NOPE + MLA_ROPE) ** -0.5)
    const2 = lambda i: (0, 0)
    return pl.pallas_call(
        functools.partial(_mla_up_kernel, scale=scale),
        out_shape=[jax.ShapeDtypeStruct((n, MLA_HEADS * 2 * LANES), BF16),
                   jax.ShapeDtypeStruct((n, MLA_HEADS * 2 * LANES), BF16),
                   jax.ShapeDtypeStruct((n, MLA_HEADS * MLA_V), BF16)],
        grid=(n // tm,),
        in_specs=[
            pl.BlockSpec((tm, SEG_C), lambda i: (i, 0)),
            pl.BlockSpec((1, MLA_Q_RANK), const2),
            pl.BlockSpec((1, MLA_KV_RANK), const2),
            pl.BlockSpec(wq.shape, const2),
            pl.BlockSpec(wk.shape, const2),
            pl.BlockSpec(wv.shape, const2),
            pl.BlockSpec((tm, LANES), lambda i: (i, 0)),
            pl.BlockSpec((tm, LANES), lambda i: (i, 0)),
        ],
        out_specs=[pl.BlockSpec((tm, MLA_HEADS * 2 * LANES), lambda i: (i, 0)),
                   pl.BlockSpec((tm, MLA_HEADS * 2 * LANES), lambda i: (i, 0)),
                   pl.BlockSpec((tm, MLA_HEADS * MLA_V), lambda i: (i, 0))],
        compiler_params=_cparams(("arbitrary",)),
        name="mla_up",
    )(pc, q_norm, kv_norm, wq, wk, wv, cos_rows, sin_rows)


def _mla_attn_kernel(q_ref, k_ref, v_ref, init_ref, o_ref, *, l_len, kc):
    del init_ref
    q = q_ref[...]
    p_len = k_ref.shape[0]

    def step(k, v, carry):
        m, l, acc = carry
        s = _dot_nt(q, k)
        m_new = jnp.maximum(m, jnp.max(s, axis=-1, keepdims=True))
        alpha = jnp.exp(m - m_new)
        p = jnp.exp(s - m_new)
        l = alpha * l + jnp.sum(p, axis=-1, keepdims=True)
        acc = alpha * acc + _dot(p.astype(BF16), v)
        return m_new, l, acc

    tq = q.shape[0]
    carry = (jnp.full((tq, 1), -jnp.inf, F32), jnp.zeros((tq, 1), F32), jnp.zeros((tq, MLA_V), F32))
    carry = step(k_ref[0:l_len, :], v_ref[0:l_len, :], carry)

    def body(j, c):
        off = pl.multiple_of(l_len + j * kc, kc // 2)
        return step(k_ref[pl.ds(off, kc), :], v_ref[pl.ds(off, kc), :], c)

    m, l, acc = lax.fori_loop(0, (p_len - l_len) // kc, body, carry)
    o_ref[...] = acc / l


def _mla_attn(qf, kf, vf, batch, t_len, l_len):
    n = qf.shape[0]
    p_len = t_len + l_len
    tq = ROW_TILE
    kc = 512 if t_len % 512 == 0 else 256
    off = l_len // tq
    tpb = p_len // tq
    return pl.pallas_call(
        functools.partial(_mla_attn_kernel, l_len=l_len, kc=kc),
        out_shape=jax.ShapeDtypeStruct((n, MLA_HEADS * MLA_V), F32),
        grid=(batch, MLA_HEADS, t_len // tq),
        in_specs=[
            pl.BlockSpec((tq, 2 * LANES), lambda b, h, i: (b * tpb + off + i, h)),
            pl.BlockSpec((p_len, 2 * LANES), lambda b, h, i: (b, h)),
            pl.BlockSpec((p_len, MLA_V), lambda b, h, i: (b, h)),
            pl.BlockSpec(memory_space=pl.ANY),
        ],
        out_specs=pl.BlockSpec((tq, MLA_V), lambda b, h, i: (b * tpb + off + i, h)),
        input_output_aliases={3: 0},
        compiler_params=_cparams(("arbitrary", "arbitrary", "arbitrary")),
        name="mla_attn",
    )(qf, kf, vf, jnp.zeros((n, MLA_HEADS * MLA_V), F32))


def _ctx_attn_kernel(*refs, has_sink, scale):
    if has_sink:
        sink_ref, q_ref, k_ref, v_ref, o_ref = refs
    else:
        q_ref, k_ref, v_ref, o_ref = refs
    h = pl.program_id(1)
    s = _dot_nt(q_ref[...].astype(BF16), k_ref[...].astype(BF16)) * scale
    m = jnp.max(s, axis=-1, keepdims=True)
    if has_sink:
        sk = jnp.full((s.shape[0], 1), sink_ref[h], F32)
        m = jnp.maximum(m, sk)
    p = jnp.exp(s - m)
    den = jnp.sum(p, axis=-1, keepdims=True)
    if has_sink:
        den = den + jnp.exp(sk - m)
    o_ref[...] = _dot(p.astype(BF16), v_ref[...].astype(BF16)) / den


def _ctx_attn(out_prev, q_arr, k_arr, v_arr, q_col, k_col, v_col, heads, group, batch, p_len, l_len,
              sink, scale, dq):
    n = out_prev.shape[0]
    qb = dq // LANES
    in_specs = [
        pl.BlockSpec((l_len, dq), lambda b, h: (b * (p_len // l_len), q_col // qb + h)),
        pl.BlockSpec((l_len, dq), lambda b, h: (b * (p_len // l_len), k_col // qb + h // group)),
        pl.BlockSpec((l_len, LANES), lambda b, h: (b * (p_len // l_len), v_col + h // group)),
        pl.BlockSpec(memory_space=pl.ANY),
    ]
    args = [q_arr, k_arr, v_arr, out_prev]
    has_sink = sink is not None
    if has_sink:
        in_specs = [pl.BlockSpec(memory_space=pltpu.SMEM)] + in_specs
        args = [sink] + args

    def kern(*refs):
        refs = list(refs)
        del refs[4 if has_sink else 3]
        _ctx_attn_kernel(*refs, has_sink=has_sink, scale=scale)

    return pl.pallas_call(
        kern,
        out_shape=jax.ShapeDtypeStruct(out_prev.shape, out_prev.dtype),
        grid=(batch, heads),
        in_specs=in_specs,
        out_specs=pl.BlockSpec((l_len, LANES), lambda b, h: (b * (p_len // l_len), h)),
        input_output_aliases={len(args) - 1: 0},
        compiler_params=_cparams(("arbitrary", "arbitrary")),
        name="ctx_attn_sink" if has_sink else "ctx_attn",
    )(*args)


def _swa_kernel(sink_ref, q_ref, kp_ref, kc_ref, kn_ref, vp_ref, vc_ref, vn_ref, kx_ref, vx_ref,
                cq_ref, sq_ref, cp_ref, sp_ref, cn_ref, sn_ref, init_ref, o_ref, *, nb, scale):
    del init_ref
    n = pl.program_id(1)
    h = pl.program_id(2)
    blk = SWA_BLOCK
    cq, sq = cq_ref[...], sq_ref[...]
    q = q_ref[...]
    q0 = _rope128(q[:, :LANES], cq, sq)
    q1 = _rope128(q[:, LANES:], cq, sq)
    qq = jnp.concatenate([q0, q1], axis=0).astype(BF16)
    kc = _rope128(kc_ref[...], cq, sq).astype(BF16)
    kp = _rope128(kp_ref[...], cp_ref[...], sp_ref[...]).astype(BF16)
    kn = _rope128(kn_ref[...], cn_ref[...], sn_ref[...]).astype(BF16)
    s_c = _dot_nt(qq, kc) * scale
    s_p = _dot_nt(qq, kp) * scale
    s_n = _dot_nt(qq, kn) * scale
    s_x = _dot_nt(qq, kx_ref[...].astype(BF16)) * scale
    ri = lax.broadcasted_iota(jnp.int32, (2 * blk, blk), 0) % blk
    ci = lax.broadcasted_iota(jnp.int32, (2 * blk, blk), 1)
    neg = jnp.float32(-jnp.inf)
    s_p = jnp.where((ci >= ri) & (n > 0), s_p, neg)
    s_n = jnp.where((ci <= ri) & (n < nb - 1), s_n, neg)
    rr = lax.broadcasted_iota(jnp.int32, (2 * blk, 1), 0)
    sk = jnp.where(rr < blk, sink_ref[2 * h], sink_ref[2 * h + 1]).astype(F32)
    mx = lambda a: jnp.max(a, axis=-1, keepdims=True)
    m = jnp.maximum(jnp.maximum(mx(s_c), mx(s_p)), jnp.maximum(mx(s_n), mx(s_x)))
    m = jnp.maximum(m, sk)
    p_c, p_p, p_n, p_x = (jnp.exp(a - m) for a in (s_c, s_p, s_n, s_x))
    sm = lambda a: jnp.sum(a, axis=-1, keepdims=True)
    den = sm(p_c) + sm(p_p) + sm(p_n) + sm(p_x) + jnp.exp(sk - m)
    o = (_dot(p_c.astype(BF16), vc_ref[...].astype(BF16))
         + _dot(p_p.astype(BF16), vp_ref[...].astype(BF16))
         + _dot(p_n.astype(BF16), vn_ref[...].astype(BF16))
         + _dot(p_x.astype(BF16), vx_ref[...].astype(BF16))) / den
    o_ref[:, :LANES] = o[:blk]
    o_ref[:, LANES:] = o[blk:]


def _swa(pa, sink, cos_t, sin_t, batch, t_len, l_len):
    n = pa.shape[0]
    p_len = t_len + l_len
    blk = SWA_BLOCK
    nb = t_len // blk
    bpb = p_len // blk
    off = l_len // blk
    scale = float(HEAD_DIM ** -0.5)
    kcol, vcol = 4, 6

    def rb(b, j):
        return b * bpb + off + j

    prv = lambda j: jnp.maximum(j - 1, 0)
    nxt = lambda j: jnp.minimum(j + 1, nb - 1)
    in_specs = [
        pl.BlockSpec(memory_space=pltpu.SMEM),
        pl.BlockSpec((blk, 2 * LANES), lambda b, j, h: (rb(b, j), h)),
        pl.BlockSpec((blk, LANES), lambda b, j, h: (rb(b, prv(j)), kcol + h)),
        pl.BlockSpec((blk, LANES), lambda b, j, h: (rb(b, j), kcol + h)),
        pl.BlockSpec((blk, LANES), lambda b, j, h: (rb(b, nxt(j)), kcol + h)),
        pl.BlockSpec((blk, LANES), lambda b, j, h: (rb(b, prv(j)), vcol + h)),
        pl.BlockSpec((blk, LANES), lambda b, j, h: (rb(b, j), vcol + h)),
        pl.BlockSpec((blk, LANES), lambda b, j, h: (rb(b, nxt(j)), vcol + h)),
        pl.BlockSpec((l_len, LANES), lambda b, j, h: (b * (p_len // l_len), kcol + h)),
        pl.BlockSpec((l_len, LANES), lambda b, j, h: (b * (p_len // l_len), vcol + h)),
        pl.BlockSpec((blk, LANES), lambda b, j, h: (j, 0)),
        pl.BlockSpec((blk, LANES), lambda b, j, h: (j, 0)),
        pl.BlockSpec((blk, LANES), lambda b, j, h: (prv(j), 0)),
        pl.BlockSpec((blk, LANES), lambda b, j, h: (prv(j), 0)),
        pl.BlockSpec((blk, LANES), lambda b, j, h: (nxt(j), 0)),
        pl.BlockSpec((blk, LANES), lambda b, j, h: (nxt(j), 0)),
        pl.BlockSpec(memory_space=pl.ANY),
    ]
    return pl.pallas_call(
        functools.partial(_swa_kernel, nb=nb, scale=scale),
        out_shape=jax.ShapeDtypeStruct((n, SWA_HEADS * HEAD_DIM), F32),
        grid=(batch, nb, SWA_KV_HEADS),
        in_specs=in_specs,
        out_specs=pl.BlockSpec((blk, 2 * LANES), lambda b, j, h: (rb(b, j), h)),
        input_output_aliases={len(in_specs) - 1: 0},
        compiler_params=_cparams(("arbitrary", "arbitrary", "arbitrary")),
        name="swa",
    )(sink, pa, pa, pa, pa, pa, pa, pa, pa, pa, cos_t, sin_t, cos_t, sin_t, cos_t, sin_t,
      jnp.zeros((n, SWA_HEADS * HEAD_DIM), F32))


def _chunk_index(d, c, n_l, n_t):
    bwd = jnp.where(c < n_l, n_l - 1 - c, n_l + (n_t - 1) - (c - n_l))
    return jnp.where(d == 0, c, bwd)


def _flip_iotas(d):
    row = lax.broadcasted_iota(jnp.int32, (CHUNK, CHUNK), 0)
    col = lax.broadcasted_iota(jnp.int32, (CHUNK, CHUNK), 1)
    rf = jnp.where(d == 0, row, CHUNK - 1 - row)
    cf = jnp.where(d == 0, col, CHUNK - 1 - col)
    return rf, cf


def _ret_kernel(s_ref, q_ref, k_ref, v_ref, o_ref, st_ref):
    d = pl.program_id(1)
    c = pl.program_id(2)

    @pl.when(c == 0)
    def _():
        st_ref[...] = jnp.zeros_like(st_ref)

    rf, cf = _flip_iotas(d)
    rff = rf.astype(F32)
    dn = (rf - cf).astype(F32)
    ks = float(RET_DK ** -0.5)
    for h in range(RET_HEADS):
        hs = slice(h * LANES, (h + 1) * LANES)
        sv = jnp.full((CHUNK, CHUNK), s_ref[d * RET_HEADS + h], F32)
        lg = jnp.log1p(-jnp.exp2(-sv))
        dec = jnp.where(dn >= 0, jnp.exp(dn * lg), 0.0)
        eq = jnp.exp((rff + 1.0) * lg)
        ek = jnp.exp((CHUNK - 1.0 - rff) * lg)
        q = q_ref[:, hs]
        k = k_ref[:, hs] * ks
        v = v_ref[:, hs].astype(BF16)
        st = st_ref[h]
        a = _dot_nt(q.astype(BF16), k.astype(BF16)) * dec
        o = _dot(a.astype(BF16), v) + _dot_nt((q * eq).astype(BF16), st.astype(BF16))
        o_ref[0, :, hs] = o
        st_ref[h] = jnp.exp(float(CHUNK) * lg) * st + _dot_tn(v, (k * ek).astype(BF16))


def _hgrn_level_mats():
    c = CHUNK
    t = np.arange(c)[:, None]
    u = np.arange(c)[None, :]
    mats = [u <= t, u > t]
    m = c // 2
    while m >= 1:
        mid = (t // (2 * m)) * 2 * m + m
        second = (t % (2 * m)) >= m
        qrole = (u >= mid) & (u <= t)
        krole = (u > t) & (u <= mid - 1)
        mats.append(np.where(second, qrole, krole))
        m //= 2
    fwd = np.concatenate(mats, 0).astype(np.float32)
    bwd = np.concatenate([mm[::-1, ::-1] for mm in mats], 0).astype(np.float32)
    return np.stack([fwd, bwd], 0)


N_LEVELS = int(math.log2(CHUNK))


def _hgrn_kernel(m_ref, lb_ref, q_ref, f_ref, v_ref, o_ref, st_ref):
    d = pl.program_id(1)
    c = pl.program_id(2)

    @pl.when(c == 0)
    def _():
        st_ref[...] = jnp.zeros_like(st_ref)

    rf, cf = _flip_iotas(d)
    mst = m_ref[0]
    for h in range(HGRN_HEADS):
        hs = slice(h * LANES, (h + 1) * LANES)
        lb = lb_ref[0, :, hs]
        f = lb + (1.0 - lb) * (1.0 / (1.0 + jnp.exp(-f_ref[:, hs])))
        k = 1.0 - f
        g = jnp.log(f)
        q = _silu(q_ref[:, hs])
        v = v_ref[:, hs].astype(BF16)
        g_hi = g.astype(BF16)
        g_lo = (g - g_hi.astype(F32)).astype(BF16)
        x2 = _dot(mst, jnp.concatenate([g_hi, g_lo], axis=1))
        e = jnp.exp(x2[:, :LANES] + x2[:, LANES:])
        e_q = e[0:CHUNK]
        q_in = q * e_q
        k_st = k * e[CHUNK:2 * CHUNK]
        scores = jnp.where(rf == cf, _dot_nt(q.astype(BF16), k.astype(BF16)), 0.0)
        for lvl in range(N_LEVELS):
            m = CHUNK >> (lvl + 1)
            sh = N_LEVELS - lvl
            el = e[(2 + lvl) * CHUNK:(3 + lvl) * CHUNK]
            second = (rf & m) != 0
            ql = jnp.where(second, q * el, 0.0).astype(BF16)
            kl = jnp.where(second, 0.0, k * el).astype(BF16)
            sl = _dot_nt(ql, kl)
            scores = scores + jnp.where((rf >> sh) == (cf >> sh), sl, 0.0)
        st = st_ref[h]
        o = _dot(scores.astype(BF16), v) + _dot_nt(q_in.astype(BF16), st.astype(BF16))
        o_ref[0, :, hs] = o
        a_last = jnp.where(d == 0, e_q[CHUNK - 1:CHUNK, :], e_q[0:1, :])
        st_ref[h] = st * a_last + _dot_tn(v, k_st.astype(BF16))


def _scan_call(kernel, name, extra_args, extra_specs, arrs, cols, batch, t_len, l_len):
    n = arrs[0].shape[0]
    p_len = t_len + l_len
    n_l, n_t = l_len // CHUNK, t_len // CHUNK
    cpb = p_len // CHUNK

    def rowblk(b, d, c):
        return b * cpb + _chunk_index(d, c, n_l, n_t)

    def col_map(col):
        return lambda b, d, c: (rowblk(b, d, c), col(d) if callable(col) else col)

    in_specs = list(extra_specs) + [pl.BlockSpec((CHUNK, WIDE), col_map(col)) for col in cols]
    return pl.pallas_call(
        kernel,
        out_shape=jax.ShapeDtypeStruct((2, n, WIDE), F32),
        grid=(batch, 2, n_l + n_t),
        in_specs=in_specs,
        out_specs=pl.BlockSpec((1, CHUNK, WIDE), lambda b, d, c: (d, rowblk(b, d, c), 0)),
        scratch_shapes=[pltpu.VMEM((4, CHUNK, CHUNK), F32)],
        compiler_params=_cparams(("arbitrary", "arbitrary", "arbitrary")),
        name=name,
    )(*extra_args, *arrs)


def _retention(pb, ret_s, batch, t_len, l_len):
    return _scan_call(_ret_kernel, "retention_scan",
                      [ret_s.reshape(-1)], [pl.BlockSpec(memory_space=pltpu.SMEM)],
                      [pb, pb, pb], [0, 1, 2], batch, t_len, l_len)


def _hgrn(pd, lower_bounds, mats, batch, t_len, l_len):
    lb3 = lower_bounds.reshape(2, 1, WIDE)
    specs = [pl.BlockSpec((1,) + mats.shape[1:], lambda b, d, c: (d, 0, 0)),
             pl.BlockSpec((1, 1, WIDE), lambda b, d, c: (d, 0, 0))]
    return _scan_call(_hgrn_kernel, "hgrn_scan", [mats, lb3], specs,
                      [pd, pd, pd], [0, lambda d: 1 + d, 3], batch, t_len, l_len)


def _route(logits):
    lane = lax.broadcasted_iota(jnp.int32, logits.shape, 1)
    lanef = lane.astype(F32)
    big = jnp.float32(1e9)
    neg = jnp.float32(-jnp.inf)
    mx = lambda a: jnp.max(a, axis=-1, keepdims=True)
    mn = lambda a: jnp.min(a, axis=-1, keepdims=True)
    sm = lambda a: jnp.sum(a, axis=-1, keepdims=True)
    gl = jnp.where(lane < N_GROUPS, logits, neg)
    gm = mx(gl)
    p_grp = 1.0 / sm(jnp.exp(gl - gm))
    grp = mn(jnp.where(gl == gm, lanef, big))
    lo = N_GROUPS + grp * EXPERTS_PER_GROUP
    ing = (lanef >= lo) & (lanef < lo + EXPERTS_PER_GROUP)
    el = jnp.where(ing, logits, neg)
    ee = jnp.exp(el - mx(el))
    p = ee / sm(ee)
    pm = jnp.where(ing, p, -1.0)
    p1 = mx(pm)
    i1 = mn(jnp.where(pm == p1, lanef, big))
    pm2 = jnp.where(lanef == i1, -1.0, pm)
    p2 = mx(pm2)
    i2 = mn(jnp.where(pm2 == p2, lanef, big))
    den = p1 + p2
    g1 = p_grp * p1 / den
    g2 = p_grp * p2 / den
    out = jnp.where(lane == 0, i1 - N_GROUPS,
                    jnp.where(lane == 1, i2 - N_GROUPS,
                              jnp.where(lane == 2, g1, jnp.where(lane == 3, g2, 0.0))))
    return out


def _out_kernel(x_ref, a_ref, b_ref, c_ref, d_ref, rg_ref, hg_ref, g1_ref, sh2_ref, sc2_ref,
                lng_ref, lnb_ref, rgn_ref, hgn_ref, wo_ref, wr_ref,
                xo_ref, h2_ref, r_ref, *, alpha):
    bsum = b_ref[0] + b_ref[1]
    dsum = d_ref[0] + d_ref[1]
    rgn = rgn_ref[...]
    hgn = hgn_ref[...]
    parts_b, parts_d = [], []
    for h in range(4):
        hs = slice(h * LANES, (h + 1) * LANES)
        parts_b.append(_ln_rows(bsum[:, hs]))
        dh = dsum[:, hs]
        parts_d.append(dh * lax.rsqrt(jnp.mean(dh * dh, axis=-1, keepdims=True) + EPS))
    bo = _silu(rg_ref[...]) * (jnp.concatenate(parts_b, axis=1) * rgn)
    do = _silu(hg_ref[...]) * (jnp.concatenate(parts_d, axis=1) * hgn)
    mix = jnp.concatenate([a_ref[...], bo, c_ref[...], do], axis=1).astype(BF16)
    o = _dot(mix, wo_ref[...])
    y = alpha * x_ref[...] + g1_ref[0] * o
    xn = _ln_rows(y) * lng_ref[...] + lnb_ref[...]
    xo_ref[...] = xn
    h2 = _ln_rows(xn) * (1.0 + sc2_ref[0]) + sh2_ref[0]
    h2_ref[...] = h2
    r_ref[...] = _route(_dot(h2.astype(BF16), wr_ref[...]))


def _out_proj(stream, a, b2, c, d2, pb, pd, g1, sh2, sc2, ln_g, ln_b, ret_gn, hgrn_gn, w_out, w_r,
              tiles_per_batch, alpha):
    n, dm = stream.shape
    tm = ROW_TILE
    gmap = lambda i: (_group_of_tile(i, tiles_per_batch), 0, 0)
    row = lambda w: pl.BlockSpec((tm, w), lambda i: (i, 0))
    row2 = pl.BlockSpec((2, tm, WIDE), lambda i: (0, i, 0))
    const = lambda w: pl.BlockSpec((1, w), lambda i: (0, 0))
    resident = pl.BlockSpec(memory_space=pltpu.VMEM)
    return pl.pallas_call(
        functools.partial(_out_kernel, alpha=alpha),
        out_shape=[jax.ShapeDtypeStruct((n, dm), F32), jax.ShapeDtypeStruct((n, dm), F32),
                   jax.ShapeDtypeStruct((n, LANES), F32)],
        grid=(n // tm,),
        in_specs=[row(dm), row(WIDE), row2, row(WIDE), row2,
                  pl.BlockSpec((tm, WIDE), lambda i: (i, 3)),
                  pl.BlockSpec((tm, WIDE), lambda i: (i, 4)),
                  pl.BlockSpec((1, 1, dm), gmap), pl.BlockSpec((1, 1, dm), gmap),
                  pl.BlockSpec((1, 1, dm), gmap),
                  const(dm), const(dm), const(WIDE), const(WIDE), resident, resident],
        out_specs=[row(dm), row(dm), row(LANES)],
        compiler_params=_cparams(("arbitrary",)),
        name="mix_outproj_norm_route",
    )(stream, a, b2, c, d2, pb, pd, g1, sh2, sc2, ln_g, ln_b, ret_gn, hgrn_gn, w_out, w_r)


def _moe_kernel(be_ref, bv_ref, tok_ref, h_hbm, gate_ref, w1_ref, w3_ref, w2_ref, y_ref,
                xbuf, sem, w1b, w3b, w2b, *, n_blocks):
    j = pl.program_id(0)
    slot = j % 2
    rows = MOE_ROWS

    def row_copy(jj, sl, r):
        tok = tok_ref[jj * rows + r]
        return pltpu.make_async_copy(h_hbm.at[tok], xbuf.at[sl, r], sem.at[sl])

    def start_gather(jj, sl):
        for r in range(rows):
            row_copy(jj, sl, r).start()

    @pl.when((j == 0) & (bv_ref[0] == 1))
    def _():
        start_gather(0, 0)

    nxt = jnp.minimum(j + 1, n_blocks - 1)

    @pl.when((j + 1 < n_blocks) & (bv_ref[nxt] == 1))
    def _():
        start_gather(nxt, 1 - slot)

    first = (j == 0) | (be_ref[j] != be_ref[jnp.maximum(j - 1, 0)])

    @pl.when(first & (bv_ref[j] == 1))
    def _():
        w1b[...] = w1_ref[0].astype(BF16)
        w3b[...] = w3_ref[0].astype(BF16)
        w2b[...] = w2_ref[0].astype(BF16)

    @pl.when(bv_ref[j] == 1)
    def _():
        for r in range(rows):
            row_copy(j, slot, r).wait()
        x = xbuf[slot].astype(BF16)
        hmid = _silu(_dot(x, w1b[...])) * _dot(x, w3b[...])
        y = _dot(hmid.astype(BF16), w2b[...])
        y_ref[...] = y * gate_ref[0]

    @pl.when(bv_ref[j] == 0)
    def _():
        y_ref[...] = jnp.zeros_like(y_ref)


def _moe_experts(h2, blk_e, blk_valid, tok_of_row, gate_of_row, w1, w3, w2):
    n, dm = h2.shape
    n_e, _, ff = w1.shape
    rows = MOE_ROWS
    n_blocks = blk_e.shape[0]
    grid_spec = pltpu.PrefetchScalarGridSpec(
        num_scalar_prefetch=3,
        grid=(n_blocks,),
        in_specs=[
            pl.BlockSpec(memory_space=pl.ANY),
            pl.BlockSpec((1, rows, 1), lambda j, be, bv, tk: (j, 0, 0)),
            pl.BlockSpec((1, dm, ff), lambda j, be, bv, tk: (be[j], 0, 0)),
            pl.BlockSpec((1, dm, ff), lambda j, be, bv, tk: (be[j], 0, 0)),
            pl.BlockSpec((1, ff, dm), lambda j, be, bv, tk: (be[j], 0, 0)),
        ],
        out_specs=pl.BlockSpec((rows, dm), lambda j, be, bv, tk: (j, 0)),
        scratch_shapes=[
            pltpu.VMEM((2, rows, dm), F32),
            pltpu.SemaphoreType.DMA((2,)),
            pltpu.VMEM((dm, ff), BF16), pltpu.VMEM((dm, ff), BF16), pltpu.VMEM((ff, dm), BF16),
        ],
    )
    return pl.pallas_call(
        functools.partial(_moe_kernel, n_blocks=n_blocks),
        out_shape=jax.ShapeDtypeStruct((n_blocks * rows, dm), F32),
        grid_spec=grid_spec,
        compiler_params=_cparams(("arbitrary",)),
        name="moe_experts",
    )(blk_e, blk_valid, tok_of_row, h2, gate_of_row.reshape(n_blocks, rows, 1), w1, w3, w2)


def _combine_kernel(pos_ref, x_ref, g2_ref, lng_ref, lnb_ref, y_hbm, o_ref, ybuf, sem,
                    *, n_tiles, tile_of_step, alpha):
    i = pl.program_id(0)
    slot = i % 2
    tm = ROW_TILE

    def row_copy(step, sl, r, k):
        p = pos_ref[(tile_of_step(step) * tm + r) * TOP_K + k]
        return pltpu.make_async_copy(y_hbm.at[p], ybuf.at[sl, k, r], sem.at[sl])

    def start_gather(step, sl):
        for r in range(tm):
            for k in range(TOP_K):
                row_copy(step, sl, r, k).start()

    @pl.when(i == 0)
    def _():
        start_gather(0, 0)

    @pl.when(i + 1 < n_tiles)
    def _():
        start_gather(jnp.minimum(i + 1, n_tiles - 1), 1 - slot)

    for r in range(tm):
        for k in range(TOP_K):
            row_copy(i, slot, r, k).wait()
    y = ybuf[slot, 0] + ybuf[slot, 1]
    z = alpha * x_ref[...] + g2_ref[0] * y
    o_ref[...] = _ln_rows(z) * lng_ref[...] + lnb_ref[...]


def _combine(x_new, ybuf, pos, g2, ln_g, ln_b, tiles_per_batch, alpha, latent_only, batch):
    n, dm = x_new.shape
    tm = ROW_TILE
    if latent_only:
        lat = tiles_per_batch - 1
        n_tiles = batch * lat
        tile_of_step = lambda s: (s // lat) * tiles_per_batch + 1 + s % lat
    else:
        n_tiles = n // tm
        tile_of_step = lambda s: s
    gmap = lambda i, p: (_group_of_tile(tile_of_step(i), tiles_per_batch), 0, 0)
    grid_spec = pltpu.PrefetchScalarGridSpec(
        num_scalar_prefetch=1,
        grid=(n_tiles,),
        in_specs=[
            pl.BlockSpec((tm, dm), lambda i, p: (tile_of_step(i), 0)),
            pl.BlockSpec((1, 1, dm), gmap),
            pl.BlockSpec((1, dm), lambda i, p: (0, 0)),
            pl.BlockSpec((1, dm), lambda i, p: (0, 0)),
            pl.BlockSpec(memory_space=pl.ANY),
        ],
        out_specs=pl.BlockSpec((tm, dm), lambda i, p: (i, 0)),
        scratch_shapes=[pltpu.VMEM((2, TOP_K, tm, dm), F32), pltpu.SemaphoreType.DMA((2,))],
    )
    return pl.pallas_call(
        functools.partial(_combine_kernel, n_tiles=n_tiles, tile_of_step=tile_of_step, alpha=alpha),
        out_shape=jax.ShapeDtypeStruct((n_tiles * tm, dm), F32),
        grid_spec=grid_spec,
        compiler_params=_cparams(("arbitrary",)),
        name="moe_combine_norm",
    )(pos, x_new, g2, ln_g, ln_b, ybuf)


def _route_plan(route, active, n_blocks):
    n = route.shape[0]
    rows = MOE_ROWS
    eid = route[:, :TOP_K].astype(jnp.int32)
    gate = route[:, TOP_K:2 * TOP_K]
    eid = jnp.where(active[:, None], eid, N_EXPERTS).reshape(-1)
    gate = gate.reshape(-1)
    order = jnp.argsort(eid)
    se = eid[order]
    counts = jnp.zeros((N_EXPERTS + 1,), jnp.int32).at[eid].add(1)[:N_EXPERTS]
    padded = (counts + rows - 1) // rows * rows
    start = jnp.cumsum(counts) - counts
    pend = jnp.cumsum(padded)
    pstart = pend - padded
    valid = se < N_EXPERTS
    sec = jnp.minimum(se, N_EXPERTS - 1)
    dest = pstart[sec] + jnp.arange(eid.shape[0], dtype=jnp.int32) - start[sec]
    dest = jnp.where(valid, dest, n_blocks * rows)
    tok_of_row = jnp.zeros((n_blocks * rows,), jnp.int32).at[dest].set(
        (order // TOP_K).astype(jnp.int32), mode="drop")
    gate_of_row = jnp.zeros((n_blocks * rows,), F32).at[dest].set(gate[order], mode="drop")
    pos = jnp.zeros((n * TOP_K,), jnp.int32).at[order].set(jnp.where(valid, dest, 0))
    blk_start = jnp.arange(n_blocks, dtype=jnp.int32) * rows
    blk_e = jnp.minimum(jnp.searchsorted(pend, blk_start, side="right"), N_EXPERTS - 1).astype(jnp.int32)
    blk_valid = (blk_start < pend[-1]).astype(jnp.int32)
    return blk_e, blk_valid, tok_of_row, gate_of_row, pos


def _pad_cols(w, width):
    return jnp.pad(w, ((0, 0), (0, width - w.shape[1])))


def _split_w_in(w_in_l):
    wa = w_in_l[:, :SEG_A]
    wb = w_in_l[:, SEG_A:SEG_A + SEG_B]
    c0 = SEG_A + SEG_B
    c_true = MLA_Q_RANK + MLA_KV_RANK + MLA_ROPE
    wc = _pad_cols(w_in_l[:, c0:c0 + c_true], SEG_C)
    wd = w_in_l[:, c0 + c_true:]
    return [w.astype(BF16) for w in (wa, wb, wc, wd)]


def kernel(x, c, ctx, c_ctx, w_ada, b_ada, w_in, swa_sink, ret_decay_exp, ret_gn, mla_q_norm, mla_kv_norm,
           mla_w_uq, mla_w_ukv, hgrn_lb_logits, hgrn_gn, w_out, ln1_g, ln1_b, router_group, router_expert,
           moe_w1, moe_w3, moe_w2, ln2_g, ln2_b):
    batch, t_len, dm = x.shape
    l_len = ctx.shape[1]
    depth = w_in.shape[0]
    p_len = t_len + l_len
    n = batch * p_len
    assert batch == 2 and l_len == ROW_TILE and t_len % 256 == 0 and t_len % GRID_W == 0
    tiles_per_batch = p_len // ROW_TILE
    alpha = float((2 * depth) ** 0.25)

    cos_swa, sin_swa, cos_rows, sin_rows = _rope_tables(t_len, l_len, batch)
    mats = jnp.asarray(_hgrn_level_mats(), BF16)

    cvec = jnp.zeros((SUBLANES, dm), F32).at[0:batch].set(c).at[batch].set(c_ctx)
    mod = _ada(cvec, w_ada, b_ada)

    lbp = jax.nn.softmax(hgrn_lb_logits.astype(F32), axis=0)
    lower_bounds = jnp.cumsum(lbp, axis=0) - lbp[0]

    stream = jnp.concatenate([ctx, x], axis=1).reshape(n, dm)
    is_latent = (jnp.arange(n, dtype=jnp.int32) % p_len) >= l_len

    out = None
    for l in range(depth):
        need_ctx = l < depth - 1
        m6 = mod[l, :3].reshape(3, 6, dm)
        sh1, sc1, g1, sh2, sc2, g2 = (m6[:, k].reshape(3, 1, dm) for k in range(6))

        pa, pb, pc, pd = _proj(stream, sh1, sc1, _split_w_in(w_in[l]), tiles_per_batch)

        wq = mla_w_uq[l].reshape(MLA_Q_RANK, MLA_HEADS, MLA_NOPE + MLA_ROPE)
        wq = jnp.pad(wq, ((0, 0), (0, 0), (0, 2 * LANES - (MLA_NOPE + MLA_ROPE))))
        wq = wq.reshape(MLA_Q_RANK, MLA_HEADS * 2 * LANES).astype(BF16)
        wkv = mla_w_ukv[l].reshape(MLA_KV_RANK, MLA_HEADS, MLA_NOPE + MLA_V)
        wk = wkv[:, :, :MLA_NOPE].reshape(MLA_KV_RANK, -1).astype(BF16)
        wv = wkv[:, :, MLA_NOPE:].reshape(MLA_KV_RANK, -1).astype(BF16)
        qf, kf, vf = _mla_up(pc, mla_q_norm[l].reshape(1, -1), mla_kv_norm[l].reshape(1, -1),
                             wq, wk, wv, cos_rows, sin_rows)

        a = _swa(pa, swa_sink[l], cos_swa, sin_swa, batch, t_len, l_len)
        cc = _mla_attn(qf, kf, vf, batch, t_len, l_len)
        if need_ctx:
            a = _ctx_attn(a, pa, pa, pa, 0, 4, 6, SWA_HEADS, SWA_HEADS // SWA_KV_HEADS, batch, p_len,
                          l_len, swa_sink[l], float(HEAD_DIM ** -0.5), LANES)
            cc = _ctx_attn(cc, qf, kf, vf, 0, 0, 0, MLA_HEADS, 1, batch, p_len, l_len, None, 1.0,
                           2 * LANES)
        b2 = _retention(pb, ret_decay_exp[l], batch, t_len, l_len)
        d2 = _hgrn(pd, lower_bounds[l], mats, batch, t_len, l_len)

        w_r = _pad_cols(jnp.concatenate([router_group[l], router_expert[l]], axis=1), LANES).astype(BF16)
        x_new, h2, route = _out_proj(stream, a, b2, cc, d2, pb, pd, g1, sh2, sc2,
                                     ln1_g[l].reshape(1, -1), ln1_b[l].reshape(1, -1),
                                     ret_gn[l].reshape(1, -1), hgrn_gn[l].reshape(1, -1),
                                     w_out[l].astype(BF16), w_r, tiles_per_batch, alpha)

        n_active = n if need_ctx else batch * t_len
        n_blocks = (n_active * TOP_K + N_EXPERTS * (MOE_ROWS - 1) + MOE_ROWS - 1) // MOE_ROWS
        active = jnp.ones((n,), bool) if need_ctx else is_latent
        blk_e, blk_valid, tok_of_row, gate_of_row, pos = _route_plan(route, active, n_blocks)
        ybuf = _moe_experts(h2, blk_e, blk_valid, tok_of_row, gate_of_row, moe_w1[l], moe_w3[l], moe_w2[l])
        res = _combine(x_new, ybuf, pos, g2, ln2_g[l].reshape(1, -1), ln2_b[l].reshape(1, -1),
                       tiles_per_batch, alpha, not need_ctx, batch)
        if need_ctx:
            stream = res
        else:
            out = res.reshape(batch, t_len, dm)
    return out
```

```python
import functools
import math

import numpy as np
import jax
import jax.numpy as jnp
from jax import lax
from jax.experimental import pallas as pl
from jax.experimental.pallas import tpu as pltpu

F32 = jnp.float32
BF16 = jnp.bfloat16

GRID_W = 64
HEAD_DIM = 128
SWA_HEADS = 4
SWA_KV_HEADS = 2
SWA_BLOCK = 128
RET_HEADS = 4
RET_DK = 128
MLA_HEADS = 4
MLA_Q_RANK = 384
MLA_KV_RANK = 128
MLA_NOPE = 128
MLA_ROPE = 64
MLA_V = 128
HGRN_HEADS = 4
N_GROUPS = 4
EXPERTS_PER_GROUP = 8
N_EXPERTS = N_GROUPS * EXPERTS_PER_GROUP
TOP_K = 2
ROPE_BASE = 10000.0
EPS = 1e-6

LANES = 128
SUBLANES = 8
VMEM_LIMIT_BYTES = 56 * 1024 * 1024

CHUNK = 128
ROW_TILE = 256
MOE_ROWS = 256
WIDE = 512

SEG_A = 1024
SEG_B = 2048
SEG_C = 640
SEG_D = 2560


def _cparams(sem, vmem=VMEM_LIMIT_BYTES):
    return pltpu.CompilerParams(dimension_semantics=sem, vmem_limit_bytes=vmem)


def _ln_rows(x):
    mu = jnp.mean(x, axis=-1, keepdims=True)
    xc = x - mu
    var = jnp.mean(xc * xc, axis=-1, keepdims=True)
    return xc * lax.rsqrt(var + EPS)


def _silu(x):
    return x * (1.0 / (1.0 + jnp.exp(-x)))


def _dot(a, b):
    return jnp.dot(a, b, preferred_element_type=F32)


def _dot_nt(a, b):
    return lax.dot_general(a, b, (((1,), (1,)), ((), ())), preferred_element_type=F32)


def _dot_tn(a, b):
    return lax.dot_general(a, b, (((0,), (0,)), ((), ())), preferred_element_type=F32)


def _ada_kernel(c_ref, w_ref, b_ref, o_ref):
    s = _silu(c_ref[...]).astype(BF16)
    o_ref[0] = _dot(s, w_ref[0].astype(BF16)) + b_ref[0]


def _ada(cvec, w_ada, b_ada):
    depth, d, n6 = w_ada.shape
    tn = n6 // 8
    return pl.pallas_call(
        _ada_kernel,
        out_shape=jax.ShapeDtypeStruct((depth, SUBLANES, n6), F32),
        grid=(depth, n6 // tn),
        in_specs=[
            pl.BlockSpec((SUBLANES, d), lambda l, j: (0, 0)),
            pl.BlockSpec((1, d, tn), lambda l, j: (l, 0, j)),
            pl.BlockSpec((1, 1, tn), lambda l, j: (l, 0, j)),
        ],
        out_specs=pl.BlockSpec((1, SUBLANES, tn), lambda l, j: (l, 0, j)),
        compiler_params=_cparams(("arbitrary", "arbitrary")),
        name="ada_mod",
    )(cvec, w_ada, b_ada.reshape(depth, 1, n6))


def _proj_kernel(x_ref, sh_ref, sc_ref, wa_ref, wb_ref, wc_ref, wd_ref, oa, ob, oc, od):
    y = _ln_rows(x_ref[...])
    h = (y * (1.0 + sc_ref[0]) + sh_ref[0]).astype(BF16)
    for w_ref, o_ref in ((wa_ref, oa), (wb_ref, ob), (wc_ref, oc), (wd_ref, od)):
        width = o_ref.shape[1]
        for j in range(0, width, WIDE):
            cw = min(WIDE, width - j)
            o_ref[:, j:j + cw] = _dot(h, w_ref[:, j:j + cw])


def _group_of_tile(i, tiles_per_batch):
    return jnp.where(i % tiles_per_batch == 0, 2, i // tiles_per_batch)


def _proj(stream, shift, scale, w_segs, tiles_per_batch):
    n, d = stream.shape
    tm = ROW_TILE
    gmap = lambda i: (_group_of_tile(i, tiles_per_batch), 0, 0)
    resident = pl.BlockSpec(memory_space=pltpu.VMEM)
    widths = [w.shape[1] for w in w_segs]
    return pl.pallas_call(
        _proj_kernel,
        out_shape=[jax.ShapeDtypeStruct((n, w), F32) for w in widths],
        grid=(n // tm,),
        in_specs=[
            pl.BlockSpec((tm, d), lambda i: (i, 0)),
            pl.BlockSpec((1, 1, d), gmap),
            pl.BlockSpec((1, 1, d), gmap),
            resident, resident, resident, resident,
        ],
        out_specs=[pl.BlockSpec((tm, w), lambda i: (i, 0)) for w in widths],
        compiler_params=_cparams(("arbitrary",)),
        name="ln_mod_inproj",
    )(stream, shift, scale, *w_segs)


def _rope128(x, cos, sin):
    return x * cos + pltpu.roll(x, 64, 1) * sin


def _rope64(x, cos, sin):
    lane = lax.broadcasted_iota(jnp.int32, x.shape, 1)
    rot = jnp.where((lane % 64) < 32, pltpu.roll(x, 96, 1), pltpu.roll(x, 32, 1))
    return x * cos + rot * sin


def _rope_tables(t_len, l_len, batch):
    rows = t_len // GRID_W
    row = np.repeat(np.arange(rows), GRID_W).astype(np.float32)
    col = np.tile(np.arange(GRID_W), rows).astype(np.float32)

    def angles(rot_dim):
        n_freq = rot_dim // 4
        inv = (ROPE_BASE ** (-np.arange(n_freq, dtype=np.float32) / n_freq)).astype(np.float32)
        return np.concatenate([row[:, None] * inv, col[:, None] * inv], -1).astype(np.float32)

    a_swa = angles(HEAD_DIM)
    cos_swa = np.concatenate([np.cos(a_swa), np.cos(a_swa)], -1)
    sin_swa = np.concatenate([-np.sin(a_swa), np.sin(a_swa)], -1)
    a_mla = angles(MLA_ROPE)
    cos_m = np.concatenate([np.cos(a_mla), np.cos(a_mla), np.ones((t_len, 64), np.float32)], -1)
    sin_m = np.concatenate([-np.sin(a_mla), np.sin(a_mla), np.zeros((t_len, 64), np.float32)], -1)
    ones = np.ones((l_len, LANES), np.float32)
    zeros = np.zeros((l_len, LANES), np.float32)
    cos_rows = np.concatenate([np.concatenate([ones, cos_m], 0)] * batch, 0)
    sin_rows = np.concatenate([np.concatenate([zeros, sin_m], 0)] * batch, 0)
    return (jnp.asarray(cos_swa, F32), jnp.asarray(sin_swa, F32),
            jnp.asarray(cos_rows, F32), jnp.asarray(sin_rows, F32))


def _mla_up_kernel(pc_ref, qn_ref, kvn_ref, wq_ref, wk_ref, wv_ref, cos_ref, sin_ref,
                   q_out, k_out, v_out, *, scale):
    pc = pc_ref[...]
    cq = pc[:, :MLA_Q_RANK]
    ckv = pc[:, MLA_Q_RANK:MLA_Q_RANK + MLA_KV_RANK]
    kr = pc[:, MLA_Q_RANK + MLA_KV_RANK:]
    cos = cos_ref[...]
    sin = sin_ref[...]

    def rms(x, g):
        return x * lax.rsqrt(jnp.mean(x * x, axis=-1, keepdims=True) + EPS) * g

    qh = _dot(rms(cq, qn_ref[...]).astype(BF16), wq_ref[...])
    ckn = rms(ckv, kvn_ref[...]).astype(BF16)
    kh = _dot(ckn, wk_ref[...])
    v_out[...] = _dot(ckn, wv_ref[...]).astype(BF16)
    kr_rot = _rope64(kr, cos, sin).astype(BF16)
    for h in range(MLA_HEADS):
        base = h * 2 * LANES
        q_out[:, base:base + LANES] = (qh[:, base:base + LANES] * scale).astype(BF16)
        q_out[:, base + LANES:base + 2 * LANES] = (
            _rope64(qh[:, base + LANES:base + 2 * LANES], cos, sin) * scale).astype(BF16)
        k_out[:, base:base + LANES] = kh[:, h * LANES:(h + 1) * LANES].astype(BF16)
        k_out[:, base + LANES:base + 2 * LANES] = kr_rot


def _mla_up(pc, q_norm, kv_norm, wq, wk, wv, cos_rows, sin_rows):
    n = pc.shape[0]
    tm = ROW_TILE
    scale = float((MLA_NOPE + MLA_ROPE) ** -0.5 * math.log2(math.e))
    const2 = lambda i: (0, 0)
    return pl.pallas_call(
        functools.partial(_mla_up_kernel, scale=scale),
        out_shape=[jax.ShapeDtypeStruct((n, MLA_HEADS * 2 * LANES), BF16),
                   jax.ShapeDtypeStruct((n, MLA_HEADS * 2 * LANES), BF16),
                   jax.ShapeDtypeStruct((n, MLA_HEADS * MLA_V), BF16)],
        grid=(n // tm,),
        in_specs=[
            pl.BlockSpec((tm, SEG_C), lambda i: (i, 0)),
            pl.BlockSpec((1, MLA_Q_RANK), const2),
            pl.BlockSpec((1, MLA_KV_RANK), const2),
            pl.BlockSpec(wq.shape, const2),
            pl.BlockSpec(wk.shape, const2),
            pl.BlockSpec(wv.shape, const2),
            pl.BlockSpec((tm, LANES), lambda i: (i, 0)),
            pl.BlockSpec((tm, LANES), lambda i: (i, 0)),
        ],
        out_specs=[pl.BlockSpec((tm, MLA_HEADS * 2 * LANES), lambda i: (i, 0)),
                   pl.BlockSpec((tm, MLA_HEADS * 2 * LANES), lambda i: (i, 0)),
                   pl.BlockSpec((tm, MLA_HEADS * MLA_V), lambda i: (i, 0))],
        compiler_params=_cparams(("arbitrary",)),
        name="mla_up",
    )(pc, q_norm, kv_norm, wq, wk, wv, cos_rows, sin_rows)


def _mla_attn_kernel(q_ref, k_ref, v_ref, init_ref, o_ref, *, l_len, kc):
    del init_ref
    q = q_ref[...]
    p_len = k_ref.shape[0]

    def step(k, v, carry):
        m, l, acc = carry
        s = _dot_nt(q, k)
        m_new = jnp.maximum(m, jnp.max(s, axis=-1, keepdims=True))
        alpha = jnp.exp2(m - m_new)
        p = jnp.exp2(s - m_new)
        l = alpha * l + jnp.sum(p, axis=-1, keepdims=True)
        acc = alpha * acc + _dot(p.astype(BF16), v)
        return m_new, l, acc

    tq = q.shape[0]
    carry = (jnp.full((tq, 1), -jnp.inf, F32), jnp.zeros((tq, 1), F32), jnp.zeros((tq, MLA_V), F32))
    carry = step(k_ref[0:l_len, :], v_ref[0:l_len, :], carry)
    for off in range(l_len, p_len, kc):
        carry = step(k_ref[off:off + kc, :], v_ref[off:off + kc, :], carry)
    m, l, acc = carry
    o_ref[...] = acc / l


def _mla_attn(qf, kf, vf, batch, t_len, l_len):
    n = qf.shape[0]
    p_len = t_len + l_len
    tq = ROW_TILE
    kc = 512 if t_len % 512 == 0 else 256
    off = l_len // tq
    tpb = p_len // tq
    return pl.pallas_call(
        functools.partial(_mla_attn_kernel, l_len=l_len, kc=kc),
        out_shape=jax.ShapeDtypeStruct((n, MLA_HEADS * MLA_V), F32),
        grid=(batch, MLA_HEADS, t_len // tq),
        in_specs=[
            pl.BlockSpec((tq, 2 * LANES), lambda b, h, i: (b * tpb + off + i, h)),
            pl.BlockSpec((p_len, 2 * LANES), lambda b, h, i: (b, h)),
            pl.BlockSpec((p_len, MLA_V), lambda b, h, i: (b, h)),
            pl.BlockSpec(memory_space=pl.ANY),
        ],
        out_specs=pl.BlockSpec((tq, MLA_V), lambda b, h, i: (b * tpb + off + i, h)),
        input_output_aliases={3: 0},
        compiler_params=_cparams(("arbitrary", "arbitrary", "arbitrary")),
        name="mla_attn",
    )(qf, kf, vf, jnp.zeros((n, MLA_HEADS * MLA_V), F32))


def _ctx_attn_kernel(*refs, has_sink, scale):
    ex = jnp.exp if has_sink else jnp.exp2
    if has_sink:
        sink_ref, q_ref, k_ref, v_ref, o_ref = refs
    else:
        q_ref, k_ref, v_ref, o_ref = refs
    h = pl.program_id(1)
    s = _dot_nt(q_ref[...].astype(BF16), k_ref[...].astype(BF16)) * scale
    m = jnp.max(s, axis=-1, keepdims=True)
    if has_sink:
        sk = jnp.full((s.shape[0], 1), sink_ref[h], F32)
        m = jnp.maximum(m, sk)
    p = ex(s - m)
    den = jnp.sum(p, axis=-1, keepdims=True)
    if has_sink:
        den = den + ex(sk - m)
    o_ref[...] = _dot(p.astype(BF16), v_ref[...].astype(BF16)) / den


def _ctx_attn(out_prev, q_arr, k_arr, v_arr, q_col, k_col, v_col, heads, group, batch, p_len, l_len,
              sink, scale, dq):
    n = out_prev.shape[0]
    qb = dq // LANES
    in_specs = [
        pl.BlockSpec((l_len, dq), lambda b, h: (b * (p_len // l_len), q_col // qb + h)),
        pl.BlockSpec((l_len, dq), lambda b, h: (b * (p_len // l_len), k_col // qb + h // group)),
        pl.BlockSpec((l_len, LANES), lambda b, h: (b * (p_len // l_len), v_col + h // group)),
        pl.BlockSpec(memory_space=pl.ANY),
    ]
    args = [q_arr, k_arr, v_arr, out_prev]
    has_sink = sink is not None
    if has_sink:
        in_specs = [pl.BlockSpec(memory_space=pltpu.SMEM)] + in_specs
        args = [sink] + args

    def kern(*refs):
        refs = list(refs)
        del refs[4 if has_sink else 3]
        _ctx_attn_kernel(*refs, has_sink=has_sink, scale=scale)

    return pl.pallas_call(
        kern,
        out_shape=jax.ShapeDtypeStruct(out_prev.shape, out_prev.dtype),
        grid=(batch, heads),
        in_specs=in_specs,
        out_specs=pl.BlockSpec((l_len, LANES), lambda b, h: (b * (p_len // l_len), h)),
        input_output_aliases={len(args) - 1: 0},
        compiler_params=_cparams(("arbitrary", "arbitrary")),
        name="ctx_attn_sink" if has_sink else "ctx_attn",
    )(*args)


def _swa_kernel(sink_ref, q_ref, kp_ref, kc_ref, kn_ref, vp_ref, vc_ref, vn_ref, kx_ref, vx_ref,
                cq_ref, sq_ref, cp_ref, sp_ref, cn_ref, sn_ref, init_ref, o_ref, *, nb, scale):
    del init_ref
    n = pl.program_id(1)
    h = pl.program_id(2)
    blk = SWA_BLOCK
    cq, sq = cq_ref[...], sq_ref[...]
    q = q_ref[...]
    q0 = _rope128(q[:, :LANES], cq, sq)
    q1 = _rope128(q[:, LANES:], cq, sq)
    qq = jnp.concatenate([q0, q1], axis=0).astype(BF16)
    kc = _rope128(kc_ref[...], cq, sq).astype(BF16)
    kp = _rope128(kp_ref[...], cp_ref[...], sp_ref[...]).astype(BF16)
    kn = _rope128(kn_ref[...], cn_ref[...], sn_ref[...]).astype(BF16)
    s_c = _dot_nt(qq, kc) * scale
    s_p = _dot_nt(qq, kp) * scale
    s_n = _dot_nt(qq, kn) * scale
    s_x = _dot_nt(qq, kx_ref[...].astype(BF16)) * scale
    ri = lax.broadcasted_iota(jnp.int32, (2 * blk, blk), 0) % blk
    ci = lax.broadcasted_iota(jnp.int32, (2 * blk, blk), 1)
    neg = jnp.float32(-jnp.inf)
    s_p = jnp.where((ci >= ri) & (n > 0), s_p, neg)
    s_n = jnp.where((ci <= ri) & (n < nb - 1), s_n, neg)
    rr = lax.broadcasted_iota(jnp.int32, (2 * blk, 1), 0)
    sk = jnp.where(rr < blk, sink_ref[2 * h], sink_ref[2 * h + 1]).astype(F32)
    mx = lambda a: jnp.max(a, axis=-1, keepdims=True)
    m = jnp.maximum(jnp.maximum(mx(s_c), mx(s_p)), jnp.maximum(mx(s_n), mx(s_x)))
    m = jnp.maximum(m, sk)
    p_c, p_p, p_n, p_x = (jnp.exp(a - m) for a in (s_c, s_p, s_n, s_x))
    sm = lambda a: jnp.sum(a, axis=-1, keepdims=True)
    den = sm(p_c) + sm(p_p) + sm(p_n) + sm(p_x) + jnp.exp(sk - m)
    o = (_dot(p_c.astype(BF16), vc_ref[...].astype(BF16))
         + _dot(p_p.astype(BF16), vp_ref[...].astype(BF16))
         + _dot(p_n.astype(BF16), vn_ref[...].astype(BF16))
         + _dot(p_x.astype(BF16), vx_ref[...].astype(BF16))) / den
    o_ref[:, :LANES] = o[:blk]
    o_ref[:, LANES:] = o[blk:]


def _swa(pa, sink, cos_t, sin_t, batch, t_len, l_len):
    n = pa.shape[0]
    p_len = t_len + l_len
    blk = SWA_BLOCK
    nb = t_len // blk
    bpb = p_len // blk
    off = l_len // blk
    scale = float(HEAD_DIM ** -0.5)
    kcol, vcol = 4, 6

    def rb(b, j):
        return b * bpb + off + j

    prv = lambda j: jnp.maximum(j - 1, 0)
    nxt = lambda j: jnp.minimum(j + 1, nb - 1)
    in_specs = [
        pl.BlockSpec(memory_space=pltpu.SMEM),
        pl.BlockSpec((blk, 2 * LANES), lambda b, j, h: (rb(b, j), h)),
        pl.BlockSpec((blk, LANES), lambda b, j, h: (rb(b, prv(j)), kcol + h)),
        pl.BlockSpec((blk, LANES), lambda b, j, h: (rb(b, j), kcol + h)),
        pl.BlockSpec((blk, LANES), lambda b, j, h: (rb(b, nxt(j)), kcol + h)),
        pl.BlockSpec((blk, LANES), lambda b, j, h: (rb(b, prv(j)), vcol + h)),
        pl.BlockSpec((blk, LANES), lambda b, j, h: (rb(b, j), vcol + h)),
        pl.BlockSpec((blk, LANES), lambda b, j, h: (rb(b, nxt(j)), vcol + h)),
        pl.BlockSpec((l_len, LANES), lambda b, j, h: (b * (p_len // l_len), kcol + h)),
        pl.BlockSpec((l_len, LANES), lambda b, j, h: (b * (p_len // l_len), vcol + h)),
        pl.BlockSpec((blk, LANES), lambda b, j, h: (j, 0)),
        pl.BlockSpec((blk, LANES), lambda b, j, h: (j, 0)),
        pl.BlockSpec((blk, LANES), lambda b, j, h: (prv(j), 0)),
        pl.BlockSpec((blk, LANES), lambda b, j, h: (prv(j), 0)),
        pl.BlockSpec((blk, LANES), lambda b, j, h: (nxt(j), 0)),
        pl.BlockSpec((blk, LANES), lambda b, j, h: (nxt(j), 0)),
        pl.BlockSpec(memory_space=pl.ANY),
    ]
    return pl.pallas_call(
        functools.partial(_swa_kernel, nb=nb, scale=scale),
        out_shape=jax.ShapeDtypeStruct((n, SWA_HEADS * HEAD_DIM), F32),
        grid=(batch, nb, SWA_KV_HEADS),
        in_specs=in_specs,
        out_specs=pl.BlockSpec((blk, 2 * LANES), lambda b, j, h: (rb(b, j), h)),
        input_output_aliases={len(in_specs) - 1: 0},
        compiler_params=_cparams(("arbitrary", "arbitrary", "arbitrary")),
        name="swa",
    )(sink, pa, pa, pa, pa, pa, pa, pa, pa, pa, cos_t, sin_t, cos_t, sin_t, cos_t, sin_t,
      jnp.zeros((n, SWA_HEADS * HEAD_DIM), F32))


def _chunk_index(d, c, n_l, n_t):
    bwd = jnp.where(c < n_l, n_l - 1 - c, n_l + (n_t - 1) - (c - n_l))
    return jnp.where(d == 0, c, bwd)


def _flip_iotas(d):
    row = lax.broadcasted_iota(jnp.int32, (CHUNK, CHUNK), 0)
    col = lax.broadcasted_iota(jnp.int32, (CHUNK, CHUNK), 1)
    rf = jnp.where(d == 0, row, CHUNK - 1 - row)
    cf = jnp.where(d == 0, col, CHUNK - 1 - col)
    return rf, cf


def _ret_kernel(s_ref, q_ref, k_ref, v_ref, o_ref, st_ref):
    d = pl.program_id(1)
    c = pl.program_id(2)

    @pl.when(c == 0)
    def _():
        st_ref[...] = jnp.zeros_like(st_ref)

    rf, cf = _flip_iotas(d)
    rff = rf.astype(F32)
    dn = (rf - cf).astype(F32)
    ks = float(RET_DK ** -0.5)
    for h in range(RET_HEADS):
        hs = slice(h * LANES, (h + 1) * LANES)
        sv = jnp.full((CHUNK, CHUNK), s_ref[d * RET_HEADS + h], F32)
        lg = jnp.log1p(-jnp.exp2(-sv))
        dec = jnp.where(dn >= 0, jnp.exp(dn * lg), 0.0)
        eq = jnp.exp((rff + 1.0) * lg)
        ek = jnp.exp((CHUNK - 1.0 - rff) * lg)
        q = q_ref[:, hs]
        k = k_ref[:, hs] * ks
        v = v_ref[:, hs].astype(BF16)
        st = st_ref[h]
        a = _dot_nt(q.astype(BF16), k.astype(BF16)) * dec
        o = _dot(a.astype(BF16), v) + _dot_nt((q * eq).astype(BF16), st.astype(BF16))
        o_ref[0, :, hs] = o
        st_ref[h] = jnp.exp(float(CHUNK) * lg) * st + _dot_tn(v, (k * ek).astype(BF16))


def _hgrn_level_mats():
    c = CHUNK
    t = np.arange(c)[:, None]
    u = np.arange(c)[None, :]
    mats = [u <= t, u > t]
    m = c // 2
    while m >= 1:
        mid = (t // (2 * m)) * 2 * m + m
        second = (t % (2 * m)) >= m
        qrole = (u >= mid) & (u <= t)
        krole = (u > t) & (u <= mid - 1)
        mats.append(np.where(second, qrole, krole))
        m //= 2
    fwd = np.concatenate(mats, 0).astype(np.float32)
    bwd = np.concatenate([mm[::-1, ::-1] for mm in mats], 0).astype(np.float32)
    return np.stack([fwd, bwd], 0)


N_LEVELS = int(math.log2(CHUNK))


def _hgrn_kernel(m_ref, lb_ref, q_ref, f_ref, v_ref, o_ref, st_ref):
    d = pl.program_id(1)
    c = pl.program_id(2)

    @pl.when(c == 0)
    def _():
        st_ref[...] = jnp.zeros_like(st_ref)

    rf, cf = _flip_iotas(d)
    mst = m_ref[0]
    for h in range(HGRN_HEADS):
        hs = slice(h * LANES, (h + 1) * LANES)
        lb = lb_ref[0, :, hs]
        f = lb + (1.0 - lb) * (1.0 / (1.0 + jnp.exp(-f_ref[:, hs])))
        k = 1.0 - f
        g = jnp.log(f)
        q = _silu(q_ref[:, hs])
        v = v_ref[:, hs].astype(BF16)
        g_hi = g.astype(BF16)
        g_lo = (g - g_hi.astype(F32)).astype(BF16)
        x2 = _dot(mst, jnp.concatenate([g_hi, g_lo], axis=1))
        e = jnp.exp(x2[:, :LANES] + x2[:, LANES:])
        e_q = e[0:CHUNK]
        q_in = q * e_q
        k_st = k * e[CHUNK:2 * CHUNK]
        scores = jnp.where(rf == cf, _dot_nt(q.astype(BF16), k.astype(BF16)), 0.0)
        for lvl in range(N_LEVELS):
            m = CHUNK >> (lvl + 1)
            sh = N_LEVELS - lvl
            el = e[(2 + lvl) * CHUNK:(3 + lvl) * CHUNK]
            second = (rf & m) != 0
            ql = jnp.where(second, q * el, 0.0).astype(BF16)
            kl = jnp.where(second, 0.0, k * el).astype(BF16)
            sl = _dot_nt(ql, kl)
            scores = scores + jnp.where((rf >> sh) == (cf >> sh), sl, 0.0)
        st = st_ref[h]
        o = _dot(scores.astype(BF16), v) + _dot_nt(q_in.astype(BF16), st.astype(BF16))
        o_ref[0, :, hs] = o
        a_last = jnp.where(d == 0, e_q[CHUNK - 1:CHUNK, :], e_q[0:1, :])
        st_ref[h] = st * a_last + _dot_tn(v, k_st.astype(BF16))


def _scan_call(kernel, name, extra_args, extra_specs, arrs, cols, batch, t_len, l_len):
    n = arrs[0].shape[0]
    p_len = t_len + l_len
    n_l, n_t = l_len // CHUNK, t_len // CHUNK
    cpb = p_len // CHUNK

    def rowblk(b, d, c):
        return b * cpb + _chunk_index(d, c, n_l, n_t)

    def col_map(col):
        return lambda b, d, c: (rowblk(b, d, c), col(d) if callable(col) else col)

    in_specs = list(extra_specs) + [pl.BlockSpec((CHUNK, WIDE), col_map(col)) for col in cols]
    return pl.pallas_call(
        kernel,
        out_shape=jax.ShapeDtypeStruct((2, n, WIDE), F32),
        grid=(batch, 2, n_l + n_t),
        in_specs=in_specs,
        out_specs=pl.BlockSpec((1, CHUNK, WIDE), lambda b, d, c: (d, rowblk(b, d, c), 0)),
        scratch_shapes=[pltpu.VMEM((4, CHUNK, CHUNK), F32)],
        compiler_params=_cparams(("arbitrary", "arbitrary", "arbitrary")),
        name=name,
    )(*extra_args, *arrs)


def _retention(pb, ret_s, batch, t_len, l_len):
    return _scan_call(_ret_kernel, "retention_scan",
                      [ret_s.reshape(-1)], [pl.BlockSpec(memory_space=pltpu.SMEM)],
                      [pb, pb, pb], [0, 1, 2], batch, t_len, l_len)


def _hgrn(pd, lower_bounds, mats, batch, t_len, l_len):
    lb3 = lower_bounds.reshape(2, 1, WIDE)
    specs = [pl.BlockSpec((1,) + mats.shape[1:], lambda b, d, c: (d, 0, 0)),
             pl.BlockSpec((1, 1, WIDE), lambda b, d, c: (d, 0, 0))]
    return _scan_call(_hgrn_kernel, "hgrn_scan", [mats, lb3], specs,
                      [pd, pd, pd], [0, lambda d: 1 + d, 3], batch, t_len, l_len)


R_E1, R_E2, R_G1, R_G2, R_S1, R_S2 = 0, 1, 2, 3, 4, 5


def _route(logits, count_ref, active):
    tm = logits.shape[0]
    lane = lax.broadcasted_iota(jnp.int32, logits.shape, 1)
    lanef = lane.astype(F32)
    big = jnp.float32(1e9)
    neg = jnp.float32(-jnp.inf)
    mx = lambda a: jnp.max(a, axis=-1, keepdims=True)
    mn = lambda a: jnp.min(a, axis=-1, keepdims=True)
    sm = lambda a: jnp.sum(a, axis=-1, keepdims=True)
    gl = jnp.where(lane < N_GROUPS, logits, neg)
    gm = mx(gl)
    p_grp = 1.0 / sm(jnp.exp(gl - gm))
    grp = mn(jnp.where(gl == gm, lanef, big))
    lo = N_GROUPS + grp * EXPERTS_PER_GROUP
    ing = (lanef >= lo) & (lanef < lo + EXPERTS_PER_GROUP)
    el = jnp.where(ing, logits, neg)
    ee = jnp.exp(el - mx(el))
    p = ee / sm(ee)
    pm = jnp.where(ing, p, -1.0)
    p1 = mx(pm)
    i1 = mn(jnp.where(pm == p1, lanef, big))
    pm2 = jnp.where(lanef == i1, -1.0, pm)
    p2 = mx(pm2)
    i2 = mn(jnp.where(pm2 == p2, lanef, big))
    den = p1 + p2
    g1 = p_grp * p1 / den
    g2 = p_grp * p2 / den
    e1 = i1 - N_GROUPS
    e2 = i2 - N_GROUPS
    oh1 = lanef == e1
    oh2 = lanef == e2
    both = jnp.where(oh1 | oh2, active, 0.0)
    ri = lax.broadcasted_iota(jnp.int32, (tm, tm), 0)
    ci = lax.broadcasted_iota(jnp.int32, (tm, tm), 1)
    earlier = jnp.where(ci < ri, 1.0, 0.0).astype(BF16)
    before = _dot(earlier, both.astype(BF16)) + count_ref[0:1, :]
    s1 = sm(jnp.where(oh1, before, 0.0))
    s2 = sm(jnp.where(oh2, before, 0.0))
    count_ref[...] = count_ref[...] + jnp.sum(both, axis=0, keepdims=True)
    out = jnp.zeros(logits.shape, F32)
    for ln, val in ((R_E1, e1), (R_E2, e2), (R_G1, g1), (R_G2, g2), (R_S1, s1), (R_S2, s2)):
        out = jnp.where(lane == ln, val, out)
    return out


def _out_kernel(x_ref, a_ref, b_ref, c_ref, d_ref, rg_ref, hg_ref, g1_ref, sh2_ref, sc2_ref,
                lng_ref, lnb_ref, rgn_ref, hgn_ref, wo_ref, wr_ref,
                xo_ref, h2_ref, r_ref, cnt_ref, *, alpha, tiles_per_batch, route_ctx):
    i = pl.program_id(0)

    @pl.when(i == 0)
    def _():
        cnt_ref[...] = jnp.zeros_like(cnt_ref)

    bsum = b_ref[0] + b_ref[1]
    dsum = d_ref[0] + d_ref[1]
    rgn = rgn_ref[...]
    hgn = hgn_ref[...]
    parts_b, parts_d = [], []
    for h in range(4):
        hs = slice(h * LANES, (h + 1) * LANES)
        parts_b.append(_ln_rows(bsum[:, hs]))
        dh = dsum[:, hs]
        parts_d.append(dh * lax.rsqrt(jnp.mean(dh * dh, axis=-1, keepdims=True) + EPS))
    bo = _silu(rg_ref[...]) * (jnp.concatenate(parts_b, axis=1) * rgn)
    do = _silu(hg_ref[...]) * (jnp.concatenate(parts_d, axis=1) * hgn)
    mix = jnp.concatenate([a_ref[...], bo, c_ref[...], do], axis=1).astype(BF16)
    o = _dot(mix, wo_ref[...])
    y = alpha * x_ref[...] + g1_ref[0] * o
    xn = _ln_rows(y) * lng_ref[...] + lnb_ref[...]
    xo_ref[...] = xn
    h2 = _ln_rows(xn) * (1.0 + sc2_ref[0]) + sh2_ref[0]
    h2_ref[...] = h2
    if route_ctx:
        active = jnp.float32(1.0)
    else:
        active = jnp.where(i % tiles_per_batch == 0, 0.0, 1.0).astype(F32)
    r_ref[...] = _route(_dot(h2.astype(BF16), wr_ref[...]), cnt_ref, active)


def _out_proj(stream, a, b2, c, d2, pb, pd, g1, sh2, sc2, ln_g, ln_b, ret_gn, hgrn_gn, w_out, w_r,
              tiles_per_batch, alpha, route_ctx):
    n, dm = stream.shape
    tm = ROW_TILE
    gmap = lambda i: (_group_of_tile(i, tiles_per_batch), 0, 0)
    row = lambda w: pl.BlockSpec((tm, w), lambda i: (i, 0))
    row2 = pl.BlockSpec((2, tm, WIDE), lambda i: (0, i, 0))
    const = lambda w: pl.BlockSpec((1, w), lambda i: (0, 0))
    resident = pl.BlockSpec(memory_space=pltpu.VMEM)
    return pl.pallas_call(
        functools.partial(_out_kernel, alpha=alpha, tiles_per_batch=tiles_per_batch, route_ctx=route_ctx),
        out_shape=[jax.ShapeDtypeStruct((n, dm), F32), jax.ShapeDtypeStruct((n, dm), F32),
                   jax.ShapeDtypeStruct((n, LANES), F32), jax.ShapeDtypeStruct((SUBLANES, LANES), F32)],
        grid=(n // tm,),
        in_specs=[row(dm), row(WIDE), row2, row(WIDE), row2,
                  pl.BlockSpec((tm, WIDE), lambda i: (i, 3)),
                  pl.BlockSpec((tm, WIDE), lambda i: (i, 4)),
                  pl.BlockSpec((1, 1, dm), gmap), pl.BlockSpec((1, 1, dm), gmap),
                  pl.BlockSpec((1, 1, dm), gmap),
                  const(dm), const(dm), const(WIDE), const(WIDE), resident, resident],
        out_specs=[row(dm), row(dm), row(LANES),
                   pl.BlockSpec((SUBLANES, LANES), lambda i: (0, 0))],
        compiler_params=_cparams(("arbitrary",)),
        name="mix_outproj_norm_route",
    )(stream, a, b2, c, d2, pb, pd, g1, sh2, sc2, ln_g, ln_b, ret_gn, hgrn_gn, w_out, w_r)


def _route_plan(route, counts_f, n_blocks):
    rows = MOE_ROWS
    counts = counts_f[0, :N_EXPERTS].astype(jnp.int32)
    padded = (counts + rows - 1) // rows * rows
    pend = jnp.cumsum(padded)
    pstart = pend - padded
    eid = route[:, R_E1:R_E2 + 1].astype(jnp.int32)
    rank = route[:, R_S1:R_S2 + 1].astype(jnp.int32)
    onehot = eid[:, :, None] == jnp.arange(N_EXPERTS, dtype=jnp.int32)
    pos = jnp.sum(jnp.where(onehot, pstart, 0), axis=-1) + rank
    blk_start = jnp.arange(n_blocks, dtype=jnp.int32) * rows
    blk_e = jnp.minimum(jnp.sum(blk_start[:, None] >= pend[None, :], axis=-1), N_EXPERTS - 1)
    blk_valid = (blk_start < pend[-1]).astype(jnp.int32)
    pad_start = pstart + counts
    return pos.reshape(-1).astype(jnp.int32), blk_e.astype(jnp.int32), blk_valid, pad_start.astype(jnp.int32)


def _dispatch_kernel(pos_ref, pad_ref, bv_ref, h_ref, x_hbm, zbuf, sem, *, tile_of_step, n_blocks):
    i = pl.program_id(0)
    tm = ROW_TILE
    rows = MOE_ROWS

    def zero_copy(e):
        start = pl.multiple_of((pad_ref[e] // SUBLANES) * SUBLANES, SUBLANES)
        return pltpu.make_async_copy(zbuf, x_hbm.at[pl.ds(start, rows + SUBLANES)], sem.at[1])

    def unused_block_copy(j):
        start = pl.multiple_of(j * rows, rows)
        return pltpu.make_async_copy(zbuf.at[pl.ds(0, rows)], x_hbm.at[pl.ds(start, rows)], sem.at[1])

    @pl.when(i == 0)
    def _():
        zbuf[...] = jnp.zeros_like(zbuf)

        def fill(j, carry):
            @pl.when(bv_ref[j] == 0)
            def _():
                unused_block_copy(j).start()
                unused_block_copy(j).wait()
            return carry

        lax.fori_loop(0, n_blocks + 2, fill, 0)
        for e in range(N_EXPERTS):
            zero_copy(e).start()
        for e in range(N_EXPERTS):
            zero_copy(e).wait()

    def row_copy(r, k):
        p = pos_ref[(tile_of_step(i) * tm + r) * TOP_K + k]
        return pltpu.make_async_copy(h_ref.at[r], x_hbm.at[p], sem.at[0])

    for r in range(tm):
        for k in range(TOP_K):
            row_copy(r, k).start()
    for r in range(tm):
        for k in range(TOP_K):
            row_copy(r, k).wait()


def _tile_schedule(n, tiles_per_batch, latent_only, batch):
    if latent_only:
        lat = tiles_per_batch - 1
        return batch * lat, (lambda s: (s // lat) * tiles_per_batch + 1 + s % lat)
    return n // ROW_TILE, (lambda s: s)


def _dispatch(h2, pos, pad_start, blk_valid, tiles_per_batch, latent_only, batch):
    n, dm = h2.shape
    tm = ROW_TILE
    n_blocks = blk_valid.shape[0]
    blk_valid = jnp.concatenate([blk_valid, jnp.zeros((2,), jnp.int32)])
    n_tiles, tile_of_step = _tile_schedule(n, tiles_per_batch, latent_only, batch)
    grid_spec = pltpu.PrefetchScalarGridSpec(
        num_scalar_prefetch=3,
        grid=(n_tiles,),
        in_specs=[pl.BlockSpec((tm, dm), lambda i, p, z, v: (tile_of_step(i), 0))],
        out_specs=pl.BlockSpec(memory_space=pl.ANY),
        scratch_shapes=[pltpu.VMEM((MOE_ROWS + SUBLANES, dm), F32), pltpu.SemaphoreType.DMA((2,))],
    )
    return pl.pallas_call(
        functools.partial(_dispatch_kernel, tile_of_step=tile_of_step, n_blocks=n_blocks),
        out_shape=jax.ShapeDtypeStruct(((n_blocks + 2) * MOE_ROWS, dm), F32),
        grid_spec=grid_spec,
        compiler_params=_cparams(("arbitrary",)),
        name="moe_dispatch",
    )(pos, pad_start, blk_valid, h2)


def _moe_kernel(be_ref, bv_ref, x_ref, w1_ref, w3_ref, w2_ref, y_ref, w1b, w3b, w2b):
    j = pl.program_id(0)
    first = (j == 0) | (be_ref[j] != be_ref[jnp.maximum(j - 1, 0)])

    @pl.when(first & (bv_ref[j] == 1))
    def _():
        w1b[...] = w1_ref[0, 0].astype(BF16)
        w3b[...] = w3_ref[0, 0].astype(BF16)
        w2b[...] = w2_ref[0, 0].astype(BF16)

    @pl.when(bv_ref[j] == 1)
    def _():
        x = x_ref[...].astype(BF16)
        hmid = _silu(_dot(x, w1b[...])) * _dot(x, w3b[...])
        y_ref[...] = _dot(hmid.astype(BF16), w2b[...])

    @pl.when(bv_ref[j] == 0)
    def _():
        y_ref[...] = jnp.zeros_like(y_ref)


def _moe_experts(xbuf, blk_e, blk_valid, w1, w3, w2, layer):
    dm = xbuf.shape[1]
    ff = w1.shape[-1]
    rows = MOE_ROWS
    n_blocks = blk_e.shape[0]
    wmap = lambda j, be, bv: (layer, be[j], 0, 0)
    grid_spec = pltpu.PrefetchScalarGridSpec(
        num_scalar_prefetch=2,
        grid=(n_blocks,),
        in_specs=[
            pl.BlockSpec((rows, dm), lambda j, be, bv: (j * bv[j], 0)),
            pl.BlockSpec((1, 1, dm, ff), wmap),
            pl.BlockSpec((1, 1, dm, ff), wmap),
            pl.BlockSpec((1, 1, ff, dm), wmap),
        ],
        out_specs=pl.BlockSpec((rows, dm), lambda j, be, bv: (j, 0)),
        scratch_shapes=[pltpu.VMEM((dm, ff), BF16), pltpu.VMEM((dm, ff), BF16), pltpu.VMEM((ff, dm), BF16)],
    )
    return pl.pallas_call(
        _moe_kernel,
        out_shape=jax.ShapeDtypeStruct((n_blocks * rows, dm), F32),
        grid_spec=grid_spec,
        compiler_params=_cparams(("arbitrary",)),
        name="moe_experts",
    )(blk_e, blk_valid, xbuf, w1, w3, w2)


def _combine_kernel(pos_ref, x_ref, r_ref, g2_ref, lng_ref, lnb_ref, y_hbm, o_ref, ybuf, sem,
                    *, n_tiles, tile_of_step, alpha):
    i = pl.program_id(0)
    slot = i % 2
    tm = ROW_TILE

    def row_copy(step, sl, r, k):
        p = pos_ref[(tile_of_step(step) * tm + r) * TOP_K + k]
        return pltpu.make_async_copy(y_hbm.at[p], ybuf.at[sl, k, r], sem.at[sl])

    def start_gather(step, sl):
        for r in range(tm):
            for k in range(TOP_K):
                row_copy(step, sl, r, k).start()

    @pl.when(i == 0)
    def _():
        start_gather(0, 0)

    @pl.when(i + 1 < n_tiles)
    def _():
        start_gather(jnp.minimum(i + 1, n_tiles - 1), 1 - slot)

    for r in range(tm):
        for k in range(TOP_K):
            row_copy(i, slot, r, k).wait()
    route = r_ref[...]
    y = ybuf[slot, 0] * route[:, R_G1:R_G1 + 1] + ybuf[slot, 1] * route[:, R_G2:R_G2 + 1]
    z = alpha * x_ref[...] + g2_ref[0] * y
    o_ref[...] = _ln_rows(z) * lng_ref[...] + lnb_ref[...]


def _combine(x_new, route, ybuf, pos, g2, ln_g, ln_b, tiles_per_batch, alpha, latent_only, batch):
    n, dm = x_new.shape
    tm = ROW_TILE
    n_tiles, tile_of_step = _tile_schedule(n, tiles_per_batch, latent_only, batch)
    gmap = lambda i, p: (_group_of_tile(tile_of_step(i), tiles_per_batch), 0, 0)
    grid_spec = pltpu.PrefetchScalarGridSpec(
        num_scalar_prefetch=1,
        grid=(n_tiles,),
        in_specs=[
            pl.BlockSpec((tm, dm), lambda i, p: (tile_of_step(i), 0)),
            pl.BlockSpec((tm, LANES), lambda i, p: (tile_of_step(i), 0)),
            pl.BlockSpec((1, 1, dm), gmap),
            pl.BlockSpec((1, dm), lambda i, p: (0, 0)),
            pl.BlockSpec((1, dm), lambda i, p: (0, 0)),
            pl.BlockSpec(memory_space=pl.ANY),
        ],
        out_specs=pl.BlockSpec((tm, dm), lambda i, p: (i, 0)),
        scratch_shapes=[pltpu.VMEM((2, TOP_K, tm, dm), F32), pltpu.SemaphoreType.DMA((2,))],
    )
    return pl.pallas_call(
        functools.partial(_combine_kernel, n_tiles=n_tiles, tile_of_step=tile_of_step, alpha=alpha),
        out_shape=jax.ShapeDtypeStruct((n_tiles * tm, dm), F32),
        grid_spec=grid_spec,
        compiler_params=_cparams(("arbitrary",)),
        name="moe_combine_norm",
    )(pos, x_new, route, g2, ln_g, ln_b, ybuf)


def _pad_cols(w, width):
    return jnp.pad(w, ((0, 0), (0, width - w.shape[1])))


def _split_w_in(w_in_l):
    wa = w_in_l[:, :SEG_A]
    wb = w_in_l[:, SEG_A:SEG_A + SEG_B]
    c0 = SEG_A + SEG_B
    c_true = MLA_Q_RANK + MLA_KV_RANK + MLA_ROPE
    wc = _pad_cols(w_in_l[:, c0:c0 + c_true], SEG_C)
    wd = w_in_l[:, c0 + c_true:]
    return [w.astype(BF16) for w in (wa, wb, wc, wd)]


def kernel(x, c, ctx, c_ctx, w_ada, b_ada, w_in, swa_sink, ret_decay_exp, ret_gn, mla_q_norm, mla_kv_norm,
           mla_w_uq, mla_w_ukv, hgrn_lb_logits, hgrn_gn, w_out, ln1_g, ln1_b, router_group, router_expert,
           moe_w1, moe_w3, moe_w2, ln2_g, ln2_b):
    batch, t_len, dm = x.shape
    l_len = ctx.shape[1]
    depth = w_in.shape[0]
    p_len = t_len + l_len
    n = batch * p_len
    assert batch == 2 and l_len == ROW_TILE and t_len % 256 == 0 and t_len % GRID_W == 0
    tiles_per_batch = p_len // ROW_TILE
    alpha = float((2 * depth) ** 0.25)

    cos_swa, sin_swa, cos_rows, sin_rows = _rope_tables(t_len, l_len, batch)
    mats = jnp.asarray(_hgrn_level_mats(), BF16)

    cvec = jnp.zeros((SUBLANES, dm), F32).at[0:batch].set(c).at[batch].set(c_ctx)
    mod = _ada(cvec, w_ada, b_ada)

    lbp = jax.nn.softmax(hgrn_lb_logits.astype(F32), axis=0)
    lower_bounds = jnp.cumsum(lbp, axis=0) - lbp[0]

    stream = jnp.concatenate([ctx, x], axis=1).reshape(n, dm)

    out = None
    for l in range(depth):
        need_ctx = l < depth - 1
        m6 = mod[l, :3].reshape(3, 6, dm)
        sh1, sc1, g1, sh2, sc2, g2 = (m6[:, k].reshape(3, 1, dm) for k in range(6))

        pa, pb, pc, pd = _proj(stream, sh1, sc1, _split_w_in(w_in[l]), tiles_per_batch)

        wq = mla_w_uq[l].reshape(MLA_Q_RANK, MLA_HEADS, MLA_NOPE + MLA_ROPE)
        wq = jnp.pad(wq, ((0, 0), (0, 0), (0, 2 * LANES - (MLA_NOPE + MLA_ROPE))))
        wq = wq.reshape(MLA_Q_RANK, MLA_HEADS * 2 * LANES).astype(BF16)
        wkv = mla_w_ukv[l].reshape(MLA_KV_RANK, MLA_HEADS, MLA_NOPE + MLA_V)
        wk = wkv[:, :, :MLA_NOPE].reshape(MLA_KV_RANK, -1).astype(BF16)
        wv = wkv[:, :, MLA_NOPE:].reshape(MLA_KV_RANK, -1).astype(BF16)
        qf, kf, vf = _mla_up(pc, mla_q_norm[l].reshape(1, -1), mla_kv_norm[l].reshape(1, -1),
                             wq, wk, wv, cos_rows, sin_rows)

        a = _swa(pa, swa_sink[l], cos_swa, sin_swa, batch, t_len, l_len)
        cc = _mla_attn(qf, kf, vf, batch, t_len, l_len)
        if need_ctx:
            a = _ctx_attn(a, pa, pa, pa, 0, 4, 6, SWA_HEADS, SWA_HEADS // SWA_KV_HEADS, batch, p_len,
                          l_len, swa_sink[l], float(HEAD_DIM ** -0.5), LANES)
            cc = _ctx_attn(cc, qf, kf, vf, 0, 0, 0, MLA_HEADS, 1, batch, p_len, l_len, None, 1.0,
                           2 * LANES)
        b2 = _retention(pb, ret_decay_exp[l], batch, t_len, l_len)
        d2 = _hgrn(pd, lower_bounds[l], mats, batch, t_len, l_len)

        w_r = _pad_cols(jnp.concatenate([router_group[l], router_expert[l]], axis=1), LANES).astype(BF16)
        x_new, h2, route, counts = _out_proj(
            stream, a, b2, cc, d2, pb, pd, g1, sh2, sc2,
            ln1_g[l].reshape(1, -1), ln1_b[l].reshape(1, -1),
            ret_gn[l].reshape(1, -1), hgrn_gn[l].reshape(1, -1),
            w_out[l].astype(BF16), w_r, tiles_per_batch, alpha, need_ctx)

        n_active = n if need_ctx else batch * t_len
        n_blocks = (n_active * TOP_K + N_EXPERTS * (MOE_ROWS - 1) + MOE_ROWS - 1) // MOE_ROWS
        pos, blk_e, blk_valid, pad_start = _route_plan(route, counts, n_blocks)
        xbuf = _dispatch(h2, pos, pad_start, blk_valid, tiles_per_batch, not need_ctx, batch)
        ybuf = _moe_experts(xbuf, blk_e, blk_valid, moe_w1, moe_w3, moe_w2, l)
        res = _combine(x_new, route, ybuf, pos, g2, ln2_g[l].reshape(1, -1), ln2_b[l].reshape(1, -1),
                       tiles_per_batch, alpha, not need_ctx, batch)
        if need_ctx:
            stream = res
        else:
            out = res.reshape(batch, t_len, dm)
    return out
```

```python
import functools
import math

import numpy as np
import jax
import jax.numpy as jnp
from jax import lax
from jax.experimental import pallas as pl
from jax.experimental.pallas import tpu as pltpu

F32 = jnp.float32
BF16 = jnp.bfloat16

GRID_W = 64
HEAD_DIM = 128
SWA_HEADS = 4
SWA_KV_HEADS = 2
SWA_BLOCK = 128
RET_HEADS = 4
RET_DK = 128
MLA_HEADS = 4
MLA_Q_RANK = 384
MLA_KV_RANK = 128
MLA_NOPE = 128
MLA_ROPE = 64
MLA_V = 128
HGRN_HEADS = 4
N_GROUPS = 4
EXPERTS_PER_GROUP = 8
N_EXPERTS = N_GROUPS * EXPERTS_PER_GROUP
TOP_K = 2
ROPE_BASE = 10000.0
EPS = 1e-6

LANES = 128
SUBLANES = 8
VMEM_LIMIT_BYTES = 56 * 1024 * 1024

CHUNK = 128
ROW_TILE = 256
MOE_ROWS = 256
MLA_KEY_CHUNK = 256
MLA_HEADS_PER_STEP = 2
WIDE = 512

SEG_A = 1024
SEG_B = 2048
SEG_C = 640
SEG_D = 2560


def _cparams(sem, vmem=VMEM_LIMIT_BYTES):
    return pltpu.CompilerParams(dimension_semantics=sem, vmem_limit_bytes=vmem)


def _ln_rows(x):
    mu = jnp.mean(x, axis=-1, keepdims=True)
    xc = x - mu
    var = jnp.mean(xc * xc, axis=-1, keepdims=True)
    return xc * lax.rsqrt(var + EPS)


def _silu(x):
    return x * (1.0 / (1.0 + jnp.exp(-x)))


def _dot(a, b):
    return jnp.dot(a, b, preferred_element_type=F32)


def _dot_nt(a, b):
    return lax.dot_general(a, b, (((1,), (1,)), ((), ())), preferred_element_type=F32)


def _dot_tn(a, b):
    return lax.dot_general(a, b, (((0,), (0,)), ((), ())), preferred_element_type=F32)


def _ada_kernel(c_ref, w_ref, b_ref, o_ref):
    s = _silu(c_ref[...]).astype(BF16)
    o_ref[0] = _dot(s, w_ref[0].astype(BF16)) + b_ref[0]


def _ada(cvec, w_ada, b_ada):
    depth, d, n6 = w_ada.shape
    tn = n6 // 8
    return pl.pallas_call(
        _ada_kernel,
        out_shape=jax.ShapeDtypeStruct((depth, SUBLANES, n6), F32),
        grid=(depth, n6 // tn),
        in_specs=[
            pl.BlockSpec((SUBLANES, d), lambda l, j: (0, 0)),
            pl.BlockSpec((1, d, tn), lambda l, j: (l, 0, j)),
            pl.BlockSpec((1, 1, tn), lambda l, j: (l, 0, j)),
        ],
        out_specs=pl.BlockSpec((1, SUBLANES, tn), lambda l, j: (l, 0, j)),
        compiler_params=_cparams(("arbitrary", "arbitrary")),
        name="ada_mod",
    )(cvec, w_ada, b_ada.reshape(depth, 1, n6))


def _proj_kernel(x_ref, sh_ref, sc_ref, wa_ref, wb_ref, wc_ref, wd_ref, oa, ob, oc, od):
    y = _ln_rows(x_ref[...])
    h = (y * (1.0 + sc_ref[0]) + sh_ref[0]).astype(BF16)
    for w_ref, o_ref in ((wa_ref, oa), (wb_ref, ob), (wc_ref, oc), (wd_ref, od)):
        width = o_ref.shape[1]
        for j in range(0, width, WIDE):
            cw = min(WIDE, width - j)
            o_ref[:, j:j + cw] = _dot(h, w_ref[:, j:j + cw])


def _group_of_tile(i, tiles_per_batch):
    return jnp.where(i % tiles_per_batch == 0, 2, i // tiles_per_batch)


def _proj(stream, shift, scale, w_segs, tiles_per_batch):
    n, d = stream.shape
    tm = ROW_TILE
    gmap = lambda i: (_group_of_tile(i, tiles_per_batch), 0, 0)
    resident = pl.BlockSpec(memory_space=pltpu.VMEM)
    widths = [w.shape[1] for w in w_segs]
    return pl.pallas_call(
        _proj_kernel,
        out_shape=[jax.ShapeDtypeStruct((n, w), F32) for w in widths],
        grid=(n // tm,),
        in_specs=[
            pl.BlockSpec((tm, d), lambda i: (i, 0)),
            pl.BlockSpec((1, 1, d), gmap),
            pl.BlockSpec((1, 1, d), gmap),
            resident, resident, resident, resident,
        ],
        out_specs=[pl.BlockSpec((tm, w), lambda i: (i, 0)) for w in widths],
        compiler_params=_cparams(("arbitrary",)),
        name="ln_mod_inproj",
    )(stream, shift, scale, *w_segs)


def _rope128(x, cos, sin):
    return x * cos + pltpu.roll(x, 64, 1) * sin


def _rope64(x, cos, sin):
    lane = lax.broadcasted_iota(jnp.int32, x.shape, 1)
    rot = jnp.where((lane % 64) < 32, pltpu.roll(x, 96, 1), pltpu.roll(x, 32, 1))
    return x * cos + rot * sin


def _rope_tables(t_len, l_len, batch):
    rows = t_len // GRID_W
    row = np.repeat(np.arange(rows), GRID_W).astype(np.float32)
    col = np.tile(np.arange(GRID_W), rows).astype(np.float32)

    def angles(rot_dim):
        n_freq = rot_dim // 4
        inv = (ROPE_BASE ** (-np.arange(n_freq, dtype=np.float32) / n_freq)).astype(np.float32)
        return np.concatenate([row[:, None] * inv, col[:, None] * inv], -1).astype(np.float32)

    a_swa = angles(HEAD_DIM)
    cos_swa = np.concatenate([np.cos(a_swa), np.cos(a_swa)], -1)
    sin_swa = np.concatenate([-np.sin(a_swa), np.sin(a_swa)], -1)
    a_mla = angles(MLA_ROPE)
    cos_m = np.concatenate([np.cos(a_mla), np.cos(a_mla), np.ones((t_len, 64), np.float32)], -1)
    sin_m = np.concatenate([-np.sin(a_mla), np.sin(a_mla), np.zeros((t_len, 64), np.float32)], -1)
    ones = np.ones((l_len, LANES), np.float32)
    zeros = np.zeros((l_len, LANES), np.float32)
    cos_rows = np.concatenate([np.concatenate([ones, cos_m], 0)] * batch, 0)
    sin_rows = np.concatenate([np.concatenate([zeros, sin_m], 0)] * batch, 0)
    return (jnp.asarray(cos_swa, F32), jnp.asarray(sin_swa, F32),
            jnp.asarray(cos_rows, F32), jnp.asarray(sin_rows, F32))


def _mla_up_kernel(pc_ref, qn_ref, kvn_ref, wq_ref, wk_ref, wv_ref, cos_ref, sin_ref,
                   q_out, k_out, v_out, *, scale):
    pc = pc_ref[...]
    cq = pc[:, :MLA_Q_RANK]
    ckv = pc[:, MLA_Q_RANK:MLA_Q_RANK + MLA_KV_RANK]
    kr = pc[:, MLA_Q_RANK + MLA_KV_RANK:]
    cos = cos_ref[...]
    sin = sin_ref[...]

    def rms(x, g):
        return x * lax.rsqrt(jnp.mean(x * x, axis=-1, keepdims=True) + EPS) * g

    qh = _dot(rms(cq, qn_ref[...]).astype(BF16), wq_ref[...])
    ckn = rms(ckv, kvn_ref[...]).astype(BF16)
    kh = _dot(ckn, wk_ref[...])
    v_out[...] = _dot_nt(wv_ref[...], ckn).astype(BF16)
    kr_rot = _rope64(kr, cos, sin).astype(BF16)
    for h in range(MLA_HEADS):
        base = h * 2 * LANES
        q_out[:, base:base + LANES] = (qh[:, base:base + LANES] * scale).astype(BF16)
        q_out[:, base + LANES:base + 2 * LANES] = (
            _rope64(qh[:, base + LANES:base + 2 * LANES], cos, sin) * scale).astype(BF16)
        k_out[:, base:base + LANES] = kh[:, h * LANES:(h + 1) * LANES].astype(BF16)
        k_out[:, base + LANES:base + 2 * LANES] = kr_rot


def _mla_up(pc, q_norm, kv_norm, wq, wk, wv, cos_rows, sin_rows):
    n = pc.shape[0]
    tm = ROW_TILE
    scale = float((MLA_NOPE + MLA_ROPE) ** -0.5 * math.log2(math.e))
    const2 = lambda i: (0, 0)
    return pl.pallas_call(
        functools.partial(_mla_up_kernel, scale=scale),
        out_shape=[jax.ShapeDtypeStruct((n, MLA_HEADS * 2 * LANES), BF16),
                   jax.ShapeDtypeStruct((n, MLA_HEADS * 2 * LANES), BF16),
                   jax.ShapeDtypeStruct((MLA_HEADS * MLA_V, n), BF16)],
        grid=(n // tm,),
        in_specs=[
            pl.BlockSpec((tm, SEG_C), lambda i: (i, 0)),
            pl.BlockSpec((1, MLA_Q_RANK), const2),
            pl.BlockSpec((1, MLA_KV_RANK), const2),
            pl.BlockSpec(wq.shape, const2),
            pl.BlockSpec(wk.shape, const2),
            pl.BlockSpec(wv.shape, const2),
            pl.BlockSpec((tm, LANES), lambda i: (i, 0)),
            pl.BlockSpec((tm, LANES), lambda i: (i, 0)),
        ],
        out_specs=[pl.BlockSpec((tm, MLA_HEADS * 2 * LANES), lambda i: (i, 0)),
                   pl.BlockSpec((tm, MLA_HEADS * 2 * LANES), lambda i: (i, 0)),
                   pl.BlockSpec((MLA_HEADS * MLA_V, tm), lambda i: (0, i))],
        compiler_params=_cparams(("arbitrary",)),
        name="mla_up",
    )(pc, q_norm, kv_norm, wq, wk, wv, cos_rows, sin_rows)


def _mla_attn_kernel(q_ref, k_ref, vt_ref, init_ref, o_ref, s_ref, p_ref, *, kc):
    del init_ref
    p_len = k_ref.shape[0]
    tq = q_ref.shape[0]
    dqk = 2 * LANES
    heads = q_ref.shape[1] // dqk
    half = p_len // 2
    for h in range(heads):
        q = q_ref[:, h * dqk:(h + 1) * dqk]
        s_ref[h, 0:half, :] = _dot_nt(k_ref[0:half, h * dqk:(h + 1) * dqk], q)
        s_ref[h, half:p_len, :] = _dot_nt(k_ref[half:p_len, h * dqk:(h + 1) * dqk], q)
    fold = lambda a: a.reshape(kc // SUBLANES, SUBLANES, tq)
    for h in range(heads):
        m8 = None
        for off in range(0, p_len, kc):
            cm = jnp.max(fold(s_ref[h, off:off + kc, :]), axis=0)
            m8 = cm if m8 is None else jnp.maximum(m8, cm)
        m = jnp.max(m8, axis=0, keepdims=True)
        l8 = jnp.zeros((SUBLANES, tq), F32)
        for off in range(0, p_len, kc):
            p = jnp.exp2(s_ref[h, off:off + kc, :] - m)
            l8 = l8 + jnp.sum(fold(p), axis=0)
            p_ref[h, off:off + kc, :] = p.astype(BF16)
        l = jnp.sum(l8, axis=0, keepdims=True)
        hv = slice(h * MLA_V, (h + 1) * MLA_V)
        acc = (_dot(vt_ref[hv, 0:half], p_ref[h, 0:half, :])
               + _dot(vt_ref[hv, half:p_len], p_ref[h, half:p_len, :]))
        o_ref[:, hv] = (acc / l).T


def _mla_attn(qf, kf, vt, batch, t_len, l_len):
    n = qf.shape[0]
    p_len = t_len + l_len
    tq = ROW_TILE
    kc = MLA_KEY_CHUNK
    hp = MLA_HEADS_PER_STEP
    off = l_len // tq
    tpb = p_len // tq
    return pl.pallas_call(
        functools.partial(_mla_attn_kernel, kc=kc),
        out_shape=jax.ShapeDtypeStruct((n, MLA_HEADS * MLA_V), F32),
        grid=(batch, MLA_HEADS // hp, t_len // tq),
        in_specs=[
            pl.BlockSpec((tq, hp * 2 * LANES), lambda b, h, i: (b * tpb + off + i, h)),
            pl.BlockSpec((p_len, hp * 2 * LANES), lambda b, h, i: (b, h)),
            pl.BlockSpec((hp * MLA_V, p_len), lambda b, h, i: (h, b)),
            pl.BlockSpec(memory_space=pl.ANY),
        ],
        out_specs=pl.BlockSpec((tq, hp * MLA_V), lambda b, h, i: (b * tpb + off + i, h)),
        scratch_shapes=[pltpu.VMEM((hp, p_len, tq), F32), pltpu.VMEM((hp, p_len, tq), BF16)],
        input_output_aliases={3: 0},
        compiler_params=_cparams(("arbitrary", "arbitrary", "arbitrary")),
        name="mla_attn",
    )(qf, kf, vt, jnp.zeros((n, MLA_HEADS * MLA_V), F32))


def _ctx_attn_kernel(*refs, has_sink, scale, v_transposed):
    ex = jnp.exp if has_sink else jnp.exp2
    if has_sink:
        sink_ref, q_ref, k_ref, v_ref, o_ref = refs
    else:
        q_ref, k_ref, v_ref, o_ref = refs
    h = pl.program_id(1)
    s = _dot_nt(q_ref[...].astype(BF16), k_ref[...].astype(BF16)) * scale
    m = jnp.max(s, axis=-1, keepdims=True)
    if has_sink:
        sk = jnp.full((s.shape[0], 1), sink_ref[h], F32)
        m = jnp.maximum(m, sk)
    p = ex(s - m)
    den = jnp.sum(p, axis=-1, keepdims=True)
    if has_sink:
        den = den + ex(sk - m)
    v = v_ref[...].astype(BF16)
    pv = _dot_nt(p.astype(BF16), v) if v_transposed else _dot(p.astype(BF16), v)
    o_ref[...] = pv / den


def _ctx_attn(out_prev, q_arr, k_arr, v_arr, q_col, k_col, v_col, heads, group, batch, p_len, l_len,
              sink, scale, dq, v_transposed):
    n = out_prev.shape[0]
    qb = dq // LANES
    if v_transposed:
        v_spec = pl.BlockSpec((LANES, l_len), lambda b, h: (v_col + h // group, b * (p_len // l_len)))
    else:
        v_spec = pl.BlockSpec((l_len, LANES), lambda b, h: (b * (p_len // l_len), v_col + h // group))
    in_specs = [
        pl.BlockSpec((l_len, dq), lambda b, h: (b * (p_len // l_len), q_col // qb + h)),
        pl.BlockSpec((l_len, dq), lambda b, h: (b * (p_len // l_len), k_col // qb + h // group)),
        v_spec,
        pl.BlockSpec(memory_space=pl.ANY),
    ]
    args = [q_arr, k_arr, v_arr, out_prev]
    has_sink = sink is not None
    if has_sink:
        in_specs = [pl.BlockSpec(memory_space=pltpu.SMEM)] + in_specs
        args = [sink] + args

    def kern(*refs):
        refs = list(refs)
        del refs[4 if has_sink else 3]
        _ctx_attn_kernel(*refs, has_sink=has_sink, scale=scale, v_transposed=v_transposed)

    return pl.pallas_call(
        kern,
        out_shape=jax.ShapeDtypeStruct(out_prev.shape, out_prev.dtype),
        grid=(batch, heads),
        in_specs=in_specs,
        out_specs=pl.BlockSpec((l_len, LANES), lambda b, h: (b * (p_len // l_len), h)),
        input_output_aliases={len(args) - 1: 0},
        compiler_params=_cparams(("arbitrary", "arbitrary")),
        name="ctx_attn_sink" if has_sink else "ctx_attn",
    )(*args)


def _swa_kernel(sink_ref, q_ref, kp_ref, kc_ref, kn_ref, vp_ref, vc_ref, vn_ref, kx_ref, vx_ref,
                cq_ref, sq_ref, cp_ref, sp_ref, cn_ref, sn_ref, init_ref, o_ref, *, nb, scale):
    del init_ref
    n = pl.program_id(1)
    h = pl.program_id(2)
    blk = SWA_BLOCK
    cq, sq = cq_ref[...], sq_ref[...]
    q = q_ref[...]
    q0 = _rope128(q[:, :LANES], cq, sq)
    q1 = _rope128(q[:, LANES:], cq, sq)
    qq = jnp.concatenate([q0, q1], axis=0).astype(BF16)
    kc = _rope128(kc_ref[...], cq, sq).astype(BF16)
    kp = _rope128(kp_ref[...], cp_ref[...], sp_ref[...]).astype(BF16)
    kn = _rope128(kn_ref[...], cn_ref[...], sn_ref[...]).astype(BF16)
    s_c = _dot_nt(qq, kc) * scale
    s_p = _dot_nt(qq, kp) * scale
    s_n = _dot_nt(qq, kn) * scale
    s_x = _dot_nt(qq, kx_ref[...].astype(BF16)) * scale
    ri = lax.broadcasted_iota(jnp.int32, (2 * blk, blk), 0) % blk
    ci = lax.broadcasted_iota(jnp.int32, (2 * blk, blk), 1)
    neg = jnp.float32(-jnp.inf)
    s_p = jnp.where((ci >= ri) & (n > 0), s_p, neg)
    s_n = jnp.where((ci <= ri) & (n < nb - 1), s_n, neg)
    rr = lax.broadcasted_iota(jnp.int32, (2 * blk, 1), 0)
    sk = jnp.where(rr < blk, sink_ref[2 * h], sink_ref[2 * h + 1]).astype(F32)
    mx = lambda a: jnp.max(a, axis=-1, keepdims=True)
    m = jnp.maximum(jnp.maximum(mx(s_c), mx(s_p)), jnp.maximum(mx(s_n), mx(s_x)))
    m = jnp.maximum(m, sk)
    p_c, p_p, p_n, p_x = (jnp.exp(a - m) for a in (s_c, s_p, s_n, s_x))
    sm = lambda a: jnp.sum(a, axis=-1, keepdims=True)
    den = sm(p_c) + sm(p_p) + sm(p_n) + sm(p_x) + jnp.exp(sk - m)
    o = (_dot(p_c.astype(BF16), vc_ref[...].astype(BF16))
         + _dot(p_p.astype(BF16), vp_ref[...].astype(BF16))
         + _dot(p_n.astype(BF16), vn_ref[...].astype(BF16))
         + _dot(p_x.astype(BF16), vx_ref[...].astype(BF16))) / den
    o_ref[:, :LANES] = o[:blk]
    o_ref[:, LANES:] = o[blk:]


def _swa(pa, sink, cos_t, sin_t, batch, t_len, l_len):
    n = pa.shape[0]
    p_len = t_len + l_len
    blk = SWA_BLOCK
    nb = t_len // blk
    bpb = p_len // blk
    off = l_len // blk
    scale = float(HEAD_DIM ** -0.5)
    kcol, vcol = 4, 6

    def rb(b, j):
        return b * bpb + off + j

    prv = lambda j: jnp.maximum(j - 1, 0)
    nxt = lambda j: jnp.minimum(j + 1, nb - 1)
    in_specs = [
        pl.BlockSpec(memory_space=pltpu.SMEM),
        pl.BlockSpec((blk, 2 * LANES), lambda b, j, h: (rb(b, j), h)),
        pl.BlockSpec((blk, LANES), lambda b, j, h: (rb(b, prv(j)), kcol + h)),
        pl.BlockSpec((blk, LANES), lambda b, j, h: (rb(b, j), kcol + h)),
        pl.BlockSpec((blk, LANES), lambda b, j, h: (rb(b, nxt(j)), kcol + h)),
        pl.BlockSpec((blk, LANES), lambda b, j, h: (rb(b, prv(j)), vcol + h)),
        pl.BlockSpec((blk, LANES), lambda b, j, h: (rb(b, j), vcol + h)),
        pl.BlockSpec((blk, LANES), lambda b, j, h: (rb(b, nxt(j)), vcol + h)),
        pl.BlockSpec((l_len, LANES), lambda b, j, h: (b * (p_len // l_len), kcol + h)),
        pl.BlockSpec((l_len, LANES), lambda b, j, h: (b * (p_len // l_len), vcol + h)),
        pl.BlockSpec((blk, LANES), lambda b, j, h: (j, 0)),
        pl.BlockSpec((blk, LANES), lambda b, j, h: (j, 0)),
        pl.BlockSpec((blk, LANES), lambda b, j, h: (prv(j), 0)),
        pl.BlockSpec((blk, LANES), lambda b, j, h: (prv(j), 0)),
        pl.BlockSpec((blk, LANES), lambda b, j, h: (nxt(j), 0)),
        pl.BlockSpec((blk, LANES), lambda b, j, h: (nxt(j), 0)),
        pl.BlockSpec(memory_space=pl.ANY),
    ]
    return pl.pallas_call(
        functools.partial(_swa_kernel, nb=nb, scale=scale),
        out_shape=jax.ShapeDtypeStruct((n, SWA_HEADS * HEAD_DIM), F32),
        grid=(batch, nb, SWA_KV_HEADS),
        in_specs=in_specs,
        out_specs=pl.BlockSpec((blk, 2 * LANES), lambda b, j, h: (rb(b, j), h)),
        input_output_aliases={len(in_specs) - 1: 0},
        compiler_params=_cparams(("arbitrary", "arbitrary", "arbitrary")),
        name="swa",
    )(sink, pa, pa, pa, pa, pa, pa, pa, pa, pa, cos_t, sin_t, cos_t, sin_t, cos_t, sin_t,
      jnp.zeros((n, SWA_HEADS * HEAD_DIM), F32))


def _chunk_index(d, c, n_l, n_t):
    bwd = jnp.where(c < n_l, n_l - 1 - c, n_l + (n_t - 1) - (c - n_l))
    return jnp.where(d == 0, c, bwd)


def _flip_iotas(d):
    row = lax.broadcasted_iota(jnp.int32, (CHUNK, CHUNK), 0)
    col = lax.broadcasted_iota(jnp.int32, (CHUNK, CHUNK), 1)
    rf = jnp.where(d == 0, row, CHUNK - 1 - row)
    cf = jnp.where(d == 0, col, CHUNK - 1 - col)
    return rf, cf


def _ret_kernel(s_ref, qf_ref, kf_ref, vf_ref, qb_ref, kb_ref, vb_ref, of_ref, ob_ref, st_ref):
    c = pl.program_id(0)
    batch = qf_ref.shape[0]

    @pl.when(c == 0)
    def _():
        st_ref[...] = jnp.zeros_like(st_ref)

    ks = float(RET_DK ** -0.5)
    dirs = ((qf_ref, kf_ref, vf_ref, of_ref), (qb_ref, kb_ref, vb_ref, ob_ref))
    for d, (q_ref, k_ref, v_ref, o_ref) in enumerate(dirs):
        rf, cf = _flip_iotas(d)
        rff = rf.astype(F32)
        dn = (rf - cf).astype(F32)
        for h in range(RET_HEADS):
            hs = slice(h * LANES, (h + 1) * LANES)
            sv = jnp.full((CHUNK, CHUNK), s_ref[d * RET_HEADS + h], F32)
            lg = jnp.log1p(-jnp.exp2(-sv))
            dec = jnp.where(dn >= 0, jnp.exp(dn * lg), 0.0)
            eq = jnp.exp((rff + 1.0) * lg)
            ek = jnp.exp((CHUNK - 1.0 - rff) * lg)
            a_chunk = jnp.exp(float(CHUNK) * lg)
            for b in range(batch):
                q = q_ref[b, :, hs]
                k = k_ref[b, :, hs] * ks
                v = v_ref[b, :, hs].astype(BF16)
                st = st_ref[d, b, h]
                a = _dot_nt(q.astype(BF16), k.astype(BF16)) * dec
                o = _dot(a.astype(BF16), v) + _dot_nt((q * eq).astype(BF16), st.astype(BF16))
                o_ref[b, :, hs] = o
                st_ref[d, b, h] = a_chunk * st + _dot_tn(v, (k * ek).astype(BF16))


def _hgrn_level_mats():
    c = CHUNK
    t = np.arange(c)[:, None]
    u = np.arange(c)[None, :]
    mats = [u <= t, u > t]
    m = c // 2
    while m >= 1:
        mid = (t // (2 * m)) * 2 * m + m
        second = (t % (2 * m)) >= m
        qrole = (u >= mid) & (u <= t)
        krole = (u > t) & (u <= mid - 1)
        mats.append(np.where(second, qrole, krole))
        m //= 2
    fwd = np.concatenate(mats, 0).astype(np.float32)
    bwd = np.concatenate([mm[::-1, ::-1] for mm in mats], 0).astype(np.float32)
    return np.stack([fwd, bwd], 0)


N_LEVELS = int(math.log2(CHUNK))


def _hgrn_kernel(m_ref, lb_ref, q_ref, f_ref, v_ref, o_ref, st_ref):
    d = pl.program_id(1)
    c = pl.program_id(2)

    @pl.when(c == 0)
    def _():
        st_ref[...] = jnp.zeros_like(st_ref)

    rf, cf = _flip_iotas(d)
    mst = m_ref[0]
    for h in range(HGRN_HEADS):
        hs = slice(h * LANES, (h + 1) * LANES)
        lb = lb_ref[0, :, hs]
        f = lb + (1.0 - lb) * (1.0 / (1.0 + jnp.exp(-f_ref[:, hs])))
        k = 1.0 - f
        g = jnp.log(f)
        q = _silu(q_ref[:, hs])
        v = v_ref[:, hs].astype(BF16)
        g_hi = g.astype(BF16)
        g_lo = (g - g_hi.astype(F32)).astype(BF16)
        x2 = _dot(mst, jnp.concatenate([g_hi, g_lo], axis=1))
        e = jnp.exp(x2[:, :LANES] + x2[:, LANES:])
        e_q = e[0:CHUNK]
        q_in = q * e_q
        k_st = k * e[CHUNK:2 * CHUNK]
        scores = jnp.where(rf == cf, _dot_nt(q.astype(BF16), k.astype(BF16)), 0.0)
        for lvl in range(N_LEVELS):
            m = CHUNK >> (lvl + 1)
            sh = N_LEVELS - lvl
            el = e[(2 + lvl) * CHUNK:(3 + lvl) * CHUNK]
            second = (rf & m) != 0
            ql = jnp.where(second, q * el, 0.0).astype(BF16)
            kl = jnp.where(second, 0.0, k * el).astype(BF16)
            sl = _dot_nt(ql, kl)
            scores = scores + jnp.where((rf >> sh) == (cf >> sh), sl, 0.0)
        st = st_ref[h]
        o = _dot(scores.astype(BF16), v) + _dot_nt(q_in.astype(BF16), st.astype(BF16))
        o_ref[0, :, hs] = o
        a_last = jnp.where(d == 0, e_q[CHUNK - 1:CHUNK, :], e_q[0:1, :])
        st_ref[h] = st * a_last + _dot_tn(v, k_st.astype(BF16))


def _scan_call(kernel, name, extra_args, extra_specs, arrs, cols, batch, t_len, l_len):
    n = arrs[0].shape[0]
    p_len = t_len + l_len
    n_l, n_t = l_len // CHUNK, t_len // CHUNK
    cpb = p_len // CHUNK

    def rowblk(b, d, c):
        return b * cpb + _chunk_index(d, c, n_l, n_t)

    def col_map(col):
        return lambda b, d, c: (rowblk(b, d, c), col(d) if callable(col) else col)

    in_specs = list(extra_specs) + [pl.BlockSpec((CHUNK, WIDE), col_map(col)) for col in cols]
    return pl.pallas_call(
        kernel,
        out_shape=jax.ShapeDtypeStruct((2, n, WIDE), F32),
        grid=(batch, 2, n_l + n_t),
        in_specs=in_specs,
        out_specs=pl.BlockSpec((1, CHUNK, WIDE), lambda b, d, c: (d, rowblk(b, d, c), 0)),
        scratch_shapes=[pltpu.VMEM((4, CHUNK, CHUNK), F32)],
        compiler_params=_cparams(("arbitrary", "arbitrary", "arbitrary")),
        name=name,
    )(*extra_args, *arrs)


def _retention(pb, ret_s, batch, t_len, l_len):
    n = pb.shape[0]
    p_len = t_len + l_len
    n_l, n_t = l_len // CHUNK, t_len // CHUNK
    pb3 = pb.reshape(batch, p_len, pb.shape[1])

    def spec(d, col):
        return pl.BlockSpec((batch, CHUNK, WIDE), lambda c: (0, _chunk_index(d, c, n_l, n_t), col))

    out_f, out_b = pl.pallas_call(
        _ret_kernel,
        out_shape=[jax.ShapeDtypeStruct((batch, p_len, WIDE), F32)] * 2,
        grid=(n_l + n_t,),
        in_specs=[pl.BlockSpec(memory_space=pltpu.SMEM)]
                 + [spec(0, col) for col in (0, 1, 2)] + [spec(1, col) for col in (0, 1, 2)],
        out_specs=[spec(0, 0), spec(1, 0)],
        scratch_shapes=[pltpu.VMEM((2, batch, RET_HEADS, CHUNK, CHUNK), F32)],
        compiler_params=_cparams(("arbitrary",)),
        name="retention_scan",
    )(ret_s.reshape(-1), pb3, pb3, pb3, pb3, pb3, pb3)
    return out_f.reshape(n, WIDE), out_b.reshape(n, WIDE)


def _hgrn(pd, lower_bounds, mats, batch, t_len, l_len):
    lb3 = lower_bounds.reshape(2, 1, WIDE)
    specs = [pl.BlockSpec((1,) + mats.shape[1:], lambda b, d, c: (d, 0, 0)),
             pl.BlockSpec((1, 1, WIDE), lambda b, d, c: (d, 0, 0))]
    return _scan_call(_hgrn_kernel, "hgrn_scan", [mats, lb3], specs,
                      [pd, pd, pd], [0, lambda d: 1 + d, 3], batch, t_len, l_len)


R_E1, R_E2, R_G1, R_G2, R_S1, R_S2 = 0, 1, 2, 3, 4, 5


def _route(logits, count_ref, active):
    tm = logits.shape[0]
    lane = lax.broadcasted_iota(jnp.int32, logits.shape, 1)
    lanef = lane.astype(F32)
    big = jnp.float32(1e9)
    neg = jnp.float32(-jnp.inf)
    mx = lambda a: jnp.max(a, axis=-1, keepdims=True)
    mn = lambda a: jnp.min(a, axis=-1, keepdims=True)
    sm = lambda a: jnp.sum(a, axis=-1, keepdims=True)
    gl = jnp.where(lane < N_GROUPS, logits, neg)
    gm = mx(gl)
    p_grp = 1.0 / sm(jnp.exp(gl - gm))
    grp = mn(jnp.where(gl == gm, lanef, big))
    lo = N_GROUPS + grp * EXPERTS_PER_GROUP
    ing = (lanef >= lo) & (lanef < lo + EXPERTS_PER_GROUP)
    el = jnp.where(ing, logits, neg)
    ee = jnp.exp(el - mx(el))
    p = ee / sm(ee)
    pm = jnp.where(ing, p, -1.0)
    p1 = mx(pm)
    i1 = mn(jnp.where(pm == p1, lanef, big))
    pm2 = jnp.where(lanef == i1, -1.0, pm)
    p2 = mx(pm2)
    i2 = mn(jnp.where(pm2 == p2, lanef, big))
    den = p1 + p2
    g1 = p_grp * p1 / den
    g2 = p_grp * p2 / den
    e1 = i1 - N_GROUPS
    e2 = i2 - N_GROUPS
    oh1 = lanef == e1
    oh2 = lanef == e2
    both = jnp.where(oh1 | oh2, active, 0.0)
    ri = lax.broadcasted_iota(jnp.int32, (tm, tm), 0)
    ci = lax.broadcasted_iota(jnp.int32, (tm, tm), 1)
    earlier = jnp.where(ci < ri, 1.0, 0.0).astype(BF16)
    before = _dot(earlier, both.astype(BF16)) + count_ref[0:1, :]
    s1 = sm(jnp.where(oh1, before, 0.0))
    s2 = sm(jnp.where(oh2, before, 0.0))
    count_ref[...] = count_ref[...] + jnp.sum(both, axis=0, keepdims=True)
    out = jnp.zeros(logits.shape, F32)
    for ln, val in ((R_E1, e1), (R_E2, e2), (R_G1, g1), (R_G2, g2), (R_S1, s1), (R_S2, s2)):
        out = jnp.where(lane == ln, val, out)
    return out


def _out_kernel(x_ref, a_ref, bf_ref, bb_ref, c_ref, d_ref, rg_ref, hg_ref, g1_ref, sh2_ref, sc2_ref,
                lng_ref, lnb_ref, rgn_ref, hgn_ref, wo_ref, wr_ref,
                xo_ref, h2_ref, r_ref, cnt_ref, *, alpha, tiles_per_batch, route_ctx):
    i = pl.program_id(0)

    @pl.when(i == 0)
    def _():
        cnt_ref[...] = jnp.zeros_like(cnt_ref)

    bsum = bf_ref[...] + bb_ref[...]
    dsum = d_ref[0] + d_ref[1]
    rgn = rgn_ref[...]
    hgn = hgn_ref[...]
    parts_b, parts_d = [], []
    for h in range(4):
        hs = slice(h * LANES, (h + 1) * LANES)
        parts_b.append(_ln_rows(bsum[:, hs]))
        dh = dsum[:, hs]
        parts_d.append(dh * lax.rsqrt(jnp.mean(dh * dh, axis=-1, keepdims=True) + EPS))
    bo = _silu(rg_ref[...]) * (jnp.concatenate(parts_b, axis=1) * rgn)
    do = _silu(hg_ref[...]) * (jnp.concatenate(parts_d, axis=1) * hgn)
    mix = jnp.concatenate([a_ref[...], bo, c_ref[...], do], axis=1).astype(BF16)
    o = _dot(mix, wo_ref[...])
    y = alpha * x_ref[...] + g1_ref[0] * o
    xn = _ln_rows(y) * lng_ref[...] + lnb_ref[...]
    xo_ref[...] = xn
    h2 = _ln_rows(xn) * (1.0 + sc2_ref[0]) + sh2_ref[0]
    h2_ref[...] = h2
    if route_ctx:
        active = jnp.float32(1.0)
    else:
        active = jnp.where(i % tiles_per_batch == 0, 0.0, 1.0).astype(F32)
    r_ref[...] = _route(_dot(h2.astype(BF16), wr_ref[...]), cnt_ref, active)


def _out_proj(stream, a, b_f, b_b, c, d2, pb, pd, g1, sh2, sc2, ln_g, ln_b, ret_gn, hgrn_gn, w_out, w_r,
              tiles_per_batch, alpha, route_ctx):
    n, dm = stream.shape
    tm = ROW_TILE
    gmap = lambda i: (_group_of_tile(i, tiles_per_batch), 0, 0)
    row = lambda w: pl.BlockSpec((tm, w), lambda i: (i, 0))
    row2 = pl.BlockSpec((2, tm, WIDE), lambda i: (0, i, 0))
    const = lambda w: pl.BlockSpec((1, w), lambda i: (0, 0))
    resident = pl.BlockSpec(memory_space=pltpu.VMEM)
    return pl.pallas_call(
        functools.partial(_out_kernel, alpha=alpha, tiles_per_batch=tiles_per_batch, route_ctx=route_ctx),
        out_shape=[jax.ShapeDtypeStruct((n, dm), F32), jax.ShapeDtypeStruct((n, dm), F32),
                   jax.ShapeDtypeStruct((n, LANES), F32), jax.ShapeDtypeStruct((SUBLANES, LANES), F32)],
        grid=(n // tm,),
        in_specs=[row(dm), row(WIDE), row(WIDE), row(WIDE), row(WIDE), row2,
                  pl.BlockSpec((tm, WIDE), lambda i: (i, 3)),
                  pl.BlockSpec((tm, WIDE), lambda i: (i, 4)),
                  pl.BlockSpec((1, 1, dm), gmap), pl.BlockSpec((1, 1, dm), gmap),
                  pl.BlockSpec((1, 1, dm), gmap),
                  const(dm), const(dm), const(WIDE), const(WIDE), resident, resident],
        out_specs=[row(dm), row(dm), row(LANES),
                   pl.BlockSpec((SUBLANES, LANES), lambda i: (0, 0))],
        compiler_params=_cparams(("arbitrary",)),
        name="mix_outproj_norm_route",
    )(stream, a, b_f, b_b, c, d2, pb, pd, g1, sh2, sc2, ln_g, ln_b, ret_gn, hgrn_gn, w_out, w_r)


def _route_plan(route, counts_f, n_blocks):
    rows = MOE_ROWS
    counts = counts_f[0, :N_EXPERTS].astype(jnp.int32)
    padded = (counts + rows - 1) // rows * rows
    pend = jnp.cumsum(padded)
    pstart = pend - padded
    eid = route[:, R_E1:R_E2 + 1].astype(jnp.int32)
    rank = route[:, R_S1:R_S2 + 1].astype(jnp.int32)
    onehot = eid[:, :, None] == jnp.arange(N_EXPERTS, dtype=jnp.int32)
    pos = jnp.sum(jnp.where(onehot, pstart, 0), axis=-1) + rank
    blk_start = jnp.arange(n_blocks, dtype=jnp.int32) * rows
    blk_e = jnp.minimum(jnp.sum(blk_start[:, None] >= pend[None, :], axis=-1), N_EXPERTS - 1)
    blk_valid = (blk_start < pend[-1]).astype(jnp.int32)
    pad_start = pstart + counts
    return pos.reshape(-1).astype(jnp.int32), blk_e.astype(jnp.int32), blk_valid, pad_start.astype(jnp.int32)


def _dispatch_kernel(pos_ref, pad_ref, bv_ref, h_ref, x_hbm, zbuf, sem, *, tile_of_step, n_blocks):
    i = pl.program_id(0)
    tm = ROW_TILE
    rows = MOE_ROWS

    def zero_copy(e):
        start = pl.multiple_of(pad_ref[e] & ~(SUBLANES - 1), SUBLANES)
        return pltpu.make_async_copy(zbuf, x_hbm.at[pl.ds(start, rows + SUBLANES)], sem.at[1])

    def unused_block_copy(j):
        start = pl.multiple_of(j * rows, rows)
        return pltpu.make_async_copy(zbuf.at[pl.ds(0, rows)], x_hbm.at[pl.ds(start, rows)], sem.at[1])

    @pl.when(i == 0)
    def _():
        zbuf[...] = jnp.zeros_like(zbuf)

        def fill(j, carry):
            @pl.when(bv_ref[j] == 0)
            def _():
                unused_block_copy(j).start()
                unused_block_copy(j).wait()
            return carry

        lax.fori_loop(0, n_blocks + 2, fill, 0)
        for e in range(N_EXPERTS):
            zero_copy(e).start()
        for e in range(N_EXPERTS):
            zero_copy(e).wait()

    def row_copy(r, k):
        p = pos_ref[(tile_of_step(i) * tm + r) * TOP_K + k]
        return pltpu.make_async_copy(h_ref.at[r], x_hbm.at[p], sem.at[0])

    for r in range(tm):
        for k in range(TOP_K):
            row_copy(r, k).start()
    for r in range(tm):
        for k in range(TOP_K):
            row_copy(r, k).wait()


def _tile_schedule(n, tiles_per_batch, latent_only, batch):
    if latent_only:
        lat = tiles_per_batch - 1
        return batch * lat, (lambda s: (s // lat) * tiles_per_batch + 1 + s % lat)
    return n // ROW_TILE, (lambda s: s)


def _dispatch(h2, pos, pad_start, blk_valid, tiles_per_batch, latent_only, batch):
    n, dm = h2.shape
    tm = ROW_TILE
    n_blocks = blk_valid.shape[0]
    blk_valid = jnp.concatenate([blk_valid, jnp.zeros((2,), jnp.int32)])
    n_tiles, tile_of_step = _tile_schedule(n, tiles_per_batch, latent_only, batch)
    grid_spec = pltpu.PrefetchScalarGridSpec(
        num_scalar_prefetch=3,
        grid=(n_tiles,),
        in_specs=[pl.BlockSpec((tm, dm), lambda i, p, z, v: (tile_of_step(i), 0))],
        out_specs=pl.BlockSpec(memory_space=pl.ANY),
        scratch_shapes=[pltpu.VMEM((MOE_ROWS + SUBLANES, dm), F32), pltpu.SemaphoreType.DMA((2,))],
    )
    return pl.pallas_call(
        functools.partial(_dispatch_kernel, tile_of_step=tile_of_step, n_blocks=n_blocks),
        out_shape=jax.ShapeDtypeStruct(((n_blocks + 2) * MOE_ROWS, dm), F32),
        grid_spec=grid_spec,
        compiler_params=_cparams(("arbitrary",)),
        name="moe_dispatch",
    )(pos, pad_start, blk_valid, h2)


def _moe_kernel(be_ref, bv_ref, x_ref, w1_ref, w3_ref, w2_ref, y_ref, w1b, w3b, w2b):
    j = pl.program_id(0)
    first = (j == 0) | (be_ref[j] != be_ref[jnp.maximum(j - 1, 0)])

    @pl.when(first & (bv_ref[j] == 1))
    def _():
        w1b[...] = w1_ref[0, 0].astype(BF16)
        w3b[...] = w3_ref[0, 0].astype(BF16)
        w2b[...] = w2_ref[0, 0].astype(BF16)

    @pl.when(bv_ref[j] == 1)
    def _():
        x = x_ref[...].astype(BF16)
        hmid = _silu(_dot(x, w1b[...])) * _dot(x, w3b[...])
        y_ref[...] = _dot(hmid.astype(BF16), w2b[...])

    @pl.when(bv_ref[j] == 0)
    def _():
        y_ref[...] = jnp.zeros_like(y_ref)


def _moe_experts(xbuf, blk_e, blk_valid, w1, w3, w2, layer):
    dm = xbuf.shape[1]
    ff = w1.shape[-1]
    rows = MOE_ROWS
    n_blocks = blk_e.shape[0]
    wmap = lambda j, be, bv: (layer, be[j], 0, 0)
    grid_spec = pltpu.PrefetchScalarGridSpec(
        num_scalar_prefetch=2,
        grid=(n_blocks,),
        in_specs=[
            pl.BlockSpec((rows, dm), lambda j, be, bv: (j * bv[j], 0)),
            pl.BlockSpec((1, 1, dm, ff), wmap),
            pl.BlockSpec((1, 1, dm, ff), wmap),
            pl.BlockSpec((1, 1, ff, dm), wmap),
        ],
        out_specs=pl.BlockSpec((rows, dm), lambda j, be, bv: (j, 0)),
        scratch_shapes=[pltpu.VMEM((dm, ff), BF16), pltpu.VMEM((dm, ff), BF16), pltpu.VMEM((ff, dm), BF16)],
    )
    return pl.pallas_call(
        _moe_kernel,
        out_shape=jax.ShapeDtypeStruct((n_blocks * rows, dm), F32),
        grid_spec=grid_spec,
        compiler_params=_cparams(("arbitrary",)),
        name="moe_experts",
    )(blk_e, blk_valid, xbuf, w1, w3, w2)


def _combine_kernel(pos_ref, x_ref, r_ref, g2_ref, lng_ref, lnb_ref, y_hbm, o_ref, ybuf, sem,
                    *, n_tiles, tile_of_step, alpha):
    i = pl.program_id(0)
    slot = i % 2
    tm = ROW_TILE

    def row_copy(step, sl, r, k):
        p = pos_ref[(tile_of_step(step) * tm + r) * TOP_K + k]
        return pltpu.make_async_copy(y_hbm.at[p], ybuf.at[sl, k, r], sem.at[sl])

    def start_gather(step, sl):
        for r in range(tm):
            for k in range(TOP_K):
                row_copy(step, sl, r, k).start()

    @pl.when(i == 0)
    def _():
        start_gather(0, 0)

    @pl.when(i + 1 < n_tiles)
    def _():
        start_gather(jnp.minimum(i + 1, n_tiles - 1), 1 - slot)

    for r in range(tm):
        for k in range(TOP_K):
            row_copy(i, slot, r, k).wait()
    route = r_ref[...]
    y = ybuf[slot, 0] * route[:, R_G1:R_G1 + 1] + ybuf[slot, 1] * route[:, R_G2:R_G2 + 1]
    z = alpha * x_ref[...] + g2_ref[0] * y
    o_ref[...] = _ln_rows(z) * lng_ref[...] + lnb_ref[...]


def _combine(x_new, route, ybuf, pos, g2, ln_g, ln_b, tiles_per_batch, alpha, latent_only, batch):
    n, dm = x_new.shape
    tm = ROW_TILE
    n_tiles, tile_of_step = _tile_schedule(n, tiles_per_batch, latent_only, batch)
    gmap = lambda i, p: (_group_of_tile(tile_of_step(i), tiles_per_batch), 0, 0)
    grid_spec = pltpu.PrefetchScalarGridSpec(
        num_scalar_prefetch=1,
        grid=(n_tiles,),
        in_specs=[
            pl.BlockSpec((tm, dm), lambda i, p: (tile_of_step(i), 0)),
            pl.BlockSpec((tm, LANES), lambda i, p: (tile_of_step(i), 0)),
            pl.BlockSpec((1, 1, dm), gmap),
            pl.BlockSpec((1, dm), lambda i, p: (0, 0)),
            pl.BlockSpec((1, dm), lambda i, p: (0, 0)),
            pl.BlockSpec(memory_space=pl.ANY),
        ],
        out_specs=pl.BlockSpec((tm, dm), lambda i, p: (i, 0)),
        scratch_shapes=[pltpu.VMEM((2, TOP_K, tm, dm), F32), pltpu.SemaphoreType.DMA((2,))],
    )
    return pl.pallas_call(
        functools.partial(_combine_kernel, n_tiles=n_tiles, tile_of_step=tile_of_step, alpha=alpha),
        out_shape=jax.ShapeDtypeStruct((n_tiles * tm, dm), F32),
        grid_spec=grid_spec,
        compiler_params=_cparams(("arbitrary",)),
        name="moe_combine_norm",
    )(pos, x_new, route, g2, ln_g, ln_b, ybuf)


def _pad_cols(w, width):
    return jnp.pad(w, ((0, 0), (0, width - w.shape[1])))


def _split_w_in(w_in_l):
    wa = w_in_l[:, :SEG_A]
    wb = w_in_l[:, SEG_A:SEG_A + SEG_B]
    c0 = SEG_A + SEG_B
    c_true = MLA_Q_RANK + MLA_KV_RANK + MLA_ROPE
    wc = _pad_cols(w_in_l[:, c0:c0 + c_true], SEG_C)
    wd = w_in_l[:, c0 + c_true:]
    return [w.astype(BF16) for w in (wa, wb, wc, wd)]


def kernel(x, c, ctx, c_ctx, w_ada, b_ada, w_in, swa_sink, ret_decay_exp, ret_gn, mla_q_norm, mla_kv_norm,
           mla_w_uq, mla_w_ukv, hgrn_lb_logits, hgrn_gn, w_out, ln1_g, ln1_b, router_group, router_expert,
           moe_w1, moe_w3, moe_w2, ln2_g, ln2_b):
    batch, t_len, dm = x.shape
    l_len = ctx.shape[1]
    depth = w_in.shape[0]
    p_len = t_len + l_len
    n = batch * p_len
    assert batch == 2 and l_len == ROW_TILE and t_len % 256 == 0 and t_len % GRID_W == 0
    tiles_per_batch = p_len // ROW_TILE
    alpha = float((2 * depth) ** 0.25)

    cos_swa, sin_swa, cos_rows, sin_rows = _rope_tables(t_len, l_len, batch)
    mats = jnp.asarray(_hgrn_level_mats(), BF16)

    cvec = jnp.zeros((SUBLANES, dm), F32).at[0:batch].set(c).at[batch].set(c_ctx)
    mod = _ada(cvec, w_ada, b_ada)

    lbp = jax.nn.softmax(hgrn_lb_logits.astype(F32), axis=0)
    lower_bounds = jnp.cumsum(lbp, axis=0) - lbp[0]

    stream = jnp.concatenate([ctx, x], axis=1).reshape(n, dm)

    out = None
    for l in range(depth):
        need_ctx = l < depth - 1
        m6 = mod[l, :3].reshape(3, 6, dm)
        sh1, sc1, g1, sh2, sc2, g2 = (m6[:, k].reshape(3, 1, dm) for k in range(6))

        pa, pb, pc, pd = _proj(stream, sh1, sc1, _split_w_in(w_in[l]), tiles_per_batch)

        wq = mla_w_uq[l].reshape(MLA_Q_RANK, MLA_HEADS, MLA_NOPE + MLA_ROPE)
        wq = jnp.pad(wq, ((0, 0), (0, 0), (0, 2 * LANES - (MLA_NOPE + MLA_ROPE))))
        wq = wq.reshape(MLA_Q_RANK, MLA_HEADS * 2 * LANES).astype(BF16)
        wkv = mla_w_ukv[l].reshape(MLA_KV_RANK, MLA_HEADS, MLA_NOPE + MLA_V)
        wk = wkv[:, :, :MLA_NOPE].reshape(MLA_KV_RANK, -1).astype(BF16)
        wv_t = wkv[:, :, MLA_NOPE:].reshape(MLA_KV_RANK, -1).T.astype(BF16)
        qf, kf, vt = _mla_up(pc, mla_q_norm[l].reshape(1, -1), mla_kv_norm[l].reshape(1, -1),
                             wq, wk, wv_t, cos_rows, sin_rows)

        a = _swa(pa, swa_sink[l], cos_swa, sin_swa, batch, t_len, l_len)
        cc = _mla_attn(qf, kf, vt, batch, t_len, l_len)
        if need_ctx:
            a = _ctx_attn(a, pa, pa, pa, 0, 4, 6, SWA_HEADS, SWA_HEADS // SWA_KV_HEADS, batch, p_len,
                          l_len, swa_sink[l], float(HEAD_DIM ** -0.5), LANES, False)
            cc = _ctx_attn(cc, qf, kf, vt, 0, 0, 0, MLA_HEADS, 1, batch, p_len, l_len, None, 1.0,
                           2 * LANES, True)
        b_f, b_b = _retention(pb, ret_decay_exp[l], batch, t_len, l_len)
        d2 = _hgrn(pd, lower_bounds[l], mats, batch, t_len, l_len)

        w_r = _pad_cols(jnp.concatenate([router_group[l], router_expert[l]], axis=1), LANES).astype(BF16)
        x_new, h2, route, counts = _out_proj(
            stream, a, b_f, b_b, cc, d2, pb, pd, g1, sh2, sc2,
            ln1_g[l].reshape(1, -1), ln1_b[l].reshape(1, -1),
            ret_gn[l].reshape(1, -1), hgrn_gn[l].reshape(1, -1),
            w_out[l].astype(BF16), w_r, tiles_per_batch, alpha, need_ctx)

        n_active = n if need_ctx else batch * t_len
        n_blocks = (n_active * TOP_K + N_EXPERTS * (MOE_ROWS - 1) + MOE_ROWS - 1) // MOE_ROWS
        pos, blk_e, blk_valid, pad_start = _route_plan(route, counts, n_blocks)
        xbuf = _dispatch(h2, pos, pad_start, blk_valid, tiles_per_batch, not need_ctx, batch)
        ybuf = _moe_experts(xbuf, blk_e, blk_valid, moe_w1, moe_w3, moe_w2, l)
        res = _combine(x_new, route, ybuf, pos, g2, ln2_g[l].reshape(1, -1), ln2_b[l].reshape(1, -1),
                       tiles_per_batch, alpha, not need_ctx, batch)
        if need_ctx:
            stream = res
        else:
            out = res.reshape(batch, t_len, dm)
    return out
```

```python
import functools
import math

import numpy as np
import jax
import jax.numpy as jnp
from jax import lax
from jax.experimental import pallas as pl
from jax.experimental.pallas import tpu as pltpu

F32 = jnp.float32
BF16 = jnp.bfloat16

GRID_W = 64
HEAD_DIM = 128
SWA_HEADS = 4
SWA_KV_HEADS = 2
SWA_BLOCK = 128
RET_HEADS = 4
RET_DK = 128
MLA_HEADS = 4
MLA_Q_RANK = 384
MLA_KV_RANK = 128
MLA_NOPE = 128
MLA_ROPE = 64
MLA_V = 128
HGRN_HEADS = 4
N_GROUPS = 4
EXPERTS_PER_GROUP = 8
N_EXPERTS = N_GROUPS * EXPERTS_PER_GROUP
TOP_K = 2
ROPE_BASE = 10000.0
EPS = 1e-6

LANES = 128
SUBLANES = 8
VMEM_LIMIT_BYTES = 56 * 1024 * 1024

CHUNK = 128
ROW_TILE = 256
MOE_ROWS = 256
MLA_KEY_CHUNK = 256
MLA_HEADS_PER_STEP = 2
WIDE = 512

SEG_A = 1024
SEG_B = 2048
SEG_C = 640
SEG_D = 2560


def _cparams(sem, vmem=VMEM_LIMIT_BYTES):
    return pltpu.CompilerParams(dimension_semantics=sem, vmem_limit_bytes=vmem)


def _ln_rows(x):
    mu = jnp.mean(x, axis=-1, keepdims=True)
    xc = x - mu
    var = jnp.mean(xc * xc, axis=-1, keepdims=True)
    return xc * lax.rsqrt(var + EPS)


def _silu(x):
    return x * (1.0 / (1.0 + jnp.exp(-x)))


def _dot(a, b):
    return jnp.dot(a, b, preferred_element_type=F32)


def _pack_bf16_pairs(x):
    k = x.shape[1] // 2
    hi = lax.bitcast_convert_type(x[:, :k].astype(BF16).astype(F32), jnp.uint32)
    lo = lax.bitcast_convert_type(x[:, k:].astype(BF16).astype(F32), jnp.uint32)
    return hi | (lo >> 16)


def _unpack_bf16_pairs(w):
    hi = lax.bitcast_convert_type(w & jnp.uint32(0xFFFF0000), F32)
    lo = lax.bitcast_convert_type(w << 16, F32)
    return jnp.concatenate([hi, lo], axis=1)


def _dot_nt(a, b):
    return lax.dot_general(a, b, (((1,), (1,)), ((), ())), preferred_element_type=F32)


def _dot_tn(a, b):
    return lax.dot_general(a, b, (((0,), (0,)), ((), ())), preferred_element_type=F32)


def _ada_kernel(c_ref, w_ref, b_ref, o_ref):
    s = _silu(c_ref[...]).astype(BF16)
    o_ref[0] = _dot(s, w_ref[0].astype(BF16)) + b_ref[0]


def _ada(cvec, w_ada, b_ada):
    depth, d, n6 = w_ada.shape
    tn = n6 // 8
    return pl.pallas_call(
        _ada_kernel,
        out_shape=jax.ShapeDtypeStruct((depth, SUBLANES, n6), F32),
        grid=(depth, n6 // tn),
        in_specs=[
            pl.BlockSpec((SUBLANES, d), lambda l, j: (0, 0)),
            pl.BlockSpec((1, d, tn), lambda l, j: (l, 0, j)),
            pl.BlockSpec((1, 1, tn), lambda l, j: (l, 0, j)),
        ],
        out_specs=pl.BlockSpec((1, SUBLANES, tn), lambda l, j: (l, 0, j)),
        compiler_params=_cparams(("arbitrary", "arbitrary")),
        name="ada_mod",
    )(cvec, w_ada, b_ada.reshape(depth, 1, n6))


def _proj_kernel(x_ref, sh_ref, sc_ref, wa_ref, wb_ref, wc_ref, wd_ref, oa, ob, oc, od):
    y = _ln_rows(x_ref[...])
    h = (y * (1.0 + sc_ref[0]) + sh_ref[0]).astype(BF16)
    for w_ref, o_ref in ((wa_ref, oa), (wb_ref, ob), (wc_ref, oc), (wd_ref, od)):
        width = o_ref.shape[1]
        for j in range(0, width, WIDE):
            cw = min(WIDE, width - j)
            o_ref[:, j:j + cw] = _dot(h, w_ref[:, j:j + cw])


def _group_of_tile(i, tiles_per_batch):
    return jnp.where(i % tiles_per_batch == 0, 2, i // tiles_per_batch)


def _proj(stream, shift, scale, w_segs, tiles_per_batch):
    n, d = stream.shape
    tm = ROW_TILE
    gmap = lambda i: (_group_of_tile(i, tiles_per_batch), 0, 0)
    resident = pl.BlockSpec(memory_space=pltpu.VMEM)
    widths = [w.shape[1] for w in w_segs]
    return pl.pallas_call(
        _proj_kernel,
        out_shape=[jax.ShapeDtypeStruct((n, w), F32) for w in widths],
        grid=(n // tm,),
        in_specs=[
            pl.BlockSpec((tm, d), lambda i: (i, 0)),
            pl.BlockSpec((1, 1, d), gmap),
            pl.BlockSpec((1, 1, d), gmap),
            resident, resident, resident, resident,
        ],
        out_specs=[pl.BlockSpec((tm, w), lambda i: (i, 0)) for w in widths],
        compiler_params=_cparams(("arbitrary",)),
        name="ln_mod_inproj",
    )(stream, shift, scale, *w_segs)


def _rope128(x, cos, sin):
    return x * cos + pltpu.roll(x, 64, 1) * sin


def _rope64(x, cos, sin):
    lane = lax.broadcasted_iota(jnp.int32, x.shape, 1)
    rot = jnp.where((lane % 64) < 32, pltpu.roll(x, 96, 1), pltpu.roll(x, 32, 1))
    return x * cos + rot * sin


def _rope_tables(t_len, l_len, batch):
    rows = t_len // GRID_W
    row = np.repeat(np.arange(rows), GRID_W).astype(np.float32)
    col = np.tile(np.arange(GRID_W), rows).astype(np.float32)

    def angles(rot_dim):
        n_freq = rot_dim // 4
        inv = (ROPE_BASE ** (-np.arange(n_freq, dtype=np.float32) / n_freq)).astype(np.float32)
        return np.concatenate([row[:, None] * inv, col[:, None] * inv], -1).astype(np.float32)

    a_swa = angles(HEAD_DIM)
    cos_swa = np.concatenate([np.cos(a_swa), np.cos(a_swa)], -1)
    sin_swa = np.concatenate([-np.sin(a_swa), np.sin(a_swa)], -1)
    a_mla = angles(MLA_ROPE)
    cos_m = np.concatenate([np.cos(a_mla), np.cos(a_mla), np.ones((t_len, 64), np.float32)], -1)
    sin_m = np.concatenate([-np.sin(a_mla), np.sin(a_mla), np.zeros((t_len, 64), np.float32)], -1)
    ones = np.ones((l_len, LANES), np.float32)
    zeros = np.zeros((l_len, LANES), np.float32)
    cos_rows = np.concatenate([np.concatenate([ones, cos_m], 0)] * batch, 0)
    sin_rows = np.concatenate([np.concatenate([zeros, sin_m], 0)] * batch, 0)
    return (jnp.asarray(cos_swa, F32), jnp.asarray(sin_swa, F32),
            jnp.asarray(cos_rows, F32), jnp.asarray(sin_rows, F32))


def _mla_up_kernel(pc_ref, qn_ref, kvn_ref, wq_ref, wk_ref, wv_ref, cos_ref, sin_ref,
                   q_out, k_out, v_out, *, scale):
    pc = pc_ref[...]
    cq = pc[:, :MLA_Q_RANK]
    ckv = pc[:, MLA_Q_RANK:MLA_Q_RANK + MLA_KV_RANK]
    kr = pc[:, MLA_Q_RANK + MLA_KV_RANK:]
    cos = cos_ref[...]
    sin = sin_ref[...]

    def rms(x, g):
        return x * lax.rsqrt(jnp.mean(x * x, axis=-1, keepdims=True) + EPS) * g

    qh = _dot(rms(cq, qn_ref[...]).astype(BF16), wq_ref[...])
    ckn = rms(ckv, kvn_ref[...]).astype(BF16)
    kh = _dot(ckn, wk_ref[...])
    v_out[...] = _dot_nt(wv_ref[...], ckn).astype(BF16)
    kr_rot = _rope64(kr, cos, sin).astype(BF16)
    for h in range(MLA_HEADS):
        base = h * 2 * LANES
        q_out[:, base:base + LANES] = (qh[:, base:base + LANES] * scale).astype(BF16)
        q_out[:, base + LANES:base + 2 * LANES] = (
            _rope64(qh[:, base + LANES:base + 2 * LANES], cos, sin) * scale).astype(BF16)
        k_out[:, base:base + LANES] = kh[:, h * LANES:(h + 1) * LANES].astype(BF16)
        k_out[:, base + LANES:base + 2 * LANES] = kr_rot


def _mla_up(pc, q_norm, kv_norm, wq, wk, wv, cos_rows, sin_rows):
    n = pc.shape[0]
    tm = ROW_TILE
    scale = float((MLA_NOPE + MLA_ROPE) ** -0.5 * math.log2(math.e))
    const2 = lambda i: (0, 0)
    return pl.pallas_call(
        functools.partial(_mla_up_kernel, scale=scale),
        out_shape=[jax.ShapeDtypeStruct((n, MLA_HEADS * 2 * LANES), BF16),
                   jax.ShapeDtypeStruct((n, MLA_HEADS * 2 * LANES), BF16),
                   jax.ShapeDtypeStruct((MLA_HEADS * MLA_V, n), BF16)],
        grid=(n // tm,),
        in_specs=[
            pl.BlockSpec((tm, SEG_C), lambda i: (i, 0)),
            pl.BlockSpec((1, MLA_Q_RANK), const2),
            pl.BlockSpec((1, MLA_KV_RANK), const2),
            pl.BlockSpec(wq.shape, const2),
            pl.BlockSpec(wk.shape, const2),
            pl.BlockSpec(wv.shape, const2),
            pl.BlockSpec((tm, LANES), lambda i: (i, 0)),
            pl.BlockSpec((tm, LANES), lambda i: (i, 0)),
        ],
        out_specs=[pl.BlockSpec((tm, MLA_HEADS * 2 * LANES), lambda i: (i, 0)),
                   pl.BlockSpec((tm, MLA_HEADS * 2 * LANES), lambda i: (i, 0)),
                   pl.BlockSpec((MLA_HEADS * MLA_V, tm), lambda i: (0, i))],
        compiler_params=_cparams(("arbitrary",)),
        name="mla_up",
    )(pc, q_norm, kv_norm, wq, wk, wv, cos_rows, sin_rows)


def _mla_attn_kernel(q_ref, k_ref, vt_ref, init_ref, o_ref, s_ref, p_ref, *, kc):
    del init_ref
    p_len = k_ref.shape[0]
    tq = q_ref.shape[0]
    dqk = 2 * LANES
    heads = q_ref.shape[1] // dqk
    half = p_len // 2
    for h in range(heads):
        q = q_ref[:, h * dqk:(h + 1) * dqk]
        s_ref[h, 0:half, :] = _dot_nt(k_ref[0:half, h * dqk:(h + 1) * dqk], q)
        s_ref[h, half:p_len, :] = _dot_nt(k_ref[half:p_len, h * dqk:(h + 1) * dqk], q)
    fold = lambda a: a.reshape(kc // SUBLANES, SUBLANES, tq)
    for h in range(heads):
        m8 = None
        for off in range(0, p_len, kc):
            cm = jnp.max(fold(s_ref[h, off:off + kc, :]), axis=0)
            m8 = cm if m8 is None else jnp.maximum(m8, cm)
        m = jnp.max(m8, axis=0, keepdims=True)
        l8 = jnp.zeros((SUBLANES, tq), F32)
        for off in range(0, p_len, kc):
            p = jnp.exp2(s_ref[h, off:off + kc, :] - m)
            l8 = l8 + jnp.sum(fold(p), axis=0)
            p_ref[h, off:off + kc, :] = p.astype(BF16)
        l = jnp.sum(l8, axis=0, keepdims=True)
        hv = slice(h * MLA_V, (h + 1) * MLA_V)
        acc = (_dot(vt_ref[hv, 0:half], p_ref[h, 0:half, :])
               + _dot(vt_ref[hv, half:p_len], p_ref[h, half:p_len, :]))
        o_ref[:, hv] = (acc / l).T


def _mla_attn(qf, kf, vt, batch, t_len, l_len):
    n = qf.shape[0]
    p_len = t_len + l_len
    tq = ROW_TILE
    kc = MLA_KEY_CHUNK
    hp = MLA_HEADS_PER_STEP
    off = l_len // tq
    tpb = p_len // tq
    return pl.pallas_call(
        functools.partial(_mla_attn_kernel, kc=kc),
        out_shape=jax.ShapeDtypeStruct((n, MLA_HEADS * MLA_V), F32),
        grid=(batch, MLA_HEADS // hp, t_len // tq),
        in_specs=[
            pl.BlockSpec((tq, hp * 2 * LANES), lambda b, h, i: (b * tpb + off + i, h)),
            pl.BlockSpec((p_len, hp * 2 * LANES), lambda b, h, i: (b, h)),
            pl.BlockSpec((hp * MLA_V, p_len), lambda b, h, i: (h, b)),
            pl.BlockSpec(memory_space=pl.ANY),
        ],
        out_specs=pl.BlockSpec((tq, hp * MLA_V), lambda b, h, i: (b * tpb + off + i, h)),
        scratch_shapes=[pltpu.VMEM((hp, p_len, tq), F32), pltpu.VMEM((hp, p_len, tq), BF16)],
        input_output_aliases={3: 0},
        compiler_params=_cparams(("arbitrary", "arbitrary", "arbitrary")),
        name="mla_attn",
    )(qf, kf, vt, jnp.zeros((n, MLA_HEADS * MLA_V), F32))


def _ctx_attn_kernel(*refs, has_sink, scale, v_transposed):
    ex = jnp.exp if has_sink else jnp.exp2
    if has_sink:
        sink_ref, q_ref, k_ref, v_ref, o_ref = refs
    else:
        q_ref, k_ref, v_ref, o_ref = refs
    h = pl.program_id(1)
    s = _dot_nt(q_ref[...].astype(BF16), k_ref[...].astype(BF16)) * scale
    m = jnp.max(s, axis=-1, keepdims=True)
    if has_sink:
        sk = jnp.full((s.shape[0], 1), sink_ref[h], F32)
        m = jnp.maximum(m, sk)
    p = ex(s - m)
    den = jnp.sum(p, axis=-1, keepdims=True)
    if has_sink:
        den = den + ex(sk - m)
    v = v_ref[...].astype(BF16)
    pv = _dot_nt(p.astype(BF16), v) if v_transposed else _dot(p.astype(BF16), v)
    o_ref[...] = pv / den


def _ctx_attn(out_prev, q_arr, k_arr, v_arr, q_col, k_col, v_col, heads, group, batch, p_len, l_len,
              sink, scale, dq, v_transposed):
    n = out_prev.shape[0]
    qb = dq // LANES
    if v_transposed:
        v_spec = pl.BlockSpec((LANES, l_len), lambda b, h: (v_col + h // group, b * (p_len // l_len)))
    else:
        v_spec = pl.BlockSpec((l_len, LANES), lambda b, h: (b * (p_len // l_len), v_col + h // group))
    in_specs = [
        pl.BlockSpec((l_len, dq), lambda b, h: (b * (p_len // l_len), q_col // qb + h)),
        pl.BlockSpec((l_len, dq), lambda b, h: (b * (p_len // l_len), k_col // qb + h // group)),
        v_spec,
        pl.BlockSpec(memory_space=pl.ANY),
    ]
    args = [q_arr, k_arr, v_arr, out_prev]
    has_sink = sink is not None
    if has_sink:
        in_specs = [pl.BlockSpec(memory_space=pltpu.SMEM)] + in_specs
        args = [sink] + args

    def kern(*refs):
        refs = list(refs)
        del refs[4 if has_sink else 3]
        _ctx_attn_kernel(*refs, has_sink=has_sink, scale=scale, v_transposed=v_transposed)

    return pl.pallas_call(
        kern,
        out_shape=jax.ShapeDtypeStruct(out_prev.shape, out_prev.dtype),
        grid=(batch, heads),
        in_specs=in_specs,
        out_specs=pl.BlockSpec((l_len, LANES), lambda b, h: (b * (p_len // l_len), h)),
        input_output_aliases={len(args) - 1: 0},
        compiler_params=_cparams(("arbitrary", "arbitrary")),
        name="ctx_attn_sink" if has_sink else "ctx_attn",
    )(*args)


def _swa_kernel(sink_ref, q_ref, kp_ref, kc_ref, kn_ref, vp_ref, vc_ref, vn_ref, kx_ref, vx_ref,
                cq_ref, sq_ref, cp_ref, sp_ref, cn_ref, sn_ref, init_ref, o_ref, *, nb, scale):
    del init_ref
    n = pl.program_id(0)
    blk = SWA_BLOCK
    batch = q_ref.shape[0]
    cq, sq = cq_ref[...], sq_ref[...]
    cp, sp = cp_ref[...], sp_ref[...]
    cn, sn = cn_ref[...], sn_ref[...]
    ri = lax.broadcasted_iota(jnp.int32, (2 * blk, blk), 0) % blk
    ci = lax.broadcasted_iota(jnp.int32, (2 * blk, blk), 1)
    rr = lax.broadcasted_iota(jnp.int32, (2 * blk, 1), 0)
    mask_p = (ci >= ri) & (n > 0)
    mask_n = (ci <= ri) & (n < nb - 1)
    neg = jnp.float32(-jnp.inf)
    mx = lambda a: jnp.max(a, axis=-1, keepdims=True)
    sm = lambda a: jnp.sum(a, axis=-1, keepdims=True)
    for b in range(batch):
        for h in range(SWA_KV_HEADS):
            ks = slice(h * LANES, (h + 1) * LANES)
            q0 = _rope128(q_ref[b, :, 2 * h * LANES:(2 * h + 1) * LANES], cq, sq)
            q1 = _rope128(q_ref[b, :, (2 * h + 1) * LANES:(2 * h + 2) * LANES], cq, sq)
            qq = jnp.concatenate([q0, q1], axis=0).astype(BF16)
            kc = _rope128(kc_ref[b, :, ks], cq, sq).astype(BF16)
            kp = _rope128(kp_ref[b, :, ks], cp, sp).astype(BF16)
            kn = _rope128(kn_ref[b, :, ks], cn, sn).astype(BF16)
            s_c = _dot_nt(qq, kc) * scale
            s_p = jnp.where(mask_p, _dot_nt(qq, kp) * scale, neg)
            s_n = jnp.where(mask_n, _dot_nt(qq, kn) * scale, neg)
            s_x = _dot_nt(qq, kx_ref[b, :, ks].astype(BF16)) * scale
            sk = jnp.where(rr < blk, sink_ref[2 * h], sink_ref[2 * h + 1]).astype(F32)
            m = jnp.maximum(jnp.maximum(mx(s_c), mx(s_p)), jnp.maximum(mx(s_n), mx(s_x)))
            m = jnp.maximum(m, sk)
            p_c, p_p, p_n, p_x = (jnp.exp(a - m) for a in (s_c, s_p, s_n, s_x))
            den = sm(p_c) + sm(p_p) + sm(p_n) + sm(p_x) + jnp.exp(sk - m)
            o = (_dot(p_c.astype(BF16), vc_ref[b, :, ks].astype(BF16))
                 + _dot(p_p.astype(BF16), vp_ref[b, :, ks].astype(BF16))
                 + _dot(p_n.astype(BF16), vn_ref[b, :, ks].astype(BF16))
                 + _dot(p_x.astype(BF16), vx_ref[b, :, ks].astype(BF16))) / den
            o_ref[b, :, 2 * h * LANES:(2 * h + 1) * LANES] = o[:blk]
            o_ref[b, :, (2 * h + 1) * LANES:(2 * h + 2) * LANES] = o[blk:]


def _swa(pa, sink, cos_t, sin_t, batch, t_len, l_len):
    n = pa.shape[0]
    p_len = t_len + l_len
    blk = SWA_BLOCK
    nb = t_len // blk
    off = l_len // blk
    scale = float(HEAD_DIM ** -0.5)
    kv_w = SWA_KV_HEADS * HEAD_DIM
    q_w = SWA_HEADS * HEAD_DIM
    kcol, vcol = q_w // kv_w, q_w // kv_w + 1
    pa3 = pa.reshape(batch, p_len, pa.shape[1])
    prv = lambda j: jnp.maximum(j - 1, 0)
    nxt = lambda j: jnp.minimum(j + 1, nb - 1)
    kv = lambda col, rowf: pl.BlockSpec((batch, blk, kv_w), lambda j: (0, off + rowf(j), col))
    tab = lambda rowf: pl.BlockSpec((blk, LANES), lambda j: (rowf(j), 0))
    same = lambda j: j
    in_specs = [
        pl.BlockSpec(memory_space=pltpu.SMEM),
        pl.BlockSpec((batch, blk, q_w), lambda j: (0, off + j, 0)),
        kv(kcol, prv), kv(kcol, same), kv(kcol, nxt),
        kv(vcol, prv), kv(vcol, same), kv(vcol, nxt),
        pl.BlockSpec((batch, l_len, kv_w), lambda j: (0, 0, kcol)),
        pl.BlockSpec((batch, l_len, kv_w), lambda j: (0, 0, vcol)),
        tab(same), tab(same), tab(prv), tab(prv), tab(nxt), tab(nxt),
        pl.BlockSpec(memory_space=pl.ANY),
    ]
    out = pl.pallas_call(
        functools.partial(_swa_kernel, nb=nb, scale=scale),
        out_shape=jax.ShapeDtypeStruct((batch, p_len, q_w), F32),
        grid=(nb,),
        in_specs=in_specs,
        out_specs=pl.BlockSpec((batch, blk, q_w), lambda j: (0, off + j, 0)),
        input_output_aliases={len(in_specs) - 1: 0},
        compiler_params=_cparams(("arbitrary",)),
        name="swa",
    )(sink, pa3, pa3, pa3, pa3, pa3, pa3, pa3, pa3, pa3, cos_t, sin_t, cos_t, sin_t, cos_t, sin_t,
      jnp.zeros((batch, p_len, q_w), F32))
    return out.reshape(n, q_w)


def _chunk_index(d, c, n_l, n_t):
    bwd = jnp.where(c < n_l, n_l - 1 - c, n_l + (n_t - 1) - (c - n_l))
    return jnp.where(d == 0, c, bwd)


def _flip_iotas(d):
    row = lax.broadcasted_iota(jnp.int32, (CHUNK, CHUNK), 0)
    col = lax.broadcasted_iota(jnp.int32, (CHUNK, CHUNK), 1)
    rf = jnp.where(d == 0, row, CHUNK - 1 - row)
    cf = jnp.where(d == 0, col, CHUNK - 1 - col)
    return rf, cf


def _ret_kernel(s_ref, qf_ref, kf_ref, vf_ref, qb_ref, kb_ref, vb_ref, of_ref, ob_ref, st_ref):
    c = pl.program_id(0)
    batch = qf_ref.shape[0]

    @pl.when(c == 0)
    def _():
        st_ref[...] = jnp.zeros_like(st_ref)

    ks = float(RET_DK ** -0.5)
    dirs = ((qf_ref, kf_ref, vf_ref, of_ref), (qb_ref, kb_ref, vb_ref, ob_ref))
    for d, (q_ref, k_ref, v_ref, o_ref) in enumerate(dirs):
        rf, cf = _flip_iotas(d)
        rff = rf.astype(F32)
        dn = (rf - cf).astype(F32)
        for h in range(RET_HEADS):
            hs = slice(h * LANES, (h + 1) * LANES)
            sv = jnp.full((CHUNK, CHUNK), s_ref[d * RET_HEADS + h], F32)
            lg = jnp.log1p(-jnp.exp2(-sv))
            dec = jnp.where(dn >= 0, jnp.exp(dn * lg), 0.0)
            eq = jnp.exp((rff + 1.0) * lg)
            ek = jnp.exp((CHUNK - 1.0 - rff) * lg)
            a_chunk = jnp.exp(float(CHUNK) * lg)
            for b in range(batch):
                q = q_ref[b, :, hs]
                k = k_ref[b, :, hs] * ks
                v = v_ref[b, :, hs].astype(BF16)
                st = st_ref[d, b, h]
                a = _dot_nt(q.astype(BF16), k.astype(BF16)) * dec
                o = _dot(a.astype(BF16), v) + _dot_nt((q * eq).astype(BF16), st.astype(BF16))
                o_ref[b, :, hs] = o
                st_ref[d, b, h] = a_chunk * st + _dot_tn(v, (k * ek).astype(BF16))


def _hgrn_level_mats():
    c = CHUNK
    t = np.arange(c)[:, None]
    u = np.arange(c)[None, :]
    mats = [u <= t, u > t]
    m = c // 2
    while m >= 1:
        mid = (t // (2 * m)) * 2 * m + m
        second = (t % (2 * m)) >= m
        qrole = (u >= mid) & (u <= t)
        krole = (u > t) & (u <= mid - 1)
        mats.append(np.where(second, qrole, krole))
        m //= 2
    fwd = np.concatenate(mats, 0).astype(np.float32)
    bwd = np.concatenate([mm[::-1, ::-1] for mm in mats], 0).astype(np.float32)
    return np.stack([fwd, bwd], 0)


N_LEVELS = int(math.log2(CHUNK))


def _hgrn_kernel(m_ref, lb_ref, qf_ref, ff_ref, vf_ref, qb_ref, fb_ref, vb_ref, of_ref, ob_ref, st_ref):
    c = pl.program_id(0)
    batch = qf_ref.shape[0]

    @pl.when(c == 0)
    def _():
        st_ref[...] = jnp.zeros_like(st_ref)

    dirs = ((qf_ref, ff_ref, vf_ref, of_ref), (qb_ref, fb_ref, vb_ref, ob_ref))
    for d, (q_ref, f_ref, v_ref, o_ref) in enumerate(dirs):
        rf, cf = _flip_iotas(d)
        mst = m_ref[d]
        last = CHUNK - 1 if d == 0 else 0
        for h in range(HGRN_HEADS):
            hs = slice(h * LANES, (h + 1) * LANES)
            lb = lb_ref[d, :, hs]
            for b in range(batch):
                f = lb + (1.0 - lb) * (1.0 / (1.0 + jnp.exp(-f_ref[b, :, hs])))
                k = 1.0 - f
                g = jnp.log(f)
                q = _silu(q_ref[b, :, hs])
                v = v_ref[b, :, hs].astype(BF16)
                g_hi = g.astype(BF16)
                g_lo = (g - g_hi.astype(F32)).astype(BF16)
                x2 = _dot(mst, jnp.concatenate([g_hi, g_lo], axis=1))
                e = jnp.exp(x2[:, :LANES] + x2[:, LANES:])
                e_q = e[0:CHUNK]
                q_in = q * e_q
                k_st = k * e[CHUNK:2 * CHUNK]
                scores = jnp.where(rf == cf, _dot_nt(q.astype(BF16), k.astype(BF16)), 0.0)
                for lvl in range(N_LEVELS):
                    m = CHUNK >> (lvl + 1)
                    sh = N_LEVELS - lvl
                    el = e[(2 + lvl) * CHUNK:(3 + lvl) * CHUNK]
                    second = (rf & m) != 0
                    ql = jnp.where(second, q * el, 0.0).astype(BF16)
                    kl = jnp.where(second, 0.0, k * el).astype(BF16)
                    sl = _dot_nt(ql, kl)
                    scores = scores + jnp.where((rf >> sh) == (cf >> sh), sl, 0.0)
                st = st_ref[d, b, h]
                o = _dot(scores.astype(BF16), v) + _dot_nt(q_in.astype(BF16), st.astype(BF16))
                o_ref[b, :, hs] = o
                st_ref[d, b, h] = st * e_q[last:last + 1, :] + _dot_tn(v, k_st.astype(BF16))


def _scan_specs(batch, t_len, l_len):
    n_l, n_t = l_len // CHUNK, t_len // CHUNK

    def spec(d, col):
        return pl.BlockSpec((batch, CHUNK, WIDE), lambda c: (0, _chunk_index(d, c, n_l, n_t), col))

    return spec, n_l + n_t


def _retention(pb, ret_s, batch, t_len, l_len):
    n = pb.shape[0]
    p_len = t_len + l_len
    pb3 = pb.reshape(batch, p_len, pb.shape[1])
    spec, n_chunks = _scan_specs(batch, t_len, l_len)
    out_f, out_b = pl.pallas_call(
        _ret_kernel,
        out_shape=[jax.ShapeDtypeStruct((batch, p_len, WIDE), F32)] * 2,
        grid=(n_chunks,),
        in_specs=[pl.BlockSpec(memory_space=pltpu.SMEM)]
                 + [spec(0, col) for col in (0, 1, 2)] + [spec(1, col) for col in (0, 1, 2)],
        out_specs=[spec(0, 0), spec(1, 0)],
        scratch_shapes=[pltpu.VMEM((2, batch, RET_HEADS, CHUNK, CHUNK), F32)],
        compiler_params=_cparams(("arbitrary",)),
        name="retention_scan",
    )(ret_s.reshape(-1), pb3, pb3, pb3, pb3, pb3, pb3)
    return out_f.reshape(n, WIDE), out_b.reshape(n, WIDE)


def _hgrn(pd, lower_bounds, mats, batch, t_len, l_len):
    n = pd.shape[0]
    p_len = t_len + l_len
    pd3 = pd.reshape(batch, p_len, pd.shape[1])
    lb3 = lower_bounds.reshape(2, 1, WIDE)
    spec, n_chunks = _scan_specs(batch, t_len, l_len)
    out_f, out_b = pl.pallas_call(
        _hgrn_kernel,
        out_shape=[jax.ShapeDtypeStruct((batch, p_len, WIDE), F32)] * 2,
        grid=(n_chunks,),
        in_specs=[pl.BlockSpec(mats.shape, lambda c: (0, 0, 0)),
                  pl.BlockSpec(lb3.shape, lambda c: (0, 0, 0)),
                  spec(0, 0), spec(0, 1), spec(0, 3), spec(1, 0), spec(1, 2), spec(1, 3)],
        out_specs=[spec(0, 0), spec(1, 0)],
        scratch_shapes=[pltpu.VMEM((2, batch, HGRN_HEADS, CHUNK, CHUNK), F32)],
        compiler_params=_cparams(("arbitrary",)),
        name="hgrn_scan",
    )(mats, lb3, pd3, pd3, pd3, pd3, pd3, pd3)
    return out_f.reshape(n, WIDE), out_b.reshape(n, WIDE)


R_E1, R_E2, R_G1, R_G2, R_S1, R_S2 = 0, 1, 2, 3, 4, 5


def _route(logits, count_ref, active):
    tm = logits.shape[0]
    lane = lax.broadcasted_iota(jnp.int32, logits.shape, 1)
    lanef = lane.astype(F32)
    big = jnp.float32(1e9)
    neg = jnp.float32(-jnp.inf)
    mx = lambda a: jnp.max(a, axis=-1, keepdims=True)
    mn = lambda a: jnp.min(a, axis=-1, keepdims=True)
    sm = lambda a: jnp.sum(a, axis=-1, keepdims=True)
    gl = jnp.where(lane < N_GROUPS, logits, neg)
    gm = mx(gl)
    p_grp = 1.0 / sm(jnp.exp(gl - gm))
    grp = mn(jnp.where(gl == gm, lanef, big))
    lo = N_GROUPS + grp * EXPERTS_PER_GROUP
    ing = (lanef >= lo) & (lanef < lo + EXPERTS_PER_GROUP)
    el = jnp.where(ing, logits, neg)
    ee = jnp.exp(el - mx(el))
    p = ee / sm(ee)
    pm = jnp.where(ing, p, -1.0)
    p1 = mx(pm)
    i1 = mn(jnp.where(pm == p1, lanef, big))
    pm2 = jnp.where(lanef == i1, -1.0, pm)
    p2 = mx(pm2)
    i2 = mn(jnp.where(pm2 == p2, lanef, big))
    den = p1 + p2
    g1 = p_grp * p1 / den
    g2 = p_grp * p2 / den
    e1 = i1 - N_GROUPS
    e2 = i2 - N_GROUPS
    oh1 = lanef == e1
    oh2 = lanef == e2
    both = jnp.where(oh1 | oh2, active, 0.0)
    ri = lax.broadcasted_iota(jnp.int32, (tm, tm), 0)
    ci = lax.broadcasted_iota(jnp.int32, (tm, tm), 1)
    earlier = jnp.where(ci < ri, 1.0, 0.0).astype(BF16)
    before = _dot(earlier, both.astype(BF16)) + count_ref[0:1, :]
    s1 = sm(jnp.where(oh1, before, 0.0))
    s2 = sm(jnp.where(oh2, before, 0.0))
    count_ref[...] = count_ref[...] + jnp.sum(both, axis=0, keepdims=True)
    out = jnp.zeros(logits.shape, F32)
    for ln, val in ((R_E1, e1), (R_E2, e2), (R_G1, g1), (R_G2, g2), (R_S1, s1), (R_S2, s2)):
        out = jnp.where(lane == ln, val, out)
    return out


def _out_kernel(x_ref, a_ref, bf_ref, bb_ref, c_ref, df_ref, db_ref, rg_ref, hg_ref, g1_ref, sh2_ref, sc2_ref,
                lng_ref, lnb_ref, rgn_ref, hgn_ref, wo_ref, wr_ref,
                xo_ref, h2_ref, r_ref, cnt_ref, *, alpha, tiles_per_batch, route_ctx):
    i = pl.program_id(0)

    @pl.when(i == 0)
    def _():
        cnt_ref[...] = jnp.zeros_like(cnt_ref)

    bsum = bf_ref[...] + bb_ref[...]
    dsum = df_ref[...] + db_ref[...]
    rgn = rgn_ref[...]
    hgn = hgn_ref[...]
    parts_b, parts_d = [], []
    for h in range(4):
        hs = slice(h * LANES, (h + 1) * LANES)
        parts_b.append(_ln_rows(bsum[:, hs]))
        dh = dsum[:, hs]
        parts_d.append(dh * lax.rsqrt(jnp.mean(dh * dh, axis=-1, keepdims=True) + EPS))
    bo = _silu(rg_ref[...]) * (jnp.concatenate(parts_b, axis=1) * rgn)
    do = _silu(hg_ref[...]) * (jnp.concatenate(parts_d, axis=1) * hgn)
    mix = jnp.concatenate([a_ref[...], bo, c_ref[...], do], axis=1).astype(BF16)
    o = _dot(mix, wo_ref[...])
    y = alpha * x_ref[...] + g1_ref[0] * o
    xn = _ln_rows(y) * lng_ref[...] + lnb_ref[...]
    xo_ref[...] = xn
    h2 = _ln_rows(xn) * (1.0 + sc2_ref[0]) + sh2_ref[0]
    h2_ref[...] = _pack_bf16_pairs(h2)
    if route_ctx:
        active = jnp.float32(1.0)
    else:
        active = jnp.where(i % tiles_per_batch == 0, 0.0, 1.0).astype(F32)
    r_ref[...] = _route(_dot(h2.astype(BF16), wr_ref[...]), cnt_ref, active)


def _out_proj(stream, a, b_f, b_b, c, d_f, d_b, pb, pd, g1, sh2, sc2, ln_g, ln_b, ret_gn, hgrn_gn, w_out, w_r,
              tiles_per_batch, alpha, route_ctx):
    n, dm = stream.shape
    tm = ROW_TILE
    gmap = lambda i: (_group_of_tile(i, tiles_per_batch), 0, 0)
    row = lambda w: pl.BlockSpec((tm, w), lambda i: (i, 0))
    const = lambda w: pl.BlockSpec((1, w), lambda i: (0, 0))
    resident = pl.BlockSpec(memory_space=pltpu.VMEM)
    return pl.pallas_call(
        functools.partial(_out_kernel, alpha=alpha, tiles_per_batch=tiles_per_batch, route_ctx=route_ctx),
        out_shape=[jax.ShapeDtypeStruct((n, dm), F32), jax.ShapeDtypeStruct((n, dm // 2), jnp.uint32),
                   jax.ShapeDtypeStruct((n, LANES), F32), jax.ShapeDtypeStruct((SUBLANES, LANES), F32)],
        grid=(n // tm,),
        in_specs=[row(dm), row(WIDE), row(WIDE), row(WIDE), row(WIDE), row(WIDE), row(WIDE),
                  pl.BlockSpec((tm, WIDE), lambda i: (i, 3)),
                  pl.BlockSpec((tm, WIDE), lambda i: (i, 4)),
                  pl.BlockSpec((1, 1, dm), gmap), pl.BlockSpec((1, 1, dm), gmap),
                  pl.BlockSpec((1, 1, dm), gmap),
                  const(dm), const(dm), const(WIDE), const(WIDE), resident, resident],
        out_specs=[row(dm), row(dm // 2), row(LANES),
                   pl.BlockSpec((SUBLANES, LANES), lambda i: (0, 0))],
        compiler_params=_cparams(("arbitrary",)),
        name="mix_outproj_norm_route",
    )(stream, a, b_f, b_b, c, d_f, d_b, pb, pd, g1, sh2, sc2, ln_g, ln_b, ret_gn, hgrn_gn, w_out, w_r)


def _route_plan(route, counts_f, n_blocks):
    rows = MOE_ROWS
    counts = counts_f[0, :N_EXPERTS].astype(jnp.int32)
    padded = (counts + rows - 1) // rows * rows
    pend = jnp.cumsum(padded)
    pstart = pend - padded
    eid = route[:, R_E1:R_E2 + 1].astype(jnp.int32)
    rank = route[:, R_S1:R_S2 + 1].astype(jnp.int32)
    onehot = eid[:, :, None] == jnp.arange(N_EXPERTS, dtype=jnp.int32)
    pos = jnp.sum(jnp.where(onehot, pstart, 0), axis=-1) + rank
    blk_start = jnp.arange(n_blocks, dtype=jnp.int32) * rows
    blk_e = jnp.minimum(jnp.sum(blk_start[:, None] >= pend[None, :], axis=-1), N_EXPERTS - 1)
    blk_valid = (blk_start < pend[-1]).astype(jnp.int32)
    pad_start = pstart + counts
    return pos.reshape(-1).astype(jnp.int32), blk_e.astype(jnp.int32), blk_valid, pad_start.astype(jnp.int32)


def _dispatch_kernel(pos_ref, pad_ref, bv_ref, h_ref, x_hbm, zbuf, sem, *, tile_of_step, n_blocks):
    i = pl.program_id(0)
    tm = ROW_TILE
    rows = MOE_ROWS

    def zero_copy(e):
        start = pl.multiple_of(pad_ref[e] & ~(SUBLANES - 1), SUBLANES)
        return pltpu.make_async_copy(zbuf, x_hbm.at[pl.ds(start, rows + SUBLANES)], sem.at[1])

    def unused_block_copy(j):
        start = pl.multiple_of(j * rows, rows)
        return pltpu.make_async_copy(zbuf.at[pl.ds(0, rows)], x_hbm.at[pl.ds(start, rows)], sem.at[1])

    @pl.when(i == 0)
    def _():
        zbuf[...] = jnp.zeros_like(zbuf)

        def fill(j, carry):
            @pl.when(bv_ref[j] == 0)
            def _():
                unused_block_copy(j).start()
                unused_block_copy(j).wait()
            return carry

        lax.fori_loop(0, n_blocks + 2, fill, 0)
        for e in range(N_EXPERTS):
            zero_copy(e).start()
        for e in range(N_EXPERTS):
            zero_copy(e).wait()

    def row_copy(r, k):
        p = pos_ref[(tile_of_step(i) * tm + r) * TOP_K + k]
        return pltpu.make_async_copy(h_ref.at[r], x_hbm.at[p], sem.at[0])

    for r in range(tm):
        for k in range(TOP_K):
            row_copy(r, k).start()
    for r in range(tm):
        for k in range(TOP_K):
            row_copy(r, k).wait()


def _tile_schedule(n, tiles_per_batch, latent_only, batch):
    if latent_only:
        lat = tiles_per_batch - 1
        return batch * lat, (lambda s: (s // lat) * tiles_per_batch + 1 + s % lat)
    return n // ROW_TILE, (lambda s: s)


def _dispatch(h2, pos, pad_start, blk_valid, tiles_per_batch, latent_only, batch):
    n, dm = h2.shape
    tm = ROW_TILE
    n_blocks = blk_valid.shape[0]
    blk_valid = jnp.concatenate([blk_valid, jnp.zeros((2,), jnp.int32)])
    n_tiles, tile_of_step = _tile_schedule(n, tiles_per_batch, latent_only, batch)
    grid_spec = pltpu.PrefetchScalarGridSpec(
        num_scalar_prefetch=3,
        grid=(n_tiles,),
        in_specs=[pl.BlockSpec((tm, dm), lambda i, p, z, v: (tile_of_step(i), 0))],
        out_specs=pl.BlockSpec(memory_space=pl.ANY),
        scratch_shapes=[pltpu.VMEM((MOE_ROWS + SUBLANES, dm), h2.dtype), pltpu.SemaphoreType.DMA((2,))],
    )
    return pl.pallas_call(
        functools.partial(_dispatch_kernel, tile_of_step=tile_of_step, n_blocks=n_blocks),
        out_shape=jax.ShapeDtypeStruct(((n_blocks + 2) * MOE_ROWS, dm), h2.dtype),
        grid_spec=grid_spec,
        compiler_params=_cparams(("arbitrary",)),
        name="moe_dispatch",
    )(pos, pad_start, blk_valid, h2)


def _moe_kernel(be_ref, bv_ref, x_ref, w1_ref, w3_ref, w2_ref, y_ref, w1b, w3b, w2b):
    j = pl.program_id(0)
    first = (j == 0) | (be_ref[j] != be_ref[jnp.maximum(j - 1, 0)])

    @pl.when(first & (bv_ref[j] == 1))
    def _():
        w1b[...] = w1_ref[0, 0].astype(BF16)
        w3b[...] = w3_ref[0, 0].astype(BF16)
        w2b[...] = w2_ref[0, 0].astype(BF16)

    @pl.when(bv_ref[j] == 1)
    def _():
        x = _unpack_bf16_pairs(x_ref[...]).astype(BF16)
        hmid = _silu(_dot(x, w1b[...])) * _dot(x, w3b[...])
        y_ref[...] = _pack_bf16_pairs(_dot(hmid.astype(BF16), w2b[...]))

    @pl.when(bv_ref[j] == 0)
    def _():
        y_ref[...] = jnp.zeros_like(y_ref)


def _moe_experts(xbuf, blk_e, blk_valid, w1, w3, w2, layer):
    dm = w1.shape[2]
    ff = w1.shape[-1]
    rows = MOE_ROWS
    n_blocks = blk_e.shape[0]
    wmap = lambda j, be, bv: (layer, be[j], 0, 0)
    grid_spec = pltpu.PrefetchScalarGridSpec(
        num_scalar_prefetch=2,
        grid=(n_blocks,),
        in_specs=[
            pl.BlockSpec((rows, dm // 2), lambda j, be, bv: (j * bv[j], 0)),
            pl.BlockSpec((1, 1, dm, ff), wmap),
            pl.BlockSpec((1, 1, dm, ff), wmap),
            pl.BlockSpec((1, 1, ff, dm), wmap),
        ],
        out_specs=pl.BlockSpec((rows, dm // 2), lambda j, be, bv: (j, 0)),
        scratch_shapes=[pltpu.VMEM((dm, ff), BF16), pltpu.VMEM((dm, ff), BF16), pltpu.VMEM((ff, dm), BF16)],
    )
    return pl.pallas_call(
        _moe_kernel,
        out_shape=jax.ShapeDtypeStruct((n_blocks * rows, dm // 2), jnp.uint32),
        grid_spec=grid_spec,
        compiler_params=_cparams(("arbitrary",)),
        name="moe_experts",
    )(blk_e, blk_valid, xbuf, w1, w3, w2)


def _combine_kernel(pos_ref, x_ref, r_ref, g2_ref, lng_ref, lnb_ref, y_hbm, o_ref, ybuf, sem,
                    *, n_tiles, tile_of_step, alpha):
    i = pl.program_id(0)
    slot = i % 2
    tm = ROW_TILE

    def row_copy(step, sl, r, k):
        p = pos_ref[(tile_of_step(step) * tm + r) * TOP_K + k]
        return pltpu.make_async_copy(y_hbm.at[p], ybuf.at[sl, k, r], sem.at[sl])

    def start_gather(step, sl):
        for r in range(tm):
            for k in range(TOP_K):
                row_copy(step, sl, r, k).start()

    @pl.when(i == 0)
    def _():
        start_gather(0, 0)

    @pl.when(i + 1 < n_tiles)
    def _():
        start_gather(jnp.minimum(i + 1, n_tiles - 1), 1 - slot)

    for r in range(tm):
        for k in range(TOP_K):
            row_copy(i, slot, r, k).wait()
    route = r_ref[...]
    y = (_unpack_bf16_pairs(ybuf[slot, 0]) * route[:, R_G1:R_G1 + 1]
         + _unpack_bf16_pairs(ybuf[slot, 1]) * route[:, R_G2:R_G2 + 1])
    z = alpha * x_ref[...] + g2_ref[0] * y
    o_ref[...] = _ln_rows(z) * lng_ref[...] + lnb_ref[...]


def _combine(x_new, route, ybuf, pos, g2, ln_g, ln_b, tiles_per_batch, alpha, latent_only, batch):
    n, dm = x_new.shape
    tm = ROW_TILE
    n_tiles, tile_of_step = _tile_schedule(n, tiles_per_batch, latent_only, batch)
    gmap = lambda i, p: (_group_of_tile(tile_of_step(i), tiles_per_batch), 0, 0)
    grid_spec = pltpu.PrefetchScalarGridSpec(
        num_scalar_prefetch=1,
        grid=(n_tiles,),
        in_specs=[
            pl.BlockSpec((tm, dm), lambda i, p: (tile_of_step(i), 0)),
            pl.BlockSpec((tm, LANES), lambda i, p: (tile_of_step(i), 0)),
            pl.BlockSpec((1, 1, dm), gmap),
            pl.BlockSpec((1, dm), lambda i, p: (0, 0)),
            pl.BlockSpec((1, dm), lambda i, p: (0, 0)),
            pl.BlockSpec(memory_space=pl.ANY),
        ],
        out_specs=pl.BlockSpec((tm, dm), lambda i, p: (i, 0)),
        scratch_shapes=[pltpu.VMEM((2, TOP_K, tm, dm // 2), jnp.uint32), pltpu.SemaphoreType.DMA((2,))],
    )
    return pl.pallas_call(
        functools.partial(_combine_kernel, n_tiles=n_tiles, tile_of_step=tile_of_step, alpha=alpha),
        out_shape=jax.ShapeDtypeStruct((n_tiles * tm, dm), F32),
        grid_spec=grid_spec,
        compiler_params=_cparams(("arbitrary",)),
        name="moe_combine_norm",
    )(pos, x_new, route, g2, ln_g, ln_b, ybuf)


def _pad_cols(w, width):
    return jnp.pad(w, ((0, 0), (0, width - w.shape[1])))


def _split_w_in(w_in_l):
    wa = w_in_l[:, :SEG_A]
    wb = w_in_l[:, SEG_A:SEG_A + SEG_B]
    c0 = SEG_A + SEG_B
    c_true = MLA_Q_RANK + MLA_KV_RANK + MLA_ROPE
    wc = _pad_cols(w_in_l[:, c0:c0 + c_true], SEG_C)
    wd = w_in_l[:, c0 + c_true:]
    return [w.astype(BF16) for w in (wa, wb, wc, wd)]


def kernel(x, c, ctx, c_ctx, w_ada, b_ada, w_in, swa_sink, ret_decay_exp, ret_gn, mla_q_norm, mla_kv_norm,
           mla_w_uq, mla_w_ukv, hgrn_lb_logits, hgrn_gn, w_out, ln1_g, ln1_b, router_group, router_expert,
           moe_w1, moe_w3, moe_w2, ln2_g, ln2_b):
    batch, t_len, dm = x.shape
    l_len = ctx.shape[1]
    depth = w_in.shape[0]
    p_len = t_len + l_len
    n = batch * p_len
    assert batch == 2 and l_len == ROW_TILE and t_len % 256 == 0 and t_len % GRID_W == 0
    tiles_per_batch = p_len // ROW_TILE
    alpha = float((2 * depth) ** 0.25)

    cos_swa, sin_swa, cos_rows, sin_rows = _rope_tables(t_len, l_len, batch)
    mats = jnp.asarray(_hgrn_level_mats(), BF16)

    cvec = jnp.zeros((SUBLANES, dm), F32).at[0:batch].set(c).at[batch].set(c_ctx)
    mod = _ada(cvec, w_ada, b_ada)

    lbp = jax.nn.softmax(hgrn_lb_logits.astype(F32), axis=0)
    lower_bounds = jnp.cumsum(lbp, axis=0) - lbp[0]

    stream = jnp.concatenate([ctx, x], axis=1).reshape(n, dm)

    out = None
    for l in range(depth):
        need_ctx = l < depth - 1
        m6 = mod[l, :3].reshape(3, 6, dm)
        sh1, sc1, g1, sh2, sc2, g2 = (m6[:, k].reshape(3, 1, dm) for k in range(6))

        pa, pb, pc, pd = _proj(stream, sh1, sc1, _split_w_in(w_in[l]), tiles_per_batch)

        wq = mla_w_uq[l].reshape(MLA_Q_RANK, MLA_HEADS, MLA_NOPE + MLA_ROPE)
        wq = jnp.pad(wq, ((0, 0), (0, 0), (0, 2 * LANES - (MLA_NOPE + MLA_ROPE))))
        wq = wq.reshape(MLA_Q_RANK, MLA_HEADS * 2 * LANES).astype(BF16)
        wkv = mla_w_ukv[l].reshape(MLA_KV_RANK, MLA_HEADS, MLA_NOPE + MLA_V)
        wk = wkv[:, :, :MLA_NOPE].reshape(MLA_KV_RANK, -1).astype(BF16)
        wv_t = wkv[:, :, MLA_NOPE:].reshape(MLA_KV_RANK, -1).T.astype(BF16)
        qf, kf, vt = _mla_up(pc, mla_q_norm[l].reshape(1, -1), mla_kv_norm[l].reshape(1, -1),
                             wq, wk, wv_t, cos_rows, sin_rows)

        a = _swa(pa, swa_sink[l], cos_swa, sin_swa, batch, t_len, l_len)
        cc = _mla_attn(qf, kf, vt, batch, t_len, l_len)
        if need_ctx:
            a = _ctx_attn(a, pa, pa, pa, 0, 4, 6, SWA_HEADS, SWA_HEADS // SWA_KV_HEADS, batch, p_len,
                          l_len, swa_sink[l], float(HEAD_DIM ** -0.5), LANES, False)
            cc = _ctx_attn(cc, qf, kf, vt, 0, 0, 0, MLA_HEADS, 1, batch, p_len, l_len, None, 1.0,
                           2 * LANES, True)
        b_f, b_b = _retention(pb, ret_decay_exp[l], batch, t_len, l_len)
        d_f, d_b = _hgrn(pd, lower_bounds[l], mats, batch, t_len, l_len)

        w_r = _pad_cols(jnp.concatenate([router_group[l], router_expert[l]], axis=1), LANES).astype(BF16)
        x_new, h2, route, counts = _out_proj(
            stream, a, b_f, b_b, cc, d_f, d_b, pb, pd, g1, sh2, sc2,
            ln1_g[l].reshape(1, -1), ln1_b[l].reshape(1, -1),
            ret_gn[l].reshape(1, -1), hgrn_gn[l].reshape(1, -1),
            w_out[l].astype(BF16), w_r, tiles_per_batch, alpha, need_ctx)

        n_active = n if need_ctx else batch * t_len
        n_blocks = (n_active * TOP_K + N_EXPERTS * (MOE_ROWS - 1) + MOE_ROWS - 1) // MOE_ROWS
        pos, blk_e, blk_valid, pad_start = _route_plan(route, counts, n_blocks)
        xbuf = _dispatch(h2, pos, pad_start, blk_valid, tiles_per_batch, not need_ctx, batch)
        ybuf = _moe_experts(xbuf, blk_e, blk_valid, moe_w1, moe_w3, moe_w2, l)
        res = _combine(x_new, route, ybuf, pos, g2, ln2_g[l].reshape(1, -1), ln2_b[l].reshape(1, -1),
                       tiles_per_batch, alpha, not need_ctx, batch)
        if need_ctx:
            stream = res
        else:
            out = res.reshape(batch, t_len, dm)
    return out
```

```python
import functools
import math

import numpy as np
import jax
import jax.numpy as jnp
from jax import lax
from jax.experimental import pallas as pl
from jax.experimental.pallas import tpu as pltpu

F32 = jnp.float32
BF16 = jnp.bfloat16

GRID_W = 64
HEAD_DIM = 128
SWA_HEADS = 4
SWA_KV_HEADS = 2
SWA_BLOCK = 128
RET_HEADS = 4
RET_DK = 128
MLA_HEADS = 4
MLA_Q_RANK = 384
MLA_KV_RANK = 128
MLA_NOPE = 128
MLA_ROPE = 64
MLA_V = 128
HGRN_HEADS = 4
N_GROUPS = 4
EXPERTS_PER_GROUP = 8
N_EXPERTS = N_GROUPS * EXPERTS_PER_GROUP
TOP_K = 2
ROPE_BASE = 10000.0
EPS = 1e-6

LANES = 128
SUBLANES = 8
VMEM_LIMIT_BYTES = 56 * 1024 * 1024
PROJ_VMEM_LIMIT_BYTES = 60 * 1024 * 1024

CHUNK = 128
ROW_TILE = 256
MOE_ROWS = 256
MLA_KEY_CHUNK = 256
MLA_HEADS_PER_STEP = 2
WIDE = 512

SEG_A = 1024
SEG_B = 2048
SEG_C = 640
SEG_D = 2560


def _cparams(sem, vmem=VMEM_LIMIT_BYTES):
    return pltpu.CompilerParams(dimension_semantics=sem, vmem_limit_bytes=vmem)


def _ln_rows(x):
    mu = jnp.mean(x, axis=-1, keepdims=True)
    xc = x - mu
    var = jnp.mean(xc * xc, axis=-1, keepdims=True)
    return xc * lax.rsqrt(var + EPS)


def _silu(x):
    return x * (1.0 / (1.0 + jnp.exp(-x)))


def _dot(a, b):
    return jnp.dot(a, b, preferred_element_type=F32)


def _pack_bf16_pairs(x):
    k = x.shape[1] // 2
    hi = lax.bitcast_convert_type(x[:, :k].astype(BF16).astype(F32), jnp.uint32)
    lo = lax.bitcast_convert_type(x[:, k:].astype(BF16).astype(F32), jnp.uint32)
    return hi | (lo >> 16)


def _unpack_bf16_pairs(w):
    hi = lax.bitcast_convert_type(w & jnp.uint32(0xFFFF0000), F32)
    lo = lax.bitcast_convert_type(w << 16, F32)
    return jnp.concatenate([hi, lo], axis=1)


def _dot_nt(a, b):
    return lax.dot_general(a, b, (((1,), (1,)), ((), ())), preferred_element_type=F32)


def _dot_tn(a, b):
    return lax.dot_general(a, b, (((0,), (0,)), ((), ())), preferred_element_type=F32)


def _ada_kernel(c_ref, w_ref, b_ref, o_ref):
    s = _silu(c_ref[...]).astype(BF16)
    o_ref[0] = _dot(s, w_ref[0].astype(BF16)) + b_ref[0]


def _ada(cvec, w_ada, b_ada):
    depth, d, n6 = w_ada.shape
    tn = n6 // 8
    return pl.pallas_call(
        _ada_kernel,
        out_shape=jax.ShapeDtypeStruct((depth, SUBLANES, n6), F32),
        grid=(depth, n6 // tn),
        in_specs=[
            pl.BlockSpec((SUBLANES, d), lambda l, j: (0, 0)),
            pl.BlockSpec((1, d, tn), lambda l, j: (l, 0, j)),
            pl.BlockSpec((1, 1, tn), lambda l, j: (l, 0, j)),
        ],
        out_specs=pl.BlockSpec((1, SUBLANES, tn), lambda l, j: (l, 0, j)),
        compiler_params=_cparams(("arbitrary", "arbitrary")),
        name="ada_mod",
    )(cvec, w_ada, b_ada.reshape(depth, 1, n6))


SEG_WIDTHS = (SEG_A, SEG_B, SEG_C, SEG_D)
SEG_C_TRUE = MLA_Q_RANK + MLA_KV_RANK + MLA_ROPE
IN_COLS = SEG_A + SEG_B + SEG_C_TRUE + SEG_D
IN_COLS_PAD = sum(SEG_WIDTHS)


def _w_in_prep_kernel(w_ref, o_ref):
    c_end = SEG_A + SEG_B + SEG_C_TRUE
    d_start = SEG_A + SEG_B + SEG_C
    o_ref[:, 0:c_end] = w_ref[0, :, 0:c_end].astype(BF16)
    o_ref[:, c_end:d_start] = jnp.zeros((o_ref.shape[0], d_start - c_end), BF16)
    o_ref[:, d_start:IN_COLS_PAD] = w_ref[0, :, c_end:IN_COLS].astype(BF16)


def _w_in_prep(w_in, layer):
    _, d, cols = w_in.shape
    assert cols == IN_COLS
    rows = ROW_TILE
    return pl.pallas_call(
        _w_in_prep_kernel,
        out_shape=jax.ShapeDtypeStruct((d, IN_COLS_PAD), BF16),
        grid=(d // rows,),
        in_specs=[pl.BlockSpec((1, rows, cols), lambda i: (layer, i, 0))],
        out_specs=pl.BlockSpec((rows, IN_COLS_PAD), lambda i: (i, 0)),
        compiler_params=_cparams(("arbitrary",)),
        name="w_in_prep",
    )(w_in)


def _proj_kernel(*refs, from_inputs, tiles_per_batch):
    if from_inputs:
        x_ref, ctx_ref, sh_ref, sc_ref, w_ref, oa, ob, oc, od, so = refs
        is_ctx = pl.program_id(0) % tiles_per_batch == 0
        x = jnp.where(is_ctx, ctx_ref[0], x_ref[0])
        so[...] = x
    else:
        x_ref, sh_ref, sc_ref, w_ref, oa, ob, oc, od = refs
        x = x_ref[...]
    y = _ln_rows(x)
    h = (y * (1.0 + sc_ref[0]) + sh_ref[0]).astype(BF16)
    base = 0
    for o_ref in (oa, ob, oc, od):
        width = o_ref.shape[1]
        for j in range(0, width, WIDE):
            cw = min(WIDE, width - j)
            o_ref[:, j:j + cw] = _dot(h, w_ref[:, base + j:base + j + cw])
        base += width


def _group_of_tile(i, tiles_per_batch):
    return jnp.where(i % tiles_per_batch == 0, 2, i // tiles_per_batch)


def _proj(stream, x, ctx, shift, scale, w_pad, tiles_per_batch):
    from_inputs = stream is None
    tm = ROW_TILE
    if from_inputs:
        batch, t_len, d = x.shape
        n = batch * (t_len + ctx.shape[1])
        tpb = tiles_per_batch
        row_specs = [
            pl.BlockSpec((1, tm, d), lambda i: (i // tpb, jnp.maximum(i % tpb - 1, 0), 0)),
            pl.BlockSpec((1, tm, d), lambda i: (i // tpb, 0, 0)),
        ]
        row_args = [x, ctx]
    else:
        n, d = stream.shape
        row_specs = [pl.BlockSpec((tm, d), lambda i: (i, 0))]
        row_args = [stream]
    gmap = lambda i: (_group_of_tile(i, tiles_per_batch), 0, 0)
    widths = list(SEG_WIDTHS) + ([d] if from_inputs else [])
    return pl.pallas_call(
        functools.partial(_proj_kernel, from_inputs=from_inputs, tiles_per_batch=tiles_per_batch),
        out_shape=[jax.ShapeDtypeStruct((n, w), F32) for w in widths],
        grid=(n // tm,),
        in_specs=row_specs + [
            pl.BlockSpec((1, 1, d), gmap),
            pl.BlockSpec((1, 1, d), gmap),
            pl.BlockSpec(memory_space=pltpu.VMEM),
        ],
        out_specs=[pl.BlockSpec((tm, w), lambda i: (i, 0)) for w in widths],
        compiler_params=_cparams(("arbitrary",), vmem=PROJ_VMEM_LIMIT_BYTES),
        name="ln_mod_inproj",
    )(*row_args, shift, scale, w_pad)


def _rope128(x, cos, sin):
    return x * cos + pltpu.roll(x, 64, 1) * sin


def _rope64(x, cos, sin):
    lane = lax.broadcasted_iota(jnp.int32, x.shape, 1)
    rot = jnp.where((lane % 64) < 32, pltpu.roll(x, 96, 1), pltpu.roll(x, 32, 1))
    return x * cos + rot * sin


def _rope_tables(t_len, l_len, batch):
    rows = t_len // GRID_W
    row = np.repeat(np.arange(rows), GRID_W).astype(np.float32)
    col = np.tile(np.arange(GRID_W), rows).astype(np.float32)

    def angles(rot_dim):
        n_freq = rot_dim // 4
        inv = (ROPE_BASE ** (-np.arange(n_freq, dtype=np.float32) / n_freq)).astype(np.float32)
        return np.concatenate([row[:, None] * inv, col[:, None] * inv], -1).astype(np.float32)

    a_swa = angles(HEAD_DIM)
    cos_swa = np.concatenate([np.cos(a_swa), np.cos(a_swa)], -1)
    sin_swa = np.concatenate([-np.sin(a_swa), np.sin(a_swa)], -1)
    a_mla = angles(MLA_ROPE)
    cos_m = np.concatenate([np.cos(a_mla), np.cos(a_mla), np.ones((t_len, 64), np.float32)], -1)
    sin_m = np.concatenate([-np.sin(a_mla), np.sin(a_mla), np.zeros((t_len, 64), np.float32)], -1)
    ones = np.ones((l_len, LANES), np.float32)
    zeros = np.zeros((l_len, LANES), np.float32)
    cos_rows = np.concatenate([np.concatenate([ones, cos_m], 0)] * batch, 0)
    sin_rows = np.concatenate([np.concatenate([zeros, sin_m], 0)] * batch, 0)
    return (jnp.asarray(cos_swa, F32), jnp.asarray(sin_swa, F32),
            jnp.asarray(cos_rows, F32), jnp.asarray(sin_rows, F32))


def _mla_up_kernel(pc_ref, qn_ref, kvn_ref, wq_ref, wk_ref, wv_ref, cos_ref, sin_ref,
                   q_out, k_out, v_out, *, scale):
    pc = pc_ref[...]
    cq = pc[:, :MLA_Q_RANK]
    ckv = pc[:, MLA_Q_RANK:MLA_Q_RANK + MLA_KV_RANK]
    kr = pc[:, MLA_Q_RANK + MLA_KV_RANK:]
    cos = cos_ref[...]
    sin = sin_ref[...]

    def rms(x, g):
        return x * lax.rsqrt(jnp.mean(x * x, axis=-1, keepdims=True) + EPS) * g

    qh = _dot(rms(cq, qn_ref[...]).astype(BF16), wq_ref[...])
    ckn = rms(ckv, kvn_ref[...]).astype(BF16)
    kh = _dot(ckn, wk_ref[...])
    v_out[...] = _dot_nt(wv_ref[...], ckn).astype(BF16)
    kr_rot = _rope64(kr, cos, sin).astype(BF16)
    for h in range(MLA_HEADS):
        base = h * 2 * LANES
        q_out[:, base:base + LANES] = (qh[:, base:base + LANES] * scale).astype(BF16)
        q_out[:, base + LANES:base + 2 * LANES] = (
            _rope64(qh[:, base + LANES:base + 2 * LANES], cos, sin) * scale).astype(BF16)
        k_out[:, base:base + LANES] = kh[:, h * LANES:(h + 1) * LANES].astype(BF16)
        k_out[:, base + LANES:base + 2 * LANES] = kr_rot


def _mla_up(pc, q_norm, kv_norm, wq, wk, wv, cos_rows, sin_rows):
    n = pc.shape[0]
    tm = ROW_TILE
    scale = float((MLA_NOPE + MLA_ROPE) ** -0.5 * math.log2(math.e))
    const2 = lambda i: (0, 0)
    return pl.pallas_call(
        functools.partial(_mla_up_kernel, scale=scale),
        out_shape=[jax.ShapeDtypeStruct((n, MLA_HEADS * 2 * LANES), BF16),
                   jax.ShapeDtypeStruct((n, MLA_HEADS * 2 * LANES), BF16),
                   jax.ShapeDtypeStruct((MLA_HEADS * MLA_V, n), BF16)],
        grid=(n // tm,),
        in_specs=[
            pl.BlockSpec((tm, SEG_C), lambda i: (i, 0)),
            pl.BlockSpec((1, MLA_Q_RANK), const2),
            pl.BlockSpec((1, MLA_KV_RANK), const2),
            pl.BlockSpec(wq.shape, const2),
            pl.BlockSpec(wk.shape, const2),
            pl.BlockSpec(wv.shape, const2),
            pl.BlockSpec((tm, LANES), lambda i: (i, 0)),
            pl.BlockSpec((tm, LANES), lambda i: (i, 0)),
        ],
        out_specs=[pl.BlockSpec((tm, MLA_HEADS * 2 * LANES), lambda i: (i, 0)),
                   pl.BlockSpec((tm, MLA_HEADS * 2 * LANES), lambda i: (i, 0)),
                   pl.BlockSpec((MLA_HEADS * MLA_V, tm), lambda i: (0, i))],
        compiler_params=_cparams(("arbitrary",)),
        name="mla_up",
    )(pc, q_norm, kv_norm, wq, wk, wv, cos_rows, sin_rows)


def _mla_attn_kernel(q_ref, k_ref, vt_ref, init_ref, o_ref, s_ref, p_ref, *, kc):
    del init_ref
    p_len = k_ref.shape[0]
    tq = q_ref.shape[0]
    dqk = 2 * LANES
    heads = q_ref.shape[1] // dqk
    half = p_len // 2
    for h in range(heads):
        q = q_ref[:, h * dqk:(h + 1) * dqk]
        s_ref[h, 0:half, :] = _dot_nt(k_ref[0:half, h * dqk:(h + 1) * dqk], q)
        s_ref[h, half:p_len, :] = _dot_nt(k_ref[half:p_len, h * dqk:(h + 1) * dqk], q)
    fold = lambda a: a.reshape(kc // SUBLANES, SUBLANES, tq)
    for h in range(heads):
        m8 = None
        for off in range(0, p_len, kc):
            cm = jnp.max(fold(s_ref[h, off:off + kc, :]), axis=0)
            m8 = cm if m8 is None else jnp.maximum(m8, cm)
        m = jnp.max(m8, axis=0, keepdims=True)
        l8 = jnp.zeros((SUBLANES, tq), F32)
        for off in range(0, p_len, kc):
            p = jnp.exp2(s_ref[h, off:off + kc, :] - m)
            l8 = l8 + jnp.sum(fold(p), axis=0)
            p_ref[h, off:off + kc, :] = p.astype(BF16)
        l = jnp.sum(l8, axis=0, keepdims=True)
        hv = slice(h * MLA_V, (h + 1) * MLA_V)
        acc = (_dot(vt_ref[hv, 0:half], p_ref[h, 0:half, :])
               + _dot(vt_ref[hv, half:p_len], p_ref[h, half:p_len, :]))
        o_ref[:, hv] = (acc / l).T


def _mla_attn(qf, kf, vt, batch, t_len, l_len):
    n = qf.shape[0]
    p_len = t_len + l_len
    tq = ROW_TILE
    kc = MLA_KEY_CHUNK
    hp = MLA_HEADS_PER_STEP
    off = l_len // tq
    tpb = p_len // tq
    return pl.pallas_call(
        functools.partial(_mla_attn_kernel, kc=kc),
        out_shape=jax.ShapeDtypeStruct((n, MLA_HEADS * MLA_V), F32),
        grid=(batch, MLA_HEADS // hp, t_len // tq),
        in_specs=[
            pl.BlockSpec((tq, hp * 2 * LANES), lambda b, h, i: (b * tpb + off + i, h)),
            pl.BlockSpec((p_len, hp * 2 * LANES), lambda b, h, i: (b, h)),
            pl.BlockSpec((hp * MLA_V, p_len), lambda b, h, i: (h, b)),
            pl.BlockSpec(memory_space=pl.ANY),
        ],
        out_specs=pl.BlockSpec((tq, hp * MLA_V), lambda b, h, i: (b * tpb + off + i, h)),
        scratch_shapes=[pltpu.VMEM((hp, p_len, tq), F32), pltpu.VMEM((hp, p_len, tq), BF16)],
        input_output_aliases={3: 0},
        compiler_params=_cparams(("arbitrary", "arbitrary", "arbitrary")),
        name="mla_attn",
    )(qf, kf, vt, jnp.zeros((n, MLA_HEADS * MLA_V), F32))


def _ctx_attn_kernel(*refs, has_sink, scale, v_transposed):
    ex = jnp.exp if has_sink else jnp.exp2
    if has_sink:
        sink_ref, q_ref, k_ref, v_ref, o_ref = refs
    else:
        q_ref, k_ref, v_ref, o_ref = refs
    h = pl.program_id(1)
    s = _dot_nt(q_ref[...].astype(BF16), k_ref[...].astype(BF16)) * scale
    m = jnp.max(s, axis=-1, keepdims=True)
    if has_sink:
        sk = jnp.full((s.shape[0], 1), sink_ref[h], F32)
        m = jnp.maximum(m, sk)
    p = ex(s - m)
    den = jnp.sum(p, axis=-1, keepdims=True)
    if has_sink:
        den = den + ex(sk - m)
    v = v_ref[...].astype(BF16)
    pv = _dot_nt(p.astype(BF16), v) if v_transposed else _dot(p.astype(BF16), v)
    o_ref[...] = pv / den


def _ctx_attn(out_prev, q_arr, k_arr, v_arr, q_col, k_col, v_col, heads, group, batch, p_len, l_len,
              sink, scale, dq, v_transposed):
    n = out_prev.shape[0]
    qb = dq // LANES
    if v_transposed:
        v_spec = pl.BlockSpec((LANES, l_len), lambda b, h: (v_col + h // group, b * (p_len // l_len)))
    else:
        v_spec = pl.BlockSpec((l_len, LANES), lambda b, h: (b * (p_len // l_len), v_col + h // group))
    in_specs = [
        pl.BlockSpec((l_len, dq), lambda b, h: (b * (p_len // l_len), q_col // qb + h)),
        pl.BlockSpec((l_len, dq), lambda b, h: (b * (p_len // l_len), k_col // qb + h // group)),
        v_spec,
        pl.BlockSpec(memory_space=pl.ANY),
    ]
    args = [q_arr, k_arr, v_arr, out_prev]
    has_sink = sink is not None
    if has_sink:
        in_specs = [pl.BlockSpec(memory_space=pltpu.SMEM)] + in_specs
        args = [sink] + args

    def kern(*refs):
        refs = list(refs)
        del refs[4 if has_sink else 3]
        _ctx_attn_kernel(*refs, has_sink=has_sink, scale=scale, v_transposed=v_transposed)

    return pl.pallas_call(
        kern,
        out_shape=jax.ShapeDtypeStruct(out_prev.shape, out_prev.dtype),
        grid=(batch, heads),
        in_specs=in_specs,
        out_specs=pl.BlockSpec((l_len, LANES), lambda b, h: (b * (p_len // l_len), h)),
        input_output_aliases={len(args) - 1: 0},
        compiler_params=_cparams(("arbitrary", "arbitrary")),
        name="ctx_attn_sink" if has_sink else "ctx_attn",
    )(*args)


def _swa_kernel(sink_ref, q_ref, kp_ref, kc_ref, kn_ref, vp_ref, vc_ref, vn_ref, kx_ref, vx_ref,
                cq_ref, sq_ref, cp_ref, sp_ref, cn_ref, sn_ref, init_ref, o_ref, *, nb, scale):
    del init_ref
    n = pl.program_id(0)
    blk = SWA_BLOCK
    batch = q_ref.shape[0]
    cq, sq = cq_ref[...], sq_ref[...]
    cp, sp = cp_ref[...], sp_ref[...]
    cn, sn = cn_ref[...], sn_ref[...]
    ri = lax.broadcasted_iota(jnp.int32, (2 * blk, blk), 0) % blk
    ci = lax.broadcasted_iota(jnp.int32, (2 * blk, blk), 1)
    rr = lax.broadcasted_iota(jnp.int32, (2 * blk, 1), 0)
    mask_p = (ci >= ri) & (n > 0)
    mask_n = (ci <= ri) & (n < nb - 1)
    neg = jnp.float32(-jnp.inf)
    mx = lambda a: jnp.max(a, axis=-1, keepdims=True)
    sm = lambda a: jnp.sum(a, axis=-1, keepdims=True)
    for b in range(batch):
        for h in range(SWA_KV_HEADS):
            ks = slice(h * LANES, (h + 1) * LANES)
            q0 = _rope128(q_ref[b, :, 2 * h * LANES:(2 * h + 1) * LANES], cq, sq)
            q1 = _rope128(q_ref[b, :, (2 * h + 1) * LANES:(2 * h + 2) * LANES], cq, sq)
            qq = jnp.concatenate([q0, q1], axis=0).astype(BF16)
            kc = _rope128(kc_ref[b, :, ks], cq, sq).astype(BF16)
            kp = _rope128(kp_ref[b, :, ks], cp, sp).astype(BF16)
            kn = _rope128(kn_ref[b, :, ks], cn, sn).astype(BF16)
            s_c = _dot_nt(qq, kc) * scale
            s_p = jnp.where(mask_p, _dot_nt(qq, kp) * scale, neg)
            s_n = jnp.where(mask_n, _dot_nt(qq, kn) * scale, neg)
            s_x = _dot_nt(qq, kx_ref[b, :, ks].astype(BF16)) * scale
            sk = jnp.where(rr < blk, sink_ref[2 * h], sink_ref[2 * h + 1]).astype(F32)
            m = jnp.maximum(jnp.maximum(mx(s_c), mx(s_p)), jnp.maximum(mx(s_n), mx(s_x)))
            m = jnp.maximum(m, sk)
            p_c, p_p, p_n, p_x = (jnp.exp(a - m) for a in (s_c, s_p, s_n, s_x))
            den = sm(p_c) + sm(p_p) + sm(p_n) + sm(p_x) + jnp.exp(sk - m)
            o = (_dot(p_c.astype(BF16), vc_ref[b, :, ks].astype(BF16))
                 + _dot(p_p.astype(BF16), vp_ref[b, :, ks].astype(BF16))
                 + _dot(p_n.astype(BF16), vn_ref[b, :, ks].astype(BF16))
                 + _dot(p_x.astype(BF16), vx_ref[b, :, ks].astype(BF16))) / den
            o_ref[b, :, 2 * h * LANES:(2 * h + 1) * LANES] = o[:blk]
            o_ref[b, :, (2 * h + 1) * LANES:(2 * h + 2) * LANES] = o[blk:]


def _swa(pa, sink, cos_t, sin_t, batch, t_len, l_len):
    n = pa.shape[0]
    p_len = t_len + l_len
    blk = SWA_BLOCK
    nb = t_len // blk
    off = l_len // blk
    scale = float(HEAD_DIM ** -0.5)
    kv_w = SWA_KV_HEADS * HEAD_DIM
    q_w = SWA_HEADS * HEAD_DIM
    kcol, vcol = q_w // kv_w, q_w // kv_w + 1
    pa3 = pa.reshape(batch, p_len, pa.shape[1])
    prv = lambda j: jnp.maximum(j - 1, 0)
    nxt = lambda j: jnp.minimum(j + 1, nb - 1)
    kv = lambda col, rowf: pl.BlockSpec((batch, blk, kv_w), lambda j: (0, off + rowf(j), col))
    tab = lambda rowf: pl.BlockSpec((blk, LANES), lambda j: (rowf(j), 0))
    same = lambda j: j
    in_specs = [
        pl.BlockSpec(memory_space=pltpu.SMEM),
        pl.BlockSpec((batch, blk, q_w), lambda j: (0, off + j, 0)),
        kv(kcol, prv), kv(kcol, same), kv(kcol, nxt),
        kv(vcol, prv), kv(vcol, same), kv(vcol, nxt),
        pl.BlockSpec((batch, l_len, kv_w), lambda j: (0, 0, kcol)),
        pl.BlockSpec((batch, l_len, kv_w), lambda j: (0, 0, vcol)),
        tab(same), tab(same), tab(prv), tab(prv), tab(nxt), tab(nxt),
        pl.BlockSpec(memory_space=pl.ANY),
    ]
    out = pl.pallas_call(
        functools.partial(_swa_kernel, nb=nb, scale=scale),
        out_shape=jax.ShapeDtypeStruct((batch, p_len, q_w), F32),
        grid=(nb,),
        in_specs=in_specs,
        out_specs=pl.BlockSpec((batch, blk, q_w), lambda j: (0, off + j, 0)),
        input_output_aliases={len(in_specs) - 1: 0},
        compiler_params=_cparams(("arbitrary",)),
        name="swa",
    )(sink, pa3, pa3, pa3, pa3, pa3, pa3, pa3, pa3, pa3, cos_t, sin_t, cos_t, sin_t, cos_t, sin_t,
      jnp.zeros((batch, p_len, q_w), F32))
    return out.reshape(n, q_w)


def _chunk_index(d, c, n_l, n_t):
    bwd = jnp.where(c < n_l, n_l - 1 - c, n_l + (n_t - 1) - (c - n_l))
    return jnp.where(d == 0, c, bwd)


def _flip_iotas(d):
    row = lax.broadcasted_iota(jnp.int32, (CHUNK, CHUNK), 0)
    col = lax.broadcasted_iota(jnp.int32, (CHUNK, CHUNK), 1)
    rf = jnp.where(d == 0, row, CHUNK - 1 - row)
    cf = jnp.where(d == 0, col, CHUNK - 1 - col)
    return rf, cf


def _ret_kernel(s_ref, qf_ref, kf_ref, vf_ref, qb_ref, kb_ref, vb_ref, of_ref, ob_ref, st_ref):
    c = pl.program_id(0)
    batch = qf_ref.shape[0]

    @pl.when(c == 0)
    def _():
        st_ref[...] = jnp.zeros_like(st_ref)

    ks = float(RET_DK ** -0.5)
    dirs = ((qf_ref, kf_ref, vf_ref, of_ref), (qb_ref, kb_ref, vb_ref, ob_ref))
    for d, (q_ref, k_ref, v_ref, o_ref) in enumerate(dirs):
        rf, cf = _flip_iotas(d)
        rff = rf.astype(F32)
        dn = (rf - cf).astype(F32)
        for h in range(RET_HEADS):
            hs = slice(h * LANES, (h + 1) * LANES)
            sv = jnp.full((CHUNK, CHUNK), s_ref[d * RET_HEADS + h], F32)
            lg = jnp.log1p(-jnp.exp2(-sv))
            dec = jnp.where(dn >= 0, jnp.exp(dn * lg), 0.0)
            eq = jnp.exp((rff + 1.0) * lg)
            ek = jnp.exp((CHUNK - 1.0 - rff) * lg)
            a_chunk = jnp.exp(float(CHUNK) * lg)
            for b in range(batch):
                q = q_ref[b, :, hs]
                k = k_ref[b, :, hs] * ks
                v = v_ref[b, :, hs].astype(BF16)
                st = st_ref[d, b, h]
                a = _dot_nt(q.astype(BF16), k.astype(BF16)) * dec
                o = _dot(a.astype(BF16), v) + _dot_nt((q * eq).astype(BF16), st.astype(BF16))
                o_ref[b, :, hs] = o
                st_ref[d, b, h] = a_chunk * st + _dot_tn(v, (k * ek).astype(BF16))


def _hgrn_level_mats():
    c = CHUNK
    t = np.arange(c)[:, None]
    u = np.arange(c)[None, :]
    mats = [u <= t, u > t]
    m = c // 2
    while m >= 1:
        mid = (t // (2 * m)) * 2 * m + m
        second = (t % (2 * m)) >= m
        qrole = (u >= mid) & (u <= t)
        krole = (u > t) & (u <= mid - 1)
        mats.append(np.where(second, qrole, krole))
        m //= 2
    fwd = np.concatenate(mats, 0).astype(np.float32)
    bwd = np.concatenate([mm[::-1, ::-1] for mm in mats], 0).astype(np.float32)
    return np.stack([fwd, bwd], 0)


N_LEVELS = int(math.log2(CHUNK))


def _hgrn_kernel(m_ref, lb_ref, qf_ref, ff_ref, vf_ref, qb_ref, fb_ref, vb_ref, of_ref, ob_ref, st_ref):
    c = pl.program_id(0)
    batch = qf_ref.shape[0]

    @pl.when(c == 0)
    def _():
        st_ref[...] = jnp.zeros_like(st_ref)

    dirs = ((qf_ref, ff_ref, vf_ref, of_ref), (qb_ref, fb_ref, vb_ref, ob_ref))
    for d, (q_ref, f_ref, v_ref, o_ref) in enumerate(dirs):
        rf, cf = _flip_iotas(d)
        mst = m_ref[d]
        last = CHUNK - 1 if d == 0 else 0
        for h in range(HGRN_HEADS):
            hs = slice(h * LANES, (h + 1) * LANES)
            lb = lb_ref[d, :, hs]
            for b in range(batch):
                f = lb + (1.0 - lb) * (1.0 / (1.0 + jnp.exp(-f_ref[b, :, hs])))
                k = 1.0 - f
                g = jnp.log(f)
                q = _silu(q_ref[b, :, hs])
                v = v_ref[b, :, hs].astype(BF16)
                g_hi = g.astype(BF16)
                g_lo = (g - g_hi.astype(F32)).astype(BF16)
                x2 = _dot(mst, jnp.concatenate([g_hi, g_lo], axis=1))
                e = jnp.exp(x2[:, :LANES] + x2[:, LANES:])
                e_q = e[0:CHUNK]
                q_in = q * e_q
                k_st = k * e[CHUNK:2 * CHUNK]
                scores = jnp.where(rf == cf, _dot_nt(q.astype(BF16), k.astype(BF16)), 0.0)
                for lvl in range(N_LEVELS):
                    m = CHUNK >> (lvl + 1)
                    sh = N_LEVELS - lvl
                    el = e[(2 + lvl) * CHUNK:(3 + lvl) * CHUNK]
                    second = (rf & m) != 0
                    ql = jnp.where(second, q * el, 0.0).astype(BF16)
                    kl = jnp.where(second, 0.0, k * el).astype(BF16)
                    sl = _dot_nt(ql, kl)
                    scores = scores + jnp.where((rf >> sh) == (cf >> sh), sl, 0.0)
                st = st_ref[d, b, h]
                o = _dot(scores.astype(BF16), v) + _dot_nt(q_in.astype(BF16), st.astype(BF16))
                o_ref[b, :, hs] = o
                st_ref[d, b, h] = st * e_q[last:last + 1, :] + _dot_tn(v, k_st.astype(BF16))


def _scan_specs(batch, t_len, l_len):
    n_l, n_t = l_len // CHUNK, t_len // CHUNK

    def spec(d, col):
        return pl.BlockSpec((batch, CHUNK, WIDE), lambda c: (0, _chunk_index(d, c, n_l, n_t), col))

    return spec, n_l + n_t


def _retention(pb, ret_s, batch, t_len, l_len):
    n = pb.shape[0]
    p_len = t_len + l_len
    pb3 = pb.reshape(batch, p_len, pb.shape[1])
    spec, n_chunks = _scan_specs(batch, t_len, l_len)
    out_f, out_b = pl.pallas_call(
        _ret_kernel,
        out_shape=[jax.ShapeDtypeStruct((batch, p_len, WIDE), F32)] * 2,
        grid=(n_chunks,),
        in_specs=[pl.BlockSpec(memory_space=pltpu.SMEM)]
                 + [spec(0, col) for col in (0, 1, 2)] + [spec(1, col) for col in (0, 1, 2)],
        out_specs=[spec(0, 0), spec(1, 0)],
        scratch_shapes=[pltpu.VMEM((2, batch, RET_HEADS, CHUNK, CHUNK), F32)],
        compiler_params=_cparams(("arbitrary",)),
        name="retention_scan",
    )(ret_s.reshape(-1), pb3, pb3, pb3, pb3, pb3, pb3)
    return out_f.reshape(n, WIDE), out_b.reshape(n, WIDE)


def _hgrn(pd, lower_bounds, mats, batch, t_len, l_len):
    n = pd.shape[0]
    p_len = t_len + l_len
    pd3 = pd.reshape(batch, p_len, pd.shape[1])
    lb3 = lower_bounds.reshape(2, 1, WIDE)
    spec, n_chunks = _scan_specs(batch, t_len, l_len)
    out_f, out_b = pl.pallas_call(
        _hgrn_kernel,
        out_shape=[jax.ShapeDtypeStruct((batch, p_len, WIDE), F32)] * 2,
        grid=(n_chunks,),
        in_specs=[pl.BlockSpec(mats.shape, lambda c: (0, 0, 0)),
                  pl.BlockSpec(lb3.shape, lambda c: (0, 0, 0)),
                  spec(0, 0), spec(0, 1), spec(0, 3), spec(1, 0), spec(1, 2), spec(1, 3)],
        out_specs=[spec(0, 0), spec(1, 0)],
        scratch_shapes=[pltpu.VMEM((2, batch, HGRN_HEADS, CHUNK, CHUNK), F32)],
        compiler_params=_cparams(("arbitrary",)),
        name="hgrn_scan",
    )(mats, lb3, pd3, pd3, pd3, pd3, pd3, pd3)
    return out_f.reshape(n, WIDE), out_b.reshape(n, WIDE)


R_E1, R_E2, R_G1, R_G2, R_S1, R_S2 = 0, 1, 2, 3, 4, 5


def _route(logits, count_ref, active):
    tm = logits.shape[0]
    lane = lax.broadcasted_iota(jnp.int32, logits.shape, 1)
    lanef = lane.astype(F32)
    big = jnp.float32(1e9)
    neg = jnp.float32(-jnp.inf)
    mx = lambda a: jnp.max(a, axis=-1, keepdims=True)
    mn = lambda a: jnp.min(a, axis=-1, keepdims=True)
    sm = lambda a: jnp.sum(a, axis=-1, keepdims=True)
    gl = jnp.where(lane < N_GROUPS, logits, neg)
    gm = mx(gl)
    p_grp = 1.0 / sm(jnp.exp(gl - gm))
    grp = mn(jnp.where(gl == gm, lanef, big))
    lo = N_GROUPS + grp * EXPERTS_PER_GROUP
    ing = (lanef >= lo) & (lanef < lo + EXPERTS_PER_GROUP)
    el = jnp.where(ing, logits, neg)
    ee = jnp.exp(el - mx(el))
    p = ee / sm(ee)
    pm = jnp.where(ing, p, -1.0)
    p1 = mx(pm)
    i1 = mn(jnp.where(pm == p1, lanef, big))
    pm2 = jnp.where(lanef == i1, -1.0, pm)
    p2 = mx(pm2)
    i2 = mn(jnp.where(pm2 == p2, lanef, big))
    den = p1 + p2
    g1 = p_grp * p1 / den
    g2 = p_grp * p2 / den
    e1 = i1 - N_GROUPS
    e2 = i2 - N_GROUPS
    oh1 = lanef == e1
    oh2 = lanef == e2
    both = jnp.where(oh1 | oh2, active, 0.0)
    ri = lax.broadcasted_iota(jnp.int32, (tm, tm), 0)
    ci = lax.broadcasted_iota(jnp.int32, (tm, tm), 1)
    earlier = jnp.where(ci < ri, 1.0, 0.0).astype(BF16)
    before = _dot(earlier, both.astype(BF16)) + count_ref[0:1, :]
    s1 = sm(jnp.where(oh1, before, 0.0))
    s2 = sm(jnp.where(oh2, before, 0.0))
    count_ref[...] = count_ref[...] + jnp.sum(both, axis=0, keepdims=True)
    out = jnp.zeros(logits.shape, F32)
    for ln, val in ((R_E1, e1), (R_E2, e2), (R_G1, g1), (R_G2, g2), (R_S1, s1), (R_S2, s2)):
        out = jnp.where(lane == ln, val, out)
    return out


def _out_kernel(x_ref, a_ref, bf_ref, bb_ref, c_ref, df_ref, db_ref, rg_ref, hg_ref, g1_ref, sh2_ref, sc2_ref,
                lng_ref, lnb_ref, rgn_ref, hgn_ref, wo_ref, wr_ref,
                xo_ref, h2_ref, r_ref, cnt_ref, *, alpha, tiles_per_batch, route_ctx):
    i = pl.program_id(0)

    @pl.when(i == 0)
    def _():
        cnt_ref[...] = jnp.zeros_like(cnt_ref)

    bsum = bf_ref[...] + bb_ref[...]
    dsum = df_ref[...] + db_ref[...]
    rgn = rgn_ref[...]
    hgn = hgn_ref[...]
    parts_b, parts_d = [], []
    for h in range(4):
        hs = slice(h * LANES, (h + 1) * LANES)
        parts_b.append(_ln_rows(bsum[:, hs]))
        dh = dsum[:, hs]
        parts_d.append(dh * lax.rsqrt(jnp.mean(dh * dh, axis=-1, keepdims=True) + EPS))
    bo = _silu(rg_ref[...]) * (jnp.concatenate(parts_b, axis=1) * rgn)
    do = _silu(hg_ref[...]) * (jnp.concatenate(parts_d, axis=1) * hgn)
    mix = jnp.concatenate([a_ref[...], bo, c_ref[...], do], axis=1).astype(BF16)
    o = _dot(mix, wo_ref[...])
    y = alpha * x_ref[...] + g1_ref[0] * o
    xn = _ln_rows(y) * lng_ref[...] + lnb_ref[...]
    xo_ref[...] = xn
    h2 = _ln_rows(xn) * (1.0 + sc2_ref[0]) + sh2_ref[0]
    h2_ref[...] = _pack_bf16_pairs(h2)
    if route_ctx:
        active = jnp.float32(1.0)
    else:
        active = jnp.where(i % tiles_per_batch == 0, 0.0, 1.0).astype(F32)
    r_ref[...] = _route(_dot(h2.astype(BF16), wr_ref[...]), cnt_ref, active)


def _out_proj(stream, a, b_f, b_b, c, d_f, d_b, pb, pd, g1, sh2, sc2, ln_g, ln_b, ret_gn, hgrn_gn, w_out, w_r,
              tiles_per_batch, alpha, route_ctx):
    n, dm = stream.shape
    tm = ROW_TILE
    gmap = lambda i: (_group_of_tile(i, tiles_per_batch), 0, 0)
    row = lambda w: pl.BlockSpec((tm, w), lambda i: (i, 0))
    const = lambda w: pl.BlockSpec((1, w), lambda i: (0, 0))
    resident = pl.BlockSpec(memory_space=pltpu.VMEM)
    return pl.pallas_call(
        functools.partial(_out_kernel, alpha=alpha, tiles_per_batch=tiles_per_batch, route_ctx=route_ctx),
        out_shape=[jax.ShapeDtypeStruct((n, dm), F32), jax.ShapeDtypeStruct((n, dm // 2), jnp.uint32),
                   jax.ShapeDtypeStruct((n, LANES), F32), jax.ShapeDtypeStruct((SUBLANES, LANES), F32)],
        grid=(n // tm,),
        in_specs=[row(dm), row(WIDE), row(WIDE), row(WIDE), row(WIDE), row(WIDE), row(WIDE),
                  pl.BlockSpec((tm, WIDE), lambda i: (i, 3)),
                  pl.BlockSpec((tm, WIDE), lambda i: (i, 4)),
                  pl.BlockSpec((1, 1, dm), gmap), pl.BlockSpec((1, 1, dm), gmap),
                  pl.BlockSpec((1, 1, dm), gmap),
                  const(dm), const(dm), const(WIDE), const(WIDE), resident, resident],
        out_specs=[row(dm), row(dm // 2), row(LANES),
                   pl.BlockSpec((SUBLANES, LANES), lambda i: (0, 0))],
        compiler_params=_cparams(("arbitrary",)),
        name="mix_outproj_norm_route",
    )(stream, a, b_f, b_b, c, d_f, d_b, pb, pd, g1, sh2, sc2, ln_g, ln_b, ret_gn, hgrn_gn, w_out, w_r)


def _route_plan(route, counts_f, n_blocks):
    rows = MOE_ROWS
    counts = counts_f[0, :N_EXPERTS].astype(jnp.int32)
    padded = (counts + rows - 1) // rows * rows
    pend = jnp.cumsum(padded)
    pstart = pend - padded
    eid = route[:, R_E1:R_E2 + 1].astype(jnp.int32)
    rank = route[:, R_S1:R_S2 + 1].astype(jnp.int32)
    onehot = eid[:, :, None] == jnp.arange(N_EXPERTS, dtype=jnp.int32)
    pos = jnp.sum(jnp.where(onehot, pstart, 0), axis=-1) + rank
    blk_start = jnp.arange(n_blocks, dtype=jnp.int32) * rows
    blk_e = jnp.minimum(jnp.sum(blk_start[:, None] >= pend[None, :], axis=-1), N_EXPERTS - 1)
    blk_valid = (blk_start < pend[-1]).astype(jnp.int32)
    pad_start = pstart + counts
    ids = jnp.arange(N_EXPERTS, dtype=jnp.int32)
    nonempty = counts > 0
    seq = jnp.cumsum(nonempty.astype(jnp.int32)) - nonempty.astype(jnp.int32)
    cand = jnp.where(nonempty, ids, N_EXPERTS)
    later = jnp.where(ids[None, :] > ids[:, None], cand[None, :], N_EXPERTS)
    nxt = jnp.min(later, axis=-1)
    nxt = jnp.where(nxt >= N_EXPERTS, -1, nxt)
    blk_oh = blk_e[:, None] == ids[None, :]
    blk_slot = jnp.sum(jnp.where(blk_oh, seq % 2, 0), axis=-1)
    blk_next = jnp.sum(jnp.where(blk_oh, nxt, 0), axis=-1)
    i32 = lambda a: a.astype(jnp.int32)
    return (i32(pos.reshape(-1)), i32(blk_e), blk_valid, i32(pad_start), i32(blk_slot), i32(blk_next))


def _dispatch_kernel(pos_ref, pad_ref, bv_ref, h_ref, x_hbm, zbuf, sem, *, tile_of_step, n_blocks):
    i = pl.program_id(0)
    tm = ROW_TILE
    rows = MOE_ROWS

    def zero_copy(e):
        start = pl.multiple_of(pad_ref[e] & ~(SUBLANES - 1), SUBLANES)
        return pltpu.make_async_copy(zbuf, x_hbm.at[pl.ds(start, rows + SUBLANES)], sem.at[1])

    def unused_block_copy(j):
        start = pl.multiple_of(j * rows, rows)
        return pltpu.make_async_copy(zbuf.at[pl.ds(0, rows)], x_hbm.at[pl.ds(start, rows)], sem.at[1])

    @pl.when(i == 0)
    def _():
        zbuf[...] = jnp.zeros_like(zbuf)

        def fill(j, carry):
            @pl.when(bv_ref[j] == 0)
            def _():
                unused_block_copy(j).start()
                unused_block_copy(j).wait()
            return carry

        lax.fori_loop(0, n_blocks + 2, fill, 0)
        for e in range(N_EXPERTS):
            zero_copy(e).start()
        for e in range(N_EXPERTS):
            zero_copy(e).wait()

    def row_copy(r, k):
        p = pos_ref[(tile_of_step(i) * tm + r) * TOP_K + k]
        return pltpu.make_async_copy(h_ref.at[r], x_hbm.at[p], sem.at[0])

    for r in range(tm):
        for k in range(TOP_K):
            row_copy(r, k).start(priority=k)
    for r in range(tm):
        for k in range(TOP_K):
            row_copy(r, k).wait()


def _tile_schedule(n, tiles_per_batch, latent_only, batch):
    if latent_only:
        lat = tiles_per_batch - 1
        return batch * lat, (lambda s: (s // lat) * tiles_per_batch + 1 + s % lat)
    return n // ROW_TILE, (lambda s: s)


def _dispatch(h2, pos, pad_start, blk_valid, tiles_per_batch, latent_only, batch):
    n, dm = h2.shape
    tm = ROW_TILE
    n_blocks = blk_valid.shape[0]
    blk_valid = jnp.concatenate([blk_valid, jnp.zeros((2,), jnp.int32)])
    n_tiles, tile_of_step = _tile_schedule(n, tiles_per_batch, latent_only, batch)
    grid_spec = pltpu.PrefetchScalarGridSpec(
        num_scalar_prefetch=3,
        grid=(n_tiles,),
        in_specs=[pl.BlockSpec((tm, dm), lambda i, p, z, v: (tile_of_step(i), 0))],
        out_specs=pl.BlockSpec(memory_space=pl.ANY),
        scratch_shapes=[pltpu.VMEM((MOE_ROWS + SUBLANES, dm), h2.dtype), pltpu.SemaphoreType.DMA((2,))],
    )
    return pl.pallas_call(
        functools.partial(_dispatch_kernel, tile_of_step=tile_of_step, n_blocks=n_blocks),
        out_shape=jax.ShapeDtypeStruct(((n_blocks + 2) * MOE_ROWS, dm), h2.dtype),
        grid_spec=grid_spec,
        compiler_params=_cparams(("arbitrary",)),
        name="moe_dispatch",
    )(pos, pad_start, blk_valid, h2)


def _moe_kernel(be_ref, bv_ref, slot_ref, next_ref, x_ref, w1_hbm, w3_hbm, w2_hbm, y_ref,
                s1, s3, s2, w1b, w3b, w2b, sem, *, layer):
    j = pl.program_id(0)
    valid = bv_ref[j] == 1
    first = valid & ((j == 0) | (be_ref[j] != be_ref[jnp.maximum(j - 1, 0)]))

    def weight_copies(e, slot):
        return (pltpu.make_async_copy(w1_hbm.at[layer, e], s1.at[slot], sem.at[slot]),
                pltpu.make_async_copy(w3_hbm.at[layer, e], s3.at[slot], sem.at[slot]),
                pltpu.make_async_copy(w2_hbm.at[layer, e], s2.at[slot], sem.at[slot]))

    @pl.when(valid & (j == 0))
    def _():
        for cp in weight_copies(be_ref[0], slot_ref[0]):
            cp.start()

    @pl.when(first)
    def _():
        slot = slot_ref[j]
        for cp in weight_copies(be_ref[j], slot):
            cp.wait()
        nxt = next_ref[j]

        @pl.when(nxt >= 0)
        def _():
            for cp in weight_copies(nxt, 1 - slot):
                cp.start()

        w1b[...] = s1[slot].astype(BF16)
        w3b[...] = s3[slot].astype(BF16)
        w2b[...] = s2[slot].astype(BF16)

    @pl.when(valid)
    def _():
        x = _unpack_bf16_pairs(x_ref[...]).astype(BF16)
        hmid = _silu(_dot(x, w1b[...])) * _dot(x, w3b[...])
        y_ref[...] = _pack_bf16_pairs(_dot(hmid.astype(BF16), w2b[...]))

    @pl.when(bv_ref[j] == 0)
    def _():
        y_ref[...] = jnp.zeros_like(y_ref)


def _moe_experts(xbuf, blk_e, blk_valid, blk_slot, blk_next, w1, w3, w2, layer):
    dm = w1.shape[2]
    ff = w1.shape[-1]
    rows = MOE_ROWS
    n_blocks = blk_e.shape[0]
    hbm = pl.BlockSpec(memory_space=pl.ANY)
    grid_spec = pltpu.PrefetchScalarGridSpec(
        num_scalar_prefetch=4,
        grid=(n_blocks,),
        in_specs=[
            pl.BlockSpec((rows, dm // 2), lambda j, be, bv, sl, nx: (j * bv[j], 0)),
            hbm, hbm, hbm,
        ],
        out_specs=pl.BlockSpec((rows, dm // 2), lambda j, be, bv, sl, nx: (j, 0)),
        scratch_shapes=[
            pltpu.VMEM((2, dm, ff), F32), pltpu.VMEM((2, dm, ff), F32), pltpu.VMEM((2, ff, dm), F32),
            pltpu.VMEM((dm, ff), BF16), pltpu.VMEM((dm, ff), BF16), pltpu.VMEM((ff, dm), BF16),
            pltpu.SemaphoreType.DMA((2,)),
        ],
    )
    return pl.pallas_call(
        functools.partial(_moe_kernel, layer=layer),
        out_shape=jax.ShapeDtypeStruct((n_blocks * rows, dm // 2), jnp.uint32),
        grid_spec=grid_spec,
        compiler_params=_cparams(("arbitrary",)),
        name="moe_experts",
    )(blk_e, blk_valid, blk_slot, blk_next, xbuf, w1, w3, w2)


def _combine_kernel(pos_ref, x_ref, r_ref, g2_ref, lng_ref, lnb_ref, y_hbm, o_ref, ybuf, sem,
                    *, n_tiles, tile_of_step, alpha):
    i = pl.program_id(0)
    slot = i % 2
    tm = ROW_TILE

    def row_copy(step, sl, r, k):
        p = pos_ref[(tile_of_step(step) * tm + r) * TOP_K + k]
        return pltpu.make_async_copy(y_hbm.at[p], ybuf.at[sl, k, r], sem.at[sl])

    def start_gather(step, sl):
        for r in range(tm):
            for k in range(TOP_K):
                row_copy(step, sl, r, k).start(priority=k)

    @pl.when(i == 0)
    def _():
        start_gather(0, 0)

    @pl.when(i + 1 < n_tiles)
    def _():
        start_gather(jnp.minimum(i + 1, n_tiles - 1), 1 - slot)

    for r in range(tm):
        for k in range(TOP_K):
            row_copy(i, slot, r, k).wait()
    route = r_ref[...]
    y = (_unpack_bf16_pairs(ybuf[slot, 0]) * route[:, R_G1:R_G1 + 1]
         + _unpack_bf16_pairs(ybuf[slot, 1]) * route[:, R_G2:R_G2 + 1])
    z = alpha * x_ref[...] + g2_ref[0] * y
    o_ref[...] = _ln_rows(z) * lng_ref[...] + lnb_ref[...]


def _combine(x_new, route, ybuf, pos, g2, ln_g, ln_b, tiles_per_batch, alpha, latent_only, batch):
    n, dm = x_new.shape
    tm = ROW_TILE
    n_tiles, tile_of_step = _tile_schedule(n, tiles_per_batch, latent_only, batch)
    gmap = lambda i, p: (_group_of_tile(tile_of_step(i), tiles_per_batch), 0, 0)
    grid_spec = pltpu.PrefetchScalarGridSpec(
        num_scalar_prefetch=1,
        grid=(n_tiles,),
        in_specs=[
            pl.BlockSpec((tm, dm), lambda i, p: (tile_of_step(i), 0)),
            pl.BlockSpec((tm, LANES), lambda i, p: (tile_of_step(i), 0)),
            pl.BlockSpec((1, 1, dm), gmap),
            pl.BlockSpec((1, dm), lambda i, p: (0, 0)),
            pl.BlockSpec((1, dm), lambda i, p: (0, 0)),
            pl.BlockSpec(memory_space=pl.ANY),
        ],
        out_specs=pl.BlockSpec((tm, dm), lambda i, p: (i, 0)),
        scratch_shapes=[pltpu.VMEM((2, TOP_K, tm, dm // 2), jnp.uint32), pltpu.SemaphoreType.DMA((2,))],
    )
    return pl.pallas_call(
        functools.partial(_combine_kernel, n_tiles=n_tiles, tile_of_step=tile_of_step, alpha=alpha),
        out_shape=jax.ShapeDtypeStruct((n_tiles * tm, dm), F32),
        grid_spec=grid_spec,
        compiler_params=_cparams(("arbitrary",)),
        name="moe_combine_norm",
    )(pos, x_new, route, g2, ln_g, ln_b, ybuf)


def _pad_cols(w, width):
    return jnp.pad(w, ((0, 0), (0, width - w.shape[1])))


def kernel(x, c, ctx, c_ctx, w_ada, b_ada, w_in, swa_sink, ret_decay_exp, ret_gn, mla_q_norm, mla_kv_norm,
           mla_w_uq, mla_w_ukv, hgrn_lb_logits, hgrn_gn, w_out, ln1_g, ln1_b, router_group, router_expert,
           moe_w1, moe_w3, moe_w2, ln2_g, ln2_b):
    batch, t_len, dm = x.shape
    l_len = ctx.shape[1]
    depth = w_in.shape[0]
    p_len = t_len + l_len
    n = batch * p_len
    assert batch == 2 and l_len == ROW_TILE and t_len % 256 == 0 and t_len % GRID_W == 0
    tiles_per_batch = p_len // ROW_TILE
    alpha = float((2 * depth) ** 0.25)

    cos_swa, sin_swa, cos_rows, sin_rows = _rope_tables(t_len, l_len, batch)
    mats = jnp.asarray(_hgrn_level_mats(), BF16)

    cvec = jnp.zeros((SUBLANES, dm), F32).at[0:batch].set(c).at[batch].set(c_ctx)
    mod = _ada(cvec, w_ada, b_ada)

    lbp = jax.nn.softmax(hgrn_lb_logits.astype(F32), axis=0)
    lower_bounds = jnp.cumsum(lbp, axis=0) - lbp[0]

    stream = None

    out = None
    for l in range(depth):
        need_ctx = l < depth - 1
        m6 = mod[l, :3].reshape(3, 6, dm)
        sh1, sc1, g1, sh2, sc2, g2 = (m6[:, k].reshape(3, 1, dm) for k in range(6))

        w_pad = _w_in_prep(w_in, l)
        if stream is None:
            pa, pb, pc, pd, stream = _proj(None, x, ctx, sh1, sc1, w_pad, tiles_per_batch)
        else:
            pa, pb, pc, pd = _proj(stream, None, None, sh1, sc1, w_pad, tiles_per_batch)

        wq = mla_w_uq[l].reshape(MLA_Q_RANK, MLA_HEADS, MLA_NOPE + MLA_ROPE)
        wq = jnp.pad(wq, ((0, 0), (0, 0), (0, 2 * LANES - (MLA_NOPE + MLA_ROPE))))
        wq = wq.reshape(MLA_Q_RANK, MLA_HEADS * 2 * LANES).astype(BF16)
        wkv = mla_w_ukv[l].reshape(MLA_KV_RANK, MLA_HEADS, MLA_NOPE + MLA_V)
        wk = wkv[:, :, :MLA_NOPE].reshape(MLA_KV_RANK, -1).astype(BF16)
        wv_t = wkv[:, :, MLA_NOPE:].reshape(MLA_KV_RANK, -1).T.astype(BF16)
        qf, kf, vt = _mla_up(pc, mla_q_norm[l].reshape(1, -1), mla_kv_norm[l].reshape(1, -1),
                             wq, wk, wv_t, cos_rows, sin_rows)

        a = _swa(pa, swa_sink[l], cos_swa, sin_swa, batch, t_len, l_len)
        cc = _mla_attn(qf, kf, vt, batch, t_len, l_len)
        if need_ctx:
            a = _ctx_attn(a, pa, pa, pa, 0, 4, 6, SWA_HEADS, SWA_HEADS // SWA_KV_HEADS, batch, p_len,
                          l_len, swa_sink[l], float(HEAD_DIM ** -0.5), LANES, False)
            cc = _ctx_attn(cc, qf, kf, vt, 0, 0, 0, MLA_HEADS, 1, batch, p_len, l_len, None, 1.0,
                           2 * LANES, True)
        b_f, b_b = _retention(pb, ret_decay_exp[l], batch, t_len, l_len)
        d_f, d_b = _hgrn(pd, lower_bounds[l], mats, batch, t_len, l_len)

        w_r = _pad_cols(jnp.concatenate([router_group[l], router_expert[l]], axis=1), LANES).astype(BF16)
        x_new, h2, route, counts = _out_proj(
            stream, a, b_f, b_b, cc, d_f, d_b, pb, pd, g1, sh2, sc2,
            ln1_g[l].reshape(1, -1), ln1_b[l].reshape(1, -1),
            ret_gn[l].reshape(1, -1), hgrn_gn[l].reshape(1, -1),
            w_out[l].astype(BF16), w_r, tiles_per_batch, alpha, need_ctx)

        n_active = n if need_ctx else batch * t_len
        n_blocks = (n_active * TOP_K + N_EXPERTS * (MOE_ROWS - 1) + MOE_ROWS - 1) // MOE_ROWS
        pos, blk_e, blk_valid, pad_start, blk_slot, blk_next = _route_plan(route, counts, n_blocks)
        xbuf = _dispatch(h2, pos, pad_start, blk_valid, tiles_per_batch, not need_ctx, batch)
        ybuf = _moe_experts(xbuf, blk_e, blk_valid, blk_slot, blk_next, moe_w1, moe_w3, moe_w2, l)
        res = _combine(x_new, route, ybuf, pos, g2, ln2_g[l].reshape(1, -1), ln2_b[l].reshape(1, -1),
                       tiles_per_batch, alpha, not need_ctx, batch)
        if need_ctx:
            stream = res
        else:
            out = res.reshape(batch, t_len, dm)
    return out
```

```python
import functools
import math

import numpy as np
import jax
import jax.numpy as jnp
from jax import lax
from jax.experimental import pallas as pl
from jax.experimental.pallas import tpu as pltpu

F32 = jnp.float32
BF16 = jnp.bfloat16

GRID_W = 64
HEAD_DIM = 128
SWA_HEADS = 4
SWA_KV_HEADS = 2
SWA_BLOCK = 128
RET_HEADS = 4
RET_DK = 128
MLA_HEADS = 4
MLA_Q_RANK = 384
MLA_KV_RANK = 128
MLA_NOPE = 128
MLA_ROPE = 64
MLA_V = 128
HGRN_HEADS = 4
N_GROUPS = 4
EXPERTS_PER_GROUP = 8
N_EXPERTS = N_GROUPS * EXPERTS_PER_GROUP
TOP_K = 2
ROPE_BASE = 10000.0
EPS = 1e-6

LANES = 128
SUBLANES = 8
VMEM_LIMIT_BYTES = 56 * 1024 * 1024
PROJ_VMEM_LIMIT_BYTES = 60 * 1024 * 1024

CHUNK = 128
ROW_TILE = 256
MOE_ROWS = 256
MLA_KEY_CHUNK = 256
MLA_HEADS_PER_STEP = 2
WIDE = 512

SEG_A = 1024
SEG_B = 2048
SEG_C = 640
SEG_D = 2560


def _cparams(sem, vmem=VMEM_LIMIT_BYTES):
    return pltpu.CompilerParams(dimension_semantics=sem, vmem_limit_bytes=vmem)


def _ln_rows(x):
    mu = jnp.mean(x, axis=-1, keepdims=True)
    xc = x - mu
    var = jnp.mean(xc * xc, axis=-1, keepdims=True)
    return xc * lax.rsqrt(var + EPS)


def _silu(x):
    return x * (1.0 / (1.0 + jnp.exp(-x)))


def _dot(a, b):
    return jnp.dot(a, b, preferred_element_type=F32)


def _pack_bf16_pairs(x):
    k = x.shape[1] // 2
    hi = lax.bitcast_convert_type(x[:, :k].astype(BF16).astype(F32), jnp.uint32)
    lo = lax.bitcast_convert_type(x[:, k:].astype(BF16).astype(F32), jnp.uint32)
    return hi | (lo >> 16)


def _unpack_bf16_pairs(w):
    hi = lax.bitcast_convert_type(w & jnp.uint32(0xFFFF0000), F32)
    lo = lax.bitcast_convert_type(w << 16, F32)
    return jnp.concatenate([hi, lo], axis=1)


def _dot_nt(a, b):
    return lax.dot_general(a, b, (((1,), (1,)), ((), ())), preferred_element_type=F32)


def _dot_tn(a, b):
    return lax.dot_general(a, b, (((0,), (0,)), ((), ())), preferred_element_type=F32)


def _ada_kernel(c_ref, w_ref, b_ref, o_ref):
    s = _silu(c_ref[...]).astype(BF16)
    o_ref[0] = _dot(s, w_ref[0].astype(BF16)) + b_ref[0]


def _ada(cvec, w_ada, b_ada):
    depth, d, n6 = w_ada.shape
    tn = n6 // 8
    return pl.pallas_call(
        _ada_kernel,
        out_shape=jax.ShapeDtypeStruct((depth, SUBLANES, n6), F32),
        grid=(depth, n6 // tn),
        in_specs=[
            pl.BlockSpec((SUBLANES, d), lambda l, j: (0, 0)),
            pl.BlockSpec((1, d, tn), lambda l, j: (l, 0, j)),
            pl.BlockSpec((1, 1, tn), lambda l, j: (l, 0, j)),
        ],
        out_specs=pl.BlockSpec((1, SUBLANES, tn), lambda l, j: (l, 0, j)),
        compiler_params=_cparams(("arbitrary", "arbitrary")),
        name="ada_mod",
    )(cvec, w_ada, b_ada.reshape(depth, 1, n6))


SEG_WIDTHS = (SEG_A, SEG_B, SEG_C, SEG_D)
SEG_C_TRUE = MLA_Q_RANK + MLA_KV_RANK + MLA_ROPE
IN_COLS = SEG_A + SEG_B + SEG_C_TRUE + SEG_D
IN_COLS_PAD = sum(SEG_WIDTHS)


PREP_ROWS = 128
_C_END = SEG_A + SEG_B + SEG_C_TRUE
assert (_C_END + SEG_C - SEG_C_TRUE) % PREP_ROWS == 0


def _w_in_prep_kernel(w_ref, o_ref, prev_ref):
    j = pl.program_id(0)
    jc = _C_END // PREP_ROWS
    rem = _C_END % PREP_ROWS
    cur = w_ref[0].astype(BF16)

    @pl.when(j < jc)
    def _():
        o_ref[...] = cur

    @pl.when(j == jc)
    def _():
        o_ref[0:rem, :] = cur[0:rem]
        o_ref[rem:PREP_ROWS, :] = jnp.zeros((PREP_ROWS - rem, cur.shape[1]), BF16)

    @pl.when(j > jc)
    def _():
        o_ref[0:PREP_ROWS - rem, :] = prev_ref[rem:PREP_ROWS, :]
        o_ref[PREP_ROWS - rem:PREP_ROWS, :] = cur[0:rem]

    prev_ref[...] = cur


def _w_in_prep(w_in, layer):
    _, d, cols = w_in.shape
    assert cols == IN_COLS
    w_t = jnp.swapaxes(w_in, 1, 2)
    return pl.pallas_call(
        _w_in_prep_kernel,
        out_shape=jax.ShapeDtypeStruct((IN_COLS_PAD, d), BF16),
        grid=(IN_COLS_PAD // PREP_ROWS,),
        in_specs=[pl.BlockSpec((1, PREP_ROWS, d), lambda j: (layer, j, 0))],
        out_specs=pl.BlockSpec((PREP_ROWS, d), lambda j: (j, 0)),
        scratch_shapes=[pltpu.VMEM((PREP_ROWS, d), BF16)],
        compiler_params=_cparams(("arbitrary",)),
        name="w_in_prep",
    )(w_t)


def _proj_kernel(*refs, from_inputs, tiles_per_batch):
    if from_inputs:
        x_ref, ctx_ref, sh_ref, sc_ref, w_ref, oa, ob, oc, od, so = refs
        is_ctx = pl.program_id(0) % tiles_per_batch == 0
        x = jnp.where(is_ctx, ctx_ref[0], x_ref[0])
        so[...] = x
    else:
        x_ref, sh_ref, sc_ref, w_ref, oa, ob, oc, od = refs
        x = x_ref[...]
    y = _ln_rows(x)
    h = (y * (1.0 + sc_ref[0]) + sh_ref[0]).astype(BF16)
    base = 0
    for o_ref in (oa, ob, oc, od):
        width = o_ref.shape[1]
        for j in range(0, width, WIDE):
            cw = min(WIDE, width - j)
            o_ref[:, j:j + cw] = _dot_nt(h, w_ref[base + j:base + j + cw, :])
        base += width


def _group_of_tile(i, tiles_per_batch):
    return jnp.where(i % tiles_per_batch == 0, 2, i // tiles_per_batch)


def _proj(stream, x, ctx, shift, scale, w_pad, tiles_per_batch):
    from_inputs = stream is None
    tm = ROW_TILE
    if from_inputs:
        batch, t_len, d = x.shape
        n = batch * (t_len + ctx.shape[1])
        tpb = tiles_per_batch
        row_specs = [
            pl.BlockSpec((1, tm, d), lambda i: (i // tpb, jnp.maximum(i % tpb - 1, 0), 0)),
            pl.BlockSpec((1, tm, d), lambda i: (i // tpb, 0, 0)),
        ]
        row_args = [x, ctx]
    else:
        n, d = stream.shape
        row_specs = [pl.BlockSpec((tm, d), lambda i: (i, 0))]
        row_args = [stream]
    gmap = lambda i: (_group_of_tile(i, tiles_per_batch), 0, 0)
    widths = list(SEG_WIDTHS) + ([d] if from_inputs else [])
    return pl.pallas_call(
        functools.partial(_proj_kernel, from_inputs=from_inputs, tiles_per_batch=tiles_per_batch),
        out_shape=[jax.ShapeDtypeStruct((n, w), F32) for w in widths],
        grid=(n // tm,),
        in_specs=row_specs + [
            pl.BlockSpec((1, 1, d), gmap),
            pl.BlockSpec((1, 1, d), gmap),
            pl.BlockSpec(memory_space=pltpu.VMEM),
        ],
        out_specs=[pl.BlockSpec((tm, w), lambda i: (i, 0)) for w in widths],
        compiler_params=_cparams(("arbitrary",), vmem=PROJ_VMEM_LIMIT_BYTES),
        name="ln_mod_inproj",
    )(*row_args, shift, scale, w_pad)


def _rope128(x, cos, sin):
    return x * cos + pltpu.roll(x, 64, 1) * sin


def _rope64(x, cos, sin):
    lane = lax.broadcasted_iota(jnp.int32, x.shape, 1)
    rot = jnp.where((lane % 64) < 32, pltpu.roll(x, 96, 1), pltpu.roll(x, 32, 1))
    return x * cos + rot * sin


def _rope_tables(t_len, l_len, batch):
    rows = t_len // GRID_W
    row = np.repeat(np.arange(rows), GRID_W).astype(np.float32)
    col = np.tile(np.arange(GRID_W), rows).astype(np.float32)

    def angles(rot_dim):
        n_freq = rot_dim // 4
        inv = (ROPE_BASE ** (-np.arange(n_freq, dtype=np.float32) / n_freq)).astype(np.float32)
        return np.concatenate([row[:, None] * inv, col[:, None] * inv], -1).astype(np.float32)

    a_swa = angles(HEAD_DIM)
    cos_swa = np.concatenate([np.cos(a_swa), np.cos(a_swa)], -1)
    sin_swa = np.concatenate([-np.sin(a_swa), np.sin(a_swa)], -1)
    a_mla = angles(MLA_ROPE)
    cos_m = np.concatenate([np.cos(a_mla), np.cos(a_mla), np.ones((t_len, 64), np.float32)], -1)
    sin_m = np.concatenate([-np.sin(a_mla), np.sin(a_mla), np.zeros((t_len, 64), np.float32)], -1)
    ones = np.ones((l_len, LANES), np.float32)
    zeros = np.zeros((l_len, LANES), np.float32)
    cos_rows = np.concatenate([np.concatenate([ones, cos_m], 0)] * batch, 0)
    sin_rows = np.concatenate([np.concatenate([zeros, sin_m], 0)] * batch, 0)
    return (jnp.asarray(cos_swa, F32), jnp.asarray(sin_swa, F32),
            jnp.asarray(cos_rows, F32), jnp.asarray(sin_rows, F32))


def _mla_up_kernel(pc_ref, qn_ref, kvn_ref, wq_ref, wk_ref, wv_ref, cos_ref, sin_ref,
                   q_out, k_out, v_out, *, scale):
    pc = pc_ref[...]
    cq = pc[:, :MLA_Q_RANK]
    ckv = pc[:, MLA_Q_RANK:MLA_Q_RANK + MLA_KV_RANK]
    kr = pc[:, MLA_Q_RANK + MLA_KV_RANK:]
    cos = cos_ref[...]
    sin = sin_ref[...]

    def rms(x, g):
        return x * lax.rsqrt(jnp.mean(x * x, axis=-1, keepdims=True) + EPS) * g

    qh = _dot(rms(cq, qn_ref[...]).astype(BF16), wq_ref[...])
    ckn = rms(ckv, kvn_ref[...]).astype(BF16)
    kh = _dot(ckn, wk_ref[...])
    v_out[...] = _dot_nt(wv_ref[...], ckn).astype(BF16)
    kr_rot = _rope64(kr, cos, sin).astype(BF16)
    for h in range(MLA_HEADS):
        base = h * 2 * LANES
        q_out[:, base:base + LANES] = (qh[:, base:base + LANES] * scale).astype(BF16)
        q_out[:, base + LANES:base + 2 * LANES] = (
            _rope64(qh[:, base + LANES:base + 2 * LANES], cos, sin) * scale).astype(BF16)
        k_out[:, base:base + LANES] = kh[:, h * LANES:(h + 1) * LANES].astype(BF16)
        k_out[:, base + LANES:base + 2 * LANES] = kr_rot


def _mla_up(pc, q_norm, kv_norm, wq, wk, wv, cos_rows, sin_rows):
    n = pc.shape[0]
    tm = ROW_TILE
    scale = float((MLA_NOPE + MLA_ROPE) ** -0.5 * math.log2(math.e))
    const2 = lambda i: (0, 0)
    return pl.pallas_call(
        functools.partial(_mla_up_kernel, scale=scale),
        out_shape=[jax.ShapeDtypeStruct((n, MLA_HEADS * 2 * LANES), BF16),
                   jax.ShapeDtypeStruct((n, MLA_HEADS * 2 * LANES), BF16),
                   jax.ShapeDtypeStruct((MLA_HEADS * MLA_V, n), BF16)],
        grid=(n // tm,),
        in_specs=[
            pl.BlockSpec((tm, SEG_C), lambda i: (i, 0)),
            pl.BlockSpec((1, MLA_Q_RANK), const2),
            pl.BlockSpec((1, MLA_KV_RANK), const2),
            pl.BlockSpec(wq.shape, const2),
            pl.BlockSpec(wk.shape, const2),
            pl.BlockSpec(wv.shape, const2),
            pl.BlockSpec((tm, LANES), lambda i: (i, 0)),
            pl.BlockSpec((tm, LANES), lambda i: (i, 0)),
        ],
        out_specs=[pl.BlockSpec((tm, MLA_HEADS * 2 * LANES), lambda i: (i, 0)),
                   pl.BlockSpec((tm, MLA_HEADS * 2 * LANES), lambda i: (i, 0)),
                   pl.BlockSpec((MLA_HEADS * MLA_V, tm), lambda i: (0, i))],
        compiler_params=_cparams(("arbitrary",)),
        name="mla_up",
    )(pc, q_norm, kv_norm, wq, wk, wv, cos_rows, sin_rows)


def _mla_attn_kernel(q_ref, k_ref, vt_ref, o_ref, s_ref, p_ref, *, kc, l_len, need_ctx):
    p_len = k_ref.shape[0]
    tq = q_ref.shape[0]
    dqk = 2 * LANES
    heads = q_ref.shape[1] // dqk
    half = p_len // 2
    is_ctx = pl.program_id(2) == 0

    @pl.when(is_ctx)
    def _():
        if not need_ctx:
            o_ref[...] = jnp.zeros_like(o_ref)
            return
        for h in range(heads):
            hq = slice(h * dqk, (h + 1) * dqk)
            hv = slice(h * MLA_V, (h + 1) * MLA_V)
            s = _dot_nt(k_ref[0:l_len, hq], q_ref[:, hq])
            p = jnp.exp2(s - jnp.max(s, axis=0, keepdims=True))
            acc = _dot(vt_ref[hv, 0:l_len], p.astype(BF16))
            o_ref[:, hv] = (acc / jnp.sum(p, axis=0, keepdims=True)).T

    @pl.when(jnp.logical_not(is_ctx))
    def _():
        for h in range(heads):
            q = q_ref[:, h * dqk:(h + 1) * dqk]
            s_ref[h, 0:half, :] = _dot_nt(k_ref[0:half, h * dqk:(h + 1) * dqk], q)
            s_ref[h, half:p_len, :] = _dot_nt(k_ref[half:p_len, h * dqk:(h + 1) * dqk], q)
        fold = lambda a: a.reshape(kc // SUBLANES, SUBLANES, tq)
        for h in range(heads):
            m8 = None
            for off in range(0, p_len, kc):
                cm = jnp.max(fold(s_ref[h, off:off + kc, :]), axis=0)
                m8 = cm if m8 is None else jnp.maximum(m8, cm)
            m = jnp.max(m8, axis=0, keepdims=True)
            l8 = jnp.zeros((SUBLANES, tq), F32)
            for off in range(0, p_len, kc):
                p = jnp.exp2(s_ref[h, off:off + kc, :] - m)
                l8 = l8 + jnp.sum(fold(p), axis=0)
                p_ref[h, off:off + kc, :] = p.astype(BF16)
            l = jnp.sum(l8, axis=0, keepdims=True)
            hv = slice(h * MLA_V, (h + 1) * MLA_V)
            acc = (_dot(vt_ref[hv, 0:half], p_ref[h, 0:half, :])
                   + _dot(vt_ref[hv, half:p_len], p_ref[h, half:p_len, :]))
            o_ref[:, hv] = (acc / l).T


def _mla_attn(qf, kf, vt, batch, t_len, l_len, need_ctx):
    n = qf.shape[0]
    p_len = t_len + l_len
    tq = ROW_TILE
    assert l_len == tq
    kc = MLA_KEY_CHUNK
    hp = MLA_HEADS_PER_STEP
    tpb = p_len // tq
    return pl.pallas_call(
        functools.partial(_mla_attn_kernel, kc=kc, l_len=l_len, need_ctx=need_ctx),
        out_shape=jax.ShapeDtypeStruct((n, MLA_HEADS * MLA_V), F32),
        grid=(batch, MLA_HEADS // hp, tpb),
        in_specs=[
            pl.BlockSpec((tq, hp * 2 * LANES), lambda b, h, i: (b * tpb + i, h)),
            pl.BlockSpec((p_len, hp * 2 * LANES), lambda b, h, i: (b, h)),
            pl.BlockSpec((hp * MLA_V, p_len), lambda b, h, i: (h, b)),
        ],
        out_specs=pl.BlockSpec((tq, hp * MLA_V), lambda b, h, i: (b * tpb + i, h)),
        scratch_shapes=[pltpu.VMEM((hp, p_len, tq), F32), pltpu.VMEM((hp, p_len, tq), BF16)],
        compiler_params=_cparams(("arbitrary", "arbitrary", "arbitrary")),
        name="mla_attn",
    )(qf, kf, vt)


def _swa_kernel(sink_ref, q_ref, kp_ref, kc_ref, kn_ref, vp_ref, vc_ref, vn_ref, kx_ref, vx_ref,
                cq_ref, sq_ref, cp_ref, sp_ref, cn_ref, sn_ref, o_ref, *, nb, n_ctx, scale, need_ctx):
    j = pl.program_id(0)
    n = j - n_ctx
    blk = SWA_BLOCK
    batch = q_ref.shape[0]
    rr = lax.broadcasted_iota(jnp.int32, (2 * blk, 1), 0)
    mx = lambda a: jnp.max(a, axis=-1, keepdims=True)
    sm = lambda a: jnp.sum(a, axis=-1, keepdims=True)

    def heads_of(b, h):
        q0 = q_ref[b, :, 2 * h * LANES:(2 * h + 1) * LANES]
        q1 = q_ref[b, :, (2 * h + 1) * LANES:(2 * h + 2) * LANES]
        sk = jnp.where(rr < blk, sink_ref[2 * h], sink_ref[2 * h + 1]).astype(F32)
        return q0, q1, sk

    def store(b, h, o):
        o_ref[b, :, 2 * h * LANES:(2 * h + 1) * LANES] = o[:blk]
        o_ref[b, :, (2 * h + 1) * LANES:(2 * h + 2) * LANES] = o[blk:]

    @pl.when(j < n_ctx)
    def _():
        if not need_ctx:
            o_ref[...] = jnp.zeros_like(o_ref)
            return
        for b in range(batch):
            for h in range(SWA_KV_HEADS):
                ks = slice(h * LANES, (h + 1) * LANES)
                q0, q1, sk = heads_of(b, h)
                qq = jnp.concatenate([q0, q1], axis=0).astype(BF16)
                s_x = _dot_nt(qq, kx_ref[b, :, ks].astype(BF16)) * scale
                m = jnp.maximum(mx(s_x), sk)
                p_x = jnp.exp(s_x - m)
                den = sm(p_x) + jnp.exp(sk - m)
                store(b, h, _dot(p_x.astype(BF16), vx_ref[b, :, ks].astype(BF16)) / den)

    @pl.when(j >= n_ctx)
    def _():
        cq, sq = cq_ref[...], sq_ref[...]
        cp, sp = cp_ref[...], sp_ref[...]
        cn, sn = cn_ref[...], sn_ref[...]
        ri = lax.broadcasted_iota(jnp.int32, (2 * blk, blk), 0) % blk
        ci = lax.broadcasted_iota(jnp.int32, (2 * blk, blk), 1)
        mask_p = (ci >= ri) & (n > 0)
        mask_n = (ci <= ri) & (n < nb - 1)
        neg = jnp.float32(-jnp.inf)
        for b in range(batch):
            for h in range(SWA_KV_HEADS):
                ks = slice(h * LANES, (h + 1) * LANES)
                q0, q1, sk = heads_of(b, h)
                qq = jnp.concatenate([_rope128(q0, cq, sq), _rope128(q1, cq, sq)], axis=0).astype(BF16)
                kc = _rope128(kc_ref[b, :, ks], cq, sq).astype(BF16)
                kp = _rope128(kp_ref[b, :, ks], cp, sp).astype(BF16)
                kn = _rope128(kn_ref[b, :, ks], cn, sn).astype(BF16)
                s_c = _dot_nt(qq, kc) * scale
                s_p = jnp.where(mask_p, _dot_nt(qq, kp) * scale, neg)
                s_n = jnp.where(mask_n, _dot_nt(qq, kn) * scale, neg)
                s_x = _dot_nt(qq, kx_ref[b, :, ks].astype(BF16)) * scale
                m = jnp.maximum(jnp.maximum(mx(s_c), mx(s_p)), jnp.maximum(mx(s_n), mx(s_x)))
                m = jnp.maximum(m, sk)
                p_c, p_p, p_n, p_x = (jnp.exp(a - m) for a in (s_c, s_p, s_n, s_x))
                den = sm(p_c) + sm(p_p) + sm(p_n) + sm(p_x) + jnp.exp(sk - m)
                store(b, h, (_dot(p_c.astype(BF16), vc_ref[b, :, ks].astype(BF16))
                             + _dot(p_p.astype(BF16), vp_ref[b, :, ks].astype(BF16))
                             + _dot(p_n.astype(BF16), vn_ref[b, :, ks].astype(BF16))
                             + _dot(p_x.astype(BF16), vx_ref[b, :, ks].astype(BF16))) / den)


def _swa(pa, sink, cos_t, sin_t, batch, t_len, l_len, need_ctx):
    n = pa.shape[0]
    p_len = t_len + l_len
    blk = SWA_BLOCK
    nb = t_len // blk
    n_ctx = l_len // blk
    scale = float(HEAD_DIM ** -0.5)
    kv_w = SWA_KV_HEADS * HEAD_DIM
    q_w = SWA_HEADS * HEAD_DIM
    kcol, vcol = q_w // kv_w, q_w // kv_w + 1
    pa3 = pa.reshape(batch, p_len, pa.shape[1])
    lat = lambda j: jnp.maximum(j - n_ctx, 0)
    same = lambda j: lat(j)
    prv = lambda j: jnp.maximum(lat(j) - 1, 0)
    nxt = lambda j: jnp.minimum(lat(j) + 1, nb - 1)
    kv = lambda col, rowf: pl.BlockSpec((batch, blk, kv_w), lambda j: (0, n_ctx + rowf(j), col))
    tab = lambda rowf: pl.BlockSpec((blk, LANES), lambda j: (rowf(j), 0))
    in_specs = [
        pl.BlockSpec(memory_space=pltpu.SMEM),
        pl.BlockSpec((batch, blk, q_w), lambda j: (0, j, 0)),
        kv(kcol, prv), kv(kcol, same), kv(kcol, nxt),
        kv(vcol, prv), kv(vcol, same), kv(vcol, nxt),
        pl.BlockSpec((batch, l_len, kv_w), lambda j: (0, 0, kcol)),
        pl.BlockSpec((batch, l_len, kv_w), lambda j: (0, 0, vcol)),
        tab(same), tab(same), tab(prv), tab(prv), tab(nxt), tab(nxt),
    ]
    out = pl.pallas_call(
        functools.partial(_swa_kernel, nb=nb, n_ctx=n_ctx, scale=scale, need_ctx=need_ctx),
        out_shape=jax.ShapeDtypeStruct((batch, p_len, q_w), F32),
        grid=(n_ctx + nb,),
        in_specs=in_specs,
        out_specs=pl.BlockSpec((batch, blk, q_w), lambda j: (0, j, 0)),
        compiler_params=_cparams(("arbitrary",)),
        name="swa",
    )(sink, pa3, pa3, pa3, pa3, pa3, pa3, pa3, pa3, pa3, cos_t, sin_t, cos_t, sin_t, cos_t, sin_t)
    return out.reshape(n, q_w)


def _chunk_index(d, c, n_l, n_t):
    bwd = jnp.where(c < n_l, n_l - 1 - c, n_l + (n_t - 1) - (c - n_l))
    return jnp.where(d == 0, c, bwd)


def _flip_iotas(d):
    row = lax.broadcasted_iota(jnp.int32, (CHUNK, CHUNK), 0)
    col = lax.broadcasted_iota(jnp.int32, (CHUNK, CHUNK), 1)
    rf = jnp.where(d == 0, row, CHUNK - 1 - row)
    cf = jnp.where(d == 0, col, CHUNK - 1 - col)
    return rf, cf


def _ret_kernel(s_ref, qf_ref, kf_ref, vf_ref, qb_ref, kb_ref, vb_ref, of_ref, ob_ref, st_ref):
    c = pl.program_id(0)
    batch = qf_ref.shape[0]

    @pl.when(c == 0)
    def _():
        st_ref[...] = jnp.zeros_like(st_ref)

    ks = float(RET_DK ** -0.5)
    dirs = ((qf_ref, kf_ref, vf_ref, of_ref), (qb_ref, kb_ref, vb_ref, ob_ref))
    for d, (q_ref, k_ref, v_ref, o_ref) in enumerate(dirs):
        rf, cf = _flip_iotas(d)
        rff = rf.astype(F32)
        dn = (rf - cf).astype(F32)
        for h in range(RET_HEADS):
            hs = slice(h * LANES, (h + 1) * LANES)
            sv = jnp.full((CHUNK, CHUNK), s_ref[d * RET_HEADS + h], F32)
            lg = jnp.log1p(-jnp.exp2(-sv))
            dec = jnp.where(dn >= 0, jnp.exp(dn * lg), 0.0)
            eq = jnp.exp((rff + 1.0) * lg)
            ek = jnp.exp((CHUNK - 1.0 - rff) * lg)
            a_chunk = jnp.exp(float(CHUNK) * lg)
            for b in range(batch):
                q = q_ref[b, :, hs]
                k = k_ref[b, :, hs] * ks
                v = v_ref[b, :, hs].astype(BF16)
                st = st_ref[d, b, h]
                a = _dot_nt(q.astype(BF16), k.astype(BF16)) * dec
                o = _dot(a.astype(BF16), v) + _dot_nt((q * eq).astype(BF16), st.astype(BF16))
                o_ref[b, :, hs] = o
                st_ref[d, b, h] = a_chunk * st + _dot_tn(v, (k * ek).astype(BF16))


def _hgrn_level_mats():
    c = CHUNK
    t = np.arange(c)[:, None]
    u = np.arange(c)[None, :]
    mats = [u <= t, u > t]
    m = c // 2
    while m >= 1:
        mid = (t // (2 * m)) * 2 * m + m
        second = (t % (2 * m)) >= m
        qrole = (u >= mid) & (u <= t)
        krole = (u > t) & (u <= mid - 1)
        mats.append(np.where(second, qrole, krole))
        m //= 2
    fwd = np.concatenate(mats, 0).astype(np.float32)
    bwd = np.concatenate([mm[::-1, ::-1] for mm in mats], 0).astype(np.float32)
    return np.stack([fwd, bwd], 0)


N_LEVELS = int(math.log2(CHUNK))


def _hgrn_kernel(m_ref, lb_ref, qf_ref, ff_ref, vf_ref, qb_ref, fb_ref, vb_ref, of_ref, ob_ref, st_ref):
    c = pl.program_id(0)
    batch = qf_ref.shape[0]

    @pl.when(c == 0)
    def _():
        st_ref[...] = jnp.zeros_like(st_ref)

    dirs = ((qf_ref, ff_ref, vf_ref, of_ref), (qb_ref, fb_ref, vb_ref, ob_ref))
    for d, (q_ref, f_ref, v_ref, o_ref) in enumerate(dirs):
        rf, cf = _flip_iotas(d)
        mst = m_ref[d]
        last = CHUNK - 1 if d == 0 else 0
        for h in range(HGRN_HEADS):
            hs = slice(h * LANES, (h + 1) * LANES)
            lb = lb_ref[d, :, hs]
            for b in range(batch):
                f = lb + (1.0 - lb) * (1.0 / (1.0 + jnp.exp(-f_ref[b, :, hs])))
                k = 1.0 - f
                g = jnp.log(f)
                q = _silu(q_ref[b, :, hs])
                v = v_ref[b, :, hs].astype(BF16)
                e = jnp.exp(_dot(mst, g.astype(BF16)))
                e_q = e[0:CHUNK]
                q_in = q * e_q
                k_st = k * e[CHUNK:2 * CHUNK]
                scores = jnp.where(rf == cf, _dot_nt(q.astype(BF16), k.astype(BF16)), 0.0)
                for lvl in range(N_LEVELS):
                    m = CHUNK >> (lvl + 1)
                    sh = N_LEVELS - lvl
                    el = e[(2 + lvl) * CHUNK:(3 + lvl) * CHUNK]
                    second = (rf & m) != 0
                    ql = jnp.where(second, q * el, 0.0).astype(BF16)
                    kl = jnp.where(second, 0.0, k * el).astype(BF16)
                    sl = _dot_nt(ql, kl)
                    scores = scores + jnp.where((rf >> sh) == (cf >> sh), sl, 0.0)
                st = st_ref[d, b, h]
                o = _dot(scores.astype(BF16), v) + _dot_nt(q_in.astype(BF16), st.astype(BF16))
                o_ref[b, :, hs] = o
                st_ref[d, b, h] = st * e_q[last:last + 1, :] + _dot_tn(v, k_st.astype(BF16))


def _scan_specs(batch, t_len, l_len):
    n_l, n_t = l_len // CHUNK, t_len // CHUNK

    def spec(d, col):
        return pl.BlockSpec((batch, CHUNK, WIDE), lambda c: (0, _chunk_index(d, c, n_l, n_t), col))

    return spec, n_l + n_t


def _retention(pb, ret_s, batch, t_len, l_len):
    n = pb.shape[0]
    p_len = t_len + l_len
    pb3 = pb.reshape(batch, p_len, pb.shape[1])
    spec, n_chunks = _scan_specs(batch, t_len, l_len)
    out_f, out_b = pl.pallas_call(
        _ret_kernel,
        out_shape=[jax.ShapeDtypeStruct((batch, p_len, WIDE), F32)] * 2,
        grid=(n_chunks,),
        in_specs=[pl.BlockSpec(memory_space=pltpu.SMEM)]
                 + [spec(0, col) for col in (0, 1, 2)] + [spec(1, col) for col in (0, 1, 2)],
        out_specs=[spec(0, 0), spec(1, 0)],
        scratch_shapes=[pltpu.VMEM((2, batch, RET_HEADS, CHUNK, CHUNK), F32)],
        compiler_params=_cparams(("arbitrary",)),
        name="retention_scan",
    )(ret_s.reshape(-1), pb3, pb3, pb3, pb3, pb3, pb3)
    return out_f.reshape(n, WIDE), out_b.reshape(n, WIDE)


def _hgrn(pd, lower_bounds, mats, batch, t_len, l_len):
    n = pd.shape[0]
    p_len = t_len + l_len
    pd3 = pd.reshape(batch, p_len, pd.shape[1])
    lb3 = lower_bounds.reshape(2, 1, WIDE)
    spec, n_chunks = _scan_specs(batch, t_len, l_len)
    out_f, out_b = pl.pallas_call(
        _hgrn_kernel,
        out_shape=[jax.ShapeDtypeStruct((batch, p_len, WIDE), F32)] * 2,
        grid=(n_chunks,),
        in_specs=[pl.BlockSpec(mats.shape, lambda c: (0, 0, 0)),
                  pl.BlockSpec(lb3.shape, lambda c: (0, 0, 0)),
                  spec(0, 0), spec(0, 1), spec(0, 3), spec(1, 0), spec(1, 2), spec(1, 3)],
        out_specs=[spec(0, 0), spec(1, 0)],
        scratch_shapes=[pltpu.VMEM((2, batch, HGRN_HEADS, CHUNK, CHUNK), F32)],
        compiler_params=_cparams(("arbitrary",)),
        name="hgrn_scan",
    )(mats, lb3, pd3, pd3, pd3, pd3, pd3, pd3)
    return out_f.reshape(n, WIDE), out_b.reshape(n, WIDE)


R_E1, R_E2, R_G1, R_G2, R_S1, R_S2 = 0, 1, 2, 3, 4, 5


def _route(logits, count_ref, active):
    tm = logits.shape[0]
    lane = lax.broadcasted_iota(jnp.int32, logits.shape, 1)
    lanef = lane.astype(F32)
    big = jnp.float32(1e9)
    neg = jnp.float32(-jnp.inf)
    mx = lambda a: jnp.max(a, axis=-1, keepdims=True)
    mn = lambda a: jnp.min(a, axis=-1, keepdims=True)
    sm = lambda a: jnp.sum(a, axis=-1, keepdims=True)
    gl = jnp.where(lane < N_GROUPS, logits, neg)
    gm = mx(gl)
    p_grp = 1.0 / sm(jnp.exp(gl - gm))
    grp = mn(jnp.where(gl == gm, lanef, big))
    lo = N_GROUPS + grp * EXPERTS_PER_GROUP
    ing = (lanef >= lo) & (lanef < lo + EXPERTS_PER_GROUP)
    el = jnp.where(ing, logits, neg)
    ee = jnp.exp(el - mx(el))
    p = ee / sm(ee)
    pm = jnp.where(ing, p, -1.0)
    p1 = mx(pm)
    i1 = mn(jnp.where(pm == p1, lanef, big))
    pm2 = jnp.where(lanef == i1, -1.0, pm)
    p2 = mx(pm2)
    i2 = mn(jnp.where(pm2 == p2, lanef, big))
    den = p1 + p2
    g1 = p_grp * p1 / den
    g2 = p_grp * p2 / den
    e1 = i1 - N_GROUPS
    e2 = i2 - N_GROUPS
    oh1 = lanef == e1
    oh2 = lanef == e2
    both = jnp.where(oh1 | oh2, active, 0.0)
    ri = lax.broadcasted_iota(jnp.int32, (tm, tm), 0)
    ci = lax.broadcasted_iota(jnp.int32, (tm, tm), 1)
    earlier = jnp.where(ci < ri, 1.0, 0.0).astype(BF16)
    before = _dot(earlier, both.astype(BF16)) + count_ref[0:1, :]
    s1 = sm(jnp.where(oh1, before, 0.0))
    s2 = sm(jnp.where(oh2, before, 0.0))
    count_ref[...] = count_ref[...] + jnp.sum(both, axis=0, keepdims=True)
    out = jnp.zeros(logits.shape, F32)
    for ln, val in ((R_E1, e1), (R_E2, e2), (R_G1, g1), (R_G2, g2), (R_S1, s1), (R_S2, s2)):
        out = jnp.where(lane == ln, val, out)
    return out


def _out_kernel(x_ref, a_ref, bf_ref, bb_ref, c_ref, df_ref, db_ref, rg_ref, hg_ref, g1_ref, sh2_ref, sc2_ref,
                lng_ref, lnb_ref, rgn_ref, hgn_ref, wo_ref, wr_ref,
                xo_ref, h2_ref, r_ref, cnt_ref, *, alpha, tiles_per_batch, route_ctx):
    i = pl.program_id(0)

    @pl.when(i == 0)
    def _():
        cnt_ref[...] = jnp.zeros_like(cnt_ref)

    bsum = bf_ref[...] + bb_ref[...]
    dsum = df_ref[...] + db_ref[...]
    rgn = rgn_ref[...]
    hgn = hgn_ref[...]
    parts_b, parts_d = [], []
    for h in range(4):
        hs = slice(h * LANES, (h + 1) * LANES)
        parts_b.append(_ln_rows(bsum[:, hs]))
        dh = dsum[:, hs]
        parts_d.append(dh * lax.rsqrt(jnp.mean(dh * dh, axis=-1, keepdims=True) + EPS))
    bo = _silu(rg_ref[...]) * (jnp.concatenate(parts_b, axis=1) * rgn)
    do = _silu(hg_ref[...]) * (jnp.concatenate(parts_d, axis=1) * hgn)
    mix = jnp.concatenate([a_ref[...], bo, c_ref[...], do], axis=1).astype(BF16)
    o = _dot(mix, wo_ref[...])
    y = alpha * x_ref[...] + g1_ref[0] * o
    xn = _ln_rows(y) * lng_ref[...] + lnb_ref[...]
    xo_ref[...] = xn
    h2 = _ln_rows(xn) * (1.0 + sc2_ref[0]) + sh2_ref[0]
    h2_ref[...] = _pack_bf16_pairs(h2)
    if route_ctx:
        active = jnp.float32(1.0)
    else:
        active = jnp.where(i % tiles_per_batch == 0, 0.0, 1.0).astype(F32)
    r_ref[...] = _route(_dot(h2.astype(BF16), wr_ref[...]), cnt_ref, active)


def _out_proj(stream, a, b_f, b_b, c, d_f, d_b, pb, pd, g1, sh2, sc2, ln_g, ln_b, ret_gn, hgrn_gn, w_out, w_r,
              tiles_per_batch, alpha, route_ctx):
    n, dm = stream.shape
    tm = ROW_TILE
    gmap = lambda i: (_group_of_tile(i, tiles_per_batch), 0, 0)
    row = lambda w: pl.BlockSpec((tm, w), lambda i: (i, 0))
    const = lambda w: pl.BlockSpec((1, w), lambda i: (0, 0))
    resident = pl.BlockSpec(memory_space=pltpu.VMEM)
    return pl.pallas_call(
        functools.partial(_out_kernel, alpha=alpha, tiles_per_batch=tiles_per_batch, route_ctx=route_ctx),
        out_shape=[jax.ShapeDtypeStruct((n, dm), F32), jax.ShapeDtypeStruct((n, dm // 2), jnp.uint32),
                   jax.ShapeDtypeStruct((n, LANES), F32), jax.ShapeDtypeStruct((SUBLANES, LANES), F32)],
        grid=(n // tm,),
        in_specs=[row(dm), row(WIDE), row(WIDE), row(WIDE), row(WIDE), row(WIDE), row(WIDE),
                  pl.BlockSpec((tm, WIDE), lambda i: (i, 3)),
                  pl.BlockSpec((tm, WIDE), lambda i: (i, 4)),
                  pl.BlockSpec((1, 1, dm), gmap), pl.BlockSpec((1, 1, dm), gmap),
                  pl.BlockSpec((1, 1, dm), gmap),
                  const(dm), const(dm), const(WIDE), const(WIDE), resident, resident],
        out_specs=[row(dm), row(dm // 2), row(LANES),
                   pl.BlockSpec((SUBLANES, LANES), lambda i: (0, 0))],
        compiler_params=_cparams(("arbitrary",)),
        name="mix_outproj_norm_route",
    )(stream, a, b_f, b_b, c, d_f, d_b, pb, pd, g1, sh2, sc2, ln_g, ln_b, ret_gn, hgrn_gn, w_out, w_r)


def _route_plan(route, counts_f, n_blocks):
    rows = MOE_ROWS
    counts = counts_f[0, :N_EXPERTS].astype(jnp.int32)
    padded = (counts + rows - 1) // rows * rows
    pend = jnp.cumsum(padded)
    pstart = pend - padded
    eid = route[:, R_E1:R_E2 + 1].astype(jnp.int32)
    rank = route[:, R_S1:R_S2 + 1].astype(jnp.int32)
    onehot = eid[:, :, None] == jnp.arange(N_EXPERTS, dtype=jnp.int32)
    pos = jnp.sum(jnp.where(onehot, pstart, 0), axis=-1) + rank
    blk_start = jnp.arange(n_blocks, dtype=jnp.int32) * rows
    blk_e = jnp.minimum(jnp.sum(blk_start[:, None] >= pend[None, :], axis=-1), N_EXPERTS - 1)
    blk_valid = (blk_start < pend[-1]).astype(jnp.int32)
    pad_start = pstart + counts
    ids = jnp.arange(N_EXPERTS, dtype=jnp.int32)
    nonempty = counts > 0
    seq = jnp.cumsum(nonempty.astype(jnp.int32)) - nonempty.astype(jnp.int32)
    cand = jnp.where(nonempty, ids, N_EXPERTS)
    later = jnp.where(ids[None, :] > ids[:, None], cand[None, :], N_EXPERTS)
    nxt = jnp.min(later, axis=-1)
    nxt = jnp.where(nxt >= N_EXPERTS, -1, nxt)
    blk_oh = blk_e[:, None] == ids[None, :]
    blk_slot = jnp.sum(jnp.where(blk_oh, seq % 2, 0), axis=-1)
    blk_next = jnp.sum(jnp.where(blk_oh, nxt, 0), axis=-1)
    i32 = lambda a: a.astype(jnp.int32)
    return (i32(pos.reshape(-1)), i32(blk_e), blk_valid, i32(pad_start), i32(blk_slot), i32(blk_next))


def _dispatch_kernel(pos_ref, pad_ref, bv_ref, h_ref, x_hbm, zbuf, sem, *, tile_of_step, n_blocks):
    i = pl.program_id(0)
    tm = ROW_TILE
    rows = MOE_ROWS

    def zero_copy(e):
        start = pl.multiple_of(pad_ref[e] & ~(SUBLANES - 1), SUBLANES)
        return pltpu.make_async_copy(zbuf, x_hbm.at[pl.ds(start, rows + SUBLANES)], sem.at[1])

    def unused_block_copy(j):
        start = pl.multiple_of(j * rows, rows)
        return pltpu.make_async_copy(zbuf.at[pl.ds(0, rows)], x_hbm.at[pl.ds(start, rows)], sem.at[1])

    @pl.when(i == 0)
    def _():
        zbuf[...] = jnp.zeros_like(zbuf)

        def fill(j, carry):
            @pl.when(bv_ref[j] == 0)
            def _():
                unused_block_copy(j).start()
                unused_block_copy(j).wait()
            return carry

        lax.fori_loop(0, n_blocks + 2, fill, 0)
        for e in range(N_EXPERTS):
            zero_copy(e).start()
        for e in range(N_EXPERTS):
            zero_copy(e).wait()

    def row_copy(r, k):
        p = pos_ref[(tile_of_step(i) * tm + r) * TOP_K + k]
        return pltpu.make_async_copy(h_ref.at[r], x_hbm.at[p], sem.at[0])

    for r in range(tm):
        for k in range(TOP_K):
            row_copy(r, k).start(priority=k)
    for r in range(tm):
        for k in range(TOP_K):
            row_copy(r, k).wait()


def _tile_schedule(n, tiles_per_batch, latent_only, batch):
    if latent_only:
        lat = tiles_per_batch - 1
        return batch * lat, (lambda s: (s // lat) * tiles_per_batch + 1 + s % lat)
    return n // ROW_TILE, (lambda s: s)


def _dispatch(h2, pos, pad_start, blk_valid, tiles_per_batch, latent_only, batch):
    n, dm = h2.shape
    tm = ROW_TILE
    n_blocks = blk_valid.shape[0]
    blk_valid = jnp.concatenate([blk_valid, jnp.zeros((2,), jnp.int32)])
    n_tiles, tile_of_step = _tile_schedule(n, tiles_per_batch, latent_only, batch)
    grid_spec = pltpu.PrefetchScalarGridSpec(
        num_scalar_prefetch=3,
        grid=(n_tiles,),
        in_specs=[pl.BlockSpec((tm, dm), lambda i, p, z, v: (tile_of_step(i), 0))],
        out_specs=pl.BlockSpec(memory_space=pl.ANY),
        scratch_shapes=[pltpu.VMEM((MOE_ROWS + SUBLANES, dm), h2.dtype), pltpu.SemaphoreType.DMA((2,))],
    )
    return pl.pallas_call(
        functools.partial(_dispatch_kernel, tile_of_step=tile_of_step, n_blocks=n_blocks),
        out_shape=jax.ShapeDtypeStruct(((n_blocks + 2) * MOE_ROWS, dm), h2.dtype),
        grid_spec=grid_spec,
        compiler_params=_cparams(("arbitrary",)),
        name="moe_dispatch",
    )(pos, pad_start, blk_valid, h2)


def _moe_kernel(be_ref, bv_ref, slot_ref, next_ref, x_ref, w1_hbm, w3_hbm, w2_hbm, y_ref,
                s1, s3, s2, w1b, w3b, w2b, sem, *, layer):
    j = pl.program_id(0)
    valid = bv_ref[j] == 1
    first = valid & ((j == 0) | (be_ref[j] != be_ref[jnp.maximum(j - 1, 0)]))

    def weight_copies(e, slot):
        return (pltpu.make_async_copy(w1_hbm.at[layer, e], s1.at[slot], sem.at[slot]),
                pltpu.make_async_copy(w3_hbm.at[layer, e], s3.at[slot], sem.at[slot]),
                pltpu.make_async_copy(w2_hbm.at[layer, e], s2.at[slot], sem.at[slot]))

    @pl.when(valid & (j == 0))
    def _():
        for cp in weight_copies(be_ref[0], slot_ref[0]):
            cp.start()

    @pl.when(first)
    def _():
        slot = slot_ref[j]
        for cp in weight_copies(be_ref[j], slot):
            cp.wait()
        nxt = next_ref[j]

        @pl.when(nxt >= 0)
        def _():
            for cp in weight_copies(nxt, 1 - slot):
                cp.start()

        w1b[...] = s1[slot].astype(BF16)
        w3b[...] = s3[slot].astype(BF16)
        w2b[...] = s2[slot].astype(BF16)

    @pl.when(valid)
    def _():
        x = _unpack_bf16_pairs(x_ref[...]).astype(BF16)
        hmid = _silu(_dot(x, w1b[...])) * _dot(x, w3b[...])
        y_ref[...] = _pack_bf16_pairs(_dot(hmid.astype(BF16), w2b[...]))

    @pl.when(bv_ref[j] == 0)
    def _():
        y_ref[...] = jnp.zeros_like(y_ref)


def _moe_experts(xbuf, blk_e, blk_valid, blk_slot, blk_next, w1, w3, w2, layer):
    dm = w1.shape[2]
    ff = w1.shape[-1]
    rows = MOE_ROWS
    n_blocks = blk_e.shape[0]
    hbm = pl.BlockSpec(memory_space=pl.ANY)
    grid_spec = pltpu.PrefetchScalarGridSpec(
        num_scalar_prefetch=4,
        grid=(n_blocks,),
        in_specs=[
            pl.BlockSpec((rows, dm // 2), lambda j, be, bv, sl, nx: (j * bv[j], 0)),
            hbm, hbm, hbm,
        ],
        out_specs=pl.BlockSpec((rows, dm // 2), lambda j, be, bv, sl, nx: (j, 0)),
        scratch_shapes=[
            pltpu.VMEM((2, dm, ff), F32), pltpu.VMEM((2, dm, ff), F32), pltpu.VMEM((2, ff, dm), F32),
            pltpu.VMEM((dm, ff), BF16), pltpu.VMEM((dm, ff), BF16), pltpu.VMEM((ff, dm), BF16),
            pltpu.SemaphoreType.DMA((2,)),
        ],
    )
    return pl.pallas_call(
        functools.partial(_moe_kernel, layer=layer),
        out_shape=jax.ShapeDtypeStruct((n_blocks * rows, dm // 2), jnp.uint32),
        grid_spec=grid_spec,
        compiler_params=_cparams(("arbitrary",)),
        name="moe_experts",
    )(blk_e, blk_valid, blk_slot, blk_next, xbuf, w1, w3, w2)


def _combine_kernel(pos_ref, x_ref, r_ref, g2_ref, lng_ref, lnb_ref, y_hbm, o_ref, ybuf, sem,
                    *, n_tiles, tile_of_step, alpha):
    i = pl.program_id(0)
    slot = i % 2
    tm = ROW_TILE

    def row_copy(step, sl, r, k):
        p = pos_ref[(tile_of_step(step) * tm + r) * TOP_K + k]
        return pltpu.make_async_copy(y_hbm.at[p], ybuf.at[sl, k, r], sem.at[sl])

    def start_gather(step, sl):
        for r in range(tm):
            for k in range(TOP_K):
                row_copy(step, sl, r, k).start(priority=k)

    @pl.when(i == 0)
    def _():
        start_gather(0, 0)

    @pl.when(i + 1 < n_tiles)
    def _():
        start_gather(jnp.minimum(i + 1, n_tiles - 1), 1 - slot)

    for r in range(tm):
        for k in range(TOP_K):
            row_copy(i, slot, r, k).wait()
    route = r_ref[...]
    y = (_unpack_bf16_pairs(ybuf[slot, 0]) * route[:, R_G1:R_G1 + 1]
         + _unpack_bf16_pairs(ybuf[slot, 1]) * route[:, R_G2:R_G2 + 1])
    z = alpha * x_ref[...] + g2_ref[0] * y
    o_ref[...] = _ln_rows(z) * lng_ref[...] + lnb_ref[...]


def _combine(x_new, route, ybuf, pos, g2, ln_g, ln_b, tiles_per_batch, alpha, latent_only, batch):
    n, dm = x_new.shape
    tm = ROW_TILE
    n_tiles, tile_of_step = _tile_schedule(n, tiles_per_batch, latent_only, batch)
    gmap = lambda i, p: (_group_of_tile(tile_of_step(i), tiles_per_batch), 0, 0)
    grid_spec = pltpu.PrefetchScalarGridSpec(
        num_scalar_prefetch=1,
        grid=(n_tiles,),
        in_specs=[
            pl.BlockSpec((tm, dm), lambda i, p: (tile_of_step(i), 0)),
            pl.BlockSpec((tm, LANES), lambda i, p: (tile_of_step(i), 0)),
            pl.BlockSpec((1, 1, dm), gmap),
            pl.BlockSpec((1, dm), lambda i, p: (0, 0)),
            pl.BlockSpec((1, dm), lambda i, p: (0, 0)),
            pl.BlockSpec(memory_space=pl.ANY),
        ],
        out_specs=pl.BlockSpec((tm, dm), lambda i, p: (i, 0)),
        scratch_shapes=[pltpu.VMEM((2, TOP_K, tm, dm // 2), jnp.uint32), pltpu.SemaphoreType.DMA((2,))],
    )
    return pl.pallas_call(
        functools.partial(_combine_kernel, n_tiles=n_tiles, tile_of_step=tile_of_step, alpha=alpha),
        out_shape=jax.ShapeDtypeStruct((n_tiles * tm, dm), F32),
        grid_spec=grid_spec,
        compiler_params=_cparams(("arbitrary",)),
        name="moe_combine_norm",
    )(pos, x_new, route, g2, ln_g, ln_b, ybuf)


def _pad_cols(w, width):
    return jnp.pad(w, ((0, 0), (0, width - w.shape[1])))


def kernel(x, c, ctx, c_ctx, w_ada, b_ada, w_in, swa_sink, ret_decay_exp, ret_gn, mla_q_norm, mla_kv_norm,
           mla_w_uq, mla_w_ukv, hgrn_lb_logits, hgrn_gn, w_out, ln1_g, ln1_b, router_group, router_expert,
           moe_w1, moe_w3, moe_w2, ln2_g, ln2_b):
    batch, t_len, dm = x.shape
    l_len = ctx.shape[1]
    depth = w_in.shape[0]
    p_len = t_len + l_len
    n = batch * p_len
    assert batch == 2 and l_len == ROW_TILE and t_len % 256 == 0 and t_len % GRID_W == 0
    tiles_per_batch = p_len // ROW_TILE
    alpha = float((2 * depth) ** 0.25)

    cos_swa, sin_swa, cos_rows, sin_rows = _rope_tables(t_len, l_len, batch)
    mats = jnp.asarray(_hgrn_level_mats(), BF16)

    cvec = jnp.zeros((SUBLANES, dm), F32).at[0:batch].set(c).at[batch].set(c_ctx)
    mod = _ada(cvec, w_ada, b_ada)

    lbp = jax.nn.softmax(hgrn_lb_logits.astype(F32), axis=0)
    lower_bounds = jnp.cumsum(lbp, axis=0) - lbp[0]

    stream = None

    out = None
    for l in range(depth):
        need_ctx = l < depth - 1
        m6 = mod[l, :3].reshape(3, 6, dm)
        sh1, sc1, g1, sh2, sc2, g2 = (m6[:, k].reshape(3, 1, dm) for k in range(6))

        w_pad = _w_in_prep(w_in, l)
        if stream is None:
            pa, pb, pc, pd, stream = _proj(None, x, ctx, sh1, sc1, w_pad, tiles_per_batch)
        else:
            pa, pb, pc, pd = _proj(stream, None, None, sh1, sc1, w_pad, tiles_per_batch)

        wq = mla_w_uq[l].reshape(MLA_Q_RANK, MLA_HEADS, MLA_NOPE + MLA_ROPE)
        wq = jnp.pad(wq, ((0, 0), (0, 0), (0, 2 * LANES - (MLA_NOPE + MLA_ROPE))))
        wq = wq.reshape(MLA_Q_RANK, MLA_HEADS * 2 * LANES).astype(BF16)
        wkv = mla_w_ukv[l].reshape(MLA_KV_RANK, MLA_HEADS, MLA_NOPE + MLA_V)
        wk = wkv[:, :, :MLA_NOPE].reshape(MLA_KV_RANK, -1).astype(BF16)
        wv_t = wkv[:, :, MLA_NOPE:].reshape(MLA_KV_RANK, -1).T.astype(BF16)
        qf, kf, vt = _mla_up(pc, mla_q_norm[l].reshape(1, -1), mla_kv_norm[l].reshape(1, -1),
                             wq, wk, wv_t, cos_rows, sin_rows)

        a = _swa(pa, swa_sink[l], cos_swa, sin_swa, batch, t_len, l_len, need_ctx)
        cc = _mla_attn(qf, kf, vt, batch, t_len, l_len, need_ctx)
        b_f, b_b = _retention(pb, ret_decay_exp[l], batch, t_len, l_len)
        d_f, d_b = _hgrn(pd, lower_bounds[l], mats, batch, t_len, l_len)

        w_r = _pad_cols(jnp.concatenate([router_group[l], router_expert[l]], axis=1), LANES).astype(BF16)
        x_new, h2, route, counts = _out_proj(
            stream, a, b_f, b_b, cc, d_f, d_b, pb, pd, g1, sh2, sc2,
            ln1_g[l].reshape(1, -1), ln1_b[l].reshape(1, -1),
            ret_gn[l].reshape(1, -1), hgrn_gn[l].reshape(1, -1),
            w_out[l].astype(BF16), w_r, tiles_per_batch, alpha, need_ctx)

        n_active = n if need_ctx else batch * t_len
        n_blocks = (n_active * TOP_K + N_EXPERTS * (MOE_ROWS - 1) + MOE_ROWS - 1) // MOE_ROWS
        pos, blk_e, blk_valid, pad_start, blk_slot, blk_next = _route_plan(route, counts, n_blocks)
        xbuf = _dispatch(h2, pos, pad_start, blk_valid, tiles_per_batch, not need_ctx, batch)
        ybuf = _moe_experts(xbuf, blk_e, blk_valid, blk_slot, blk_next, moe_w1, moe_w3, moe_w2, l)
        res = _combine(x_new, route, ybuf, pos, g2, ln2_g[l].reshape(1, -1), ln2_b[l].reshape(1, -1),
                       tiles_per_batch, alpha, not need_ctx, batch)
        if need_ctx:
            stream = res
        else:
            out = res.reshape(batch, t_len, dm)
    return out
```

```python
import functools
import math

import numpy as np
import jax
import jax.numpy as jnp
from jax import lax
from jax.experimental import pallas as pl
from jax.experimental.pallas import tpu as pltpu

F32 = jnp.float32
BF16 = jnp.bfloat16

GRID_W = 64
HEAD_DIM = 128
SWA_HEADS = 4
SWA_KV_HEADS = 2
SWA_BLOCK = 128
RET_HEADS = 4
RET_DK = 128
MLA_HEADS = 4
MLA_Q_RANK = 384
MLA_KV_RANK = 128
MLA_NOPE = 128
MLA_ROPE = 64
MLA_V = 128
HGRN_HEADS = 4
N_GROUPS = 4
EXPERTS_PER_GROUP = 8
N_EXPERTS = N_GROUPS * EXPERTS_PER_GROUP
TOP_K = 2
ROPE_BASE = 10000.0
EPS = 1e-6

LANES = 128
SUBLANES = 8
VMEM_LIMIT_BYTES = 56 * 1024 * 1024
PROJ_VMEM_LIMIT_BYTES = 60 * 1024 * 1024

CHUNK = 128
ROW_TILE = 256
MOE_ROWS = 256
MLA_KEY_CHUNK = 256
MLA_HEADS_PER_STEP = 2
WIDE = 512

SEG_A = 1024
SEG_B = 2048
SEG_C = 640
SEG_D = 2560


def _cparams(sem, vmem=VMEM_LIMIT_BYTES):
    return pltpu.CompilerParams(dimension_semantics=sem, vmem_limit_bytes=vmem)


def _ln_rows(x):
    mu = jnp.mean(x, axis=-1, keepdims=True)
    xc = x - mu
    var = jnp.mean(xc * xc, axis=-1, keepdims=True)
    return xc * lax.rsqrt(var + EPS)


def _silu(x):
    return x * (1.0 / (1.0 + jnp.exp(-x)))


def _dot(a, b):
    return jnp.dot(a, b, preferred_element_type=F32)


def _pack_bf16_pairs(x):
    k = x.shape[1] // 2
    hi = lax.bitcast_convert_type(x[:, :k].astype(BF16).astype(F32), jnp.uint32)
    lo = lax.bitcast_convert_type(x[:, k:].astype(BF16).astype(F32), jnp.uint32)
    return hi | (lo >> 16)


def _unpack_bf16_pairs(w):
    hi = lax.bitcast_convert_type(w & jnp.uint32(0xFFFF0000), F32)
    lo = lax.bitcast_convert_type(w << 16, F32)
    return jnp.concatenate([hi, lo], axis=1)


def _dot_nt(a, b):
    return lax.dot_general(a, b, (((1,), (1,)), ((), ())), preferred_element_type=F32)


def _dot_tn(a, b):
    return lax.dot_general(a, b, (((0,), (0,)), ((), ())), preferred_element_type=F32)


def _ada_kernel(c_ref, w_ref, b_ref, o_ref):
    s = _silu(c_ref[...]).astype(BF16)
    o_ref[0] = _dot(s, w_ref[0].astype(BF16)) + b_ref[0]


def _ada(cvec, w_ada, b_ada):
    depth, d, n6 = w_ada.shape
    tn = n6 // 8
    return pl.pallas_call(
        _ada_kernel,
        out_shape=jax.ShapeDtypeStruct((depth, SUBLANES, n6), F32),
        grid=(depth, n6 // tn),
        in_specs=[
            pl.BlockSpec((SUBLANES, d), lambda l, j: (0, 0)),
            pl.BlockSpec((1, d, tn), lambda l, j: (l, 0, j)),
            pl.BlockSpec((1, 1, tn), lambda l, j: (l, 0, j)),
        ],
        out_specs=pl.BlockSpec((1, SUBLANES, tn), lambda l, j: (l, 0, j)),
        compiler_params=_cparams(("arbitrary", "arbitrary")),
        name="ada_mod",
    )(cvec, w_ada, b_ada.reshape(depth, 1, n6))


SEG_WIDTHS = (SEG_A, SEG_B, SEG_C, SEG_D)
SEG_C_TRUE = MLA_Q_RANK + MLA_KV_RANK + MLA_ROPE
IN_COLS = SEG_A + SEG_B + SEG_C_TRUE + SEG_D
IN_COLS_PAD = sum(SEG_WIDTHS)


PREP_ROWS = 256
_C_END = SEG_A + SEG_B + SEG_C_TRUE
_C_PAD = SEG_C - SEG_C_TRUE
assert _C_END % PREP_ROWS + _C_PAD <= PREP_ROWS and IN_COLS_PAD % PREP_ROWS in (0, PREP_ROWS // 2)


def _w_in_prep_kernel(w_ref, o_ref, prev_ref):
    j = pl.program_id(0)
    jc = _C_END // PREP_ROWS
    rem = _C_END % PREP_ROWS
    keep = PREP_ROWS - _C_PAD
    cur = w_ref[0].astype(BF16)

    @pl.when(j < jc)
    def _():
        o_ref[...] = cur

    @pl.when(j == jc)
    def _():
        o_ref[0:rem, :] = cur[0:rem]
        o_ref[rem:rem + _C_PAD, :] = jnp.zeros((_C_PAD, cur.shape[1]), BF16)
        o_ref[rem + _C_PAD:PREP_ROWS, :] = cur[rem:keep]

    @pl.when(j > jc)
    def _():
        o_ref[0:_C_PAD, :] = prev_ref[keep:PREP_ROWS, :]
        o_ref[_C_PAD:PREP_ROWS, :] = cur[0:keep]

    prev_ref[...] = cur


def _w_in_prep(w_in, layer):
    _, d, cols = w_in.shape
    assert cols == IN_COLS
    w_t = jnp.swapaxes(w_in, 1, 2)
    return pl.pallas_call(
        _w_in_prep_kernel,
        out_shape=jax.ShapeDtypeStruct((IN_COLS_PAD, d), BF16),
        grid=(pl.cdiv(IN_COLS_PAD, PREP_ROWS),),
        in_specs=[pl.BlockSpec((1, PREP_ROWS, d), lambda j: (layer, j, 0))],
        out_specs=pl.BlockSpec((PREP_ROWS, d), lambda j: (j, 0)),
        scratch_shapes=[pltpu.VMEM((PREP_ROWS, d), BF16)],
        compiler_params=_cparams(("arbitrary",)),
        name="w_in_prep",
    )(w_t)


def _column_blocks(outs):
    return [(o_ref, j, min(WIDE, o_ref.shape[1] - j)) for o_ref in outs for j in range(0, o_ref.shape[1], WIDE)]


def _project_rows(x, sh_ref, sc_ref, w_ref, outs, between=None):
    y = _ln_rows(x)
    h = (y * (1.0 + sc_ref[0]) + sh_ref[0]).astype(BF16)
    base = 0
    for idx, (o_ref, j, cw) in enumerate(_column_blocks(outs)):
        o_ref[:, j:j + cw] = _dot_nt(h, w_ref[base:base + cw, :])
        base += cw
        if between is not None:
            between(idx)


def _proj_kernel(x_ref, ctx_ref, sh_ref, sc_ref, w_ref, oa, ob, oc, od, so, *, tiles_per_batch):
    is_ctx = pl.program_id(0) % tiles_per_batch == 0
    x = jnp.where(is_ctx, ctx_ref[0], x_ref[0])
    so[...] = x
    _project_rows(x, sh_ref, sc_ref, w_ref, (oa, ob, oc, od))


def _group_of_tile(i, tiles_per_batch):
    return jnp.where(i % tiles_per_batch == 0, 2, i // tiles_per_batch)


def _proj(x, ctx, shift, scale, w_pad, tiles_per_batch):
    tm = ROW_TILE
    batch, t_len, d = x.shape
    n = batch * (t_len + ctx.shape[1])
    tpb = tiles_per_batch
    gmap = lambda i: (_group_of_tile(i, tiles_per_batch), 0, 0)
    widths = list(SEG_WIDTHS) + [d]
    return pl.pallas_call(
        functools.partial(_proj_kernel, tiles_per_batch=tiles_per_batch),
        out_shape=[jax.ShapeDtypeStruct((n, w), F32) for w in widths],
        grid=(n // tm,),
        in_specs=[
            pl.BlockSpec((1, tm, d), lambda i: (i // tpb, jnp.maximum(i % tpb - 1, 0), 0)),
            pl.BlockSpec((1, tm, d), lambda i: (i // tpb, 0, 0)),
            pl.BlockSpec((1, 1, d), gmap),
            pl.BlockSpec((1, 1, d), gmap),
            pl.BlockSpec(memory_space=pltpu.VMEM),
        ],
        out_specs=[pl.BlockSpec((tm, w), lambda i: (i, 0)) for w in widths],
        compiler_params=_cparams(("arbitrary",), vmem=PROJ_VMEM_LIMIT_BYTES),
        name="ln_mod_inproj",
    )(x, ctx, shift, scale, w_pad)


def _rope128(x, cos, sin):
    return x * cos + pltpu.roll(x, 64, 1) * sin


def _rope64(x, cos, sin):
    lane = lax.broadcasted_iota(jnp.int32, x.shape, 1)
    rot = jnp.where((lane % 64) < 32, pltpu.roll(x, 96, 1), pltpu.roll(x, 32, 1))
    return x * cos + rot * sin


def _rope_tables(t_len, l_len, batch):
    rows = t_len // GRID_W
    row = np.repeat(np.arange(rows), GRID_W).astype(np.float32)
    col = np.tile(np.arange(GRID_W), rows).astype(np.float32)

    def angles(rot_dim):
        n_freq = rot_dim // 4
        inv = (ROPE_BASE ** (-np.arange(n_freq, dtype=np.float32) / n_freq)).astype(np.float32)
        return np.concatenate([row[:, None] * inv, col[:, None] * inv], -1).astype(np.float32)

    a_swa = angles(HEAD_DIM)
    cos_swa = np.concatenate([np.cos(a_swa), np.cos(a_swa)], -1)
    sin_swa = np.concatenate([-np.sin(a_swa), np.sin(a_swa)], -1)
    a_mla = angles(MLA_ROPE)
    cos_m = np.concatenate([np.cos(a_mla), np.cos(a_mla), np.ones((t_len, 64), np.float32)], -1)
    sin_m = np.concatenate([-np.sin(a_mla), np.sin(a_mla), np.zeros((t_len, 64), np.float32)], -1)
    ones = np.ones((l_len, LANES), np.float32)
    zeros = np.zeros((l_len, LANES), np.float32)
    cos_rows = np.concatenate([np.concatenate([ones, cos_m], 0)] * batch, 0)
    sin_rows = np.concatenate([np.concatenate([zeros, sin_m], 0)] * batch, 0)
    return (jnp.asarray(cos_swa, F32), jnp.asarray(sin_swa, F32),
            jnp.asarray(cos_rows, F32), jnp.asarray(sin_rows, F32))


def _mla_up_kernel(pc_ref, qn_ref, kvn_ref, wq_ref, wk_ref, wv_ref, cos_ref, sin_ref,
                   q_out, k_out, v_out, *, scale):
    pc = pc_ref[...]
    cq = pc[:, :MLA_Q_RANK]
    ckv = pc[:, MLA_Q_RANK:MLA_Q_RANK + MLA_KV_RANK]
    kr = pc[:, MLA_Q_RANK + MLA_KV_RANK:]
    cos = cos_ref[...]
    sin = sin_ref[...]

    def rms(x, g):
        return x * lax.rsqrt(jnp.mean(x * x, axis=-1, keepdims=True) + EPS) * g

    qh = _dot(rms(cq, qn_ref[...]).astype(BF16), wq_ref[...])
    ckn = rms(ckv, kvn_ref[...]).astype(BF16)
    kh = _dot(ckn, wk_ref[...])
    v_out[...] = _dot_nt(wv_ref[...], ckn).astype(BF16)
    kr_rot = _rope64(kr, cos, sin).astype(BF16)
    for h in range(MLA_HEADS):
        base = h * 2 * LANES
        q_out[:, base:base + LANES] = (qh[:, base:base + LANES] * scale).astype(BF16)
        q_out[:, base + LANES:base + 2 * LANES] = (
            _rope64(qh[:, base + LANES:base + 2 * LANES], cos, sin) * scale).astype(BF16)
        k_out[:, base:base + LANES] = kh[:, h * LANES:(h + 1) * LANES].astype(BF16)
        k_out[:, base + LANES:base + 2 * LANES] = kr_rot


def _mla_up(pc, q_norm, kv_norm, wq, wk, wv, cos_rows, sin_rows):
    n = pc.shape[0]
    tm = ROW_TILE
    scale = float((MLA_NOPE + MLA_ROPE) ** -0.5 * math.log2(math.e))
    const2 = lambda i: (0, 0)
    return pl.pallas_call(
        functools.partial(_mla_up_kernel, scale=scale),
        out_shape=[jax.ShapeDtypeStruct((n, MLA_HEADS * 2 * LANES), BF16),
                   jax.ShapeDtypeStruct((n, MLA_HEADS * 2 * LANES), BF16),
                   jax.ShapeDtypeStruct((MLA_HEADS * MLA_V, n), BF16)],
        grid=(n // tm,),
        in_specs=[
            pl.BlockSpec((tm, SEG_C), lambda i: (i, 0)),
            pl.BlockSpec((1, MLA_Q_RANK), const2),
            pl.BlockSpec((1, MLA_KV_RANK), const2),
            pl.BlockSpec(wq.shape, const2),
            pl.BlockSpec(wk.shape, const2),
            pl.BlockSpec(wv.shape, const2),
            pl.BlockSpec((tm, LANES), lambda i: (i, 0)),
            pl.BlockSpec((tm, LANES), lambda i: (i, 0)),
        ],
        out_specs=[pl.BlockSpec((tm, MLA_HEADS * 2 * LANES), lambda i: (i, 0)),
                   pl.BlockSpec((tm, MLA_HEADS * 2 * LANES), lambda i: (i, 0)),
                   pl.BlockSpec((MLA_HEADS * MLA_V, tm), lambda i: (0, i))],
        compiler_params=_cparams(("arbitrary",)),
        name="mla_up",
    )(pc, q_norm, kv_norm, wq, wk, wv, cos_rows, sin_rows)


def _mla_attn_kernel(q_ref, k_ref, vt_ref, o_ref, s_ref, p_ref, *, kc, l_len, need_ctx):
    p_len = k_ref.shape[0]
    tq = q_ref.shape[0]
    dqk = 2 * LANES
    heads = q_ref.shape[1] // dqk
    half = p_len // 2
    is_ctx = pl.program_id(2) == 0

    @pl.when(is_ctx)
    def _():
        if not need_ctx:
            o_ref[...] = jnp.zeros_like(o_ref)
            return
        for h in range(heads):
            hq = slice(h * dqk, (h + 1) * dqk)
            hv = slice(h * MLA_V, (h + 1) * MLA_V)
            s = _dot_nt(k_ref[0:l_len, hq], q_ref[:, hq])
            p = jnp.exp2(s - jnp.max(s, axis=0, keepdims=True))
            acc = _dot(vt_ref[hv, 0:l_len], p.astype(BF16))
            o_ref[:, hv] = (acc / jnp.sum(p, axis=0, keepdims=True)).T

    @pl.when(jnp.logical_not(is_ctx))
    def _():
        fold = lambda a: a.reshape(kc // SUBLANES, SUBLANES, tq)
        m8s = []
        for h in range(heads):
            q = q_ref[:, h * dqk:(h + 1) * dqk]
            m8 = None
            for off in range(0, p_len, kc):
                s = _dot_nt(k_ref[off:off + kc, h * dqk:(h + 1) * dqk], q)
                s_ref[h, off:off + kc, :] = s
                cm = jnp.max(fold(s), axis=0)
                m8 = cm if m8 is None else jnp.maximum(m8, cm)
            m8s.append(m8)
        for h in range(heads):
            m = jnp.max(m8s[h], axis=0, keepdims=True)
            l8 = jnp.zeros((SUBLANES, tq), F32)
            for off in range(0, p_len, kc):
                p = jnp.exp2(s_ref[h, off:off + kc, :] - m)
                l8 = l8 + jnp.sum(fold(p), axis=0)
                p_ref[h, off:off + kc, :] = p.astype(BF16)
            l = jnp.sum(l8, axis=0, keepdims=True)
            hv = slice(h * MLA_V, (h + 1) * MLA_V)
            acc = (_dot(vt_ref[hv, 0:half], p_ref[h, 0:half, :])
                   + _dot(vt_ref[hv, half:p_len], p_ref[h, half:p_len, :]))
            o_ref[:, hv] = (acc / l).T


def _mla_attn(qf, kf, vt, batch, t_len, l_len, need_ctx):
    n = qf.shape[0]
    p_len = t_len + l_len
    tq = ROW_TILE
    assert l_len == tq
    kc = MLA_KEY_CHUNK
    hp = MLA_HEADS_PER_STEP
    tpb = p_len // tq
    return pl.pallas_call(
        functools.partial(_mla_attn_kernel, kc=kc, l_len=l_len, need_ctx=need_ctx),
        out_shape=jax.ShapeDtypeStruct((n, MLA_HEADS * MLA_V), F32),
        grid=(batch, MLA_HEADS // hp, tpb),
        in_specs=[
            pl.BlockSpec((tq, hp * 2 * LANES), lambda b, h, i: (b * tpb + i, h)),
            pl.BlockSpec((p_len, hp * 2 * LANES), lambda b, h, i: (b, h)),
            pl.BlockSpec((hp * MLA_V, p_len), lambda b, h, i: (h, b)),
        ],
        out_specs=pl.BlockSpec((tq, hp * MLA_V), lambda b, h, i: (b * tpb + i, h)),
        scratch_shapes=[pltpu.VMEM((hp, p_len, tq), F32), pltpu.VMEM((hp, p_len, tq), BF16)],
        compiler_params=_cparams(("arbitrary", "arbitrary", "arbitrary")),
        name="mla_attn",
    )(qf, kf, vt)


def _swa_kernel(sink_ref, q_ref, kp_ref, kc_ref, kn_ref, vp_ref, vc_ref, vn_ref, kx_ref, vx_ref,
                cq_ref, sq_ref, cp_ref, sp_ref, cn_ref, sn_ref, o_ref, *, nb, n_ctx, scale, need_ctx):
    j = pl.program_id(0)
    n = j - n_ctx
    blk = SWA_BLOCK
    batch = q_ref.shape[0]
    rr = lax.broadcasted_iota(jnp.int32, (2 * blk, 1), 0)
    mx = lambda a: jnp.max(a, axis=-1, keepdims=True)
    sm = lambda a: jnp.sum(a, axis=-1, keepdims=True)

    def heads_of(b, h):
        q0 = q_ref[b, :, 2 * h * LANES:(2 * h + 1) * LANES]
        q1 = q_ref[b, :, (2 * h + 1) * LANES:(2 * h + 2) * LANES]
        sk = jnp.where(rr < blk, sink_ref[2 * h], sink_ref[2 * h + 1]).astype(F32)
        return q0, q1, sk

    def store(b, h, o):
        o_ref[b, :, 2 * h * LANES:(2 * h + 1) * LANES] = o[:blk]
        o_ref[b, :, (2 * h + 1) * LANES:(2 * h + 2) * LANES] = o[blk:]

    @pl.when(j < n_ctx)
    def _():
        if not need_ctx:
            o_ref[...] = jnp.zeros_like(o_ref)
            return
        for b in range(batch):
            for h in range(SWA_KV_HEADS):
                ks = slice(h * LANES, (h + 1) * LANES)
                q0, q1, sk = heads_of(b, h)
                qq = jnp.concatenate([q0, q1], axis=0).astype(BF16)
                s_x = _dot_nt(qq, kx_ref[b, :, ks].astype(BF16)) * scale
                m = jnp.maximum(mx(s_x), sk)
                p_x = jnp.exp(s_x - m)
                den = sm(p_x) + jnp.exp(sk - m)
                store(b, h, _dot(p_x.astype(BF16), vx_ref[b, :, ks].astype(BF16)) / den)

    @pl.when(j >= n_ctx)
    def _():
        cq, sq = cq_ref[...], sq_ref[...]
        cp, sp = cp_ref[...], sp_ref[...]
        cn, sn = cn_ref[...], sn_ref[...]
        ri = lax.broadcasted_iota(jnp.int32, (2 * blk, blk), 0) % blk
        ci = lax.broadcasted_iota(jnp.int32, (2 * blk, blk), 1)
        mask_p = (ci >= ri) & (n > 0)
        mask_n = (ci <= ri) & (n < nb - 1)
        neg = jnp.float32(-jnp.inf)
        for b in range(batch):
            for h in range(SWA_KV_HEADS):
                ks = slice(h * LANES, (h + 1) * LANES)
                q0, q1, sk = heads_of(b, h)
                qq = jnp.concatenate([_rope128(q0, cq, sq), _rope128(q1, cq, sq)], axis=0).astype(BF16)
                kc = _rope128(kc_ref[b, :, ks], cq, sq).astype(BF16)
                kp = _rope128(kp_ref[b, :, ks], cp, sp).astype(BF16)
                kn = _rope128(kn_ref[b, :, ks], cn, sn).astype(BF16)
                s_c = _dot_nt(qq, kc) * scale
                s_p = jnp.where(mask_p, _dot_nt(qq, kp) * scale, neg)
                s_n = jnp.where(mask_n, _dot_nt(qq, kn) * scale, neg)
                s_x = _dot_nt(qq, kx_ref[b, :, ks].astype(BF16)) * scale
                m = jnp.maximum(jnp.maximum(mx(s_c), mx(s_p)), jnp.maximum(mx(s_n), mx(s_x)))
                m = jnp.maximum(m, sk)
                p_c, p_p, p_n, p_x = (jnp.exp(a - m) for a in (s_c, s_p, s_n, s_x))
                den = sm(p_c) + sm(p_p) + sm(p_n) + sm(p_x) + jnp.exp(sk - m)
                store(b, h, (_dot(p_c.astype(BF16), vc_ref[b, :, ks].astype(BF16))
                             + _dot(p_p.astype(BF16), vp_ref[b, :, ks].astype(BF16))
                             + _dot(p_n.astype(BF16), vn_ref[b, :, ks].astype(BF16))
                             + _dot(p_x.astype(BF16), vx_ref[b, :, ks].astype(BF16))) / den)


def _swa(pa, sink, cos_t, sin_t, batch, t_len, l_len, need_ctx):
    n = pa.shape[0]
    p_len = t_len + l_len
    blk = SWA_BLOCK
    nb = t_len // blk
    n_ctx = l_len // blk
    scale = float(HEAD_DIM ** -0.5)
    kv_w = SWA_KV_HEADS * HEAD_DIM
    q_w = SWA_HEADS * HEAD_DIM
    kcol, vcol = q_w // kv_w, q_w // kv_w + 1
    pa3 = pa.reshape(batch, p_len, pa.shape[1])
    lat = lambda j: jnp.maximum(j - n_ctx, 0)
    same = lambda j: lat(j)
    prv = lambda j: jnp.maximum(lat(j) - 1, 0)
    nxt = lambda j: jnp.minimum(lat(j) + 1, nb - 1)
    kv = lambda col, rowf: pl.BlockSpec((batch, blk, kv_w), lambda j: (0, n_ctx + rowf(j), col))
    tab = lambda rowf: pl.BlockSpec((blk, LANES), lambda j: (rowf(j), 0))
    in_specs = [
        pl.BlockSpec(memory_space=pltpu.SMEM),
        pl.BlockSpec((batch, blk, q_w), lambda j: (0, j, 0)),
        kv(kcol, prv), kv(kcol, same), kv(kcol, nxt),
        kv(vcol, prv), kv(vcol, same), kv(vcol, nxt),
        pl.BlockSpec((batch, l_len, kv_w), lambda j: (0, 0, kcol)),
        pl.BlockSpec((batch, l_len, kv_w), lambda j: (0, 0, vcol)),
        tab(same), tab(same), tab(prv), tab(prv), tab(nxt), tab(nxt),
    ]
    out = pl.pallas_call(
        functools.partial(_swa_kernel, nb=nb, n_ctx=n_ctx, scale=scale, need_ctx=need_ctx),
        out_shape=jax.ShapeDtypeStruct((batch, p_len, q_w), F32),
        grid=(n_ctx + nb,),
        in_specs=in_specs,
        out_specs=pl.BlockSpec((batch, blk, q_w), lambda j: (0, j, 0)),
        compiler_params=_cparams(("arbitrary",)),
        name="swa",
    )(sink, pa3, pa3, pa3, pa3, pa3, pa3, pa3, pa3, pa3, cos_t, sin_t, cos_t, sin_t, cos_t, sin_t)
    return out.reshape(n, q_w)


def _chunk_index(d, c, n_l, n_t):
    bwd = jnp.where(c < n_l, n_l - 1 - c, n_l + (n_t - 1) - (c - n_l))
    return jnp.where(d == 0, c, bwd)


def _flip_iotas(d):
    row = lax.broadcasted_iota(jnp.int32, (CHUNK, CHUNK), 0)
    col = lax.broadcasted_iota(jnp.int32, (CHUNK, CHUNK), 1)
    rf = jnp.where(d == 0, row, CHUNK - 1 - row)
    cf = jnp.where(d == 0, col, CHUNK - 1 - col)
    return rf, cf


def _ret_kernel(s_ref, qf_ref, kf_ref, vf_ref, qb_ref, kb_ref, vb_ref, of_ref, ob_ref, st_ref):
    c = pl.program_id(0)
    batch = qf_ref.shape[0]

    @pl.when(c == 0)
    def _():
        st_ref[...] = jnp.zeros_like(st_ref)

    ks = float(RET_DK ** -0.5)
    dirs = ((qf_ref, kf_ref, vf_ref, of_ref), (qb_ref, kb_ref, vb_ref, ob_ref))
    for d, (q_ref, k_ref, v_ref, o_ref) in enumerate(dirs):
        rf, cf = _flip_iotas(d)
        rff = rf.astype(F32)
        dn = (rf - cf).astype(F32)
        for h in range(RET_HEADS):
            hs = slice(h * LANES, (h + 1) * LANES)
            sv = jnp.full((CHUNK, CHUNK), s_ref[d * RET_HEADS + h], F32)
            lg = jnp.log1p(-jnp.exp2(-sv))
            dec = jnp.where(dn >= 0, jnp.exp(dn * lg), 0.0)
            eq = jnp.exp((rff + 1.0) * lg)
            ek = jnp.exp((CHUNK - 1.0 - rff) * lg)
            a_chunk = jnp.exp(float(CHUNK) * lg)
            for b in range(batch):
                q = q_ref[b, :, hs]
                k = k_ref[b, :, hs] * ks
                v = v_ref[b, :, hs].astype(BF16)
                st = st_ref[d, b, h]
                a = _dot_nt(q.astype(BF16), k.astype(BF16)) * dec
                o = _dot(a.astype(BF16), v) + _dot_nt((q * eq).astype(BF16), st.astype(BF16))
                o_ref[b, :, hs] = o
                st_ref[d, b, h] = a_chunk * st + _dot_tn(v, (k * ek).astype(BF16))


def _hgrn_level_mats():
    c = CHUNK
    t = np.arange(c)[:, None]
    u = np.arange(c)[None, :]
    mats = [u <= t, u > t]
    m = c // 2
    while m >= 1:
        mid = (t // (2 * m)) * 2 * m + m
        second = (t % (2 * m)) >= m
        qrole = (u >= mid) & (u <= t)
        krole = (u > t) & (u <= mid - 1)
        mats.append(np.where(second, qrole, krole))
        m //= 2
    fwd = np.concatenate(mats, 0).astype(np.float32)
    bwd = np.concatenate([mm[::-1, ::-1] for mm in mats], 0).astype(np.float32)
    return np.stack([fwd, bwd], 0)


N_LEVELS = int(math.log2(CHUNK))


def _hgrn_kernel(m_ref, lb_ref, qf_ref, ff_ref, vf_ref, qb_ref, fb_ref, vb_ref, of_ref, ob_ref, st_ref):
    c = pl.program_id(0)
    batch = qf_ref.shape[0]

    @pl.when(c == 0)
    def _():
        st_ref[...] = jnp.zeros_like(st_ref)

    dirs = ((qf_ref, ff_ref, vf_ref, of_ref), (qb_ref, fb_ref, vb_ref, ob_ref))
    for d, (q_ref, f_ref, v_ref, o_ref) in enumerate(dirs):
        rf, cf = _flip_iotas(d)
        mst = m_ref[d]
        last = CHUNK - 1 if d == 0 else 0
        lvl_masks = []
        for lvl in range(N_LEVELS):
            m = CHUNK >> (lvl + 1)
            sh = N_LEVELS - lvl
            lvl_masks.append(((rf >> sh) == (cf >> sh)) & ((rf & m) != 0) & ((cf & m) == 0))
        diag = rf == cf
        for h in range(HGRN_HEADS):
            hs = slice(h * LANES, (h + 1) * LANES)
            lb = lb_ref[d, :, hs]
            for b in range(batch):
                f = lb + (1.0 - lb) * (1.0 / (1.0 + jnp.exp(-f_ref[b, :, hs])))
                k = 1.0 - f
                g = jnp.log(f)
                q = _silu(q_ref[b, :, hs])
                v = v_ref[b, :, hs].astype(BF16)
                e = jnp.exp(_dot(mst, g.astype(BF16)))
                e_q = e[0:CHUNK]
                q_in = q * e_q
                k_st = k * e[CHUNK:2 * CHUNK]
                scores = jnp.where(diag, _dot_nt(q.astype(BF16), k.astype(BF16)), 0.0)
                for lvl in range(N_LEVELS):
                    el = e[(2 + lvl) * CHUNK:(3 + lvl) * CHUNK]
                    sl = _dot_nt((q * el).astype(BF16), (k * el).astype(BF16))
                    scores = scores + jnp.where(lvl_masks[lvl], sl, 0.0)
                st = st_ref[d, b, h]
                o = _dot(scores.astype(BF16), v) + _dot_nt(q_in.astype(BF16), st.astype(BF16))
                o_ref[b, :, hs] = o
                st_ref[d, b, h] = st * e_q[last:last + 1, :] + _dot_tn(v, k_st.astype(BF16))


def _scan_specs(batch, t_len, l_len):
    n_l, n_t = l_len // CHUNK, t_len // CHUNK

    def spec(d, col):
        return pl.BlockSpec((batch, CHUNK, WIDE), lambda c: (0, _chunk_index(d, c, n_l, n_t), col))

    return spec, n_l + n_t


def _retention(pb, ret_s, batch, t_len, l_len):
    n = pb.shape[0]
    p_len = t_len + l_len
    pb3 = pb.reshape(batch, p_len, pb.shape[1])
    spec, n_chunks = _scan_specs(batch, t_len, l_len)
    out_f, out_b = pl.pallas_call(
        _ret_kernel,
        out_shape=[jax.ShapeDtypeStruct((batch, p_len, WIDE), F32)] * 2,
        grid=(n_chunks,),
        in_specs=[pl.BlockSpec(memory_space=pltpu.SMEM)]
                 + [spec(0, col) for col in (0, 1, 2)] + [spec(1, col) for col in (0, 1, 2)],
        out_specs=[spec(0, 0), spec(1, 0)],
        scratch_shapes=[pltpu.VMEM((2, batch, RET_HEADS, CHUNK, CHUNK), F32)],
        compiler_params=_cparams(("arbitrary",)),
        name="retention_scan",
    )(ret_s.reshape(-1), pb3, pb3, pb3, pb3, pb3, pb3)
    return out_f.reshape(n, WIDE), out_b.reshape(n, WIDE)


def _hgrn(pd, lower_bounds, mats, batch, t_len, l_len):
    n = pd.shape[0]
    p_len = t_len + l_len
    pd3 = pd.reshape(batch, p_len, pd.shape[1])
    lb3 = lower_bounds.reshape(2, 1, WIDE)
    spec, n_chunks = _scan_specs(batch, t_len, l_len)
    out_f, out_b = pl.pallas_call(
        _hgrn_kernel,
        out_shape=[jax.ShapeDtypeStruct((batch, p_len, WIDE), F32)] * 2,
        grid=(n_chunks,),
        in_specs=[pl.BlockSpec(mats.shape, lambda c: (0, 0, 0)),
                  pl.BlockSpec(lb3.shape, lambda c: (0, 0, 0)),
                  spec(0, 0), spec(0, 1), spec(0, 3), spec(1, 0), spec(1, 2), spec(1, 3)],
        out_specs=[spec(0, 0), spec(1, 0)],
        scratch_shapes=[pltpu.VMEM((2, batch, HGRN_HEADS, CHUNK, CHUNK), F32)],
        compiler_params=_cparams(("arbitrary",)),
        name="hgrn_scan",
    )(mats, lb3, pd3, pd3, pd3, pd3, pd3, pd3)
    return out_f.reshape(n, WIDE), out_b.reshape(n, WIDE)


R_E1, R_E2, R_G1, R_G2, R_S1, R_S2 = 0, 1, 2, 3, 4, 5


def _route(logits, count_ref, active):
    tm = logits.shape[0]
    lane = lax.broadcasted_iota(jnp.int32, logits.shape, 1)
    lanef = lane.astype(F32)
    big = jnp.float32(1e9)
    neg = jnp.float32(-jnp.inf)
    mx = lambda a: jnp.max(a, axis=-1, keepdims=True)
    mn = lambda a: jnp.min(a, axis=-1, keepdims=True)
    sm = lambda a: jnp.sum(a, axis=-1, keepdims=True)
    gl = jnp.where(lane < N_GROUPS, logits, neg)
    gm = mx(gl)
    p_grp = 1.0 / sm(jnp.exp(gl - gm))
    grp = mn(jnp.where(gl == gm, lanef, big))
    lo = N_GROUPS + grp * EXPERTS_PER_GROUP
    ing = (lanef >= lo) & (lanef < lo + EXPERTS_PER_GROUP)
    el = jnp.where(ing, logits, neg)
    ee = jnp.exp(el - mx(el))
    p = ee / sm(ee)
    pm = jnp.where(ing, p, -1.0)
    p1 = mx(pm)
    i1 = mn(jnp.where(pm == p1, lanef, big))
    pm2 = jnp.where(lanef == i1, -1.0, pm)
    p2 = mx(pm2)
    i2 = mn(jnp.where(pm2 == p2, lanef, big))
    den = p1 + p2
    g1 = p_grp * p1 / den
    g2 = p_grp * p2 / den
    e1 = i1 - N_GROUPS
    e2 = i2 - N_GROUPS
    oh1 = lanef == e1
    oh2 = lanef == e2
    both = jnp.where(oh1 | oh2, active, 0.0)
    ri = lax.broadcasted_iota(jnp.int32, (tm, tm), 0)
    ci = lax.broadcasted_iota(jnp.int32, (tm, tm), 1)
    earlier = jnp.where(ci < ri, 1.0, 0.0).astype(BF16)
    before = _dot(earlier, both.astype(BF16)) + count_ref[0:1, :]
    s1 = sm(jnp.where(oh1, before, 0.0))
    s2 = sm(jnp.where(oh2, before, 0.0))
    count_ref[...] = count_ref[...] + jnp.sum(both, axis=0, keepdims=True)
    out = jnp.zeros(logits.shape, F32)
    for ln, val in ((R_E1, e1), (R_E2, e2), (R_G1, g1), (R_G2, g2), (R_S1, s1), (R_S2, s2)):
        out = jnp.where(lane == ln, val, out)
    return out


def _out_kernel(x_ref, a_ref, bf_ref, bb_ref, c_ref, df_ref, db_ref, rg_ref, hg_ref, g1_ref, sh2_ref, sc2_ref,
                lng_ref, lnb_ref, rgn_ref, hgn_ref, wo_ref, wr_ref,
                xo_ref, h2_ref, r_ref, cnt_ref, *, alpha, tiles_per_batch, route_ctx):
    i = pl.program_id(0)

    @pl.when(i == 0)
    def _():
        cnt_ref[...] = jnp.zeros_like(cnt_ref)

    bsum = bf_ref[...] + bb_ref[...]
    dsum = df_ref[...] + db_ref[...]
    rgn = rgn_ref[...]
    hgn = hgn_ref[...]
    parts_b, parts_d = [], []
    for h in range(4):
        hs = slice(h * LANES, (h + 1) * LANES)
        parts_b.append(_ln_rows(bsum[:, hs]))
        dh = dsum[:, hs]
        parts_d.append(dh * lax.rsqrt(jnp.mean(dh * dh, axis=-1, keepdims=True) + EPS))
    bo = _silu(rg_ref[...]) * (jnp.concatenate(parts_b, axis=1) * rgn)
    do = _silu(hg_ref[...]) * (jnp.concatenate(parts_d, axis=1) * hgn)
    mix = jnp.concatenate([a_ref[...], bo, c_ref[...], do], axis=1).astype(BF16)
    o = _dot(mix, wo_ref[...])
    y = alpha * x_ref[...] + g1_ref[0] * o
    xn = _ln_rows(y) * lng_ref[...] + lnb_ref[...]
    xo_ref[...] = xn
    h2 = _ln_rows(xn) * (1.0 + sc2_ref[0]) + sh2_ref[0]
    h2_ref[...] = _pack_bf16_pairs(h2)
    if route_ctx:
        active = jnp.float32(1.0)
    else:
        active = jnp.where(i % tiles_per_batch == 0, 0.0, 1.0).astype(F32)
    r_ref[...] = _route(_dot(h2.astype(BF16), wr_ref[...]), cnt_ref, active)


def _out_proj(stream, a, b_f, b_b, c, d_f, d_b, pb, pd, g1, sh2, sc2, ln_g, ln_b, ret_gn, hgrn_gn, w_out, w_r,
              tiles_per_batch, alpha, route_ctx):
    n, dm = stream.shape
    tm = ROW_TILE
    gmap = lambda i: (_group_of_tile(i, tiles_per_batch), 0, 0)
    row = lambda w: pl.BlockSpec((tm, w), lambda i: (i, 0))
    const = lambda w: pl.BlockSpec((1, w), lambda i: (0, 0))
    resident = pl.BlockSpec(memory_space=pltpu.VMEM)
    return pl.pallas_call(
        functools.partial(_out_kernel, alpha=alpha, tiles_per_batch=tiles_per_batch, route_ctx=route_ctx),
        out_shape=[jax.ShapeDtypeStruct((n, dm), F32), jax.ShapeDtypeStruct((n, dm // 2), jnp.uint32),
                   jax.ShapeDtypeStruct((n, LANES), F32), jax.ShapeDtypeStruct((SUBLANES, LANES), F32)],
        grid=(n // tm,),
        in_specs=[row(dm), row(WIDE), row(WIDE), row(WIDE), row(WIDE), row(WIDE), row(WIDE),
                  pl.BlockSpec((tm, WIDE), lambda i: (i, 3)),
                  pl.BlockSpec((tm, WIDE), lambda i: (i, 4)),
                  pl.BlockSpec((1, 1, dm), gmap), pl.BlockSpec((1, 1, dm), gmap),
                  pl.BlockSpec((1, 1, dm), gmap),
                  const(dm), const(dm), const(WIDE), const(WIDE), resident, resident],
        out_specs=[row(dm), row(dm // 2), row(LANES),
                   pl.BlockSpec((SUBLANES, LANES), lambda i: (0, 0))],
        compiler_params=_cparams(("arbitrary",)),
        name="mix_outproj_norm_route",
    )(stream, a, b_f, b_b, c, d_f, d_b, pb, pd, g1, sh2, sc2, ln_g, ln_b, ret_gn, hgrn_gn, w_out, w_r)


def _route_plan(route, counts_f, n_blocks):
    rows = MOE_ROWS
    counts = counts_f[0, :N_EXPERTS].astype(jnp.int32)
    padded = (counts + rows - 1) // rows * rows
    pend = jnp.cumsum(padded)
    pstart = pend - padded
    eid = route[:, R_E1:R_E2 + 1].astype(jnp.int32)
    rank = route[:, R_S1:R_S2 + 1].astype(jnp.int32)
    onehot = eid[:, :, None] == jnp.arange(N_EXPERTS, dtype=jnp.int32)
    pos = jnp.sum(jnp.where(onehot, pstart, 0), axis=-1) + rank
    blk_start = jnp.arange(n_blocks, dtype=jnp.int32) * rows
    blk_e = jnp.minimum(jnp.sum(blk_start[:, None] >= pend[None, :], axis=-1), N_EXPERTS - 1)
    blk_valid = (blk_start < pend[-1]).astype(jnp.int32)
    pad_start = pstart + counts
    ids = jnp.arange(N_EXPERTS, dtype=jnp.int32)
    nonempty = counts > 0
    seq = jnp.cumsum(nonempty.astype(jnp.int32)) - nonempty.astype(jnp.int32)
    cand = jnp.where(nonempty, ids, N_EXPERTS)
    later = jnp.where(ids[None, :] > ids[:, None], cand[None, :], N_EXPERTS)
    nxt = jnp.min(later, axis=-1)
    nxt = jnp.where(nxt >= N_EXPERTS, -1, nxt)
    blk_oh = blk_e[:, None] == ids[None, :]
    blk_slot = jnp.sum(jnp.where(blk_oh, seq % 2, 0), axis=-1)
    blk_next = jnp.sum(jnp.where(blk_oh, nxt, 0), axis=-1)
    i32 = lambda a: a.astype(jnp.int32)
    return (i32(pos.reshape(-1)), i32(blk_e), blk_valid, i32(pad_start), i32(blk_slot), i32(blk_next))


def _dispatch_kernel(pos_ref, pad_ref, bv_ref, h_ref, x_hbm, zbuf, sem, *, tile_of_step, n_blocks):
    i = pl.program_id(0)
    tm = ROW_TILE
    rows = MOE_ROWS

    def zero_copy(e):
        start = pl.multiple_of(pad_ref[e] & ~(SUBLANES - 1), SUBLANES)
        return pltpu.make_async_copy(zbuf, x_hbm.at[pl.ds(start, rows + SUBLANES)], sem.at[1])

    def unused_block_copy(j):
        start = pl.multiple_of(j * rows, rows)
        return pltpu.make_async_copy(zbuf.at[pl.ds(0, rows)], x_hbm.at[pl.ds(start, rows)], sem.at[1])

    @pl.when(i == 0)
    def _():
        zbuf[...] = jnp.zeros_like(zbuf)

        def fill(j, carry):
            @pl.when(bv_ref[j] == 0)
            def _():
                unused_block_copy(j).start()
                unused_block_copy(j).wait()
            return carry

        lax.fori_loop(0, n_blocks + 2, fill, 0)
        for e in range(N_EXPERTS):
            zero_copy(e).start()
        for e in range(N_EXPERTS):
            zero_copy(e).wait()

    def row_copy(r, k):
        p = pos_ref[(tile_of_step(i) * tm + r) * TOP_K + k]
        return pltpu.make_async_copy(h_ref.at[r], x_hbm.at[p], sem.at[0])

    for r in range(tm):
        for k in range(TOP_K):
            row_copy(r, k).start(priority=k)
    for r in range(tm):
        for k in range(TOP_K):
            row_copy(r, k).wait()


def _tile_schedule(n, tiles_per_batch, latent_only, batch):
    if latent_only:
        lat = tiles_per_batch - 1
        return batch * lat, (lambda s: (s // lat) * tiles_per_batch + 1 + s % lat)
    return n // ROW_TILE, (lambda s: s)


def _dispatch(h2, pos, pad_start, blk_valid, tiles_per_batch, latent_only, batch):
    n, dm = h2.shape
    tm = ROW_TILE
    n_blocks = blk_valid.shape[0]
    blk_valid = jnp.concatenate([blk_valid, jnp.zeros((2,), jnp.int32)])
    n_tiles, tile_of_step = _tile_schedule(n, tiles_per_batch, latent_only, batch)
    grid_spec = pltpu.PrefetchScalarGridSpec(
        num_scalar_prefetch=3,
        grid=(n_tiles,),
        in_specs=[pl.BlockSpec((tm, dm), lambda i, p, z, v: (tile_of_step(i), 0))],
        out_specs=pl.BlockSpec(memory_space=pl.ANY),
        scratch_shapes=[pltpu.VMEM((MOE_ROWS + SUBLANES, dm), h2.dtype), pltpu.SemaphoreType.DMA((2,))],
    )
    return pl.pallas_call(
        functools.partial(_dispatch_kernel, tile_of_step=tile_of_step, n_blocks=n_blocks),
        out_shape=jax.ShapeDtypeStruct(((n_blocks + 2) * MOE_ROWS, dm), h2.dtype),
        grid_spec=grid_spec,
        compiler_params=_cparams(("arbitrary",)),
        name="moe_dispatch",
    )(pos, pad_start, blk_valid, h2)


def _moe_kernel(be_ref, bv_ref, slot_ref, next_ref, x_ref, w1_hbm, w3_hbm, w2_hbm, y_ref,
                s1, s3, s2, w1b, w3b, w2b, sem, *, layer):
    j = pl.program_id(0)
    valid = bv_ref[j] == 1
    first = valid & ((j == 0) | (be_ref[j] != be_ref[jnp.maximum(j - 1, 0)]))

    def weight_copies(e, slot):
        return (pltpu.make_async_copy(w1_hbm.at[layer, e], s1.at[slot], sem.at[slot]),
                pltpu.make_async_copy(w3_hbm.at[layer, e], s3.at[slot], sem.at[slot]),
                pltpu.make_async_copy(w2_hbm.at[layer, e], s2.at[slot], sem.at[slot]))

    @pl.when(valid & (j == 0))
    def _():
        for cp in weight_copies(be_ref[0], slot_ref[0]):
            cp.start()

    @pl.when(first)
    def _():
        slot = slot_ref[j]
        for cp in weight_copies(be_ref[j], slot):
            cp.wait()
        nxt = next_ref[j]

        @pl.when(nxt >= 0)
        def _():
            for cp in weight_copies(nxt, 1 - slot):
                cp.start()

        w1b[...] = s1[slot].astype(BF16)
        w3b[...] = s3[slot].astype(BF16)
        w2b[...] = s2[slot].astype(BF16)

    @pl.when(valid)
    def _():
        x = _unpack_bf16_pairs(x_ref[...]).astype(BF16)
        hmid = _silu(_dot(x, w1b[...])) * _dot(x, w3b[...])
        y_ref[...] = _pack_bf16_pairs(_dot(hmid.astype(BF16), w2b[...]))

    @pl.when(bv_ref[j] == 0)
    def _():
        y_ref[...] = jnp.zeros_like(y_ref)


def _moe_experts(xbuf, blk_e, blk_valid, blk_slot, blk_next, w1, w3, w2, layer):
    dm = w1.shape[2]
    ff = w1.shape[-1]
    rows = MOE_ROWS
    n_blocks = blk_e.shape[0]
    hbm = pl.BlockSpec(memory_space=pl.ANY)
    grid_spec = pltpu.PrefetchScalarGridSpec(
        num_scalar_prefetch=4,
        grid=(n_blocks,),
        in_specs=[
            pl.BlockSpec((rows, dm // 2), lambda j, be, bv, sl, nx: (j * bv[j], 0)),
            hbm, hbm, hbm,
        ],
        out_specs=pl.BlockSpec((rows, dm // 2), lambda j, be, bv, sl, nx: (j, 0)),
        scratch_shapes=[
            pltpu.VMEM((2, dm, ff), F32), pltpu.VMEM((2, dm, ff), F32), pltpu.VMEM((2, ff, dm), F32),
            pltpu.VMEM((dm, ff), BF16), pltpu.VMEM((dm, ff), BF16), pltpu.VMEM((ff, dm), BF16),
            pltpu.SemaphoreType.DMA((2,)),
        ],
    )
    return pl.pallas_call(
        functools.partial(_moe_kernel, layer=layer),
        out_shape=jax.ShapeDtypeStruct((n_blocks * rows, dm // 2), jnp.uint32),
        grid_spec=grid_spec,
        compiler_params=_cparams(("arbitrary",)),
        name="moe_experts",
    )(blk_e, blk_valid, blk_slot, blk_next, xbuf, w1, w3, w2)


def _expert_row_gather(pos_ref, y_hbm, ybuf, sem, tile_of_step):
    tm = ROW_TILE

    def row_copy(step, sl, r, k):
        p = pos_ref[(tile_of_step(step) * tm + r) * TOP_K + k]
        return pltpu.make_async_copy(y_hbm.at[p], ybuf.at[sl, k, r], sem.at[sl])

    def start_rows(step, sl, r0, r1):
        for r in range(r0, r1):
            for k in range(TOP_K):
                row_copy(step, sl, r, k).start(priority=k)

    def wait_rows(step, sl):
        for r in range(tm):
            for k in range(TOP_K):
                row_copy(step, sl, r, k).wait()

    return start_rows, wait_rows


def _gate_weighted(ybuf, slot, r_ref):
    route = r_ref[...]
    return (_unpack_bf16_pairs(ybuf[slot, 0]) * route[:, R_G1:R_G1 + 1]
            + _unpack_bf16_pairs(ybuf[slot, 1]) * route[:, R_G2:R_G2 + 1])


def _gathered_expert_rows(pos_ref, r_ref, y_hbm, ybuf, sem, *, n_tiles, tile_of_step):
    i = pl.program_id(0)
    slot = i % 2
    start_rows, wait_rows = _expert_row_gather(pos_ref, y_hbm, ybuf, sem, tile_of_step)

    @pl.when(i == 0)
    def _():
        start_rows(0, 0, 0, ROW_TILE)

    @pl.when(i + 1 < n_tiles)
    def _():
        start_rows(jnp.minimum(i + 1, n_tiles - 1), 1 - slot, 0, ROW_TILE)

    wait_rows(i, slot)
    return _gate_weighted(ybuf, slot, r_ref)


def _combine_kernel(pos_ref, x_ref, r_ref, g2_ref, lng_ref, lnb_ref, y_hbm, o_ref, ybuf, sem,
                    *, n_tiles, tile_of_step, alpha):
    y = _gathered_expert_rows(pos_ref, r_ref, y_hbm, ybuf, sem, n_tiles=n_tiles, tile_of_step=tile_of_step)
    z = alpha * x_ref[...] + g2_ref[0] * y
    o_ref[...] = _ln_rows(z) * lng_ref[...] + lnb_ref[...]


def _combine(x_new, route, ybuf, pos, g2, ln_g, ln_b, tiles_per_batch, alpha, latent_only, batch):
    n, dm = x_new.shape
    tm = ROW_TILE
    n_tiles, tile_of_step = _tile_schedule(n, tiles_per_batch, latent_only, batch)
    gmap = lambda i, p: (_group_of_tile(tile_of_step(i), tiles_per_batch), 0, 0)
    grid_spec = pltpu.PrefetchScalarGridSpec(
        num_scalar_prefetch=1,
        grid=(n_tiles,),
        in_specs=[
            pl.BlockSpec((tm, dm), lambda i, p: (tile_of_step(i), 0)),
            pl.BlockSpec((tm, LANES), lambda i, p: (tile_of_step(i), 0)),
            pl.BlockSpec((1, 1, dm), gmap),
            pl.BlockSpec((1, dm), lambda i, p: (0, 0)),
            pl.BlockSpec((1, dm), lambda i, p: (0, 0)),
            pl.BlockSpec(memory_space=pl.ANY),
        ],
        out_specs=pl.BlockSpec((tm, dm), lambda i, p: (i, 0)),
        scratch_shapes=[pltpu.VMEM((2, TOP_K, tm, dm // 2), jnp.uint32), pltpu.SemaphoreType.DMA((2,))],
    )
    return pl.pallas_call(
        functools.partial(_combine_kernel, n_tiles=n_tiles, tile_of_step=tile_of_step, alpha=alpha),
        out_shape=jax.ShapeDtypeStruct((n_tiles * tm, dm), F32),
        grid_spec=grid_spec,
        compiler_params=_cparams(("arbitrary",)),
        name="moe_combine_norm",
    )(pos, x_new, route, g2, ln_g, ln_b, ybuf)


def _combine_proj_kernel(pos_ref, x_ref, r_ref, g2_ref, lng_ref, lnb_ref, y_hbm, sh_ref, sc_ref, w_ref,
                         oa, ob, oc, od, so, ybuf, sem, *, n_tiles, alpha):
    i = pl.program_id(0)
    slot = i % 2
    tm = ROW_TILE
    start_rows, wait_rows = _expert_row_gather(pos_ref, y_hbm, ybuf, sem, lambda s: s)

    @pl.when(i == 0)
    def _():
        start_rows(0, 0, 0, tm)

    wait_rows(i, slot)
    z = alpha * x_ref[...] + g2_ref[0] * _gate_weighted(ybuf, slot, r_ref)
    stream = _ln_rows(z) * lng_ref[...] + lnb_ref[...]
    so[...] = stream
    outs = (oa, ob, oc, od)
    n_blk = len(_column_blocks(outs))
    per = -(-tm // n_blk)
    nxt = jnp.minimum(i + 1, n_tiles - 1)

    def issue_slice(idx):
        start_rows(nxt, 1 - slot, min(idx * per, tm), min((idx + 1) * per, tm))

    _project_rows(stream, sh_ref, sc_ref, w_ref, outs, between=issue_slice)

    @pl.when(i == n_tiles - 1)
    def _():
        wait_rows(nxt, 1 - slot)


def _combine_proj(x_new, route, ybuf, pos, g2, ln_g, ln_b, shift, scale, w_pad, tiles_per_batch, alpha):
    n, dm = x_new.shape
    tm = ROW_TILE
    n_tiles = n // tm
    gmap = lambda i, p: (_group_of_tile(i, tiles_per_batch), 0, 0)
    row = lambda w: pl.BlockSpec((tm, w), lambda i, p: (i, 0))
    const = pl.BlockSpec((1, dm), lambda i, p: (0, 0))
    mod = pl.BlockSpec((1, 1, dm), gmap)
    widths = list(SEG_WIDTHS) + [dm]
    grid_spec = pltpu.PrefetchScalarGridSpec(
        num_scalar_prefetch=1,
        grid=(n_tiles,),
        in_specs=[row(dm), row(LANES), mod, const, const, pl.BlockSpec(memory_space=pl.ANY),
                  mod, mod, pl.BlockSpec(memory_space=pltpu.VMEM)],
        out_specs=[row(w) for w in widths],
        scratch_shapes=[pltpu.VMEM((2, TOP_K, tm, dm // 2), jnp.uint32), pltpu.SemaphoreType.DMA((2,))],
    )
    return pl.pallas_call(
        functools.partial(_combine_proj_kernel, n_tiles=n_tiles, alpha=alpha),
        out_shape=[jax.ShapeDtypeStruct((n, w), F32) for w in widths],
        grid_spec=grid_spec,
        compiler_params=_cparams(("arbitrary",), vmem=PROJ_VMEM_LIMIT_BYTES),
        name="moe_combine_ln_mod_inproj",
    )(pos, x_new, route, g2, ln_g, ln_b, ybuf, shift, scale, w_pad)


def _pad_cols(w, width):
    return jnp.pad(w, ((0, 0), (0, width - w.shape[1])))


def kernel(x, c, ctx, c_ctx, w_ada, b_ada, w_in, swa_sink, ret_decay_exp, ret_gn, mla_q_norm, mla_kv_norm,
           mla_w_uq, mla_w_ukv, hgrn_lb_logits, hgrn_gn, w_out, ln1_g, ln1_b, router_group, router_expert,
           moe_w1, moe_w3, moe_w2, ln2_g, ln2_b):
    batch, t_len, dm = x.shape
    l_len = ctx.shape[1]
    depth = w_in.shape[0]
    p_len = t_len + l_len
    n = batch * p_len
    assert batch == 2 and l_len == ROW_TILE and t_len % 256 == 0 and t_len % GRID_W == 0
    tiles_per_batch = p_len // ROW_TILE
    alpha = float((2 * depth) ** 0.25)

    cos_swa, sin_swa, cos_rows, sin_rows = _rope_tables(t_len, l_len, batch)
    mats = jnp.asarray(_hgrn_level_mats(), BF16)

    cvec = jnp.zeros((SUBLANES, dm), F32).at[0:batch].set(c).at[batch].set(c_ctx)
    mod = _ada(cvec, w_ada, b_ada)

    lbp = jax.nn.softmax(hgrn_lb_logits.astype(F32), axis=0)
    lower_bounds = jnp.cumsum(lbp, axis=0) - lbp[0]

    pending = None

    out = None
    for l in range(depth):
        need_ctx = l < depth - 1
        m6 = mod[l, :3].reshape(3, 6, dm)
        sh1, sc1, g1, sh2, sc2, g2 = (m6[:, k].reshape(3, 1, dm) for k in range(6))

        w_pad = _w_in_prep(w_in, l)
        if pending is None:
            pa, pb, pc, pd, stream = _proj(x, ctx, sh1, sc1, w_pad, tiles_per_batch)
        else:
            pa, pb, pc, pd, stream = _combine_proj(*pending, sh1, sc1, w_pad, tiles_per_batch, alpha)

        wq = mla_w_uq[l].reshape(MLA_Q_RANK, MLA_HEADS, MLA_NOPE + MLA_ROPE)
        wq = jnp.pad(wq, ((0, 0), (0, 0), (0, 2 * LANES - (MLA_NOPE + MLA_ROPE))))
        wq = wq.reshape(MLA_Q_RANK, MLA_HEADS * 2 * LANES).astype(BF16)
        wkv = mla_w_ukv[l].reshape(MLA_KV_RANK, MLA_HEADS, MLA_NOPE + MLA_V)
        wk = wkv[:, :, :MLA_NOPE].reshape(MLA_KV_RANK, -1).astype(BF16)
        wv_t = wkv[:, :, MLA_NOPE:].reshape(MLA_KV_RANK, -1).T.astype(BF16)
        qf, kf, vt = _mla_up(pc, mla_q_norm[l].reshape(1, -1), mla_kv_norm[l].reshape(1, -1),
                             wq, wk, wv_t, cos_rows, sin_rows)

        a = _swa(pa, swa_sink[l], cos_swa, sin_swa, batch, t_len, l_len, need_ctx)
        cc = _mla_attn(qf, kf, vt, batch, t_len, l_len, need_ctx)
        b_f, b_b = _retention(pb, ret_decay_exp[l], batch, t_len, l_len)
        d_f, d_b = _hgrn(pd, lower_bounds[l], mats, batch, t_len, l_len)

        w_r = _pad_cols(jnp.concatenate([router_group[l], router_expert[l]], axis=1), LANES).astype(BF16)
        x_new, h2, route, counts = _out_proj(
            stream, a, b_f, b_b, cc, d_f, d_b, pb, pd, g1, sh2, sc2,
            ln1_g[l].reshape(1, -1), ln1_b[l].reshape(1, -1),
            ret_gn[l].reshape(1, -1), hgrn_gn[l].reshape(1, -1),
            w_out[l].astype(BF16), w_r, tiles_per_batch, alpha, need_ctx)

        n_active = n if need_ctx else batch * t_len
        n_blocks = (n_active * TOP_K + N_EXPERTS * (MOE_ROWS - 1) + MOE_ROWS - 1) // MOE_ROWS
        pos, blk_e, blk_valid, pad_start, blk_slot, blk_next = _route_plan(route, counts, n_blocks)
        xbuf = _dispatch(h2, pos, pad_start, blk_valid, tiles_per_batch, not need_ctx, batch)
        ybuf = _moe_experts(xbuf, blk_e, blk_valid, blk_slot, blk_next, moe_w1, moe_w3, moe_w2, l)
        combine_args = (x_new, route, ybuf, pos, g2, ln2_g[l].reshape(1, -1), ln2_b[l].reshape(1, -1))
        if need_ctx:
            pending = combine_args
        else:
            out = _combine(*combine_args, tiles_per_batch, alpha, True, batch).reshape(batch, t_len, dm)
    return out
```

```python
import functools
import math

import numpy as np
import jax
import jax.numpy as jnp
from jax import lax
from jax.experimental import pallas as pl
from jax.experimental.pallas import tpu as pltpu

F32 = jnp.float32
BF16 = jnp.bfloat16

GRID_W = 64
HEAD_DIM = 128
SWA_HEADS = 4
SWA_KV_HEADS = 2
SWA_BLOCK = 128
RET_HEADS = 4
RET_DK = 128
MLA_HEADS = 4
MLA_Q_RANK = 384
MLA_KV_RANK = 128
MLA_NOPE = 128
MLA_ROPE = 64
MLA_V = 128
HGRN_HEADS = 4
N_GROUPS = 4
EXPERTS_PER_GROUP = 8
N_EXPERTS = N_GROUPS * EXPERTS_PER_GROUP
TOP_K = 2
ROPE_BASE = 10000.0
EPS = 1e-6

LANES = 128
SUBLANES = 8
VMEM_LIMIT_BYTES = 56 * 1024 * 1024
PROJ_VMEM_LIMIT_BYTES = 60 * 1024 * 1024

CHUNK = 128
ROW_TILE = 256
MOE_ROWS = 256
MLA_KEY_CHUNK = 256
MLA_HEADS_PER_STEP = 4
WIDE = 512

SEG_A = 1024
SEG_B = 2048
SEG_C = 640
SEG_D = 2560


def _cparams(sem, vmem=VMEM_LIMIT_BYTES):
    return pltpu.CompilerParams(dimension_semantics=sem, vmem_limit_bytes=vmem)


def _ln_rows(x):
    mu = jnp.mean(x, axis=-1, keepdims=True)
    xc = x - mu
    var = jnp.mean(xc * xc, axis=-1, keepdims=True)
    return xc * lax.rsqrt(var + EPS)


def _silu(x):
    return x * (1.0 / (1.0 + jnp.exp(-x)))


def _dot(a, b):
    return jnp.dot(a, b, preferred_element_type=F32)


def _pack_bf16_pairs(x):
    k = x.shape[1] // 2
    hi = lax.bitcast_convert_type(x[:, :k].astype(BF16).astype(F32), jnp.uint32)
    lo = lax.bitcast_convert_type(x[:, k:].astype(BF16).astype(F32), jnp.uint32)
    return hi | (lo >> 16)


def _unpack_bf16_pairs(w):
    hi = lax.bitcast_convert_type(w & jnp.uint32(0xFFFF0000), F32)
    lo = lax.bitcast_convert_type(w << 16, F32)
    return jnp.concatenate([hi, lo], axis=1)


def _dot_nt(a, b):
    return lax.dot_general(a, b, (((1,), (1,)), ((), ())), preferred_element_type=F32)


def _dot_tn(a, b):
    return lax.dot_general(a, b, (((0,), (0,)), ((), ())), preferred_element_type=F32)


def _ada_kernel(c_ref, w_ref, b_ref, o_ref):
    s = _silu(c_ref[...]).astype(BF16)
    o_ref[0] = _dot(s, w_ref[0].astype(BF16)) + b_ref[0]


def _ada(cvec, w_ada, b_ada):
    depth, d, n6 = w_ada.shape
    tn = n6 // 8
    return pl.pallas_call(
        _ada_kernel,
        out_shape=jax.ShapeDtypeStruct((depth, SUBLANES, n6), F32),
        grid=(depth, n6 // tn),
        in_specs=[
            pl.BlockSpec((SUBLANES, d), lambda l, j: (0, 0)),
            pl.BlockSpec((1, d, tn), lambda l, j: (l, 0, j)),
            pl.BlockSpec((1, 1, tn), lambda l, j: (l, 0, j)),
        ],
        out_specs=pl.BlockSpec((1, SUBLANES, tn), lambda l, j: (l, 0, j)),
        compiler_params=_cparams(("arbitrary", "arbitrary")),
        name="ada_mod",
    )(cvec, w_ada, b_ada.reshape(depth, 1, n6))


SEG_WIDTHS = (SEG_A, SEG_B, SEG_C, SEG_D)
SEG_C_TRUE = MLA_Q_RANK + MLA_KV_RANK + MLA_ROPE
IN_COLS = SEG_A + SEG_B + SEG_C_TRUE + SEG_D
IN_COLS_PAD = sum(SEG_WIDTHS)


PREP_ROWS = 256
_C_END = SEG_A + SEG_B + SEG_C_TRUE
_C_PAD = SEG_C - SEG_C_TRUE
assert _C_END % PREP_ROWS + _C_PAD <= PREP_ROWS and IN_COLS_PAD % PREP_ROWS in (0, PREP_ROWS // 2)


def _w_in_prep_kernel(w_ref, o_ref, prev_ref):
    j = pl.program_id(0)
    jc = _C_END // PREP_ROWS
    rem = _C_END % PREP_ROWS
    keep = PREP_ROWS - _C_PAD
    cur = w_ref[0].astype(BF16)

    @pl.when(j < jc)
    def _():
        o_ref[...] = cur

    @pl.when(j == jc)
    def _():
        o_ref[0:rem, :] = cur[0:rem]
        o_ref[rem:rem + _C_PAD, :] = jnp.zeros((_C_PAD, cur.shape[1]), BF16)
        o_ref[rem + _C_PAD:PREP_ROWS, :] = cur[rem:keep]

    @pl.when(j > jc)
    def _():
        o_ref[0:_C_PAD, :] = prev_ref[keep:PREP_ROWS, :]
        o_ref[_C_PAD:PREP_ROWS, :] = cur[0:keep]

    prev_ref[...] = cur


def _w_in_prep(w_in, layer):
    _, d, cols = w_in.shape
    assert cols == IN_COLS
    w_t = jnp.swapaxes(w_in, 1, 2)
    return pl.pallas_call(
        _w_in_prep_kernel,
        out_shape=jax.ShapeDtypeStruct((IN_COLS_PAD, d), BF16),
        grid=(pl.cdiv(IN_COLS_PAD, PREP_ROWS),),
        in_specs=[pl.BlockSpec((1, PREP_ROWS, d), lambda j: (layer, j, 0))],
        out_specs=pl.BlockSpec((PREP_ROWS, d), lambda j: (j, 0)),
        scratch_shapes=[pltpu.VMEM((PREP_ROWS, d), BF16)],
        compiler_params=_cparams(("arbitrary",)),
        name="w_in_prep",
    )(w_t)


def _column_blocks(outs):
    return [(o_ref, j, min(WIDE, o_ref.shape[1] - j)) for o_ref in outs for j in range(0, o_ref.shape[1], WIDE)]


def _project_rows(x, sh_ref, sc_ref, w_ref, outs, between=None):
    y = _ln_rows(x)
    h = (y * (1.0 + sc_ref[0]) + sh_ref[0]).astype(BF16)
    base = 0
    for idx, (o_ref, j, cw) in enumerate(_column_blocks(outs)):
        o_ref[:, j:j + cw] = _dot_nt(h, w_ref[base:base + cw, :])
        base += cw
        if between is not None:
            between(idx)


def _proj_kernel(x_ref, ctx_ref, sh_ref, sc_ref, w_ref, *rest, tiles_per_batch):
    mla_in, (oa, ob, od, qf, kf, vt, so, pc_buf) = rest[:7], rest[7:]
    is_ctx = pl.program_id(0) % tiles_per_batch == 0
    x = jnp.where(is_ctx, ctx_ref[0], x_ref[0])
    so[...] = x
    outs = (oa, ob, pc_buf, od)
    c_done = max(k for k, blk in enumerate(_column_blocks(outs)) if blk[0] is pc_buf)

    def after_block(idx):
        if idx == c_done:
            _mla_up_rows(pc_buf[...], *mla_in, qf, kf, vt)

    _project_rows(x, sh_ref, sc_ref, w_ref, outs, between=after_block)


def _group_of_tile(i, tiles_per_batch):
    return jnp.where(i % tiles_per_batch == 0, 2, i // tiles_per_batch)


def _proj(x, ctx, shift, scale, w_pad, mla_args, tiles_per_batch):
    tm = ROW_TILE
    batch, t_len, d = x.shape
    n = batch * (t_len + ctx.shape[1])
    tpb = tiles_per_batch
    gmap = lambda i: (_group_of_tile(i, tiles_per_batch), 0, 0)
    row = lambda w: pl.BlockSpec((tm, w), lambda i: (i, 0))
    f32 = lambda w: jax.ShapeDtypeStruct((n, w), F32)
    mla_in, mla_out, mla_shapes = _mla_up_specs(n, mla_args)
    return pl.pallas_call(
        functools.partial(_proj_kernel, tiles_per_batch=tiles_per_batch),
        out_shape=[f32(SEG_A), f32(SEG_B), f32(SEG_D)] + mla_shapes + [f32(d)],
        grid=(n // tm,),
        in_specs=[
            pl.BlockSpec((1, tm, d), lambda i: (i // tpb, jnp.maximum(i % tpb - 1, 0), 0)),
            pl.BlockSpec((1, tm, d), lambda i: (i // tpb, 0, 0)),
            pl.BlockSpec((1, 1, d), gmap),
            pl.BlockSpec((1, 1, d), gmap),
            pl.BlockSpec(memory_space=pltpu.VMEM),
        ] + mla_in,
        out_specs=[row(SEG_A), row(SEG_B), row(SEG_D)] + mla_out + [row(d)],
        scratch_shapes=[pltpu.VMEM((tm, SEG_C), F32)],
        compiler_params=_cparams(("arbitrary",), vmem=PROJ_VMEM_LIMIT_BYTES),
        name="ln_mod_inproj",
    )(x, ctx, shift, scale, w_pad, *mla_args)


def _rope128(x, cos, sin):
    return x * cos + pltpu.roll(x, 64, 1) * sin


def _rope64(x, cos, sin):
    lane = lax.broadcasted_iota(jnp.int32, x.shape, 1)
    rot = jnp.where((lane % 64) < 32, pltpu.roll(x, 96, 1), pltpu.roll(x, 32, 1))
    return x * cos + rot * sin


def _rope_tables(t_len, l_len, batch):
    rows = t_len // GRID_W
    row = np.repeat(np.arange(rows), GRID_W).astype(np.float32)
    col = np.tile(np.arange(GRID_W), rows).astype(np.float32)

    def angles(rot_dim):
        n_freq = rot_dim // 4
        inv = (ROPE_BASE ** (-np.arange(n_freq, dtype=np.float32) / n_freq)).astype(np.float32)
        return np.concatenate([row[:, None] * inv, col[:, None] * inv], -1).astype(np.float32)

    a_swa = angles(HEAD_DIM)
    cos_swa = np.concatenate([np.cos(a_swa), np.cos(a_swa)], -1)
    sin_swa = np.concatenate([-np.sin(a_swa), np.sin(a_swa)], -1)
    a_mla = angles(MLA_ROPE)
    cos_m = np.concatenate([np.cos(a_mla), np.cos(a_mla), np.ones((t_len, 64), np.float32)], -1)
    sin_m = np.concatenate([-np.sin(a_mla), np.sin(a_mla), np.zeros((t_len, 64), np.float32)], -1)
    ones = np.ones((l_len, LANES), np.float32)
    zeros = np.zeros((l_len, LANES), np.float32)
    cos_rows = np.concatenate([np.concatenate([ones, cos_m], 0)] * batch, 0)
    sin_rows = np.concatenate([np.concatenate([zeros, sin_m], 0)] * batch, 0)
    return (jnp.asarray(cos_swa, F32), jnp.asarray(sin_swa, F32),
            jnp.asarray(cos_rows, F32), jnp.asarray(sin_rows, F32))


def _mla_up_rows(pc, qn_ref, kvn_ref, wq_ref, wk_ref, wv_ref, cos_ref, sin_ref, q_out, k_out, v_out):
    scale = float((MLA_NOPE + MLA_ROPE) ** -0.5 * math.log2(math.e))
    cq = pc[:, :MLA_Q_RANK]
    ckv = pc[:, MLA_Q_RANK:MLA_Q_RANK + MLA_KV_RANK]
    kr = pc[:, MLA_Q_RANK + MLA_KV_RANK:]
    cos = cos_ref[...]
    sin = sin_ref[...]

    def rms(x, g):
        return x * lax.rsqrt(jnp.mean(x * x, axis=-1, keepdims=True) + EPS) * g

    qh = _dot(rms(cq, qn_ref[...]).astype(BF16), wq_ref[...])
    ckn = rms(ckv, kvn_ref[...]).astype(BF16)
    kh = _dot(ckn, wk_ref[...])
    v_out[...] = _dot_nt(wv_ref[...], ckn).astype(BF16)
    kr_rot = _rope64(kr, cos, sin).astype(BF16)
    for h in range(MLA_HEADS):
        base = h * 2 * LANES
        q_out[:, base:base + LANES] = (qh[:, base:base + LANES] * scale).astype(BF16)
        q_out[:, base + LANES:base + 2 * LANES] = (
            _rope64(qh[:, base + LANES:base + 2 * LANES], cos, sin) * scale).astype(BF16)
        k_out[:, base:base + LANES] = kh[:, h * LANES:(h + 1) * LANES].astype(BF16)
        k_out[:, base + LANES:base + 2 * LANES] = kr_rot


def _mla_up_specs(n, mla_args):
    tm = ROW_TILE
    const2 = lambda i, *_: (0, 0)
    rows = lambda i, *_: (i, 0)
    in_specs = [pl.BlockSpec(a.shape, const2) for a in mla_args[:5]]
    in_specs += [pl.BlockSpec((tm, LANES), rows), pl.BlockSpec((tm, LANES), rows)]
    qk_w = MLA_HEADS * 2 * LANES
    out_specs = [pl.BlockSpec((tm, qk_w), rows), pl.BlockSpec((tm, qk_w), rows),
                 pl.BlockSpec((MLA_HEADS * MLA_V, tm), lambda i, *_: (0, i))]
    out_shapes = [jax.ShapeDtypeStruct((n, qk_w), BF16), jax.ShapeDtypeStruct((n, qk_w), BF16),
                  jax.ShapeDtypeStruct((MLA_HEADS * MLA_V, n), BF16)]
    return in_specs, out_specs, out_shapes


def _mla_attn_kernel(q_ref, k_ref, vt_ref, o_ref, s_ref, p_ref, *, kc, l_len, need_ctx):
    p_len = k_ref.shape[0]
    tq = q_ref.shape[0]
    dqk = 2 * LANES
    heads = q_ref.shape[1] // dqk
    half = p_len // 2
    is_ctx = pl.program_id(2) == 0

    @pl.when(is_ctx)
    def _():
        if not need_ctx:
            o_ref[...] = jnp.zeros_like(o_ref)
            return
        for h in range(heads):
            hq = slice(h * dqk, (h + 1) * dqk)
            hv = slice(h * MLA_V, (h + 1) * MLA_V)
            s = _dot_nt(k_ref[0:l_len, hq], q_ref[:, hq])
            p = jnp.exp2(s - jnp.max(s, axis=0, keepdims=True))
            acc = _dot(vt_ref[hv, 0:l_len], p.astype(BF16))
            o_ref[:, hv] = (acc / jnp.sum(p, axis=0, keepdims=True)).T

    @pl.when(jnp.logical_not(is_ctx))
    def _():
        fold = lambda a: a.reshape(kc // SUBLANES, SUBLANES, tq)
        m8s = []
        for h in range(heads):
            q = q_ref[:, h * dqk:(h + 1) * dqk]
            m8 = None
            for off in range(0, p_len, kc):
                s = _dot_nt(k_ref[off:off + kc, h * dqk:(h + 1) * dqk], q)
                s_ref[h, off:off + kc, :] = s
                cm = jnp.max(fold(s), axis=0)
                m8 = cm if m8 is None else jnp.maximum(m8, cm)
            m8s.append(m8)
        for h in range(heads):
            m = jnp.max(m8s[h], axis=0, keepdims=True)
            l8 = jnp.zeros((SUBLANES, tq), F32)
            for off in range(0, p_len, kc):
                p = jnp.exp2(s_ref[h, off:off + kc, :] - m)
                l8 = l8 + jnp.sum(fold(p), axis=0)
                p_ref[h, off:off + kc, :] = p.astype(BF16)
            l = jnp.sum(l8, axis=0, keepdims=True)
            hv = slice(h * MLA_V, (h + 1) * MLA_V)
            acc = (_dot(vt_ref[hv, 0:half], p_ref[h, 0:half, :])
                   + _dot(vt_ref[hv, half:p_len], p_ref[h, half:p_len, :]))
            o_ref[:, hv] = (acc / l).T


def _mla_attn(qf, kf, vt, batch, t_len, l_len, need_ctx):
    n = qf.shape[0]
    p_len = t_len + l_len
    tq = ROW_TILE
    assert l_len == tq
    kc = MLA_KEY_CHUNK
    hp = MLA_HEADS_PER_STEP
    tpb = p_len // tq
    return pl.pallas_call(
        functools.partial(_mla_attn_kernel, kc=kc, l_len=l_len, need_ctx=need_ctx),
        out_shape=jax.ShapeDtypeStruct((n, MLA_HEADS * MLA_V), F32),
        grid=(batch, MLA_HEADS // hp, tpb),
        in_specs=[
            pl.BlockSpec((tq, hp * 2 * LANES), lambda b, h, i: (b * tpb + i, h)),
            pl.BlockSpec((p_len, hp * 2 * LANES), lambda b, h, i: (b, h), pipeline_mode=pl.Buffered(1)),
            pl.BlockSpec((hp * MLA_V, p_len), lambda b, h, i: (h, b), pipeline_mode=pl.Buffered(1)),
        ],
        out_specs=pl.BlockSpec((tq, hp * MLA_V), lambda b, h, i: (b * tpb + i, h)),
        scratch_shapes=[pltpu.VMEM((hp, p_len, tq), F32), pltpu.VMEM((hp, p_len, tq), BF16)],
        compiler_params=_cparams(("arbitrary", "arbitrary", "arbitrary")),
        name="mla_attn",
    )(qf, kf, vt)


def _swa_kernel(sink_ref, q_ref, kp_ref, kc_ref, kn_ref, vp_ref, vc_ref, vn_ref, kx_ref, vx_ref,
                cq_ref, sq_ref, cp_ref, sp_ref, cn_ref, sn_ref, o_ref, *, nb, n_ctx, scale, need_ctx):
    j = pl.program_id(0)
    n = j - n_ctx
    blk = SWA_BLOCK
    batch = q_ref.shape[0]
    rr = lax.broadcasted_iota(jnp.int32, (2 * blk, 1), 0)
    mx = lambda a: jnp.max(a, axis=-1, keepdims=True)
    sm = lambda a: jnp.sum(a, axis=-1, keepdims=True)

    def heads_of(b, h):
        q0 = q_ref[b, :, 2 * h * LANES:(2 * h + 1) * LANES]
        q1 = q_ref[b, :, (2 * h + 1) * LANES:(2 * h + 2) * LANES]
        sk = jnp.where(rr < blk, sink_ref[2 * h], sink_ref[2 * h + 1]).astype(F32)
        return q0, q1, sk

    def store(b, h, o):
        o_ref[b, :, 2 * h * LANES:(2 * h + 1) * LANES] = o[:blk]
        o_ref[b, :, (2 * h + 1) * LANES:(2 * h + 2) * LANES] = o[blk:]

    @pl.when(j < n_ctx)
    def _():
        if not need_ctx:
            o_ref[...] = jnp.zeros_like(o_ref)
            return
        for b in range(batch):
            for h in range(SWA_KV_HEADS):
                ks = slice(h * LANES, (h + 1) * LANES)
                q0, q1, sk = heads_of(b, h)
                qq = jnp.concatenate([q0, q1], axis=0).astype(BF16)
                s_x = _dot_nt(qq, kx_ref[b, :, ks].astype(BF16)) * scale
                m = jnp.maximum(mx(s_x), sk)
                p_x = jnp.exp(s_x - m)
                den = sm(p_x) + jnp.exp(sk - m)
                store(b, h, _dot(p_x.astype(BF16), vx_ref[b, :, ks].astype(BF16)) / den)

    @pl.when(j >= n_ctx)
    def _():
        cq, sq = cq_ref[...], sq_ref[...]
        cp, sp = cp_ref[...], sp_ref[...]
        cn, sn = cn_ref[...], sn_ref[...]
        ri = lax.broadcasted_iota(jnp.int32, (2 * blk, blk), 0) % blk
        ci = lax.broadcasted_iota(jnp.int32, (2 * blk, blk), 1)
        mask_p = (ci >= ri) & (n > 0)
        mask_n = (ci <= ri) & (n < nb - 1)
        neg = jnp.float32(-jnp.inf)
        for b in range(batch):
            for h in range(SWA_KV_HEADS):
                ks = slice(h * LANES, (h + 1) * LANES)
                q0, q1, sk = heads_of(b, h)
                qq = jnp.concatenate([_rope128(q0, cq, sq), _rope128(q1, cq, sq)], axis=0).astype(BF16)
                kc = _rope128(kc_ref[b, :, ks], cq, sq).astype(BF16)
                kp = _rope128(kp_ref[b, :, ks], cp, sp).astype(BF16)
                kn = _rope128(kn_ref[b, :, ks], cn, sn).astype(BF16)
                s_c = _dot_nt(qq, kc) * scale
                s_p = jnp.where(mask_p, _dot_nt(qq, kp) * scale, neg)
                s_n = jnp.where(mask_n, _dot_nt(qq, kn) * scale, neg)
                s_x = _dot_nt(qq, kx_ref[b, :, ks].astype(BF16)) * scale
                m = jnp.maximum(jnp.maximum(mx(s_c), mx(s_p)), jnp.maximum(mx(s_n), mx(s_x)))
                m = jnp.maximum(m, sk)
                p_c, p_p, p_n, p_x = (jnp.exp(a - m) for a in (s_c, s_p, s_n, s_x))
                den = sm(p_c) + sm(p_p) + sm(p_n) + sm(p_x) + jnp.exp(sk - m)
                store(b, h, (_dot(p_c.astype(BF16), vc_ref[b, :, ks].astype(BF16))
                             + _dot(p_p.astype(BF16), vp_ref[b, :, ks].astype(BF16))
                             + _dot(p_n.astype(BF16), vn_ref[b, :, ks].astype(BF16))
                             + _dot(p_x.astype(BF16), vx_ref[b, :, ks].astype(BF16))) / den)


def _swa(pa, sink, cos_t, sin_t, batch, t_len, l_len, need_ctx):
    n = pa.shape[0]
    p_len = t_len + l_len
    blk = SWA_BLOCK
    nb = t_len // blk
    n_ctx = l_len // blk
    scale = float(HEAD_DIM ** -0.5)
    kv_w = SWA_KV_HEADS * HEAD_DIM
    q_w = SWA_HEADS * HEAD_DIM
    kcol, vcol = q_w // kv_w, q_w // kv_w + 1
    pa3 = pa.reshape(batch, p_len, pa.shape[1])
    lat = lambda j: jnp.maximum(j - n_ctx, 0)
    same = lambda j: lat(j)
    prv = lambda j: jnp.maximum(lat(j) - 1, 0)
    nxt = lambda j: jnp.minimum(lat(j) + 1, nb - 1)
    kv = lambda col, rowf: pl.BlockSpec((batch, blk, kv_w), lambda j: (0, n_ctx + rowf(j), col))
    tab = lambda rowf: pl.BlockSpec((blk, LANES), lambda j: (rowf(j), 0))
    in_specs = [
        pl.BlockSpec(memory_space=pltpu.SMEM),
        pl.BlockSpec((batch, blk, q_w), lambda j: (0, j, 0)),
        kv(kcol, prv), kv(kcol, same), kv(kcol, nxt),
        kv(vcol, prv), kv(vcol, same), kv(vcol, nxt),
        pl.BlockSpec((batch, l_len, kv_w), lambda j: (0, 0, kcol)),
        pl.BlockSpec((batch, l_len, kv_w), lambda j: (0, 0, vcol)),
        tab(same), tab(same), tab(prv), tab(prv), tab(nxt), tab(nxt),
    ]
    out = pl.pallas_call(
        functools.partial(_swa_kernel, nb=nb, n_ctx=n_ctx, scale=scale, need_ctx=need_ctx),
        out_shape=jax.ShapeDtypeStruct((batch, p_len, q_w), F32),
        grid=(n_ctx + nb,),
        in_specs=in_specs,
        out_specs=pl.BlockSpec((batch, blk, q_w), lambda j: (0, j, 0)),
        compiler_params=_cparams(("arbitrary",)),
        name="swa",
    )(sink, pa3, pa3, pa3, pa3, pa3, pa3, pa3, pa3, pa3, cos_t, sin_t, cos_t, sin_t, cos_t, sin_t)
    return out.reshape(n, q_w)


def _chunk_index(d, c, n_l, n_t):
    bwd = jnp.where(c < n_l, n_l - 1 - c, n_l + (n_t - 1) - (c - n_l))
    return jnp.where(d == 0, c, bwd)


def _flip_iotas(d):
    row = lax.broadcasted_iota(jnp.int32, (CHUNK, CHUNK), 0)
    col = lax.broadcasted_iota(jnp.int32, (CHUNK, CHUNK), 1)
    rf = jnp.where(d == 0, row, CHUNK - 1 - row)
    cf = jnp.where(d == 0, col, CHUNK - 1 - col)
    return rf, cf


def _ret_kernel(s_ref, qf_ref, kf_ref, vf_ref, qb_ref, kb_ref, vb_ref, of_ref, ob_ref, st_ref):
    c = pl.program_id(0)
    batch = qf_ref.shape[0]

    @pl.when(c == 0)
    def _():
        st_ref[...] = jnp.zeros_like(st_ref)

    ks = float(RET_DK ** -0.5)
    dirs = ((qf_ref, kf_ref, vf_ref, of_ref), (qb_ref, kb_ref, vb_ref, ob_ref))
    for d, (q_ref, k_ref, v_ref, o_ref) in enumerate(dirs):
        rf, cf = _flip_iotas(d)
        rff = rf.astype(F32)
        dn = (rf - cf).astype(F32)
        for h in range(RET_HEADS):
            hs = slice(h * LANES, (h + 1) * LANES)
            sv = jnp.full((CHUNK, CHUNK), s_ref[d * RET_HEADS + h], F32)
            lg = jnp.log1p(-jnp.exp2(-sv))
            dec = jnp.where(dn >= 0, jnp.exp(dn * lg), 0.0)
            eq = jnp.exp((rff + 1.0) * lg)
            ek = jnp.exp((CHUNK - 1.0 - rff) * lg)
            a_chunk = jnp.exp(float(CHUNK) * lg)
            for b in range(batch):
                q = q_ref[b, :, hs]
                k = k_ref[b, :, hs] * ks
                v = v_ref[b, :, hs].astype(BF16)
                st = st_ref[d, b, h]
                a = _dot_nt(q.astype(BF16), k.astype(BF16)) * dec
                o = _dot(a.astype(BF16), v) + _dot_nt((q * eq).astype(BF16), st.astype(BF16))
                o_ref[b, :, hs] = o
                st_ref[d, b, h] = a_chunk * st + _dot_tn(v, (k * ek).astype(BF16))


def _hgrn_level_mats():
    c = CHUNK
    t = np.arange(c)[:, None]
    u = np.arange(c)[None, :]
    mats = [u <= t, u > t]
    m = c // 2
    while m >= 1:
        mid = (t // (2 * m)) * 2 * m + m
        second = (t % (2 * m)) >= m
        qrole = (u >= mid) & (u <= t)
        krole = (u > t) & (u <= mid - 1)
        mats.append(np.where(second, qrole, krole))
        m //= 2
    fwd = np.concatenate(mats, 0).astype(np.float32)
    bwd = np.concatenate([mm[::-1, ::-1] for mm in mats], 0).astype(np.float32)
    return np.stack([fwd, bwd], 0)


N_LEVELS = int(math.log2(CHUNK))


def _hgrn_kernel(m_ref, lb_ref, qf_ref, ff_ref, vf_ref, qb_ref, fb_ref, vb_ref, of_ref, ob_ref, st_ref):
    c = pl.program_id(0)
    batch = qf_ref.shape[0]

    @pl.when(c == 0)
    def _():
        st_ref[...] = jnp.zeros_like(st_ref)

    dirs = ((qf_ref, ff_ref, vf_ref, of_ref), (qb_ref, fb_ref, vb_ref, ob_ref))
    for d, (q_ref, f_ref, v_ref, o_ref) in enumerate(dirs):
        rf, cf = _flip_iotas(d)
        mst = m_ref[d]
        last = CHUNK - 1 if d == 0 else 0
        lvl_masks = []
        for lvl in range(N_LEVELS):
            m = CHUNK >> (lvl + 1)
            sh = N_LEVELS - lvl
            lvl_masks.append(((rf >> sh) == (cf >> sh)) & ((rf & m) != 0) & ((cf & m) == 0))
        diag = rf == cf
        for h in range(HGRN_HEADS):
            hs = slice(h * LANES, (h + 1) * LANES)
            lb = lb_ref[d, :, hs]
            for b in range(batch):
                f = lb + (1.0 - lb) * (1.0 / (1.0 + jnp.exp(-f_ref[b, :, hs])))
                k = 1.0 - f
                g = jnp.log(f)
                q = _silu(q_ref[b, :, hs])
                v = v_ref[b, :, hs].astype(BF16)
                e = jnp.exp(_dot(mst, g.astype(BF16)))
                e_q = e[0:CHUNK]
                q_in = q * e_q
                k_st = k * e[CHUNK:2 * CHUNK]
                scores = jnp.where(diag, _dot_nt(q.astype(BF16), k.astype(BF16)), 0.0)
                for lvl in range(N_LEVELS):
                    el = e[(2 + lvl) * CHUNK:(3 + lvl) * CHUNK]
                    sl = _dot_nt((q * el).astype(BF16), (k * el).astype(BF16))
                    scores = scores + jnp.where(lvl_masks[lvl], sl, 0.0)
                st = st_ref[d, b, h]
                o = _dot(scores.astype(BF16), v) + _dot_nt(q_in.astype(BF16), st.astype(BF16))
                o_ref[b, :, hs] = o
                st_ref[d, b, h] = st * e_q[last:last + 1, :] + _dot_tn(v, k_st.astype(BF16))


def _scan_specs(batch, t_len, l_len):
    n_l, n_t = l_len // CHUNK, t_len // CHUNK

    def spec(d, col):
        return pl.BlockSpec((batch, CHUNK, WIDE), lambda c: (0, _chunk_index(d, c, n_l, n_t), col))

    return spec, n_l + n_t


def _retention(pb, ret_s, batch, t_len, l_len):
    n = pb.shape[0]
    p_len = t_len + l_len
    pb3 = pb.reshape(batch, p_len, pb.shape[1])
    spec, n_chunks = _scan_specs(batch, t_len, l_len)
    out_f, out_b = pl.pallas_call(
        _ret_kernel,
        out_shape=[jax.ShapeDtypeStruct((batch, p_len, WIDE), F32)] * 2,
        grid=(n_chunks,),
        in_specs=[pl.BlockSpec(memory_space=pltpu.SMEM)]
                 + [spec(0, col) for col in (0, 1, 2)] + [spec(1, col) for col in (0, 1, 2)],
        out_specs=[spec(0, 0), spec(1, 0)],
        scratch_shapes=[pltpu.VMEM((2, batch, RET_HEADS, CHUNK, CHUNK), F32)],
        compiler_params=_cparams(("arbitrary",)),
        name="retention_scan",
    )(ret_s.reshape(-1), pb3, pb3, pb3, pb3, pb3, pb3)
    return out_f.reshape(n, WIDE), out_b.reshape(n, WIDE)


def _hgrn(pd, lower_bounds, mats, batch, t_len, l_len):
    n = pd.shape[0]
    p_len = t_len + l_len
    pd3 = pd.reshape(batch, p_len, pd.shape[1])
    lb3 = lower_bounds.reshape(2, 1, WIDE)
    spec, n_chunks = _scan_specs(batch, t_len, l_len)
    out_f, out_b = pl.pallas_call(
        _hgrn_kernel,
        out_shape=[jax.ShapeDtypeStruct((batch, p_len, WIDE), F32)] * 2,
        grid=(n_chunks,),
        in_specs=[pl.BlockSpec(mats.shape, lambda c: (0, 0, 0)),
                  pl.BlockSpec(lb3.shape, lambda c: (0, 0, 0)),
                  spec(0, 0), spec(0, 1), spec(0, 3), spec(1, 0), spec(1, 2), spec(1, 3)],
        out_specs=[spec(0, 0), spec(1, 0)],
        scratch_shapes=[pltpu.VMEM((2, batch, HGRN_HEADS, CHUNK, CHUNK), F32)],
        compiler_params=_cparams(("arbitrary",)),
        name="hgrn_scan",
    )(mats, lb3, pd3, pd3, pd3, pd3, pd3, pd3)
    return out_f.reshape(n, WIDE), out_b.reshape(n, WIDE)


R_E1, R_E2, R_G1, R_G2, R_S1, R_S2 = 0, 1, 2, 3, 4, 5


def _route(logits, count_ref, active):
    tm = logits.shape[0]
    lane = lax.broadcasted_iota(jnp.int32, logits.shape, 1)
    lanef = lane.astype(F32)
    big = jnp.float32(1e9)
    neg = jnp.float32(-jnp.inf)
    mx = lambda a: jnp.max(a, axis=-1, keepdims=True)
    mn = lambda a: jnp.min(a, axis=-1, keepdims=True)
    sm = lambda a: jnp.sum(a, axis=-1, keepdims=True)
    gl = jnp.where(lane < N_GROUPS, logits, neg)
    gm = mx(gl)
    p_grp = 1.0 / sm(jnp.exp(gl - gm))
    grp = mn(jnp.where(gl == gm, lanef, big))
    lo = N_GROUPS + grp * EXPERTS_PER_GROUP
    ing = (lanef >= lo) & (lanef < lo + EXPERTS_PER_GROUP)
    el = jnp.where(ing, logits, neg)
    l1 = mx(el)
    i1 = mn(jnp.where(el == l1, lanef, big))
    el2 = jnp.where(lanef == i1, neg, el)
    l2 = mx(el2)
    i2 = mn(jnp.where(el2 == l2, lanef, big))
    r = jnp.exp(l2 - l1)
    g1 = p_grp / (1.0 + r)
    g2 = p_grp * r / (1.0 + r)
    e1 = i1 - N_GROUPS
    e2 = i2 - N_GROUPS
    oh1 = lanef == e1
    oh2 = lanef == e2
    both = jnp.where(oh1 | oh2, active, 0.0)
    ri = lax.broadcasted_iota(jnp.int32, (tm, tm), 0)
    ci = lax.broadcasted_iota(jnp.int32, (tm, tm), 1)
    earlier = jnp.where(ci < ri, 1.0, 0.0).astype(BF16)
    before = _dot(earlier, both.astype(BF16)) + count_ref[0:1, :]
    s1 = sm(jnp.where(oh1, before, 0.0))
    s2 = sm(jnp.where(oh2, before, 0.0))
    count_ref[...] = count_ref[...] + jnp.sum(both, axis=0, keepdims=True)
    out = jnp.zeros(logits.shape, F32)
    for ln, val in ((R_E1, e1), (R_E2, e2), (R_G1, g1), (R_G2, g2), (R_S1, s1), (R_S2, s2)):
        out = jnp.where(lane == ln, val, out)
    return out


def _out_kernel(x_ref, a_ref, bf_ref, bb_ref, c_ref, df_ref, db_ref, rg_ref, hg_ref, g1_ref, sh2_ref, sc2_ref,
                lng_ref, lnb_ref, rgn_ref, hgn_ref, wo_ref, wr_ref,
                xo_ref, h2_ref, r_ref, cnt_ref, *, alpha, tiles_per_batch, route_ctx):
    i = pl.program_id(0)

    @pl.when(i == 0)
    def _():
        cnt_ref[...] = jnp.zeros_like(cnt_ref)

    bsum = bf_ref[...] + bb_ref[...]
    dsum = df_ref[...] + db_ref[...]
    rgn = rgn_ref[...]
    hgn = hgn_ref[...]
    parts_b, parts_d = [], []
    for h in range(4):
        hs = slice(h * LANES, (h + 1) * LANES)
        parts_b.append(_ln_rows(bsum[:, hs]))
        dh = dsum[:, hs]
        parts_d.append(dh * lax.rsqrt(jnp.mean(dh * dh, axis=-1, keepdims=True) + EPS))
    bo = _silu(rg_ref[...]) * (jnp.concatenate(parts_b, axis=1) * rgn)
    do = _silu(hg_ref[...]) * (jnp.concatenate(parts_d, axis=1) * hgn)
    mix = jnp.concatenate([a_ref[...], bo, c_ref[...], do], axis=1).astype(BF16)
    o = _dot(mix, wo_ref[...])
    y = alpha * x_ref[...] + g1_ref[0] * o
    xn = _ln_rows(y) * lng_ref[...] + lnb_ref[...]
    xo_ref[...] = xn
    h2 = _ln_rows(xn) * (1.0 + sc2_ref[0]) + sh2_ref[0]
    h2_ref[...] = _pack_bf16_pairs(h2)
    if route_ctx:
        active = jnp.float32(1.0)
    else:
        active = jnp.where(i % tiles_per_batch == 0, 0.0, 1.0).astype(F32)
    r_ref[...] = _route(_dot(h2.astype(BF16), wr_ref[...]), cnt_ref, active)


def _out_proj(stream, a, b_f, b_b, c, d_f, d_b, pb, pd, g1, sh2, sc2, ln_g, ln_b, ret_gn, hgrn_gn, w_out, w_r,
              tiles_per_batch, alpha, route_ctx):
    n, dm = stream.shape
    tm = ROW_TILE
    gmap = lambda i: (_group_of_tile(i, tiles_per_batch), 0, 0)
    row = lambda w: pl.BlockSpec((tm, w), lambda i: (i, 0))
    const = lambda w: pl.BlockSpec((1, w), lambda i: (0, 0))
    resident = pl.BlockSpec(memory_space=pltpu.VMEM)
    return pl.pallas_call(
        functools.partial(_out_kernel, alpha=alpha, tiles_per_batch=tiles_per_batch, route_ctx=route_ctx),
        out_shape=[jax.ShapeDtypeStruct((n, dm), F32), jax.ShapeDtypeStruct((n, dm // 2), jnp.uint32),
                   jax.ShapeDtypeStruct((n, LANES), F32), jax.ShapeDtypeStruct((SUBLANES, LANES), F32)],
        grid=(n // tm,),
        in_specs=[row(dm), row(WIDE), row(WIDE), row(WIDE), row(WIDE), row(WIDE), row(WIDE),
                  pl.BlockSpec((tm, WIDE), lambda i: (i, 3)),
                  pl.BlockSpec((tm, WIDE), lambda i: (i, 4)),
                  pl.BlockSpec((1, 1, dm), gmap), pl.BlockSpec((1, 1, dm), gmap),
                  pl.BlockSpec((1, 1, dm), gmap),
                  const(dm), const(dm), const(WIDE), const(WIDE), resident, resident],
        out_specs=[row(dm), row(dm // 2), row(LANES),
                   pl.BlockSpec((SUBLANES, LANES), lambda i: (0, 0))],
        compiler_params=_cparams(("arbitrary",)),
        name="mix_outproj_norm_route",
    )(stream, a, b_f, b_b, c, d_f, d_b, pb, pd, g1, sh2, sc2, ln_g, ln_b, ret_gn, hgrn_gn, w_out, w_r)


def _route_plan(route, counts_f, n_blocks):
    rows = MOE_ROWS
    counts = counts_f[0, :N_EXPERTS].astype(jnp.int32)
    padded = (counts + rows - 1) // rows * rows
    pend = jnp.cumsum(padded)
    pstart = pend - padded
    eid = route[:, R_E1:R_E2 + 1].astype(jnp.int32)
    rank = route[:, R_S1:R_S2 + 1].astype(jnp.int32)
    onehot = eid[:, :, None] == jnp.arange(N_EXPERTS, dtype=jnp.int32)
    pos = jnp.sum(jnp.where(onehot, pstart, 0), axis=-1) + rank
    blk_start = jnp.arange(n_blocks, dtype=jnp.int32) * rows
    blk_e = jnp.minimum(jnp.sum(blk_start[:, None] >= pend[None, :], axis=-1), N_EXPERTS - 1)
    blk_valid = (blk_start < pend[-1]).astype(jnp.int32)
    pad_start = pstart + counts
    ids = jnp.arange(N_EXPERTS, dtype=jnp.int32)
    nonempty = counts > 0
    seq = jnp.cumsum(nonempty.astype(jnp.int32)) - nonempty.astype(jnp.int32)
    cand = jnp.where(nonempty, ids, N_EXPERTS)
    later = jnp.where(ids[None, :] > ids[:, None], cand[None, :], N_EXPERTS)
    nxt = jnp.min(later, axis=-1)
    nxt = jnp.where(nxt >= N_EXPERTS, -1, nxt)
    blk_oh = blk_e[:, None] == ids[None, :]
    blk_slot = jnp.sum(jnp.where(blk_oh, seq % 2, 0), axis=-1)
    blk_next = jnp.sum(jnp.where(blk_oh, nxt, 0), axis=-1)
    i32 = lambda a: a.astype(jnp.int32)
    return (i32(pos.reshape(-1)), i32(blk_e), blk_valid, i32(pad_start), i32(blk_slot), i32(blk_next))


def _dispatch_kernel(pos_ref, pad_ref, bv_ref, h_ref, x_hbm, zbuf, sem, *, tile_of_step, n_blocks):
    i = pl.program_id(0)
    tm = ROW_TILE
    rows = MOE_ROWS

    def zero_copy(e):
        start = pl.multiple_of(pad_ref[e] & ~(SUBLANES - 1), SUBLANES)
        return pltpu.make_async_copy(zbuf, x_hbm.at[pl.ds(start, rows + SUBLANES)], sem.at[1])

    def unused_block_copy(j):
        start = pl.multiple_of(j * rows, rows)
        return pltpu.make_async_copy(zbuf.at[pl.ds(0, rows)], x_hbm.at[pl.ds(start, rows)], sem.at[1])

    @pl.when(i == 0)
    def _():
        zbuf[...] = jnp.zeros_like(zbuf)

        def fill(j, carry):
            @pl.when(bv_ref[j] == 0)
            def _():
                unused_block_copy(j).start()
                unused_block_copy(j).wait()
            return carry

        lax.fori_loop(0, n_blocks + 2, fill, 0)
        for e in range(N_EXPERTS):
            zero_copy(e).start()
        for e in range(N_EXPERTS):
            zero_copy(e).wait()

    def row_copy(r, k):
        p = pos_ref[(tile_of_step(i) * tm + r) * TOP_K + k]
        return pltpu.make_async_copy(h_ref.at[r], x_hbm.at[p], sem.at[0])

    for r in range(tm):
        for k in range(TOP_K):
            row_copy(r, k).start(priority=k)
    for r in range(tm):
        for k in range(TOP_K):
            row_copy(r, k).wait()


def _tile_schedule(n, tiles_per_batch, latent_only, batch):
    if latent_only:
        lat = tiles_per_batch - 1
        return batch * lat, (lambda s: (s // lat) * tiles_per_batch + 1 + s % lat)
    return n // ROW_TILE, (lambda s: s)


def _dispatch(h2, pos, pad_start, blk_valid, tiles_per_batch, latent_only, batch):
    n, dm = h2.shape
    tm = ROW_TILE
    n_blocks = blk_valid.shape[0]
    blk_valid = jnp.concatenate([blk_valid, jnp.zeros((2,), jnp.int32)])
    n_tiles, tile_of_step = _tile_schedule(n, tiles_per_batch, latent_only, batch)
    grid_spec = pltpu.PrefetchScalarGridSpec(
        num_scalar_prefetch=3,
        grid=(n_tiles,),
        in_specs=[pl.BlockSpec((tm, dm), lambda i, p, z, v: (tile_of_step(i), 0))],
        out_specs=pl.BlockSpec(memory_space=pl.ANY),
        scratch_shapes=[pltpu.VMEM((MOE_ROWS + SUBLANES, dm), h2.dtype), pltpu.SemaphoreType.DMA((2,))],
    )
    return pl.pallas_call(
        functools.partial(_dispatch_kernel, tile_of_step=tile_of_step, n_blocks=n_blocks),
        out_shape=jax.ShapeDtypeStruct(((n_blocks + 2) * MOE_ROWS, dm), h2.dtype),
        grid_spec=grid_spec,
        compiler_params=_cparams(("arbitrary",)),
        name="moe_dispatch",
    )(pos, pad_start, blk_valid, h2)


def _moe_kernel(be_ref, bv_ref, slot_ref, next_ref, x_ref, w1_hbm, w3_hbm, w2_hbm, y_ref,
                s1, s3, s2, w1b, w3b, w2b, sem, *, layer):
    j = pl.program_id(0)
    valid = bv_ref[j] == 1
    first = valid & ((j == 0) | (be_ref[j] != be_ref[jnp.maximum(j - 1, 0)]))

    def weight_copies(e, slot):
        return (pltpu.make_async_copy(w1_hbm.at[layer, e], s1.at[slot], sem.at[slot]),
                pltpu.make_async_copy(w3_hbm.at[layer, e], s3.at[slot], sem.at[slot]),
                pltpu.make_async_copy(w2_hbm.at[layer, e], s2.at[slot], sem.at[slot]))

    @pl.when(valid & (j == 0))
    def _():
        for cp in weight_copies(be_ref[0], slot_ref[0]):
            cp.start()

    @pl.when(first)
    def _():
        slot = slot_ref[j]
        for cp in weight_copies(be_ref[j], slot):
            cp.wait()
        nxt = next_ref[j]

        @pl.when(nxt >= 0)
        def _():
            for cp in weight_copies(nxt, 1 - slot):
                cp.start()

        w1b[...] = s1[slot].astype(BF16)
        w3b[...] = s3[slot].astype(BF16)
        w2b[...] = s2[slot].astype(BF16)

    @pl.when(valid)
    def _():
        x = _unpack_bf16_pairs(x_ref[...]).astype(BF16)
        hmid = _silu(_dot(x, w1b[...])) * _dot(x, w3b[...])
        y_ref[...] = _pack_bf16_pairs(_dot(hmid.astype(BF16), w2b[...]))

    @pl.when(bv_ref[j] == 0)
    def _():
        y_ref[...] = jnp.zeros_like(y_ref)


def _moe_experts(xbuf, blk_e, blk_valid, blk_slot, blk_next, w1, w3, w2, layer):
    dm = w1.shape[2]
    ff = w1.shape[-1]
    rows = MOE_ROWS
    n_blocks = blk_e.shape[0]
    hbm = pl.BlockSpec(memory_space=pl.ANY)
    grid_spec = pltpu.PrefetchScalarGridSpec(
        num_scalar_prefetch=4,
        grid=(n_blocks,),
        in_specs=[
            pl.BlockSpec((rows, dm // 2), lambda j, be, bv, sl, nx: (j * bv[j], 0)),
            hbm, hbm, hbm,
        ],
        out_specs=pl.BlockSpec((rows, dm // 2), lambda j, be, bv, sl, nx: (j, 0)),
        scratch_shapes=[
            pltpu.VMEM((2, dm, ff), F32), pltpu.VMEM((2, dm, ff), F32), pltpu.VMEM((2, ff, dm), F32),
            pltpu.VMEM((dm, ff), BF16), pltpu.VMEM((dm, ff), BF16), pltpu.VMEM((ff, dm), BF16),
            pltpu.SemaphoreType.DMA((2,)),
        ],
    )
    return pl.pallas_call(
        functools.partial(_moe_kernel, layer=layer),
        out_shape=jax.ShapeDtypeStruct((n_blocks * rows, dm // 2), jnp.uint32),
        grid_spec=grid_spec,
        compiler_params=_cparams(("arbitrary",)),
        name="moe_experts",
    )(blk_e, blk_valid, blk_slot, blk_next, xbuf, w1, w3, w2)


def _expert_row_gather(pos_ref, y_hbm, ybuf, sem, tile_of_step):
    tm = ROW_TILE

    def row_copy(step, sl, r, k):
        p = pos_ref[(tile_of_step(step) * tm + r) * TOP_K + k]
        return pltpu.make_async_copy(y_hbm.at[p], ybuf.at[sl, k, r], sem.at[sl])

    def start_rows(step, sl, r0, r1):
        for r in range(r0, r1):
            for k in range(TOP_K):
                row_copy(step, sl, r, k).start(priority=k)

    def wait_rows(step, sl):
        for r in range(tm):
            for k in range(TOP_K):
                row_copy(step, sl, r, k).wait()

    return start_rows, wait_rows


def _gate_weighted(ybuf, slot, r_ref):
    route = r_ref[...]
    return (_unpack_bf16_pairs(ybuf[slot, 0]) * route[:, R_G1:R_G1 + 1]
            + _unpack_bf16_pairs(ybuf[slot, 1]) * route[:, R_G2:R_G2 + 1])


def _gathered_expert_rows(pos_ref, r_ref, y_hbm, ybuf, sem, *, n_tiles, tile_of_step):
    i = pl.program_id(0)
    slot = i % 2
    start_rows, wait_rows = _expert_row_gather(pos_ref, y_hbm, ybuf, sem, tile_of_step)

    @pl.when(i == 0)
    def _():
        start_rows(0, 0, 0, ROW_TILE)

    @pl.when(i + 1 < n_tiles)
    def _():
        start_rows(jnp.minimum(i + 1, n_tiles - 1), 1 - slot, 0, ROW_TILE)

    wait_rows(i, slot)
    return _gate_weighted(ybuf, slot, r_ref)


def _combine_kernel(pos_ref, x_ref, r_ref, g2_ref, lng_ref, lnb_ref, y_hbm, o_ref, ybuf, sem,
                    *, n_tiles, tile_of_step, alpha):
    y = _gathered_expert_rows(pos_ref, r_ref, y_hbm, ybuf, sem, n_tiles=n_tiles, tile_of_step=tile_of_step)
    z = alpha * x_ref[...] + g2_ref[0] * y
    o_ref[...] = _ln_rows(z) * lng_ref[...] + lnb_ref[...]


def _combine(x_new, route, ybuf, pos, g2, ln_g, ln_b, tiles_per_batch, alpha, latent_only, batch):
    n, dm = x_new.shape
    tm = ROW_TILE
    n_tiles, tile_of_step = _tile_schedule(n, tiles_per_batch, latent_only, batch)
    gmap = lambda i, p: (_group_of_tile(tile_of_step(i), tiles_per_batch), 0, 0)
    grid_spec = pltpu.PrefetchScalarGridSpec(
        num_scalar_prefetch=1,
        grid=(n_tiles,),
        in_specs=[
            pl.BlockSpec((tm, dm), lambda i, p: (tile_of_step(i), 0)),
            pl.BlockSpec((tm, LANES), lambda i, p: (tile_of_step(i), 0)),
            pl.BlockSpec((1, 1, dm), gmap),
            pl.BlockSpec((1, dm), lambda i, p: (0, 0)),
            pl.BlockSpec((1, dm), lambda i, p: (0, 0)),
            pl.BlockSpec(memory_space=pl.ANY),
        ],
        out_specs=pl.BlockSpec((tm, dm), lambda i, p: (i, 0)),
        scratch_shapes=[pltpu.VMEM((2, TOP_K, tm, dm // 2), jnp.uint32), pltpu.SemaphoreType.DMA((2,))],
    )
    return pl.pallas_call(
        functools.partial(_combine_kernel, n_tiles=n_tiles, tile_of_step=tile_of_step, alpha=alpha),
        out_shape=jax.ShapeDtypeStruct((n_tiles * tm, dm), F32),
        grid_spec=grid_spec,
        compiler_params=_cparams(("arbitrary",)),
        name="moe_combine_norm",
    )(pos, x_new, route, g2, ln_g, ln_b, ybuf)


def _combine_proj_kernel(pos_ref, x_ref, r_ref, g2_ref, lng_ref, lnb_ref, y_hbm, sh_ref, sc_ref, w_ref,
                         *rest, n_tiles, alpha):
    mla_in, (oa, ob, od, qf, kf, vt, so, ybuf, sem, oc) = rest[:7], rest[7:]
    i = pl.program_id(0)
    slot = i % 2
    tm = ROW_TILE
    start_rows, wait_rows = _expert_row_gather(pos_ref, y_hbm, ybuf, sem, lambda s: s)

    @pl.when(i == 0)
    def _():
        start_rows(0, 0, 0, tm)

    wait_rows(i, slot)
    z = alpha * x_ref[...] + g2_ref[0] * _gate_weighted(ybuf, slot, r_ref)
    stream = _ln_rows(z) * lng_ref[...] + lnb_ref[...]
    so[...] = stream
    outs = (oa, ob, oc, od)
    n_blk = len(_column_blocks(outs))
    per = -(-tm // n_blk)
    nxt = jnp.minimum(i + 1, n_tiles - 1)

    c_done = max(k for k, blk in enumerate(_column_blocks(outs)) if blk[0] is oc)

    def after_block(idx):
        start_rows(nxt, 1 - slot, min(idx * per, tm), min((idx + 1) * per, tm))
        if idx == c_done:
            _mla_up_rows(oc[...], *mla_in, qf, kf, vt)

    _project_rows(stream, sh_ref, sc_ref, w_ref, outs, between=after_block)

    @pl.when(i == n_tiles - 1)
    def _():
        wait_rows(nxt, 1 - slot)


def _combine_proj(x_new, route, ybuf, pos, g2, ln_g, ln_b, shift, scale, w_pad, mla_args,
                  tiles_per_batch, alpha):
    n, dm = x_new.shape
    tm = ROW_TILE
    n_tiles = n // tm
    gmap = lambda i, p: (_group_of_tile(i, tiles_per_batch), 0, 0)
    row = lambda w: pl.BlockSpec((tm, w), lambda i, p: (i, 0))
    f32 = lambda w: jax.ShapeDtypeStruct((n, w), F32)
    const = pl.BlockSpec((1, dm), lambda i, p: (0, 0))
    mod = pl.BlockSpec((1, 1, dm), gmap)
    mla_in, mla_out, mla_shapes = _mla_up_specs(n, mla_args)
    grid_spec = pltpu.PrefetchScalarGridSpec(
        num_scalar_prefetch=1,
        grid=(n_tiles,),
        in_specs=[row(dm), row(LANES), mod, const, const, pl.BlockSpec(memory_space=pl.ANY),
                  mod, mod, pl.BlockSpec(memory_space=pltpu.VMEM)] + mla_in,
        out_specs=[row(SEG_A), row(SEG_B), row(SEG_D)] + mla_out + [row(dm)],
        scratch_shapes=[pltpu.VMEM((2, TOP_K, tm, dm // 2), jnp.uint32), pltpu.SemaphoreType.DMA((2,)),
                        pltpu.VMEM((tm, SEG_C), F32)],
    )
    return pl.pallas_call(
        functools.partial(_combine_proj_kernel, n_tiles=n_tiles, alpha=alpha),
        out_shape=[f32(SEG_A), f32(SEG_B), f32(SEG_D)] + mla_shapes + [f32(dm)],
        grid_spec=grid_spec,
        compiler_params=_cparams(("arbitrary",), vmem=PROJ_VMEM_LIMIT_BYTES),
        name="moe_combine_ln_mod_inproj",
    )(pos, x_new, route, g2, ln_g, ln_b, ybuf, shift, scale, w_pad, *mla_args)


def _pad_cols(w, width):
    return jnp.pad(w, ((0, 0), (0, width - w.shape[1])))


def kernel(x, c, ctx, c_ctx, w_ada, b_ada, w_in, swa_sink, ret_decay_exp, ret_gn, mla_q_norm, mla_kv_norm,
           mla_w_uq, mla_w_ukv, hgrn_lb_logits, hgrn_gn, w_out, ln1_g, ln1_b, router_group, router_expert,
           moe_w1, moe_w3, moe_w2, ln2_g, ln2_b):
    batch, t_len, dm = x.shape
    l_len = ctx.shape[1]
    depth = w_in.shape[0]
    p_len = t_len + l_len
    n = batch * p_len
    assert batch == 2 and l_len == ROW_TILE and t_len % 256 == 0 and t_len % GRID_W == 0
    tiles_per_batch = p_len // ROW_TILE
    alpha = float((2 * depth) ** 0.25)

    cos_swa, sin_swa, cos_rows, sin_rows = _rope_tables(t_len, l_len, batch)
    mats = jnp.asarray(_hgrn_level_mats(), BF16)

    cvec = jnp.zeros((SUBLANES, dm), F32).at[0:batch].set(c).at[batch].set(c_ctx)
    mod = _ada(cvec, w_ada, b_ada)

    lbp = jax.nn.softmax(hgrn_lb_logits.astype(F32), axis=0)
    lower_bounds = jnp.cumsum(lbp, axis=0) - lbp[0]

    pending = None

    out = None
    for l in range(depth):
        need_ctx = l < depth - 1
        m6 = mod[l, :3].reshape(3, 6, dm)
        sh1, sc1, g1, sh2, sc2, g2 = (m6[:, k].reshape(3, 1, dm) for k in range(6))

        wq = mla_w_uq[l].reshape(MLA_Q_RANK, MLA_HEADS, MLA_NOPE + MLA_ROPE)
        wq = jnp.pad(wq, ((0, 0), (0, 0), (0, 2 * LANES - (MLA_NOPE + MLA_ROPE))))
        wq = wq.reshape(MLA_Q_RANK, MLA_HEADS * 2 * LANES).astype(BF16)
        wkv = mla_w_ukv[l].reshape(MLA_KV_RANK, MLA_HEADS, MLA_NOPE + MLA_V)
        wk = wkv[:, :, :MLA_NOPE].reshape(MLA_KV_RANK, -1).astype(BF16)
        wv_t = wkv[:, :, MLA_NOPE:].reshape(MLA_KV_RANK, -1).T.astype(BF16)
        mla_args = (mla_q_norm[l].reshape(1, -1), mla_kv_norm[l].reshape(1, -1), wq, wk, wv_t,
                    cos_rows, sin_rows)

        w_pad = _w_in_prep(w_in, l)
        if pending is None:
            pa, pb, pd, qf, kf, vt, stream = _proj(x, ctx, sh1, sc1, w_pad, mla_args, tiles_per_batch)
        else:
            pa, pb, pd, qf, kf, vt, stream = _combine_proj(*pending, sh1, sc1, w_pad, mla_args,
                                                           tiles_per_batch, alpha)

        a = _swa(pa, swa_sink[l], cos_swa, sin_swa, batch, t_len, l_len, need_ctx)
        cc = _mla_attn(qf, kf, vt, batch, t_len, l_len, need_ctx)
        b_f, b_b = _retention(pb, ret_decay_exp[l], batch, t_len, l_len)
        d_f, d_b = _hgrn(pd, lower_bounds[l], mats, batch, t_len, l_len)

        w_r = _pad_cols(jnp.concatenate([router_group[l], router_expert[l]], axis=1), LANES).astype(BF16)
        x_new, h2, route, counts = _out_proj(
            stream, a, b_f, b_b, cc, d_f, d_b, pb, pd, g1, sh2, sc2,
            ln1_g[l].reshape(1, -1), ln1_b[l].reshape(1, -1),
            ret_gn[l].reshape(1, -1), hgrn_gn[l].reshape(1, -1),
            w_out[l].astype(BF16), w_r, tiles_per_batch, alpha, need_ctx)

        n_active = n if need_ctx else batch * t_len
        n_blocks = (n_active * TOP_K + N_EXPERTS * (MOE_ROWS - 1) + MOE_ROWS - 1) // MOE_ROWS
        pos, blk_e, blk_valid, pad_start, blk_slot, blk_next = _route_plan(route, counts, n_blocks)
        xbuf = _dispatch(h2, pos, pad_start, blk_valid, tiles_per_batch, not need_ctx, batch)
        ybuf = _moe_experts(xbuf, blk_e, blk_valid, blk_slot, blk_next, moe_w1, moe_w3, moe_w2, l)
        combine_args = (x_new, route, ybuf, pos, g2, ln2_g[l].reshape(1, -1), ln2_b[l].reshape(1, -1))
        if need_ctx:
            pending = combine_args
        else:
            out = _combine(*combine_args, tiles_per_batch, alpha, True, batch).reshape(batch, t_len, dm)
    return out
```

```python
import functools
import math

import numpy as np
import jax
import jax.numpy as jnp
from jax import lax
from jax.experimental import pallas as pl
from jax.experimental.pallas import tpu as pltpu

F32 = jnp.float32
BF16 = jnp.bfloat16

GRID_W = 64
HEAD_DIM = 128
SWA_HEADS = 4
SWA_KV_HEADS = 2
SWA_BLOCK = 128
RET_HEADS = 4
RET_DK = 128
MLA_HEADS = 4
MLA_Q_RANK = 384
MLA_KV_RANK = 128
MLA_NOPE = 128
MLA_ROPE = 64
MLA_V = 128
HGRN_HEADS = 4
N_GROUPS = 4
EXPERTS_PER_GROUP = 8
N_EXPERTS = N_GROUPS * EXPERTS_PER_GROUP
TOP_K = 2
ROPE_BASE = 10000.0
EPS = 1e-6

LANES = 128
SUBLANES = 8
VMEM_LIMIT_BYTES = 56 * 1024 * 1024
PROJ_VMEM_LIMIT_BYTES = 60 * 1024 * 1024

CHUNK = 128
ROW_TILE = 256
MOE_ROWS = 256
MLA_KEY_CHUNK = 256
MLA_HEADS_PER_STEP = 4
WIDE = 512

SEG_A = 1024
SEG_B = 2048
SEG_C = 640
SEG_D = 2560


def _cparams(sem, vmem=VMEM_LIMIT_BYTES):
    return pltpu.CompilerParams(dimension_semantics=sem, vmem_limit_bytes=vmem)


def _ln_rows(x):
    mu = jnp.mean(x, axis=-1, keepdims=True)
    xc = x - mu
    var = jnp.mean(xc * xc, axis=-1, keepdims=True)
    return xc * lax.rsqrt(var + EPS)


def _silu(x):
    return x * (1.0 / (1.0 + jnp.exp(-x)))


def _dot(a, b):
    return jnp.dot(a, b, preferred_element_type=F32)


def _pack_bf16_pairs(x):
    k = x.shape[1] // 2
    hi = lax.bitcast_convert_type(x[:, :k].astype(BF16).astype(F32), jnp.uint32)
    lo = lax.bitcast_convert_type(x[:, k:].astype(BF16).astype(F32), jnp.uint32)
    return hi | (lo >> 16)


def _unpack_bf16_pairs(w):
    hi = lax.bitcast_convert_type(w & jnp.uint32(0xFFFF0000), F32)
    lo = lax.bitcast_convert_type(w << 16, F32)
    return jnp.concatenate([hi, lo], axis=1)


def _dot_nt(a, b):
    return lax.dot_general(a, b, (((1,), (1,)), ((), ())), preferred_element_type=F32)


def _dot_tn(a, b):
    return lax.dot_general(a, b, (((0,), (0,)), ((), ())), preferred_element_type=F32)


def _ada_kernel(c_ref, w_ref, b_ref, o_ref):
    s = _silu(c_ref[...]).astype(BF16)
    o_ref[0] = _dot(s, w_ref[0].astype(BF16)) + b_ref[0]


def _ada(cvec, w_ada, b_ada):
    depth, d, n6 = w_ada.shape
    tn = n6 // 8
    return pl.pallas_call(
        _ada_kernel,
        out_shape=jax.ShapeDtypeStruct((depth, SUBLANES, n6), F32),
        grid=(depth, n6 // tn),
        in_specs=[
            pl.BlockSpec((SUBLANES, d), lambda l, j: (0, 0)),
            pl.BlockSpec((1, d, tn), lambda l, j: (l, 0, j)),
            pl.BlockSpec((1, 1, tn), lambda l, j: (l, 0, j)),
        ],
        out_specs=pl.BlockSpec((1, SUBLANES, tn), lambda l, j: (l, 0, j)),
        compiler_params=_cparams(("arbitrary", "arbitrary")),
        name="ada_mod",
    )(cvec, w_ada, b_ada.reshape(depth, 1, n6))


SEG_WIDTHS = (SEG_A, SEG_B, SEG_C, SEG_D)
SEG_C_TRUE = MLA_Q_RANK + MLA_KV_RANK + MLA_ROPE
IN_COLS = SEG_A + SEG_B + SEG_C_TRUE + SEG_D
IN_COLS_PAD = sum(SEG_WIDTHS)


PREP_ROWS = 256
_C_END = SEG_A + SEG_B + SEG_C_TRUE
_C_PAD = SEG_C - SEG_C_TRUE
assert _C_END % PREP_ROWS + _C_PAD <= PREP_ROWS and IN_COLS_PAD % PREP_ROWS in (0, PREP_ROWS // 2)


def _w_in_prep_kernel(w_ref, o_ref, prev_ref):
    j = pl.program_id(0)
    jc = _C_END // PREP_ROWS
    rem = _C_END % PREP_ROWS
    keep = PREP_ROWS - _C_PAD
    cur = w_ref[0].astype(BF16)

    @pl.when(j < jc)
    def _():
        o_ref[...] = cur

    @pl.when(j == jc)
    def _():
        o_ref[0:rem, :] = cur[0:rem]
        o_ref[rem:rem + _C_PAD, :] = jnp.zeros((_C_PAD, cur.shape[1]), BF16)
        o_ref[rem + _C_PAD:PREP_ROWS, :] = cur[rem:keep]

    @pl.when(j > jc)
    def _():
        o_ref[0:_C_PAD, :] = prev_ref[keep:PREP_ROWS, :]
        o_ref[_C_PAD:PREP_ROWS, :] = cur[0:keep]

    prev_ref[...] = cur


def _w_in_prep(w_in, layer):
    _, d, cols = w_in.shape
    assert cols == IN_COLS
    w_t = jnp.swapaxes(w_in, 1, 2)
    return pl.pallas_call(
        _w_in_prep_kernel,
        out_shape=jax.ShapeDtypeStruct((IN_COLS_PAD, d), BF16),
        grid=(pl.cdiv(IN_COLS_PAD, PREP_ROWS),),
        in_specs=[pl.BlockSpec((1, PREP_ROWS, d), lambda j: (layer, j, 0))],
        out_specs=pl.BlockSpec((PREP_ROWS, d), lambda j: (j, 0)),
        scratch_shapes=[pltpu.VMEM((PREP_ROWS, d), BF16)],
        compiler_params=_cparams(("arbitrary",)),
        name="w_in_prep",
    )(w_t)


def _column_blocks(outs):
    return [(o_ref, j, min(WIDE, o_ref.shape[1] - j)) for o_ref in outs for j in range(0, o_ref.shape[1], WIDE)]


def _project_rows(x, sh_ref, sc_ref, w_ref, outs, between=None):
    y = _ln_rows(x)
    h = (y * (1.0 + sc_ref[0]) + sh_ref[0]).astype(BF16)
    base = 0
    for idx, (o_ref, j, cw) in enumerate(_column_blocks(outs)):
        o_ref[:, j:j + cw] = _dot_nt(h, w_ref[base:base + cw, :])
        base += cw
        if between is not None:
            between(idx)


def _proj_kernel(x_ref, ctx_ref, sh_ref, sc_ref, w_ref, *rest, tiles_per_batch):
    mla_in, (oa, ob, od, qf, kf, vt, so, pc_buf) = rest[:7], rest[7:]
    is_ctx = pl.program_id(0) % tiles_per_batch == 0
    x = jnp.where(is_ctx, ctx_ref[0], x_ref[0])
    so[...] = x
    outs = (oa, ob, pc_buf, od)
    c_done = max(k for k, blk in enumerate(_column_blocks(outs)) if blk[0] is pc_buf)

    def after_block(idx):
        if idx == c_done:
            _mla_up_rows(pc_buf[...], *mla_in, qf, kf, vt)

    _project_rows(x, sh_ref, sc_ref, w_ref, outs, between=after_block)


def _group_of_tile(i, tiles_per_batch):
    return jnp.where(i % tiles_per_batch == 0, 2, i // tiles_per_batch)


def _proj(x, ctx, shift, scale, w_pad, mla_args, tiles_per_batch):
    tm = ROW_TILE
    batch, t_len, d = x.shape
    n = batch * (t_len + ctx.shape[1])
    tpb = tiles_per_batch
    gmap = lambda i: (_group_of_tile(i, tiles_per_batch), 0, 0)
    row = lambda w: pl.BlockSpec((tm, w), lambda i: (i, 0))
    f32 = lambda w: jax.ShapeDtypeStruct((n, w), F32)
    mla_in, mla_out, mla_shapes = _mla_up_specs(n, mla_args)
    return pl.pallas_call(
        functools.partial(_proj_kernel, tiles_per_batch=tiles_per_batch),
        out_shape=[f32(SEG_A), f32(SEG_B), f32(SEG_D)] + mla_shapes + [f32(d)],
        grid=(n // tm,),
        in_specs=[
            pl.BlockSpec((1, tm, d), lambda i: (i // tpb, jnp.maximum(i % tpb - 1, 0), 0)),
            pl.BlockSpec((1, tm, d), lambda i: (i // tpb, 0, 0)),
            pl.BlockSpec((1, 1, d), gmap),
            pl.BlockSpec((1, 1, d), gmap),
            pl.BlockSpec(memory_space=pltpu.VMEM),
        ] + mla_in,
        out_specs=[row(SEG_A), row(SEG_B), row(SEG_D)] + mla_out + [row(d)],
        scratch_shapes=[pltpu.VMEM((tm, SEG_C), F32)],
        compiler_params=_cparams(("arbitrary",), vmem=PROJ_VMEM_LIMIT_BYTES),
        name="ln_mod_inproj",
    )(x, ctx, shift, scale, w_pad, *mla_args)


def _rope128(x, cos, sin):
    return x * cos + pltpu.roll(x, 64, 1) * sin


def _rope64(x, cos, sin):
    lane = lax.broadcasted_iota(jnp.int32, x.shape, 1)
    rot = jnp.where((lane % 64) < 32, pltpu.roll(x, 96, 1), pltpu.roll(x, 32, 1))
    return x * cos + rot * sin


def _rope_tables(t_len, l_len, batch):
    rows = t_len // GRID_W
    row = np.repeat(np.arange(rows), GRID_W).astype(np.float32)
    col = np.tile(np.arange(GRID_W), rows).astype(np.float32)

    def angles(rot_dim):
        n_freq = rot_dim // 4
        inv = (ROPE_BASE ** (-np.arange(n_freq, dtype=np.float32) / n_freq)).astype(np.float32)
        return np.concatenate([row[:, None] * inv, col[:, None] * inv], -1).astype(np.float32)

    a_swa = angles(HEAD_DIM)
    cos_swa = np.concatenate([np.cos(a_swa), np.cos(a_swa)], -1)
    sin_swa = np.concatenate([-np.sin(a_swa), np.sin(a_swa)], -1)
    a_mla = angles(MLA_ROPE)
    cos_m = np.concatenate([np.cos(a_mla), np.cos(a_mla), np.ones((t_len, 64), np.float32)], -1)
    sin_m = np.concatenate([-np.sin(a_mla), np.sin(a_mla), np.zeros((t_len, 64), np.float32)], -1)
    ones = np.ones((l_len, LANES), np.float32)
    zeros = np.zeros((l_len, LANES), np.float32)
    cos_rows = np.concatenate([np.concatenate([ones, cos_m], 0)] * batch, 0)
    sin_rows = np.concatenate([np.concatenate([zeros, sin_m], 0)] * batch, 0)
    return (jnp.asarray(cos_swa, F32), jnp.asarray(sin_swa, F32),
            jnp.asarray(cos_rows, F32), jnp.asarray(sin_rows, F32))


def _mla_up_rows(pc, qn_ref, kvn_ref, wq_ref, wk_ref, wv_ref, cos_ref, sin_ref, q_out, k_out, v_out):
    scale = float((MLA_NOPE + MLA_ROPE) ** -0.5 * math.log2(math.e))
    cq = pc[:, :MLA_Q_RANK]
    ckv = pc[:, MLA_Q_RANK:MLA_Q_RANK + MLA_KV_RANK]
    kr = pc[:, MLA_Q_RANK + MLA_KV_RANK:]
    cos = cos_ref[...]
    sin = sin_ref[...]

    def rms(x, g):
        return x * lax.rsqrt(jnp.mean(x * x, axis=-1, keepdims=True) + EPS) * g

    qh = _dot(rms(cq, qn_ref[...]).astype(BF16), wq_ref[...])
    ckn = rms(ckv, kvn_ref[...]).astype(BF16)
    kh = _dot(ckn, wk_ref[...])
    v_out[...] = _dot_nt(wv_ref[...], ckn).astype(BF16)
    kr_rot = _rope64(kr, cos, sin).astype(BF16)
    for h in range(MLA_HEADS):
        base = h * 2 * LANES
        q_out[:, base:base + LANES] = (qh[:, base:base + LANES] * scale).astype(BF16)
        q_out[:, base + LANES:base + 2 * LANES] = (
            _rope64(qh[:, base + LANES:base + 2 * LANES], cos, sin) * scale).astype(BF16)
        k_out[:, base:base + LANES] = kh[:, h * LANES:(h + 1) * LANES].astype(BF16)
        k_out[:, base + LANES:base + 2 * LANES] = kr_rot


def _mla_up_specs(n, mla_args):
    tm = ROW_TILE
    const2 = lambda i, *_: (0, 0)
    rows = lambda i, *_: (i, 0)
    in_specs = [pl.BlockSpec(a.shape, const2) for a in mla_args[:5]]
    in_specs += [pl.BlockSpec((tm, LANES), rows), pl.BlockSpec((tm, LANES), rows)]
    qk_w = MLA_HEADS * 2 * LANES
    out_specs = [pl.BlockSpec((tm, qk_w), rows), pl.BlockSpec((tm, qk_w), rows),
                 pl.BlockSpec((MLA_HEADS * MLA_V, tm), lambda i, *_: (0, i))]
    out_shapes = [jax.ShapeDtypeStruct((n, qk_w), BF16), jax.ShapeDtypeStruct((n, qk_w), BF16),
                  jax.ShapeDtypeStruct((MLA_HEADS * MLA_V, n), BF16)]
    return in_specs, out_specs, out_shapes


def _mla_attn_kernel(q_ref, k_ref, vt_ref, o_ref, s_ref, p_ref, *, kc, l_len, need_ctx):
    p_len = k_ref.shape[0]
    tq = q_ref.shape[0]
    dqk = 2 * LANES
    heads = q_ref.shape[1] // dqk
    half = p_len // 2
    is_ctx = pl.program_id(2) == 0

    @pl.when(is_ctx)
    def _():
        if not need_ctx:
            o_ref[...] = jnp.zeros_like(o_ref)
            return
        for h in range(heads):
            hq = slice(h * dqk, (h + 1) * dqk)
            hv = slice(h * MLA_V, (h + 1) * MLA_V)
            s = _dot_nt(k_ref[0:l_len, hq], q_ref[:, hq])
            p = jnp.exp2(s - jnp.max(s, axis=0, keepdims=True))
            acc = _dot(vt_ref[hv, 0:l_len], p.astype(BF16))
            o_ref[:, hv] = (acc / jnp.sum(p, axis=0, keepdims=True)).T

    @pl.when(jnp.logical_not(is_ctx))
    def _():
        fold = lambda a: a.reshape(kc // SUBLANES, SUBLANES, tq)
        m8s = []
        for h in range(heads):
            q = q_ref[:, h * dqk:(h + 1) * dqk]
            m8 = None
            for off in range(0, p_len, kc):
                s = _dot_nt(k_ref[off:off + kc, h * dqk:(h + 1) * dqk], q)
                s_ref[h, off:off + kc, :] = s
                cm = jnp.max(fold(s), axis=0)
                m8 = cm if m8 is None else jnp.maximum(m8, cm)
            m8s.append(m8)
        for h in range(heads):
            m = jnp.max(m8s[h], axis=0, keepdims=True)
            l8 = jnp.zeros((SUBLANES, tq), F32)
            for off in range(0, p_len, kc):
                p = jnp.exp2(s_ref[h, off:off + kc, :] - m)
                l8 = l8 + jnp.sum(fold(p), axis=0)
                p_ref[h, off:off + kc, :] = p.astype(BF16)
            l = jnp.sum(l8, axis=0, keepdims=True)
            hv = slice(h * MLA_V, (h + 1) * MLA_V)
            acc = (_dot(vt_ref[hv, 0:half], p_ref[h, 0:half, :])
                   + _dot(vt_ref[hv, half:p_len], p_ref[h, half:p_len, :]))
            o_ref[:, hv] = (acc / l).T


def _mla_attn(qf, kf, vt, batch, t_len, l_len, need_ctx):
    n = qf.shape[0]
    p_len = t_len + l_len
    tq = ROW_TILE
    assert l_len == tq
    kc = MLA_KEY_CHUNK
    hp = MLA_HEADS_PER_STEP
    tpb = p_len // tq
    return pl.pallas_call(
        functools.partial(_mla_attn_kernel, kc=kc, l_len=l_len, need_ctx=need_ctx),
        out_shape=jax.ShapeDtypeStruct((n, MLA_HEADS * MLA_V), F32),
        grid=(batch, MLA_HEADS // hp, tpb),
        in_specs=[
            pl.BlockSpec((tq, hp * 2 * LANES), lambda b, h, i: (b * tpb + i, h)),
            pl.BlockSpec((p_len, hp * 2 * LANES), lambda b, h, i: (b, h), pipeline_mode=pl.Buffered(1)),
            pl.BlockSpec((hp * MLA_V, p_len), lambda b, h, i: (h, b), pipeline_mode=pl.Buffered(1)),
        ],
        out_specs=pl.BlockSpec((tq, hp * MLA_V), lambda b, h, i: (b * tpb + i, h)),
        scratch_shapes=[pltpu.VMEM((hp, p_len, tq), F32), pltpu.VMEM((hp, p_len, tq), BF16)],
        compiler_params=_cparams(("arbitrary", "arbitrary", "arbitrary")),
        name="mla_attn",
    )(qf, kf, vt)


def _swa_kernel(sink_ref, q_ref, kp_ref, kc_ref, kn_ref, vp_ref, vc_ref, vn_ref, kx_ref, vx_ref,
                cq_ref, sq_ref, cp_ref, sp_ref, cn_ref, sn_ref, o_ref, *, nb, n_ctx, scale, need_ctx):
    j = pl.program_id(0)
    n = j - n_ctx
    blk = SWA_BLOCK
    batch = q_ref.shape[0]
    rr = lax.broadcasted_iota(jnp.int32, (2 * blk, 1), 0)
    mx = lambda a: jnp.max(a, axis=-1, keepdims=True)
    sm = lambda a: jnp.sum(a, axis=-1, keepdims=True)

    def heads_of(b, h):
        q0 = q_ref[b, :, 2 * h * LANES:(2 * h + 1) * LANES]
        q1 = q_ref[b, :, (2 * h + 1) * LANES:(2 * h + 2) * LANES]
        sk = jnp.where(rr < blk, sink_ref[2 * h], sink_ref[2 * h + 1]).astype(F32)
        return q0, q1, sk

    def store(b, h, o):
        o_ref[b, :, 2 * h * LANES:(2 * h + 1) * LANES] = o[:blk]
        o_ref[b, :, (2 * h + 1) * LANES:(2 * h + 2) * LANES] = o[blk:]

    @pl.when(j < n_ctx)
    def _():
        if not need_ctx:
            o_ref[...] = jnp.zeros_like(o_ref)
            return
        for b in range(batch):
            for h in range(SWA_KV_HEADS):
                ks = slice(h * LANES, (h + 1) * LANES)
                q0, q1, sk = heads_of(b, h)
                qq = jnp.concatenate([q0, q1], axis=0).astype(BF16)
                s_x = _dot_nt(qq, kx_ref[b, :, ks].astype(BF16)) * scale
                m = jnp.maximum(mx(s_x), sk)
                p_x = jnp.exp(s_x - m)
                den = sm(p_x) + jnp.exp(sk - m)
                store(b, h, _dot(p_x.astype(BF16), vx_ref[b, :, ks].astype(BF16)) / den)

    @pl.when(j >= n_ctx)
    def _():
        cq, sq = cq_ref[...], sq_ref[...]
        cp, sp = cp_ref[...], sp_ref[...]
        cn, sn = cn_ref[...], sn_ref[...]
        ri = lax.broadcasted_iota(jnp.int32, (2 * blk, blk), 0) % blk
        ci = lax.broadcasted_iota(jnp.int32, (2 * blk, blk), 1)
        mask_p = (ci >= ri) & (n > 0)
        mask_n = (ci <= ri) & (n < nb - 1)
        neg = jnp.float32(-jnp.inf)
        for b in range(batch):
            for h in range(SWA_KV_HEADS):
                ks = slice(h * LANES, (h + 1) * LANES)
                q0, q1, sk = heads_of(b, h)
                qq = jnp.concatenate([_rope128(q0, cq, sq), _rope128(q1, cq, sq)], axis=0).astype(BF16)
                kc = _rope128(kc_ref[b, :, ks], cq, sq).astype(BF16)
                kp = _rope128(kp_ref[b, :, ks], cp, sp).astype(BF16)
                kn = _rope128(kn_ref[b, :, ks], cn, sn).astype(BF16)
                s_c = _dot_nt(qq, kc) * scale
                s_p = jnp.where(mask_p, _dot_nt(qq, kp) * scale, neg)
                s_n = jnp.where(mask_n, _dot_nt(qq, kn) * scale, neg)
                s_x = _dot_nt(qq, kx_ref[b, :, ks].astype(BF16)) * scale
                m = jnp.maximum(jnp.maximum(mx(s_c), mx(s_p)), jnp.maximum(mx(s_n), mx(s_x)))
                m = jnp.maximum(m, sk)
                p_c, p_p, p_n, p_x = (jnp.exp(a - m) for a in (s_c, s_p, s_n, s_x))
                den = sm(p_c) + sm(p_p) + sm(p_n) + sm(p_x) + jnp.exp(sk - m)
                store(b, h, (_dot(p_c.astype(BF16), vc_ref[b, :, ks].astype(BF16))
                             + _dot(p_p.astype(BF16), vp_ref[b, :, ks].astype(BF16))
                             + _dot(p_n.astype(BF16), vn_ref[b, :, ks].astype(BF16))
                             + _dot(p_x.astype(BF16), vx_ref[b, :, ks].astype(BF16))) / den)


def _swa_part(pa, sink, cos_t, sin_t, batch, t_len, l_len, need_ctx):
    p_len = t_len + l_len
    blk = SWA_BLOCK
    nb = t_len // blk
    n_ctx = l_len // blk
    scale = float(HEAD_DIM ** -0.5)
    kv_w = SWA_KV_HEADS * HEAD_DIM
    q_w = SWA_HEADS * HEAD_DIM
    kcol, vcol = q_w // kv_w, q_w // kv_w + 1
    pa3 = pa.reshape(batch, p_len, pa.shape[1])
    lat = lambda j: jnp.maximum(j - n_ctx, 0)
    same = lambda j: lat(j)
    prv = lambda j: jnp.maximum(lat(j) - 1, 0)
    nxt = lambda j: jnp.minimum(lat(j) + 1, nb - 1)
    kv = lambda col, rowf: pl.BlockSpec((batch, blk, kv_w), lambda j: (0, n_ctx + rowf(j), col))
    tab = lambda rowf: pl.BlockSpec((blk, LANES), lambda j: (rowf(j), 0))
    in_specs = [
        pl.BlockSpec(memory_space=pltpu.SMEM),
        pl.BlockSpec((batch, blk, q_w), lambda j: (0, j, 0)),
        kv(kcol, prv), kv(kcol, same), kv(kcol, nxt),
        kv(vcol, prv), kv(vcol, same), kv(vcol, nxt),
        pl.BlockSpec((batch, l_len, kv_w), lambda j: (0, 0, kcol)),
        pl.BlockSpec((batch, l_len, kv_w), lambda j: (0, 0, vcol)),
        tab(same), tab(same), tab(prv), tab(prv), tab(nxt), tab(nxt),
    ]
    return dict(
        kernel=functools.partial(_swa_kernel, nb=nb, n_ctx=n_ctx, scale=scale, need_ctx=need_ctx),
        in_specs=in_specs,
        args=[sink] + [pa3] * 9 + [cos_t, sin_t] * 3,
        out_specs=[pl.BlockSpec((batch, blk, q_w), lambda j: (0, j, 0))],
        out_shapes=[jax.ShapeDtypeStruct((batch, p_len, q_w), F32)],
        scratch=[],
        steps=n_ctx + nb)


def _chunk_index(d, c, n_l, n_t):
    bwd = jnp.where(c < n_l, n_l - 1 - c, n_l + (n_t - 1) - (c - n_l))
    return jnp.where(d == 0, c, bwd)


def _flip_iotas(d):
    row = lax.broadcasted_iota(jnp.int32, (CHUNK, CHUNK), 0)
    col = lax.broadcasted_iota(jnp.int32, (CHUNK, CHUNK), 1)
    rf = jnp.where(d == 0, row, CHUNK - 1 - row)
    cf = jnp.where(d == 0, col, CHUNK - 1 - col)
    return rf, cf


def _ret_kernel(s_ref, qf_ref, kf_ref, vf_ref, qb_ref, kb_ref, vb_ref, of_ref, ob_ref, st_ref):
    c = pl.program_id(0)
    batch = qf_ref.shape[0]

    @pl.when(c == 0)
    def _():
        st_ref[...] = jnp.zeros_like(st_ref)

    ks = float(RET_DK ** -0.5)
    dirs = ((qf_ref, kf_ref, vf_ref, of_ref), (qb_ref, kb_ref, vb_ref, ob_ref))
    for d, (q_ref, k_ref, v_ref, o_ref) in enumerate(dirs):
        rf, cf = _flip_iotas(d)
        rff = rf.astype(F32)
        dn = (rf - cf).astype(F32)
        for h in range(RET_HEADS):
            hs = slice(h * LANES, (h + 1) * LANES)
            sv = jnp.full((CHUNK, CHUNK), s_ref[d * RET_HEADS + h], F32)
            lg = jnp.log1p(-jnp.exp2(-sv))
            dec = jnp.where(dn >= 0, jnp.exp(dn * lg), 0.0)
            eq = jnp.exp((rff + 1.0) * lg)
            ek = jnp.exp((CHUNK - 1.0 - rff) * lg)
            a_chunk = jnp.exp(float(CHUNK) * lg)
            for b in range(batch):
                q = q_ref[b, :, hs]
                k = k_ref[b, :, hs] * ks
                v = v_ref[b, :, hs].astype(BF16)
                st = st_ref[d, b, h]
                a = _dot_nt(q.astype(BF16), k.astype(BF16)) * dec
                o = _dot(a.astype(BF16), v) + _dot_nt((q * eq).astype(BF16), st.astype(BF16))
                o_ref[b, :, hs] = o
                st_ref[d, b, h] = a_chunk * st + _dot_tn(v, (k * ek).astype(BF16))


def _hgrn_level_mats():
    c = CHUNK
    t = np.arange(c)[:, None]
    u = np.arange(c)[None, :]
    mats = [u <= t, u > t]
    m = c // 2
    while m >= 1:
        mid = (t // (2 * m)) * 2 * m + m
        second = (t % (2 * m)) >= m
        qrole = (u >= mid) & (u <= t)
        krole = (u > t) & (u <= mid - 1)
        mats.append(np.where(second, qrole, krole))
        m //= 2
    fwd = np.concatenate(mats, 0).astype(np.float32)
    bwd = np.concatenate([mm[::-1, ::-1] for mm in mats], 0).astype(np.float32)
    return np.stack([fwd, bwd], 0)


N_LEVELS = int(math.log2(CHUNK))


def _hgrn_kernel(m_ref, lb_ref, qf_ref, ff_ref, vf_ref, qb_ref, fb_ref, vb_ref, of_ref, ob_ref, st_ref):
    c = pl.program_id(0)
    batch = qf_ref.shape[0]

    @pl.when(c == 0)
    def _():
        st_ref[...] = jnp.zeros_like(st_ref)

    dirs = ((qf_ref, ff_ref, vf_ref, of_ref), (qb_ref, fb_ref, vb_ref, ob_ref))
    for d, (q_ref, f_ref, v_ref, o_ref) in enumerate(dirs):
        rf, cf = _flip_iotas(d)
        mst = m_ref[d]
        last = CHUNK - 1 if d == 0 else 0
        lvl_masks = []
        for lvl in range(N_LEVELS):
            m = CHUNK >> (lvl + 1)
            sh = N_LEVELS - lvl
            lvl_masks.append(((rf >> sh) == (cf >> sh)) & ((rf & m) != 0) & ((cf & m) == 0))
        diag = rf == cf
        for h in range(HGRN_HEADS):
            hs = slice(h * LANES, (h + 1) * LANES)
            lb = lb_ref[d, :, hs]
            for b in range(batch):
                f = lb + (1.0 - lb) * (1.0 / (1.0 + jnp.exp(-f_ref[b, :, hs])))
                k = 1.0 - f
                g = jnp.log(f)
                q = _silu(q_ref[b, :, hs])
                v = v_ref[b, :, hs].astype(BF16)
                e = jnp.exp(_dot(mst, g.astype(BF16)))
                e_q = e[0:CHUNK]
                q_in = q * e_q
                k_st = k * e[CHUNK:2 * CHUNK]
                scores = jnp.where(diag, _dot_nt(q.astype(BF16), k.astype(BF16)), 0.0)
                for lvl in range(N_LEVELS):
                    el = e[(2 + lvl) * CHUNK:(3 + lvl) * CHUNK]
                    sl = _dot_nt((q * el).astype(BF16), (k * el).astype(BF16))
                    scores = scores + jnp.where(lvl_masks[lvl], sl, 0.0)
                st = st_ref[d, b, h]
                o = _dot(scores.astype(BF16), v) + _dot_nt(q_in.astype(BF16), st.astype(BF16))
                o_ref[b, :, hs] = o
                st_ref[d, b, h] = st * e_q[last:last + 1, :] + _dot_tn(v, k_st.astype(BF16))


def _scan_specs(batch, t_len, l_len):
    n_l, n_t = l_len // CHUNK, t_len // CHUNK

    def spec(d, col):
        return pl.BlockSpec((batch, CHUNK, WIDE), lambda c: (0, _chunk_index(d, c, n_l, n_t), col))

    return spec, n_l + n_t


def _retention_part(pb, ret_s, batch, t_len, l_len):
    p_len = t_len + l_len
    pb3 = pb.reshape(batch, p_len, pb.shape[1])
    spec, n_chunks = _scan_specs(batch, t_len, l_len)
    return dict(
        kernel=_ret_kernel,
        in_specs=[pl.BlockSpec(memory_space=pltpu.SMEM)]
                 + [spec(0, col) for col in (0, 1, 2)] + [spec(1, col) for col in (0, 1, 2)],
        args=[ret_s.reshape(-1)] + [pb3] * 6,
        out_specs=[spec(0, 0), spec(1, 0)],
        out_shapes=[jax.ShapeDtypeStruct((batch, p_len, WIDE), F32)] * 2,
        scratch=[pltpu.VMEM((2, batch, RET_HEADS, CHUNK, CHUNK), F32)],
        steps=n_chunks)


def _hgrn_part(pd, lower_bounds, mats, batch, t_len, l_len):
    p_len = t_len + l_len
    pd3 = pd.reshape(batch, p_len, pd.shape[1])
    lb3 = lower_bounds.reshape(2, 1, WIDE)
    spec, n_chunks = _scan_specs(batch, t_len, l_len)
    return dict(
        kernel=_hgrn_kernel,
        in_specs=[pl.BlockSpec(mats.shape, lambda c: (0, 0, 0)),
                  pl.BlockSpec(lb3.shape, lambda c: (0, 0, 0)),
                  spec(0, 0), spec(0, 1), spec(0, 3), spec(1, 0), spec(1, 2), spec(1, 3)],
        args=[mats, lb3] + [pd3] * 6,
        out_specs=[spec(0, 0), spec(1, 0)],
        out_shapes=[jax.ShapeDtypeStruct((batch, p_len, WIDE), F32)] * 2,
        scratch=[pltpu.VMEM((2, batch, HGRN_HEADS, CHUNK, CHUNK), F32)],
        steps=n_chunks)


def _mixers(parts, n):
    steps = parts[0]["steps"]
    assert all(p["steps"] == steps for p in parts)
    n_in = [len(p["args"]) for p in parts]
    n_out = [len(p["out_shapes"]) for p in parts]
    n_scr = [len(p["scratch"]) for p in parts]

    def kern(*refs):
        ins, outs, scr = refs[:sum(n_in)], refs[sum(n_in):sum(n_in) + sum(n_out)], refs[sum(n_in) + sum(n_out):]
        a = b = c = 0
        for p, ni, no, ns in zip(parts, n_in, n_out, n_scr):
            p["kernel"](*ins[a:a + ni], *outs[b:b + no], *scr[c:c + ns])
            a, b, c = a + ni, b + no, c + ns

    outs = pl.pallas_call(
        kern,
        out_shape=[s for p in parts for s in p["out_shapes"]],
        grid=(steps,),
        in_specs=[s for p in parts for s in p["in_specs"]],
        out_specs=[s for p in parts for s in p["out_specs"]],
        scratch_shapes=[s for p in parts for s in p["scratch"]],
        compiler_params=_cparams(("arbitrary",)),
        name="swa_retention_hgrn",
    )(*[a for p in parts for a in p["args"]])
    return [o.reshape(n, o.shape[-1]) for o in outs]


R_E1, R_E2, R_G1, R_G2, R_S1, R_S2 = 0, 1, 2, 3, 4, 5


def _route(logits, count_ref, active):
    tm = logits.shape[0]
    lane = lax.broadcasted_iota(jnp.int32, logits.shape, 1)
    lanef = lane.astype(F32)
    big = jnp.float32(1e9)
    neg = jnp.float32(-jnp.inf)
    mx = lambda a: jnp.max(a, axis=-1, keepdims=True)
    mn = lambda a: jnp.min(a, axis=-1, keepdims=True)
    sm = lambda a: jnp.sum(a, axis=-1, keepdims=True)
    gl = jnp.where(lane < N_GROUPS, logits, neg)
    gm = mx(gl)
    p_grp = 1.0 / sm(jnp.exp(gl - gm))
    grp = mn(jnp.where(gl == gm, lanef, big))
    lo = N_GROUPS + grp * EXPERTS_PER_GROUP
    ing = (lanef >= lo) & (lanef < lo + EXPERTS_PER_GROUP)
    el = jnp.where(ing, logits, neg)
    l1 = mx(el)
    i1 = mn(jnp.where(el == l1, lanef, big))
    el2 = jnp.where(lanef == i1, neg, el)
    l2 = mx(el2)
    i2 = mn(jnp.where(el2 == l2, lanef, big))
    r = jnp.exp(l2 - l1)
    g1 = p_grp / (1.0 + r)
    g2 = p_grp * r / (1.0 + r)
    e1 = i1 - N_GROUPS
    e2 = i2 - N_GROUPS
    oh1 = lanef == e1
    oh2 = lanef == e2
    both = jnp.where(oh1 | oh2, active, 0.0)
    ri = lax.broadcasted_iota(jnp.int32, (tm, tm), 0)
    ci = lax.broadcasted_iota(jnp.int32, (tm, tm), 1)
    earlier = jnp.where(ci < ri, 1.0, 0.0).astype(BF16)
    before = _dot(earlier, both.astype(BF16)) + count_ref[0:1, :]
    s1 = sm(jnp.where(oh1, before, 0.0))
    s2 = sm(jnp.where(oh2, before, 0.0))
    count_ref[...] = count_ref[...] + jnp.sum(both, axis=0, keepdims=True)
    out = jnp.zeros(logits.shape, F32)
    for ln, val in ((R_E1, e1), (R_E2, e2), (R_G1, g1), (R_G2, g2), (R_S1, s1), (R_S2, s2)):
        out = jnp.where(lane == ln, val, out)
    return out


def _out_kernel(x_ref, a_ref, bf_ref, bb_ref, c_ref, df_ref, db_ref, rg_ref, hg_ref, g1_ref, sh2_ref, sc2_ref,
                lng_ref, lnb_ref, rgn_ref, hgn_ref, wo_ref, wr_ref,
                xo_ref, h2_ref, r_ref, cnt_ref, *, alpha, tiles_per_batch, route_ctx):
    i = pl.program_id(0)

    @pl.when(i == 0)
    def _():
        cnt_ref[...] = jnp.zeros_like(cnt_ref)

    bsum = bf_ref[...] + bb_ref[...]
    dsum = df_ref[...] + db_ref[...]
    rgn = rgn_ref[...]
    hgn = hgn_ref[...]
    parts_b, parts_d = [], []
    for h in range(4):
        hs = slice(h * LANES, (h + 1) * LANES)
        parts_b.append(_ln_rows(bsum[:, hs]))
        dh = dsum[:, hs]
        parts_d.append(dh * lax.rsqrt(jnp.mean(dh * dh, axis=-1, keepdims=True) + EPS))
    bo = _silu(rg_ref[...]) * (jnp.concatenate(parts_b, axis=1) * rgn)
    do = _silu(hg_ref[...]) * (jnp.concatenate(parts_d, axis=1) * hgn)
    o = (_dot(a_ref[...].astype(BF16), wo_ref[0:WIDE, :])
         + _dot(c_ref[...].astype(BF16), wo_ref[2 * WIDE:3 * WIDE, :])
         + _dot(bo.astype(BF16), wo_ref[WIDE:2 * WIDE, :])
         + _dot(do.astype(BF16), wo_ref[3 * WIDE:4 * WIDE, :]))
    y = alpha * x_ref[...] + g1_ref[0] * o
    xn = _ln_rows(y) * lng_ref[...] + lnb_ref[...]
    xo_ref[...] = xn
    h2 = _ln_rows(xn) * (1.0 + sc2_ref[0]) + sh2_ref[0]
    h2_ref[...] = _pack_bf16_pairs(h2)
    if route_ctx:
        active = jnp.float32(1.0)
    else:
        active = jnp.where(i % tiles_per_batch == 0, 0.0, 1.0).astype(F32)
    r_ref[...] = _route(_dot(h2.astype(BF16), wr_ref[...]), cnt_ref, active)


def _out_proj(stream, a, b_f, b_b, c, d_f, d_b, pb, pd, g1, sh2, sc2, ln_g, ln_b, ret_gn, hgrn_gn, w_out, w_r,
              tiles_per_batch, alpha, route_ctx):
    n, dm = stream.shape
    tm = ROW_TILE
    gmap = lambda i: (_group_of_tile(i, tiles_per_batch), 0, 0)
    row = lambda w: pl.BlockSpec((tm, w), lambda i: (i, 0))
    const = lambda w: pl.BlockSpec((1, w), lambda i: (0, 0))
    resident = pl.BlockSpec(memory_space=pltpu.VMEM)
    return pl.pallas_call(
        functools.partial(_out_kernel, alpha=alpha, tiles_per_batch=tiles_per_batch, route_ctx=route_ctx),
        out_shape=[jax.ShapeDtypeStruct((n, dm), F32), jax.ShapeDtypeStruct((n, dm // 2), jnp.uint32),
                   jax.ShapeDtypeStruct((n, LANES), F32), jax.ShapeDtypeStruct((SUBLANES, LANES), F32)],
        grid=(n // tm,),
        in_specs=[row(dm), row(WIDE), row(WIDE), row(WIDE), row(WIDE), row(WIDE), row(WIDE),
                  pl.BlockSpec((tm, WIDE), lambda i: (i, 3)),
                  pl.BlockSpec((tm, WIDE), lambda i: (i, 4)),
                  pl.BlockSpec((1, 1, dm), gmap), pl.BlockSpec((1, 1, dm), gmap),
                  pl.BlockSpec((1, 1, dm), gmap),
                  const(dm), const(dm), const(WIDE), const(WIDE), resident, resident],
        out_specs=[row(dm), row(dm // 2), row(LANES),
                   pl.BlockSpec((SUBLANES, LANES), lambda i: (0, 0))],
        compiler_params=_cparams(("arbitrary",)),
        name="mix_outproj_norm_route",
    )(stream, a, b_f, b_b, c, d_f, d_b, pb, pd, g1, sh2, sc2, ln_g, ln_b, ret_gn, hgrn_gn, w_out, w_r)


def _route_plan(route, counts_f, n_blocks):
    rows = MOE_ROWS
    counts = counts_f[0, :N_EXPERTS].astype(jnp.int32)
    padded = (counts + rows - 1) // rows * rows
    pend = jnp.cumsum(padded)
    pstart = pend - padded
    eid = route[:, R_E1:R_E2 + 1].astype(jnp.int32)
    rank = route[:, R_S1:R_S2 + 1].astype(jnp.int32)
    onehot = eid[:, :, None] == jnp.arange(N_EXPERTS, dtype=jnp.int32)
    pos = jnp.sum(jnp.where(onehot, pstart, 0), axis=-1) + rank
    blk_start = jnp.arange(n_blocks, dtype=jnp.int32) * rows
    blk_e = jnp.minimum(jnp.sum(blk_start[:, None] >= pend[None, :], axis=-1), N_EXPERTS - 1)
    blk_valid = (blk_start < pend[-1]).astype(jnp.int32)
    pad_start = pstart + counts
    ids = jnp.arange(N_EXPERTS, dtype=jnp.int32)
    nonempty = counts > 0
    seq = jnp.cumsum(nonempty.astype(jnp.int32)) - nonempty.astype(jnp.int32)
    cand = jnp.where(nonempty, ids, N_EXPERTS)
    later = jnp.where(ids[None, :] > ids[:, None], cand[None, :], N_EXPERTS)
    nxt = jnp.min(later, axis=-1)
    nxt = jnp.where(nxt >= N_EXPERTS, -1, nxt)
    blk_oh = blk_e[:, None] == ids[None, :]
    blk_slot = jnp.sum(jnp.where(blk_oh, seq % 2, 0), axis=-1)
    blk_next = jnp.sum(jnp.where(blk_oh, nxt, 0), axis=-1)
    i32 = lambda a: a.astype(jnp.int32)
    return (i32(pos.reshape(-1)), i32(blk_e), blk_valid, i32(pad_start), i32(blk_slot), i32(blk_next))


def _dispatch_kernel(pos_ref, pad_ref, bv_ref, h_ref, x_hbm, zbuf, sem, *, tile_of_step, n_blocks):
    i = pl.program_id(0)
    tm = ROW_TILE
    rows = MOE_ROWS

    def zero_copy(e):
        start = pl.multiple_of(pad_ref[e] & ~(SUBLANES - 1), SUBLANES)
        return pltpu.make_async_copy(zbuf, x_hbm.at[pl.ds(start, rows + SUBLANES)], sem.at[1])

    def unused_block_copy(j):
        start = pl.multiple_of(j * rows, rows)
        return pltpu.make_async_copy(zbuf.at[pl.ds(0, rows)], x_hbm.at[pl.ds(start, rows)], sem.at[1])

    @pl.when(i == 0)
    def _():
        zbuf[...] = jnp.zeros_like(zbuf)

        def fill(j, carry):
            @pl.when(bv_ref[j] == 0)
            def _():
                unused_block_copy(j).start()
                unused_block_copy(j).wait()
            return carry

        lax.fori_loop(0, n_blocks + 2, fill, 0)
        for e in range(N_EXPERTS):
            zero_copy(e).start()
        for e in range(N_EXPERTS):
            zero_copy(e).wait()

    def row_copy(r, k):
        p = pos_ref[(tile_of_step(i) * tm + r) * TOP_K + k]
        return pltpu.make_async_copy(h_ref.at[r], x_hbm.at[p], sem.at[0])

    for r in range(tm):
        for k in range(TOP_K):
            row_copy(r, k).start(priority=k)
    for r in range(tm):
        for k in range(TOP_K):
            row_copy(r, k).wait()


def _tile_schedule(n, tiles_per_batch, latent_only, batch):
    if latent_only:
        lat = tiles_per_batch - 1
        return batch * lat, (lambda s: (s // lat) * tiles_per_batch + 1 + s % lat)
    return n // ROW_TILE, (lambda s: s)


def _dispatch(h2, pos, pad_start, blk_valid, tiles_per_batch, latent_only, batch):
    n, dm = h2.shape
    tm = ROW_TILE
    n_blocks = blk_valid.shape[0]
    blk_valid = jnp.concatenate([blk_valid, jnp.zeros((2,), jnp.int32)])
    n_tiles, tile_of_step = _tile_schedule(n, tiles_per_batch, latent_only, batch)
    grid_spec = pltpu.PrefetchScalarGridSpec(
        num_scalar_prefetch=3,
        grid=(n_tiles,),
        in_specs=[pl.BlockSpec((tm, dm), lambda i, p, z, v: (tile_of_step(i), 0))],
        out_specs=pl.BlockSpec(memory_space=pl.ANY),
        scratch_shapes=[pltpu.VMEM((MOE_ROWS + SUBLANES, dm), h2.dtype), pltpu.SemaphoreType.DMA((2,))],
    )
    return pl.pallas_call(
        functools.partial(_dispatch_kernel, tile_of_step=tile_of_step, n_blocks=n_blocks),
        out_shape=jax.ShapeDtypeStruct(((n_blocks + 2) * MOE_ROWS, dm), h2.dtype),
        grid_spec=grid_spec,
        compiler_params=_cparams(("arbitrary",)),
        name="moe_dispatch",
    )(pos, pad_start, blk_valid, h2)


def _moe_kernel(be_ref, bv_ref, slot_ref, next_ref, x_ref, w1_hbm, w3_hbm, w2_hbm, y_ref,
                s1, s3, s2, w1b, w3b, w2b, sem, *, layer):
    j = pl.program_id(0)
    valid = bv_ref[j] == 1
    first = valid & ((j == 0) | (be_ref[j] != be_ref[jnp.maximum(j - 1, 0)]))

    def weight_copies(e, slot):
        return (pltpu.make_async_copy(w1_hbm.at[layer, e], s1.at[slot], sem.at[slot]),
                pltpu.make_async_copy(w3_hbm.at[layer, e], s3.at[slot], sem.at[slot]),
                pltpu.make_async_copy(w2_hbm.at[layer, e], s2.at[slot], sem.at[slot]))

    @pl.when(valid & (j == 0))
    def _():
        for cp in weight_copies(be_ref[0], slot_ref[0]):
            cp.start()

    @pl.when(first)
    def _():
        slot = slot_ref[j]
        for cp in weight_copies(be_ref[j], slot):
            cp.wait()
        nxt = next_ref[j]

        @pl.when(nxt >= 0)
        def _():
            for cp in weight_copies(nxt, 1 - slot):
                cp.start()

        w1b[...] = s1[slot].astype(BF16)
        w3b[...] = s3[slot].astype(BF16)
        w2b[...] = s2[slot].astype(BF16)

    @pl.when(valid)
    def _():
        x = _unpack_bf16_pairs(x_ref[...]).astype(BF16)
        hmid = _silu(_dot(x, w1b[...])) * _dot(x, w3b[...])
        y_ref[...] = _pack_bf16_pairs(_dot(hmid.astype(BF16), w2b[...]))

    @pl.when(bv_ref[j] == 0)
    def _():
        y_ref[...] = jnp.zeros_like(y_ref)


def _moe_experts(xbuf, blk_e, blk_valid, blk_slot, blk_next, w1, w3, w2, layer):
    dm = w1.shape[2]
    ff = w1.shape[-1]
    rows = MOE_ROWS
    n_blocks = blk_e.shape[0]
    hbm = pl.BlockSpec(memory_space=pl.ANY)
    grid_spec = pltpu.PrefetchScalarGridSpec(
        num_scalar_prefetch=4,
        grid=(n_blocks,),
        in_specs=[
            pl.BlockSpec((rows, dm // 2), lambda j, be, bv, sl, nx: (j * bv[j], 0)),
            hbm, hbm, hbm,
        ],
        out_specs=pl.BlockSpec((rows, dm // 2), lambda j, be, bv, sl, nx: (j, 0)),
        scratch_shapes=[
            pltpu.VMEM((2, dm, ff), F32), pltpu.VMEM((2, dm, ff), F32), pltpu.VMEM((2, ff, dm), F32),
            pltpu.VMEM((dm, ff), BF16), pltpu.VMEM((dm, ff), BF16), pltpu.VMEM((ff, dm), BF16),
            pltpu.SemaphoreType.DMA((2,)),
        ],
    )
    return pl.pallas_call(
        functools.partial(_moe_kernel, layer=layer),
        out_shape=jax.ShapeDtypeStruct((n_blocks * rows, dm // 2), jnp.uint32),
        grid_spec=grid_spec,
        compiler_params=_cparams(("arbitrary",)),
        name="moe_experts",
    )(blk_e, blk_valid, blk_slot, blk_next, xbuf, w1, w3, w2)


def _expert_row_gather(pos_ref, y_hbm, ybuf, sem, tile_of_step):
    tm = ROW_TILE

    def row_copy(step, sl, r, k):
        p = pos_ref[(tile_of_step(step) * tm + r) * TOP_K + k]
        return pltpu.make_async_copy(y_hbm.at[p], ybuf.at[sl, k, r], sem.at[sl])

    def start_rows(step, sl, r0, r1):
        for r in range(r0, r1):
            for k in range(TOP_K):
                row_copy(step, sl, r, k).start(priority=k)

    def wait_rows(step, sl):
        for r in range(tm):
            for k in range(TOP_K):
                row_copy(step, sl, r, k).wait()

    return start_rows, wait_rows


def _gate_weighted(ybuf, slot, r_ref):
    route = r_ref[...]
    return (_unpack_bf16_pairs(ybuf[slot, 0]) * route[:, R_G1:R_G1 + 1]
            + _unpack_bf16_pairs(ybuf[slot, 1]) * route[:, R_G2:R_G2 + 1])


def _gathered_expert_rows(pos_ref, r_ref, y_hbm, ybuf, sem, *, n_tiles, tile_of_step):
    i = pl.program_id(0)
    slot = i % 2
    start_rows, wait_rows = _expert_row_gather(pos_ref, y_hbm, ybuf, sem, tile_of_step)

    @pl.when(i == 0)
    def _():
        start_rows(0, 0, 0, ROW_TILE)

    @pl.when(i + 1 < n_tiles)
    def _():
        start_rows(jnp.minimum(i + 1, n_tiles - 1), 1 - slot, 0, ROW_TILE)

    wait_rows(i, slot)
    return _gate_weighted(ybuf, slot, r_ref)


def _combine_kernel(pos_ref, x_ref, r_ref, g2_ref, lng_ref, lnb_ref, y_hbm, o_ref, ybuf, sem,
                    *, n_tiles, tile_of_step, alpha):
    y = _gathered_expert_rows(pos_ref, r_ref, y_hbm, ybuf, sem, n_tiles=n_tiles, tile_of_step=tile_of_step)
    z = alpha * x_ref[...] + g2_ref[0] * y
    o_ref[...] = _ln_rows(z) * lng_ref[...] + lnb_ref[...]


def _combine(x_new, route, ybuf, pos, g2, ln_g, ln_b, tiles_per_batch, alpha, latent_only, batch):
    n, dm = x_new.shape
    tm = ROW_TILE
    n_tiles, tile_of_step = _tile_schedule(n, tiles_per_batch, latent_only, batch)
    gmap = lambda i, p: (_group_of_tile(tile_of_step(i), tiles_per_batch), 0, 0)
    grid_spec = pltpu.PrefetchScalarGridSpec(
        num_scalar_prefetch=1,
        grid=(n_tiles,),
        in_specs=[
            pl.BlockSpec((tm, dm), lambda i, p: (tile_of_step(i), 0)),
            pl.BlockSpec((tm, LANES), lambda i, p: (tile_of_step(i), 0)),
            pl.BlockSpec((1, 1, dm), gmap),
            pl.BlockSpec((1, dm), lambda i, p: (0, 0)),
            pl.BlockSpec((1, dm), lambda i, p: (0, 0)),
            pl.BlockSpec(memory_space=pl.ANY),
        ],
        out_specs=pl.BlockSpec((tm, dm), lambda i, p: (i, 0)),
        scratch_shapes=[pltpu.VMEM((2, TOP_K, tm, dm // 2), jnp.uint32), pltpu.SemaphoreType.DMA((2,))],
    )
    return pl.pallas_call(
        functools.partial(_combine_kernel, n_tiles=n_tiles, tile_of_step=tile_of_step, alpha=alpha),
        out_shape=jax.ShapeDtypeStruct((n_tiles * tm, dm), F32),
        grid_spec=grid_spec,
        compiler_params=_cparams(("arbitrary",)),
        name="moe_combine_norm",
    )(pos, x_new, route, g2, ln_g, ln_b, ybuf)


def _combine_proj_kernel(pos_ref, x_ref, r_ref, g2_ref, lng_ref, lnb_ref, y_hbm, sh_ref, sc_ref, w_ref,
                         *rest, n_tiles, alpha):
    mla_in, (oa, ob, od, qf, kf, vt, so, ybuf, sem, oc) = rest[:7], rest[7:]
    i = pl.program_id(0)
    slot = i % 2
    tm = ROW_TILE
    start_rows, wait_rows = _expert_row_gather(pos_ref, y_hbm, ybuf, sem, lambda s: s)

    @pl.when(i == 0)
    def _():
        start_rows(0, 0, 0, tm)

    wait_rows(i, slot)
    z = alpha * x_ref[...] + g2_ref[0] * _gate_weighted(ybuf, slot, r_ref)
    stream = _ln_rows(z) * lng_ref[...] + lnb_ref[...]
    so[...] = stream
    outs = (oa, ob, oc, od)
    n_blk = len(_column_blocks(outs))
    per = -(-tm // n_blk)
    nxt = jnp.minimum(i + 1, n_tiles - 1)

    c_done = max(k for k, blk in enumerate(_column_blocks(outs)) if blk[0] is oc)

    def after_block(idx):
        start_rows(nxt, 1 - slot, min(idx * per, tm), min((idx + 1) * per, tm))
        if idx == c_done:
            _mla_up_rows(oc[...], *mla_in, qf, kf, vt)

    _project_rows(stream, sh_ref, sc_ref, w_ref, outs, between=after_block)

    @pl.when(i == n_tiles - 1)
    def _():
        wait_rows(nxt, 1 - slot)


def _combine_proj(x_new, route, ybuf, pos, g2, ln_g, ln_b, shift, scale, w_pad, mla_args,
                  tiles_per_batch, alpha):
    n, dm = x_new.shape
    tm = ROW_TILE
    n_tiles = n // tm
    gmap = lambda i, p: (_group_of_tile(i, tiles_per_batch), 0, 0)
    row = lambda w: pl.BlockSpec((tm, w), lambda i, p: (i, 0))
    f32 = lambda w: jax.ShapeDtypeStruct((n, w), F32)
    const = pl.BlockSpec((1, dm), lambda i, p: (0, 0))
    mod = pl.BlockSpec((1, 1, dm), gmap)
    mla_in, mla_out, mla_shapes = _mla_up_specs(n, mla_args)
    grid_spec = pltpu.PrefetchScalarGridSpec(
        num_scalar_prefetch=1,
        grid=(n_tiles,),
        in_specs=[row(dm), row(LANES), mod, const, const, pl.BlockSpec(memory_space=pl.ANY),
                  mod, mod, pl.BlockSpec(memory_space=pltpu.VMEM)] + mla_in,
        out_specs=[row(SEG_A), row(SEG_B), row(SEG_D)] + mla_out + [row(dm)],
        scratch_shapes=[pltpu.VMEM((2, TOP_K, tm, dm // 2), jnp.uint32), pltpu.SemaphoreType.DMA((2,)),
                        pltpu.VMEM((tm, SEG_C), F32)],
    )
    return pl.pallas_call(
        functools.partial(_combine_proj_kernel, n_tiles=n_tiles, alpha=alpha),
        out_shape=[f32(SEG_A), f32(SEG_B), f32(SEG_D)] + mla_shapes + [f32(dm)],
        grid_spec=grid_spec,
        compiler_params=_cparams(("arbitrary",), vmem=PROJ_VMEM_LIMIT_BYTES),
        name="moe_combine_ln_mod_inproj",
    )(pos, x_new, route, g2, ln_g, ln_b, ybuf, shift, scale, w_pad, *mla_args)


def _pad_cols(w, width):
    return jnp.pad(w, ((0, 0), (0, width - w.shape[1])))


def kernel(x, c, ctx, c_ctx, w_ada, b_ada, w_in, swa_sink, ret_decay_exp, ret_gn, mla_q_norm, mla_kv_norm,
           mla_w_uq, mla_w_ukv, hgrn_lb_logits, hgrn_gn, w_out, ln1_g, ln1_b, router_group, router_expert,
           moe_w1, moe_w3, moe_w2, ln2_g, ln2_b):
    batch, t_len, dm = x.shape
    l_len = ctx.shape[1]
    depth = w_in.shape[0]
    p_len = t_len + l_len
    n = batch * p_len
    assert batch == 2 and l_len == ROW_TILE and t_len % 256 == 0 and t_len % GRID_W == 0
    tiles_per_batch = p_len // ROW_TILE
    alpha = float((2 * depth) ** 0.25)

    cos_swa, sin_swa, cos_rows, sin_rows = _rope_tables(t_len, l_len, batch)
    mats = jnp.asarray(_hgrn_level_mats(), BF16)

    cvec = jnp.zeros((SUBLANES, dm), F32).at[0:batch].set(c).at[batch].set(c_ctx)
    mod = _ada(cvec, w_ada, b_ada)

    lbp = jax.nn.softmax(hgrn_lb_logits.astype(F32), axis=0)
    lower_bounds = jnp.cumsum(lbp, axis=0) - lbp[0]

    pending = None

    out = None
    for l in range(depth):
        need_ctx = l < depth - 1
        m6 = mod[l, :3].reshape(3, 6, dm)
        sh1, sc1, g1, sh2, sc2, g2 = (m6[:, k].reshape(3, 1, dm) for k in range(6))

        wq = mla_w_uq[l].reshape(MLA_Q_RANK, MLA_HEADS, MLA_NOPE + MLA_ROPE)
        wq = jnp.pad(wq, ((0, 0), (0, 0), (0, 2 * LANES - (MLA_NOPE + MLA_ROPE))))
        wq = wq.reshape(MLA_Q_RANK, MLA_HEADS * 2 * LANES).astype(BF16)
        wkv = mla_w_ukv[l].reshape(MLA_KV_RANK, MLA_HEADS, MLA_NOPE + MLA_V)
        wk = wkv[:, :, :MLA_NOPE].reshape(MLA_KV_RANK, -1).astype(BF16)
        wv_t = wkv[:, :, MLA_NOPE:].reshape(MLA_KV_RANK, -1).T.astype(BF16)
        mla_args = (mla_q_norm[l].reshape(1, -1), mla_kv_norm[l].reshape(1, -1), wq, wk, wv_t,
                    cos_rows, sin_rows)

        w_pad = _w_in_prep(w_in, l)
        if pending is None:
            pa, pb, pd, qf, kf, vt, stream = _proj(x, ctx, sh1, sc1, w_pad, mla_args, tiles_per_batch)
        else:
            pa, pb, pd, qf, kf, vt, stream = _combine_proj(*pending, sh1, sc1, w_pad, mla_args,
                                                           tiles_per_batch, alpha)

        cc = _mla_attn(qf, kf, vt, batch, t_len, l_len, need_ctx)
        a, b_f, b_b, d_f, d_b = _mixers(
            [_swa_part(pa, swa_sink[l], cos_swa, sin_swa, batch, t_len, l_len, need_ctx),
             _retention_part(pb, ret_decay_exp[l], batch, t_len, l_len),
             _hgrn_part(pd, lower_bounds[l], mats, batch, t_len, l_len)], n)

        w_r = _pad_cols(jnp.concatenate([router_group[l], router_expert[l]], axis=1), LANES).astype(BF16)
        x_new, h2, route, counts = _out_proj(
            stream, a, b_f, b_b, cc, d_f, d_b, pb, pd, g1, sh2, sc2,
            ln1_g[l].reshape(1, -1), ln1_b[l].reshape(1, -1),
            ret_gn[l].reshape(1, -1), hgrn_gn[l].reshape(1, -1),
            w_out[l].astype(BF16), w_r, tiles_per_batch, alpha, need_ctx)

        n_active = n if need_ctx else batch * t_len
        n_blocks = (n_active * TOP_K + N_EXPERTS * (MOE_ROWS - 1) + MOE_ROWS - 1) // MOE_ROWS
        pos, blk_e, blk_valid, pad_start, blk_slot, blk_next = _route_plan(route, counts, n_blocks)
        xbuf = _dispatch(h2, pos, pad_start, blk_valid, tiles_per_batch, not need_ctx, batch)
        ybuf = _moe_experts(xbuf, blk_e, blk_valid, blk_slot, blk_next, moe_w1, moe_w3, moe_w2, l)
        combine_args = (x_new, route, ybuf, pos, g2, ln2_g[l].reshape(1, -1), ln2_b[l].reshape(1, -1))
        if need_ctx:
            pending = combine_args
        else:
            out = _combine(*combine_args, tiles_per_batch, alpha, True, batch).reshape(batch, t_len, dm)
    return out
```

```python
import functools
import math

import numpy as np
import jax
import jax.numpy as jnp
from jax import lax
from jax.experimental import pallas as pl
from jax.experimental.pallas import tpu as pltpu

F32 = jnp.float32
BF16 = jnp.bfloat16

GRID_W = 64
HEAD_DIM = 128
SWA_HEADS = 4
SWA_KV_HEADS = 2
SWA_BLOCK = 128
RET_HEADS = 4
RET_DK = 128
MLA_HEADS = 4
MLA_Q_RANK = 384
MLA_KV_RANK = 128
MLA_NOPE = 128
MLA_ROPE = 64
MLA_V = 128
HGRN_HEADS = 4
N_GROUPS = 4
EXPERTS_PER_GROUP = 8
N_EXPERTS = N_GROUPS * EXPERTS_PER_GROUP
TOP_K = 2
ROPE_BASE = 10000.0
EPS = 1e-6

LANES = 128
SUBLANES = 8
VMEM_LIMIT_BYTES = 56 * 1024 * 1024
PROJ_VMEM_LIMIT_BYTES = 60 * 1024 * 1024

CHUNK = 128
ROW_TILE = 256
MOE_ROWS = 256
MLA_KEY_CHUNK = 256
MLA_HEADS_PER_STEP = 4
WIDE = 512

SEG_A = 1024
SEG_B = 2048
SEG_C = 640
SEG_D = 2560


def _cparams(sem, vmem=VMEM_LIMIT_BYTES):
    return pltpu.CompilerParams(dimension_semantics=sem, vmem_limit_bytes=vmem)


def _ln_rows(x):
    mu = jnp.mean(x, axis=-1, keepdims=True)
    xc = x - mu
    var = jnp.mean(xc * xc, axis=-1, keepdims=True)
    return xc * lax.rsqrt(var + EPS)


def _silu(x):
    return x * (1.0 / (1.0 + jnp.exp(-x)))


def _dot(a, b):
    return jnp.dot(a, b, preferred_element_type=F32)


def _pack_bf16_pairs(x):
    k = x.shape[1] // 2
    hi = lax.bitcast_convert_type(x[:, :k].astype(BF16).astype(F32), jnp.uint32)
    lo = lax.bitcast_convert_type(x[:, k:].astype(BF16).astype(F32), jnp.uint32)
    return hi | (lo >> 16)


def _unpack_bf16_pairs(w):
    hi = lax.bitcast_convert_type(w & jnp.uint32(0xFFFF0000), F32)
    lo = lax.bitcast_convert_type(w << 16, F32)
    return jnp.concatenate([hi, lo], axis=1)


def _dot_nt(a, b):
    return lax.dot_general(a, b, (((1,), (1,)), ((), ())), preferred_element_type=F32)


def _dot_tn(a, b):
    return lax.dot_general(a, b, (((0,), (0,)), ((), ())), preferred_element_type=F32)


def _ada_kernel(c_ref, w_ref, b_ref, o_ref):
    s = _silu(c_ref[...]).astype(BF16)
    o_ref[0] = _dot(s, w_ref[0].astype(BF16)) + b_ref[0]


def _ada(cvec, w_ada, b_ada):
    depth, d, n6 = w_ada.shape
    tn = n6 // 8
    return pl.pallas_call(
        _ada_kernel,
        out_shape=jax.ShapeDtypeStruct((depth, SUBLANES, n6), F32),
        grid=(depth, n6 // tn),
        in_specs=[
            pl.BlockSpec((SUBLANES, d), lambda l, j: (0, 0)),
            pl.BlockSpec((1, d, tn), lambda l, j: (l, 0, j)),
            pl.BlockSpec((1, 1, tn), lambda l, j: (l, 0, j)),
        ],
        out_specs=pl.BlockSpec((1, SUBLANES, tn), lambda l, j: (l, 0, j)),
        compiler_params=_cparams(("arbitrary", "arbitrary")),
        name="ada_mod",
    )(cvec, w_ada, b_ada.reshape(depth, 1, n6))


SEG_WIDTHS = (SEG_A, SEG_B, SEG_C, SEG_D)
SEG_C_TRUE = MLA_Q_RANK + MLA_KV_RANK + MLA_ROPE
IN_COLS = SEG_A + SEG_B + SEG_C_TRUE + SEG_D
IN_COLS_PAD = sum(SEG_WIDTHS)


PREP_ROWS = 256
_C_END = SEG_A + SEG_B + SEG_C_TRUE
_C_PAD = SEG_C - SEG_C_TRUE
assert _C_END % PREP_ROWS + _C_PAD <= PREP_ROWS and IN_COLS_PAD % PREP_ROWS in (0, PREP_ROWS // 2)


def _w_in_prep_kernel(w_ref, o_ref, prev_ref):
    j = pl.program_id(0)
    jc = _C_END // PREP_ROWS
    rem = _C_END % PREP_ROWS
    keep = PREP_ROWS - _C_PAD
    cur = w_ref[0].astype(BF16)

    @pl.when(j < jc)
    def _():
        o_ref[...] = cur

    @pl.when(j == jc)
    def _():
        o_ref[0:rem, :] = cur[0:rem]
        o_ref[rem:rem + _C_PAD, :] = jnp.zeros((_C_PAD, cur.shape[1]), BF16)
        o_ref[rem + _C_PAD:PREP_ROWS, :] = cur[rem:keep]

    @pl.when(j > jc)
    def _():
        o_ref[0:_C_PAD, :] = prev_ref[keep:PREP_ROWS, :]
        o_ref[_C_PAD:PREP_ROWS, :] = cur[0:keep]

    prev_ref[...] = cur


def _w_in_prep(w_in, layer):
    _, d, cols = w_in.shape
    assert cols == IN_COLS
    w_t = jnp.swapaxes(w_in, 1, 2)
    return pl.pallas_call(
        _w_in_prep_kernel,
        out_shape=jax.ShapeDtypeStruct((IN_COLS_PAD, d), BF16),
        grid=(pl.cdiv(IN_COLS_PAD, PREP_ROWS),),
        in_specs=[pl.BlockSpec((1, PREP_ROWS, d), lambda j: (layer, j, 0))],
        out_specs=pl.BlockSpec((PREP_ROWS, d), lambda j: (j, 0)),
        scratch_shapes=[pltpu.VMEM((PREP_ROWS, d), BF16)],
        compiler_params=_cparams(("arbitrary",)),
        name="w_in_prep",
    )(w_t)


def _column_blocks(outs):
    return [(o_ref, j, min(WIDE, o_ref.shape[1] - j)) for o_ref in outs for j in range(0, o_ref.shape[1], WIDE)]


def _project_rows(x, sh_ref, sc_ref, w_ref, outs, between=None):
    y = _ln_rows(x)
    h = (y * (1.0 + sc_ref[0]) + sh_ref[0]).astype(BF16)
    base = 0
    for idx, (o_ref, j, cw) in enumerate(_column_blocks(outs)):
        o_ref[:, j:j + cw] = _dot_nt(h, w_ref[base:base + cw, :])
        base += cw
        if between is not None:
            between(idx)


def _proj_kernel(x_ref, ctx_ref, sh_ref, sc_ref, w_ref, *rest, tiles_per_batch):
    mla_in, (oa, ob, od, qf, kf, vt, so, pc_buf) = rest[:7], rest[7:]
    is_ctx = pl.program_id(0) % tiles_per_batch == 0
    x = jnp.where(is_ctx, ctx_ref[0], x_ref[0])
    so[...] = x
    outs = (oa, ob, pc_buf, od)
    c_done = max(k for k, blk in enumerate(_column_blocks(outs)) if blk[0] is pc_buf)

    def after_block(idx):
        if idx == c_done:
            _mla_up_rows(pc_buf[...], *mla_in, qf, kf, vt)

    _project_rows(x, sh_ref, sc_ref, w_ref, outs, between=after_block)


def _group_of_tile(i, tiles_per_batch):
    return jnp.where(i % tiles_per_batch == 0, 2, i // tiles_per_batch)


def _proj(x, ctx, shift, scale, w_pad, mla_args, tiles_per_batch):
    tm = ROW_TILE
    batch, t_len, d = x.shape
    n = batch * (t_len + ctx.shape[1])
    tpb = tiles_per_batch
    gmap = lambda i: (_group_of_tile(i, tiles_per_batch), 0, 0)
    row = lambda w: pl.BlockSpec((tm, w), lambda i: (i, 0))
    f32 = lambda w: jax.ShapeDtypeStruct((n, w), F32)
    mla_in, mla_out, mla_shapes = _mla_up_specs(n, mla_args)
    return pl.pallas_call(
        functools.partial(_proj_kernel, tiles_per_batch=tiles_per_batch),
        out_shape=[f32(SEG_A), f32(SEG_B), f32(SEG_D)] + mla_shapes + [f32(d)],
        grid=(n // tm,),
        in_specs=[
            pl.BlockSpec((1, tm, d), lambda i: (i // tpb, jnp.maximum(i % tpb - 1, 0), 0)),
            pl.BlockSpec((1, tm, d), lambda i: (i // tpb, 0, 0)),
            pl.BlockSpec((1, 1, d), gmap),
            pl.BlockSpec((1, 1, d), gmap),
            pl.BlockSpec(memory_space=pltpu.VMEM),
        ] + mla_in,
        out_specs=[row(SEG_A), row(SEG_B), row(SEG_D)] + mla_out + [row(d)],
        scratch_shapes=[pltpu.VMEM((tm, SEG_C), F32)],
        compiler_params=_cparams(("arbitrary",), vmem=PROJ_VMEM_LIMIT_BYTES),
        name="ln_mod_inproj",
    )(x, ctx, shift, scale, w_pad, *mla_args)


def _rope128(x, cos, sin):
    return x * cos + pltpu.roll(x, 64, 1) * sin


def _rope64(x, cos, sin):
    lane = lax.broadcasted_iota(jnp.int32, x.shape, 1)
    rot = jnp.where((lane % 64) < 32, pltpu.roll(x, 96, 1), pltpu.roll(x, 32, 1))
    return x * cos + rot * sin


def _rope_tables(t_len, l_len, batch):
    rows = t_len // GRID_W
    row = np.repeat(np.arange(rows), GRID_W).astype(np.float32)
    col = np.tile(np.arange(GRID_W), rows).astype(np.float32)

    def angles(rot_dim):
        n_freq = rot_dim // 4
        inv = (ROPE_BASE ** (-np.arange(n_freq, dtype=np.float32) / n_freq)).astype(np.float32)
        return np.concatenate([row[:, None] * inv, col[:, None] * inv], -1).astype(np.float32)

    a_swa = angles(HEAD_DIM)
    cos_swa = np.concatenate([np.cos(a_swa), np.cos(a_swa)], -1)
    sin_swa = np.concatenate([-np.sin(a_swa), np.sin(a_swa)], -1)
    a_mla = angles(MLA_ROPE)
    cos_m = np.concatenate([np.cos(a_mla), np.cos(a_mla), np.ones((t_len, 64), np.float32)], -1)
    sin_m = np.concatenate([-np.sin(a_mla), np.sin(a_mla), np.zeros((t_len, 64), np.float32)], -1)
    ones = np.ones((l_len, LANES), np.float32)
    zeros = np.zeros((l_len, LANES), np.float32)
    cos_rows = np.concatenate([np.concatenate([ones, cos_m], 0)] * batch, 0)
    sin_rows = np.concatenate([np.concatenate([zeros, sin_m], 0)] * batch, 0)
    return (jnp.asarray(cos_swa, F32), jnp.asarray(sin_swa, F32),
            jnp.asarray(cos_rows, F32), jnp.asarray(sin_rows, F32))


def _mla_up_rows(pc, qn_ref, kvn_ref, wq_ref, wk_ref, wv_ref, cos_ref, sin_ref, q_out, k_out, v_out):
    scale = float((MLA_NOPE + MLA_ROPE) ** -0.5 * math.log2(math.e))
    cq = pc[:, :MLA_Q_RANK]
    ckv = pc[:, MLA_Q_RANK:MLA_Q_RANK + MLA_KV_RANK]
    kr = pc[:, MLA_Q_RANK + MLA_KV_RANK:]
    cos = cos_ref[...]
    sin = sin_ref[...]

    def rms(x, g):
        return x * lax.rsqrt(jnp.mean(x * x, axis=-1, keepdims=True) + EPS) * g

    qh = _dot(rms(cq, qn_ref[...]).astype(BF16), wq_ref[...])
    ckn = rms(ckv, kvn_ref[...]).astype(BF16)
    kh = _dot(ckn, wk_ref[...])
    v_out[...] = _dot_nt(wv_ref[...], ckn).astype(BF16)
    kr_rot = _rope64(kr, cos, sin).astype(BF16)
    for h in range(MLA_HEADS):
        base = h * 2 * LANES
        q_out[:, base:base + LANES] = (qh[:, base:base + LANES] * scale).astype(BF16)
        q_out[:, base + LANES:base + 2 * LANES] = (
            _rope64(qh[:, base + LANES:base + 2 * LANES], cos, sin) * scale).astype(BF16)
        k_out[:, base:base + LANES] = kh[:, h * LANES:(h + 1) * LANES].astype(BF16)
        k_out[:, base + LANES:base + 2 * LANES] = kr_rot


def _mla_up_specs(n, mla_args):
    tm = ROW_TILE
    const2 = lambda i, *_: (0, 0)
    rows = lambda i, *_: (i, 0)
    in_specs = [pl.BlockSpec(a.shape, const2) for a in mla_args[:5]]
    in_specs += [pl.BlockSpec((tm, LANES), rows), pl.BlockSpec((tm, LANES), rows)]
    qk_w = MLA_HEADS * 2 * LANES
    out_specs = [pl.BlockSpec((tm, qk_w), rows), pl.BlockSpec((tm, qk_w), rows),
                 pl.BlockSpec((MLA_HEADS * MLA_V, tm), lambda i, *_: (0, i))]
    out_shapes = [jax.ShapeDtypeStruct((n, qk_w), BF16), jax.ShapeDtypeStruct((n, qk_w), BF16),
                  jax.ShapeDtypeStruct((MLA_HEADS * MLA_V, n), BF16)]
    return in_specs, out_specs, out_shapes


def _mla_attn_kernel(q_ref, k_ref, vt_ref, o_ref, s_ref, p_ref, *, kc, l_len, need_ctx):
    p_len = k_ref.shape[0]
    tq = q_ref.shape[0]
    dqk = 2 * LANES
    heads = q_ref.shape[1] // dqk
    half = p_len // 2
    is_ctx = pl.program_id(2) == 0

    @pl.when(is_ctx)
    def _():
        if not need_ctx:
            o_ref[...] = jnp.zeros_like(o_ref)
            return
        for h in range(heads):
            hq = slice(h * dqk, (h + 1) * dqk)
            hv = slice(h * MLA_V, (h + 1) * MLA_V)
            s = _dot_nt(k_ref[0:l_len, hq], q_ref[:, hq])
            p = jnp.exp2(s - jnp.max(s, axis=0, keepdims=True))
            acc = _dot(vt_ref[hv, 0:l_len], p.astype(BF16))
            o_ref[:, hv] = (acc / jnp.sum(p, axis=0, keepdims=True)).T

    @pl.when(jnp.logical_not(is_ctx))
    def _():
        fold = lambda a: a.reshape(kc // SUBLANES, SUBLANES, tq)
        m8s = []
        for h in range(heads):
            q = q_ref[:, h * dqk:(h + 1) * dqk]
            m8 = None
            for off in range(0, p_len, kc):
                s = _dot_nt(k_ref[off:off + kc, h * dqk:(h + 1) * dqk], q)
                s_ref[h, off:off + kc, :] = s
                cm = jnp.max(fold(s), axis=0)
                m8 = cm if m8 is None else jnp.maximum(m8, cm)
            m8s.append(m8)
        for h in range(heads):
            m = jnp.max(m8s[h], axis=0, keepdims=True)
            l8 = jnp.zeros((SUBLANES, tq), F32)
            for off in range(0, p_len, kc):
                p = jnp.exp2(s_ref[h, off:off + kc, :] - m)
                l8 = l8 + jnp.sum(fold(p), axis=0)
                p_ref[h, off:off + kc, :] = p.astype(BF16)
            l = jnp.sum(l8, axis=0, keepdims=True)
            hv = slice(h * MLA_V, (h + 1) * MLA_V)
            acc = (_dot(vt_ref[hv, 0:half], p_ref[h, 0:half, :])
                   + _dot(vt_ref[hv, half:p_len], p_ref[h, half:p_len, :]))
            o_ref[:, hv] = (acc / l).T


def _mla_attn(qf, kf, vt, batch, t_len, l_len, need_ctx):
    n = qf.shape[0]
    p_len = t_len + l_len
    tq = ROW_TILE
    assert l_len == tq
    kc = MLA_KEY_CHUNK
    hp = MLA_HEADS_PER_STEP
    tpb = p_len // tq
    return pl.pallas_call(
        functools.partial(_mla_attn_kernel, kc=kc, l_len=l_len, need_ctx=need_ctx),
        out_shape=jax.ShapeDtypeStruct((n, MLA_HEADS * MLA_V), F32),
        grid=(batch, MLA_HEADS // hp, tpb),
        in_specs=[
            pl.BlockSpec((tq, hp * 2 * LANES), lambda b, h, i: (b * tpb + i, h)),
            pl.BlockSpec((p_len, hp * 2 * LANES), lambda b, h, i: (b, h), pipeline_mode=pl.Buffered(1)),
            pl.BlockSpec((hp * MLA_V, p_len), lambda b, h, i: (h, b), pipeline_mode=pl.Buffered(1)),
        ],
        out_specs=pl.BlockSpec((tq, hp * MLA_V), lambda b, h, i: (b * tpb + i, h)),
        scratch_shapes=[pltpu.VMEM((hp, p_len, tq), F32), pltpu.VMEM((hp, p_len, tq), BF16)],
        compiler_params=_cparams(("arbitrary", "arbitrary", "arbitrary")),
        name="mla_attn",
    )(qf, kf, vt)


def _swa_kernel(sink_ref, q_ref, kp_ref, kc_ref, kn_ref, vp_ref, vc_ref, vn_ref, kx_ref, vx_ref,
                cq_ref, sq_ref, cp_ref, sp_ref, cn_ref, sn_ref, o_ref, *, nb, n_ctx, scale, need_ctx):
    j = pl.program_id(0)
    n = j - n_ctx
    blk = SWA_BLOCK
    batch = q_ref.shape[0]
    rr = lax.broadcasted_iota(jnp.int32, (2 * blk, 1), 0)
    mx = lambda a: jnp.max(a, axis=-1, keepdims=True)
    sm = lambda a: jnp.sum(a, axis=-1, keepdims=True)

    def heads_of(b, h):
        q0 = q_ref[b, :, 2 * h * LANES:(2 * h + 1) * LANES]
        q1 = q_ref[b, :, (2 * h + 1) * LANES:(2 * h + 2) * LANES]
        sk = jnp.where(rr < blk, sink_ref[2 * h], sink_ref[2 * h + 1]).astype(F32)
        return q0, q1, sk

    def store(b, h, o):
        o_ref[b, :, 2 * h * LANES:(2 * h + 1) * LANES] = o[:blk]
        o_ref[b, :, (2 * h + 1) * LANES:(2 * h + 2) * LANES] = o[blk:]

    @pl.when(j < n_ctx)
    def _():
        if not need_ctx:
            o_ref[...] = jnp.zeros_like(o_ref)
            return
        for b in range(batch):
            for h in range(SWA_KV_HEADS):
                ks = slice(h * LANES, (h + 1) * LANES)
                q0, q1, sk = heads_of(b, h)
                qq = jnp.concatenate([q0, q1], axis=0).astype(BF16)
                s_x = _dot_nt(qq, kx_ref[b, :, ks].astype(BF16)) * scale
                m = jnp.maximum(mx(s_x), sk)
                p_x = jnp.exp(s_x - m)
                den = sm(p_x) + jnp.exp(sk - m)
                store(b, h, _dot(p_x.astype(BF16), vx_ref[b, :, ks].astype(BF16)) / den)

    @pl.when(j >= n_ctx)
    def _():
        cq, sq = cq_ref[...], sq_ref[...]
        cp, sp = cp_ref[...], sp_ref[...]
        cn, sn = cn_ref[...], sn_ref[...]
        ri = lax.broadcasted_iota(jnp.int32, (2 * blk, blk), 0) % blk
        ci = lax.broadcasted_iota(jnp.int32, (2 * blk, blk), 1)
        mask_p = (ci >= ri) & (n > 0)
        mask_n = (ci <= ri) & (n < nb - 1)
        neg = jnp.float32(-jnp.inf)
        for b in range(batch):
            for h in range(SWA_KV_HEADS):
                ks = slice(h * LANES, (h + 1) * LANES)
                q0, q1, sk = heads_of(b, h)
                qq = jnp.concatenate([_rope128(q0, cq, sq), _rope128(q1, cq, sq)], axis=0).astype(BF16)
                kc = _rope128(kc_ref[b, :, ks], cq, sq).astype(BF16)
                kp = _rope128(kp_ref[b, :, ks], cp, sp).astype(BF16)
                kn = _rope128(kn_ref[b, :, ks], cn, sn).astype(BF16)
                s_c = _dot_nt(qq, kc) * scale
                s_p = jnp.where(mask_p, _dot_nt(qq, kp) * scale, neg)
                s_n = jnp.where(mask_n, _dot_nt(qq, kn) * scale, neg)
                s_x = _dot_nt(qq, kx_ref[b, :, ks].astype(BF16)) * scale
                m = jnp.maximum(jnp.maximum(mx(s_c), mx(s_p)), jnp.maximum(mx(s_n), mx(s_x)))
                m = jnp.maximum(m, sk)
                p_c, p_p, p_n, p_x = (jnp.exp(a - m) for a in (s_c, s_p, s_n, s_x))
                den = sm(p_c) + sm(p_p) + sm(p_n) + sm(p_x) + jnp.exp(sk - m)
                store(b, h, (_dot(p_c.astype(BF16), vc_ref[b, :, ks].astype(BF16))
                             + _dot(p_p.astype(BF16), vp_ref[b, :, ks].astype(BF16))
                             + _dot(p_n.astype(BF16), vn_ref[b, :, ks].astype(BF16))
                             + _dot(p_x.astype(BF16), vx_ref[b, :, ks].astype(BF16))) / den)


def _swa_part(pa, sink, cos_t, sin_t, batch, t_len, l_len, need_ctx):
    p_len = t_len + l_len
    blk = SWA_BLOCK
    nb = t_len // blk
    n_ctx = l_len // blk
    scale = float(HEAD_DIM ** -0.5)
    kv_w = SWA_KV_HEADS * HEAD_DIM
    q_w = SWA_HEADS * HEAD_DIM
    kcol, vcol = q_w // kv_w, q_w // kv_w + 1
    pa3 = pa.reshape(batch, p_len, pa.shape[1])
    lat = lambda j: jnp.maximum(j - n_ctx, 0)
    same = lambda j: lat(j)
    prv = lambda j: jnp.maximum(lat(j) - 1, 0)
    nxt = lambda j: jnp.minimum(lat(j) + 1, nb - 1)
    kv = lambda col, rowf: pl.BlockSpec((batch, blk, kv_w), lambda j: (0, n_ctx + rowf(j), col))
    tab = lambda rowf: pl.BlockSpec((blk, LANES), lambda j: (rowf(j), 0))
    in_specs = [
        pl.BlockSpec(memory_space=pltpu.SMEM),
        pl.BlockSpec((batch, blk, q_w), lambda j: (0, j, 0)),
        kv(kcol, prv), kv(kcol, same), kv(kcol, nxt),
        kv(vcol, prv), kv(vcol, same), kv(vcol, nxt),
        pl.BlockSpec((batch, l_len, kv_w), lambda j: (0, 0, kcol)),
        pl.BlockSpec((batch, l_len, kv_w), lambda j: (0, 0, vcol)),
        tab(same), tab(same), tab(prv), tab(prv), tab(nxt), tab(nxt),
    ]
    return dict(
        kernel=functools.partial(_swa_kernel, nb=nb, n_ctx=n_ctx, scale=scale, need_ctx=need_ctx),
        in_specs=in_specs,
        args=[sink] + [pa3] * 9 + [cos_t, sin_t] * 3,
        out_specs=[pl.BlockSpec((batch, blk, q_w), lambda j: (0, j, 0))],
        out_shapes=[jax.ShapeDtypeStruct((batch, p_len, q_w), F32)],
        scratch=[],
        steps=n_ctx + nb)


def _chunk_index(d, c, n_l, n_t):
    bwd = jnp.where(c < n_l, n_l - 1 - c, n_l + (n_t - 1) - (c - n_l))
    return jnp.where(d == 0, c, bwd)


def _flip_iotas(d):
    row = lax.broadcasted_iota(jnp.int32, (CHUNK, CHUNK), 0)
    col = lax.broadcasted_iota(jnp.int32, (CHUNK, CHUNK), 1)
    rf = jnp.where(d == 0, row, CHUNK - 1 - row)
    cf = jnp.where(d == 0, col, CHUNK - 1 - col)
    return rf, cf


def _ret_kernel(s_ref, qf_ref, kf_ref, vf_ref, qb_ref, kb_ref, vb_ref, of_ref, ob_ref, st_ref):
    c = pl.program_id(0)
    batch = qf_ref.shape[0]

    @pl.when(c == 0)
    def _():
        st_ref[...] = jnp.zeros_like(st_ref)

    ks = float(RET_DK ** -0.5)
    dirs = ((qf_ref, kf_ref, vf_ref, of_ref), (qb_ref, kb_ref, vb_ref, ob_ref))
    for d, (q_ref, k_ref, v_ref, o_ref) in enumerate(dirs):
        rf, cf = _flip_iotas(d)
        rff = rf.astype(F32)
        dn = (rf - cf).astype(F32)
        for h in range(RET_HEADS):
            hs = slice(h * LANES, (h + 1) * LANES)
            sv = jnp.full((CHUNK, CHUNK), s_ref[d * RET_HEADS + h], F32)
            lg = jnp.log1p(-jnp.exp2(-sv))
            dec = jnp.where(dn >= 0, jnp.exp(dn * lg), 0.0)
            eq = jnp.exp((rff + 1.0) * lg)
            ek = jnp.exp((CHUNK - 1.0 - rff) * lg)
            a_chunk = jnp.exp(float(CHUNK) * lg)
            for b in range(batch):
                q = q_ref[b, :, hs]
                k = k_ref[b, :, hs] * ks
                v = v_ref[b, :, hs].astype(BF16)
                st = st_ref[d, b, h]
                a = _dot_nt(q.astype(BF16), k.astype(BF16)) * dec
                o = _dot(a.astype(BF16), v) + _dot_nt((q * eq).astype(BF16), st.astype(BF16))
                o_ref[b, :, hs] = o
                st_ref[d, b, h] = a_chunk * st + _dot_tn(v, (k * ek).astype(BF16))


def _hgrn_level_mats():
    c = CHUNK
    t = np.arange(c)[:, None]
    u = np.arange(c)[None, :]
    mats = [u <= t, u > t]
    m = c // 2
    while m >= 1:
        mid = (t // (2 * m)) * 2 * m + m
        second = (t % (2 * m)) >= m
        qrole = (u >= mid) & (u <= t)
        krole = (u > t) & (u <= mid - 1)
        mats.append(np.where(second, qrole, krole))
        m //= 2
    fwd = np.concatenate(mats, 0).astype(np.float32)
    bwd = np.concatenate([mm[::-1, ::-1] for mm in mats], 0).astype(np.float32)
    return np.stack([fwd, bwd], 0)


N_LEVELS = int(math.log2(CHUNK))


def _hgrn_kernel(m_ref, lb_ref, qf_ref, ff_ref, vf_ref, qb_ref, fb_ref, vb_ref, of_ref, ob_ref, st_ref):
    c = pl.program_id(0)
    batch = qf_ref.shape[0]

    @pl.when(c == 0)
    def _():
        st_ref[...] = jnp.zeros_like(st_ref)

    dirs = ((qf_ref, ff_ref, vf_ref, of_ref), (qb_ref, fb_ref, vb_ref, ob_ref))
    for d, (q_ref, f_ref, v_ref, o_ref) in enumerate(dirs):
        rf, cf = _flip_iotas(d)
        mst = m_ref[d]
        last = CHUNK - 1 if d == 0 else 0
        lvl_masks = []
        for lvl in range(N_LEVELS):
            m = CHUNK >> (lvl + 1)
            sh = N_LEVELS - lvl
            lvl_masks.append(((rf >> sh) == (cf >> sh)) & ((rf & m) != 0) & ((cf & m) == 0))
        diag = rf == cf
        for h in range(HGRN_HEADS):
            hs = slice(h * LANES, (h + 1) * LANES)
            lb = lb_ref[d, :, hs]
            for b in range(batch):
                f = lb + (1.0 - lb) * (1.0 / (1.0 + jnp.exp(-f_ref[b, :, hs])))
                k = 1.0 - f
                g = jnp.log(f)
                q = _silu(q_ref[b, :, hs])
                v = v_ref[b, :, hs].astype(BF16)
                e = jnp.exp(_dot(mst, g.astype(BF16)))
                e_q = e[0:CHUNK]
                q_in = q * e_q
                k_st = k * e[CHUNK:2 * CHUNK]
                scores = jnp.where(diag, _dot_nt(q.astype(BF16), k.astype(BF16)), 0.0)
                for lvl in range(N_LEVELS):
                    el = e[(2 + lvl) * CHUNK:(3 + lvl) * CHUNK]
                    sl = _dot_nt((q * el).astype(BF16), (k * el).astype(BF16))
                    scores = scores + jnp.where(lvl_masks[lvl], sl, 0.0)
                st = st_ref[d, b, h]
                o = _dot(scores.astype(BF16), v) + _dot_nt(q_in.astype(BF16), st.astype(BF16))
                o_ref[b, :, hs] = o
                st_ref[d, b, h] = st * e_q[last:last + 1, :] + _dot_tn(v, k_st.astype(BF16))


def _scan_specs(batch, t_len, l_len):
    n_l, n_t = l_len // CHUNK, t_len // CHUNK

    def spec(d, col):
        return pl.BlockSpec((batch, CHUNK, WIDE), lambda c: (0, _chunk_index(d, c, n_l, n_t), col))

    return spec, n_l + n_t


def _retention_part(pb, ret_s, batch, t_len, l_len):
    p_len = t_len + l_len
    pb3 = pb.reshape(batch, p_len, pb.shape[1])
    spec, n_chunks = _scan_specs(batch, t_len, l_len)
    return dict(
        kernel=_ret_kernel,
        in_specs=[pl.BlockSpec(memory_space=pltpu.SMEM)]
                 + [spec(0, col) for col in (0, 1, 2)] + [spec(1, col) for col in (0, 1, 2)],
        args=[ret_s.reshape(-1)] + [pb3] * 6,
        out_specs=[spec(0, 0), spec(1, 0)],
        out_shapes=[jax.ShapeDtypeStruct((batch, p_len, WIDE), F32)] * 2,
        scratch=[pltpu.VMEM((2, batch, RET_HEADS, CHUNK, CHUNK), F32)],
        steps=n_chunks)


def _hgrn_part(pd, lower_bounds, mats, batch, t_len, l_len):
    p_len = t_len + l_len
    pd3 = pd.reshape(batch, p_len, pd.shape[1])
    lb3 = lower_bounds.reshape(2, 1, WIDE)
    spec, n_chunks = _scan_specs(batch, t_len, l_len)
    return dict(
        kernel=_hgrn_kernel,
        in_specs=[pl.BlockSpec(mats.shape, lambda c: (0, 0, 0)),
                  pl.BlockSpec(lb3.shape, lambda c: (0, 0, 0)),
                  spec(0, 0), spec(0, 1), spec(0, 3), spec(1, 0), spec(1, 2), spec(1, 3)],
        args=[mats, lb3] + [pd3] * 6,
        out_specs=[spec(0, 0), spec(1, 0)],
        out_shapes=[jax.ShapeDtypeStruct((batch, p_len, WIDE), F32)] * 2,
        scratch=[pltpu.VMEM((2, batch, HGRN_HEADS, CHUNK, CHUNK), F32)],
        steps=n_chunks)


def _mixers(parts, n):
    steps = parts[0]["steps"]
    assert all(p["steps"] == steps for p in parts)
    n_in = [len(p["args"]) for p in parts]
    n_out = [len(p["out_shapes"]) for p in parts]
    n_scr = [len(p["scratch"]) for p in parts]

    def kern(*refs):
        ins, outs, scr = refs[:sum(n_in)], refs[sum(n_in):sum(n_in) + sum(n_out)], refs[sum(n_in) + sum(n_out):]
        a = b = c = 0
        for p, ni, no, ns in zip(parts, n_in, n_out, n_scr):
            p["kernel"](*ins[a:a + ni], *outs[b:b + no], *scr[c:c + ns])
            a, b, c = a + ni, b + no, c + ns

    outs = pl.pallas_call(
        kern,
        out_shape=[s for p in parts for s in p["out_shapes"]],
        grid=(steps,),
        in_specs=[s for p in parts for s in p["in_specs"]],
        out_specs=[s for p in parts for s in p["out_specs"]],
        scratch_shapes=[s for p in parts for s in p["scratch"]],
        compiler_params=_cparams(("arbitrary",)),
        name="swa_retention_hgrn",
    )(*[a for p in parts for a in p["args"]])
    return [o.reshape(n, o.shape[-1]) for o in outs]


R_E1, R_E2, R_G1, R_G2, R_S1, R_S2 = 0, 1, 2, 3, 4, 5


def _route(logits, count_ref, active):
    tm = logits.shape[0]
    lane = lax.broadcasted_iota(jnp.int32, logits.shape, 1)
    lanef = lane.astype(F32)
    big = jnp.float32(1e9)
    neg = jnp.float32(-jnp.inf)
    mx = lambda a: jnp.max(a, axis=-1, keepdims=True)
    mn = lambda a: jnp.min(a, axis=-1, keepdims=True)
    sm = lambda a: jnp.sum(a, axis=-1, keepdims=True)
    gl = jnp.where(lane < N_GROUPS, logits, neg)
    gm = mx(gl)
    p_grp = 1.0 / sm(jnp.exp(gl - gm))
    grp = mn(jnp.where(gl == gm, lanef, big))
    lo = N_GROUPS + grp * EXPERTS_PER_GROUP
    ing = (lanef >= lo) & (lanef < lo + EXPERTS_PER_GROUP)
    el = jnp.where(ing, logits, neg)
    l1 = mx(el)
    i1 = mn(jnp.where(el == l1, lanef, big))
    el2 = jnp.where(lanef == i1, neg, el)
    l2 = mx(el2)
    i2 = mn(jnp.where(el2 == l2, lanef, big))
    r = jnp.exp(l2 - l1)
    g1 = p_grp / (1.0 + r)
    g2 = p_grp * r / (1.0 + r)
    e1 = i1 - N_GROUPS
    e2 = i2 - N_GROUPS
    oh1 = lanef == e1
    oh2 = lanef == e2
    both = jnp.where(oh1 | oh2, active, 0.0)
    ri = lax.broadcasted_iota(jnp.int32, (tm, tm), 0)
    ci = lax.broadcasted_iota(jnp.int32, (tm, tm), 1)
    earlier = jnp.where(ci < ri, 1.0, 0.0).astype(BF16)
    before = _dot(earlier, both.astype(BF16)) + count_ref[0:1, :]
    s1 = sm(jnp.where(oh1, before, 0.0))
    s2 = sm(jnp.where(oh2, before, 0.0))
    count_ref[...] = count_ref[...] + jnp.sum(both, axis=0, keepdims=True)
    out = jnp.zeros(logits.shape, F32)
    for ln, val in ((R_E1, e1), (R_E2, e2), (R_G1, g1), (R_G2, g2), (R_S1, s1), (R_S2, s2)):
        out = jnp.where(lane == ln, val, out)
    return out


def _out_kernel(x_ref, a_ref, bf_ref, bb_ref, c_ref, df_ref, db_ref, rg_ref, hg_ref, g1_ref, sh2_ref, sc2_ref,
                lng_ref, lnb_ref, rgn_ref, hgn_ref, wo_ref, wr_ref,
                xo_ref, h2_ref, r_ref, cnt_ref, *, alpha, tiles_per_batch, route_ctx):
    i = pl.program_id(0)

    @pl.when(i == 0)
    def _():
        cnt_ref[...] = jnp.zeros_like(cnt_ref)

    bsum = bf_ref[...] + bb_ref[...]
    dsum = df_ref[...] + db_ref[...]
    rgn = rgn_ref[...]
    hgn = hgn_ref[...]
    parts_b, parts_d = [], []
    for h in range(4):
        hs = slice(h * LANES, (h + 1) * LANES)
        parts_b.append(_ln_rows(bsum[:, hs]))
        dh = dsum[:, hs]
        parts_d.append(dh * lax.rsqrt(jnp.mean(dh * dh, axis=-1, keepdims=True) + EPS))
    bo = _silu(rg_ref[...]) * (jnp.concatenate(parts_b, axis=1) * rgn)
    do = _silu(hg_ref[...]) * (jnp.concatenate(parts_d, axis=1) * hgn)
    o = (_dot(a_ref[...].astype(BF16), wo_ref[0:WIDE, :])
         + _dot(c_ref[...].astype(BF16), wo_ref[2 * WIDE:3 * WIDE, :])
         + _dot(bo.astype(BF16), wo_ref[WIDE:2 * WIDE, :])
         + _dot(do.astype(BF16), wo_ref[3 * WIDE:4 * WIDE, :]))
    y = alpha * x_ref[...] + g1_ref[0] * o
    xn = _ln_rows(y) * lng_ref[...] + lnb_ref[...]
    xo_ref[...] = xn
    h2 = _ln_rows(xn) * (1.0 + sc2_ref[0]) + sh2_ref[0]
    h2_ref[...] = _pack_bf16_pairs(h2)
    if route_ctx:
        active = jnp.float32(1.0)
    else:
        active = jnp.where(i % tiles_per_batch == 0, 0.0, 1.0).astype(F32)
    r_ref[...] = _route(_dot(h2.astype(BF16), wr_ref[...]), cnt_ref, active)


def _out_proj(stream, a, b_f, b_b, c, d_f, d_b, pb, pd, g1, sh2, sc2, ln_g, ln_b, ret_gn, hgrn_gn, w_out, w_r,
              tiles_per_batch, alpha, route_ctx):
    n, dm = stream.shape
    tm = ROW_TILE
    gmap = lambda i: (_group_of_tile(i, tiles_per_batch), 0, 0)
    row = lambda w: pl.BlockSpec((tm, w), lambda i: (i, 0))
    const = lambda w: pl.BlockSpec((1, w), lambda i: (0, 0))
    resident = pl.BlockSpec(memory_space=pltpu.VMEM)
    return pl.pallas_call(
        functools.partial(_out_kernel, alpha=alpha, tiles_per_batch=tiles_per_batch, route_ctx=route_ctx),
        out_shape=[jax.ShapeDtypeStruct((n, dm), F32), jax.ShapeDtypeStruct((n, dm // 2), jnp.uint32),
                   jax.ShapeDtypeStruct((n, LANES), F32), jax.ShapeDtypeStruct((SUBLANES, LANES), F32)],
        grid=(n // tm,),
        in_specs=[row(dm), row(WIDE), row(WIDE), row(WIDE), row(WIDE), row(WIDE), row(WIDE),
                  pl.BlockSpec((tm, WIDE), lambda i: (i, 3)),
                  pl.BlockSpec((tm, WIDE), lambda i: (i, 4)),
                  pl.BlockSpec((1, 1, dm), gmap), pl.BlockSpec((1, 1, dm), gmap),
                  pl.BlockSpec((1, 1, dm), gmap),
                  const(dm), const(dm), const(WIDE), const(WIDE), resident, resident],
        out_specs=[row(dm), row(dm // 2), row(LANES),
                   pl.BlockSpec((SUBLANES, LANES), lambda i: (0, 0))],
        compiler_params=_cparams(("arbitrary",)),
        name="mix_outproj_norm_route",
    )(stream, a, b_f, b_b, c, d_f, d_b, pb, pd, g1, sh2, sc2, ln_g, ln_b, ret_gn, hgrn_gn, w_out, w_r)


def _route_plan(route, counts_f, n_blocks):
    rows = MOE_ROWS
    counts = counts_f[0, :N_EXPERTS].astype(jnp.int32)
    padded = (counts + rows - 1) // rows * rows
    pend = jnp.cumsum(padded)
    pstart = pend - padded
    eid = route[:, R_E1:R_E2 + 1].astype(jnp.int32)
    rank = route[:, R_S1:R_S2 + 1].astype(jnp.int32)
    onehot = eid[:, :, None] == jnp.arange(N_EXPERTS, dtype=jnp.int32)
    pos = jnp.sum(jnp.where(onehot, pstart, 0), axis=-1) + rank
    blk_start = jnp.arange(n_blocks, dtype=jnp.int32) * rows
    blk_e = jnp.minimum(jnp.sum(blk_start[:, None] >= pend[None, :], axis=-1), N_EXPERTS - 1)
    blk_valid = (blk_start < pend[-1]).astype(jnp.int32)
    blk_ids = jnp.arange(n_blocks, dtype=jnp.int32)
    has_pad = padded > counts
    last_blk = pend // rows - 1
    blk_fill = (blk_valid == 0) | jnp.any((blk_ids[:, None] == last_blk[None, :]) & has_pad[None, :], axis=-1)
    ids = jnp.arange(N_EXPERTS, dtype=jnp.int32)
    nonempty = counts > 0
    seq = jnp.cumsum(nonempty.astype(jnp.int32)) - nonempty.astype(jnp.int32)
    cand = jnp.where(nonempty, ids, N_EXPERTS)
    later = jnp.where(ids[None, :] > ids[:, None], cand[None, :], N_EXPERTS)
    nxt = jnp.min(later, axis=-1)
    nxt = jnp.where(nxt >= N_EXPERTS, -1, nxt)
    blk_oh = blk_e[:, None] == ids[None, :]
    blk_slot = jnp.sum(jnp.where(blk_oh, seq % 2, 0), axis=-1)
    blk_next = jnp.sum(jnp.where(blk_oh, nxt, 0), axis=-1)
    i32 = lambda a: a.astype(jnp.int32)
    return (i32(pos.reshape(-1)), i32(blk_e), blk_valid, i32(blk_fill), i32(blk_slot), i32(blk_next))


def _dispatch_kernel(pos_ref, fill_ref, h_ref, x_hbm, zbuf, sem, *, tile_of_step, n_blocks):
    i = pl.program_id(0)
    tm = ROW_TILE
    rows = MOE_ROWS

    def zero_block_copy(j):
        start = pl.multiple_of(j * rows, rows)
        return pltpu.make_async_copy(zbuf, x_hbm.at[pl.ds(start, rows)], sem.at[1])

    @pl.when(i == 0)
    def _():
        zbuf[...] = jnp.zeros_like(zbuf)

        def fill_start(j, carry):
            @pl.when(fill_ref[j] == 1)
            def _():
                zero_block_copy(j).start()
            return carry

        def fill_wait(j, carry):
            @pl.when(fill_ref[j] == 1)
            def _():
                zero_block_copy(j).wait()
            return carry

        lax.fori_loop(0, n_blocks, fill_start, 0)
        lax.fori_loop(0, n_blocks, fill_wait, 0)

    def row_copy(r, k):
        p = pos_ref[(tile_of_step(i) * tm + r) * TOP_K + k]
        return pltpu.make_async_copy(h_ref.at[r], x_hbm.at[p], sem.at[0])

    for r in range(tm):
        for k in range(TOP_K):
            row_copy(r, k).start(priority=k)
    for r in range(tm):
        for k in range(TOP_K):
            row_copy(r, k).wait()


def _tile_schedule(n, tiles_per_batch, latent_only, batch):
    if latent_only:
        lat = tiles_per_batch - 1
        return batch * lat, (lambda s: (s // lat) * tiles_per_batch + 1 + s % lat)
    return n // ROW_TILE, (lambda s: s)


def _dispatch(h2, pos, blk_fill, tiles_per_batch, latent_only, batch):
    n, dm = h2.shape
    tm = ROW_TILE
    n_blocks = blk_fill.shape[0]
    n_tiles, tile_of_step = _tile_schedule(n, tiles_per_batch, latent_only, batch)
    grid_spec = pltpu.PrefetchScalarGridSpec(
        num_scalar_prefetch=2,
        grid=(n_tiles,),
        in_specs=[pl.BlockSpec((tm, dm), lambda i, p, f: (tile_of_step(i), 0))],
        out_specs=pl.BlockSpec(memory_space=pl.ANY),
        scratch_shapes=[pltpu.VMEM((MOE_ROWS, dm), h2.dtype), pltpu.SemaphoreType.DMA((2,))],
    )
    return pl.pallas_call(
        functools.partial(_dispatch_kernel, tile_of_step=tile_of_step, n_blocks=n_blocks),
        out_shape=jax.ShapeDtypeStruct((n_blocks * MOE_ROWS, dm), h2.dtype),
        grid_spec=grid_spec,
        compiler_params=_cparams(("arbitrary",)),
        name="moe_dispatch",
    )(pos, blk_fill, h2)


def _moe_kernel(be_ref, bv_ref, slot_ref, next_ref, x_ref, w1_hbm, w3_hbm, w2_hbm, y_ref,
                s1, s3, s2, w1b, w3b, w2b, sem, *, layer):
    j = pl.program_id(0)
    valid = bv_ref[j] == 1
    first = valid & ((j == 0) | (be_ref[j] != be_ref[jnp.maximum(j - 1, 0)]))

    def weight_copies(e, slot):
        return (pltpu.make_async_copy(w1_hbm.at[layer, e], s1.at[slot], sem.at[slot]),
                pltpu.make_async_copy(w3_hbm.at[layer, e], s3.at[slot], sem.at[slot]),
                pltpu.make_async_copy(w2_hbm.at[layer, e], s2.at[slot], sem.at[slot]))

    @pl.when(valid & (j == 0))
    def _():
        for cp in weight_copies(be_ref[0], slot_ref[0]):
            cp.start()

    @pl.when(first)
    def _():
        slot = slot_ref[j]
        for cp in weight_copies(be_ref[j], slot):
            cp.wait()
        nxt = next_ref[j]

        @pl.when(nxt >= 0)
        def _():
            for cp in weight_copies(nxt, 1 - slot):
                cp.start(priority=1)

        w1b[...] = s1[slot].astype(BF16)
        w3b[...] = s3[slot].astype(BF16)
        w2b[...] = s2[slot].astype(BF16)

    @pl.when(valid)
    def _():
        x = _unpack_bf16_pairs(x_ref[...]).astype(BF16)
        hmid = _silu(_dot(x, w1b[...])) * _dot(x, w3b[...])
        y_ref[...] = _pack_bf16_pairs(_dot(hmid.astype(BF16), w2b[...]))

    @pl.when(bv_ref[j] == 0)
    def _():
        y_ref[...] = jnp.zeros_like(y_ref)


def _moe_experts(xbuf, blk_e, blk_valid, blk_slot, blk_next, w1, w3, w2, layer):
    dm = w1.shape[2]
    ff = w1.shape[-1]
    rows = MOE_ROWS
    n_blocks = blk_e.shape[0]
    hbm = pl.BlockSpec(memory_space=pl.ANY)
    grid_spec = pltpu.PrefetchScalarGridSpec(
        num_scalar_prefetch=4,
        grid=(n_blocks,),
        in_specs=[
            pl.BlockSpec((rows, dm // 2), lambda j, be, bv, sl, nx: (j * bv[j], 0)),
            hbm, hbm, hbm,
        ],
        out_specs=pl.BlockSpec((rows, dm // 2), lambda j, be, bv, sl, nx: (j, 0)),
        scratch_shapes=[
            pltpu.VMEM((2, dm, ff), F32), pltpu.VMEM((2, dm, ff), F32), pltpu.VMEM((2, ff, dm), F32),
            pltpu.VMEM((dm, ff), BF16), pltpu.VMEM((dm, ff), BF16), pltpu.VMEM((ff, dm), BF16),
            pltpu.SemaphoreType.DMA((2,)),
        ],
    )
    return pl.pallas_call(
        functools.partial(_moe_kernel, layer=layer),
        out_shape=jax.ShapeDtypeStruct((n_blocks * rows, dm // 2), jnp.uint32),
        grid_spec=grid_spec,
        compiler_params=_cparams(("arbitrary",)),
        name="moe_experts",
    )(blk_e, blk_valid, blk_slot, blk_next, xbuf, w1, w3, w2)


def _expert_row_gather(pos_ref, y_hbm, ybuf, sem, tile_of_step):
    tm = ROW_TILE

    def row_copy(step, sl, r, k):
        p = pos_ref[(tile_of_step(step) * tm + r) * TOP_K + k]
        return pltpu.make_async_copy(y_hbm.at[p], ybuf.at[sl, k, r], sem.at[sl])

    def start_rows(step, sl, r0, r1):
        for r in range(r0, r1):
            for k in range(TOP_K):
                row_copy(step, sl, r, k).start(priority=k)

    def wait_rows(step, sl):
        for r in range(tm):
            for k in range(TOP_K):
                row_copy(step, sl, r, k).wait()

    return start_rows, wait_rows


def _gate_weighted(ybuf, slot, r_ref):
    route = r_ref[...]
    return (_unpack_bf16_pairs(ybuf[slot, 0]) * route[:, R_G1:R_G1 + 1]
            + _unpack_bf16_pairs(ybuf[slot, 1]) * route[:, R_G2:R_G2 + 1])


def _gathered_expert_rows(pos_ref, r_ref, y_hbm, ybuf, sem, *, n_tiles, tile_of_step):
    i = pl.program_id(0)
    slot = i % 2
    start_rows, wait_rows = _expert_row_gather(pos_ref, y_hbm, ybuf, sem, tile_of_step)

    @pl.when(i == 0)
    def _():
        start_rows(0, 0, 0, ROW_TILE)

    @pl.when(i + 1 < n_tiles)
    def _():
        start_rows(jnp.minimum(i + 1, n_tiles - 1), 1 - slot, 0, ROW_TILE)

    wait_rows(i, slot)
    return _gate_weighted(ybuf, slot, r_ref)


def _combine_kernel(pos_ref, x_ref, r_ref, g2_ref, lng_ref, lnb_ref, y_hbm, o_ref, ybuf, sem,
                    *, n_tiles, tile_of_step, alpha):
    y = _gathered_expert_rows(pos_ref, r_ref, y_hbm, ybuf, sem, n_tiles=n_tiles, tile_of_step=tile_of_step)
    z = alpha * x_ref[...] + g2_ref[0] * y
    o_ref[...] = _ln_rows(z) * lng_ref[...] + lnb_ref[...]


def _combine(x_new, route, ybuf, pos, g2, ln_g, ln_b, tiles_per_batch, alpha, latent_only, batch):
    n, dm = x_new.shape
    tm = ROW_TILE
    n_tiles, tile_of_step = _tile_schedule(n, tiles_per_batch, latent_only, batch)
    gmap = lambda i, p: (_group_of_tile(tile_of_step(i), tiles_per_batch), 0, 0)
    grid_spec = pltpu.PrefetchScalarGridSpec(
        num_scalar_prefetch=1,
        grid=(n_tiles,),
        in_specs=[
            pl.BlockSpec((tm, dm), lambda i, p: (tile_of_step(i), 0)),
            pl.BlockSpec((tm, LANES), lambda i, p: (tile_of_step(i), 0)),
            pl.BlockSpec((1, 1, dm), gmap),
            pl.BlockSpec((1, dm), lambda i, p: (0, 0)),
            pl.BlockSpec((1, dm), lambda i, p: (0, 0)),
            pl.BlockSpec(memory_space=pl.ANY),
        ],
        out_specs=pl.BlockSpec((tm, dm), lambda i, p: (i, 0)),
        scratch_shapes=[pltpu.VMEM((2, TOP_K, tm, dm // 2), jnp.uint32), pltpu.SemaphoreType.DMA((2,))],
    )
    return pl.pallas_call(
        functools.partial(_combine_kernel, n_tiles=n_tiles, tile_of_step=tile_of_step, alpha=alpha),
        out_shape=jax.ShapeDtypeStruct((n_tiles * tm, dm), F32),
        grid_spec=grid_spec,
        compiler_params=_cparams(("arbitrary",)),
        name="moe_combine_norm",
    )(pos, x_new, route, g2, ln_g, ln_b, ybuf)


def _combine_proj_kernel(pos_ref, x_ref, r_ref, g2_ref, lng_ref, lnb_ref, y_hbm, sh_ref, sc_ref, w_ref,
                         *rest, n_tiles, alpha):
    mla_in, (oa, ob, od, qf, kf, vt, so, ybuf, sem, oc) = rest[:7], rest[7:]
    i = pl.program_id(0)
    slot = i % 2
    tm = ROW_TILE
    start_rows, wait_rows = _expert_row_gather(pos_ref, y_hbm, ybuf, sem, lambda s: s)

    @pl.when(i == 0)
    def _():
        start_rows(0, 0, 0, tm)

    wait_rows(i, slot)
    z = alpha * x_ref[...] + g2_ref[0] * _gate_weighted(ybuf, slot, r_ref)
    stream = _ln_rows(z) * lng_ref[...] + lnb_ref[...]
    so[...] = stream
    outs = (oa, ob, oc, od)
    n_blk = len(_column_blocks(outs))
    per = -(-tm // n_blk)
    nxt = jnp.minimum(i + 1, n_tiles - 1)

    c_done = max(k for k, blk in enumerate(_column_blocks(outs)) if blk[0] is oc)

    def after_block(idx):
        start_rows(nxt, 1 - slot, min(idx * per, tm), min((idx + 1) * per, tm))
        if idx == c_done:
            _mla_up_rows(oc[...], *mla_in, qf, kf, vt)

    _project_rows(stream, sh_ref, sc_ref, w_ref, outs, between=after_block)

    @pl.when(i == n_tiles - 1)
    def _():
        wait_rows(nxt, 1 - slot)


def _combine_proj(x_new, route, ybuf, pos, g2, ln_g, ln_b, shift, scale, w_pad, mla_args,
                  tiles_per_batch, alpha):
    n, dm = x_new.shape
    tm = ROW_TILE
    n_tiles = n // tm
    gmap = lambda i, p: (_group_of_tile(i, tiles_per_batch), 0, 0)
    row = lambda w: pl.BlockSpec((tm, w), lambda i, p: (i, 0))
    f32 = lambda w: jax.ShapeDtypeStruct((n, w), F32)
    const = pl.BlockSpec((1, dm), lambda i, p: (0, 0))
    mod = pl.BlockSpec((1, 1, dm), gmap)
    mla_in, mla_out, mla_shapes = _mla_up_specs(n, mla_args)
    grid_spec = pltpu.PrefetchScalarGridSpec(
        num_scalar_prefetch=1,
        grid=(n_tiles,),
        in_specs=[row(dm), row(LANES), mod, const, const, pl.BlockSpec(memory_space=pl.ANY),
                  mod, mod, pl.BlockSpec(memory_space=pltpu.VMEM)] + mla_in,
        out_specs=[row(SEG_A), row(SEG_B), row(SEG_D)] + mla_out + [row(dm)],
        scratch_shapes=[pltpu.VMEM((2, TOP_K, tm, dm // 2), jnp.uint32), pltpu.SemaphoreType.DMA((2,)),
                        pltpu.VMEM((tm, SEG_C), F32)],
    )
    return pl.pallas_call(
        functools.partial(_combine_proj_kernel, n_tiles=n_tiles, alpha=alpha),
        out_shape=[f32(SEG_A), f32(SEG_B), f32(SEG_D)] + mla_shapes + [f32(dm)],
        grid_spec=grid_spec,
        compiler_params=_cparams(("arbitrary",), vmem=PROJ_VMEM_LIMIT_BYTES),
        name="moe_combine_ln_mod_inproj",
    )(pos, x_new, route, g2, ln_g, ln_b, ybuf, shift, scale, w_pad, *mla_args)


def _pad_cols(w, width):
    return jnp.pad(w, ((0, 0), (0, width - w.shape[1])))


def kernel(x, c, ctx, c_ctx, w_ada, b_ada, w_in, swa_sink, ret_decay_exp, ret_gn, mla_q_norm, mla_kv_norm,
           mla_w_uq, mla_w_ukv, hgrn_lb_logits, hgrn_gn, w_out, ln1_g, ln1_b, router_group, router_expert,
           moe_w1, moe_w3, moe_w2, ln2_g, ln2_b):
    batch, t_len, dm = x.shape
    l_len = ctx.shape[1]
    depth = w_in.shape[0]
    p_len = t_len + l_len
    n = batch * p_len
    assert batch == 2 and l_len == ROW_TILE and t_len % 256 == 0 and t_len % GRID_W == 0
    tiles_per_batch = p_len // ROW_TILE
    alpha = float((2 * depth) ** 0.25)

    cos_swa, sin_swa, cos_rows, sin_rows = _rope_tables(t_len, l_len, batch)
    mats = jnp.asarray(_hgrn_level_mats(), BF16)

    cvec = jnp.zeros((SUBLANES, dm), F32).at[0:batch].set(c).at[batch].set(c_ctx)
    mod = _ada(cvec, w_ada, b_ada)

    lbp = jax.nn.softmax(hgrn_lb_logits.astype(F32), axis=0)
    lower_bounds = jnp.cumsum(lbp, axis=0) - lbp[0]

    pending = None

    out = None
    for l in range(depth):
        need_ctx = l < depth - 1
        m6 = mod[l, :3].reshape(3, 6, dm)
        sh1, sc1, g1, sh2, sc2, g2 = (m6[:, k].reshape(3, 1, dm) for k in range(6))

        wq = mla_w_uq[l].reshape(MLA_Q_RANK, MLA_HEADS, MLA_NOPE + MLA_ROPE)
        wq = jnp.pad(wq, ((0, 0), (0, 0), (0, 2 * LANES - (MLA_NOPE + MLA_ROPE))))
        wq = wq.reshape(MLA_Q_RANK, MLA_HEADS * 2 * LANES).astype(BF16)
        wkv = mla_w_ukv[l].reshape(MLA_KV_RANK, MLA_HEADS, MLA_NOPE + MLA_V)
        wk = wkv[:, :, :MLA_NOPE].reshape(MLA_KV_RANK, -1).astype(BF16)
        wv_t = wkv[:, :, MLA_NOPE:].reshape(MLA_KV_RANK, -1).T.astype(BF16)
        mla_args = (mla_q_norm[l].reshape(1, -1), mla_kv_norm[l].reshape(1, -1), wq, wk, wv_t,
                    cos_rows, sin_rows)

        w_pad = _w_in_prep(w_in, l)
        if pending is None:
            pa, pb, pd, qf, kf, vt, stream = _proj(x, ctx, sh1, sc1, w_pad, mla_args, tiles_per_batch)
        else:
            pa, pb, pd, qf, kf, vt, stream = _combine_proj(*pending, sh1, sc1, w_pad, mla_args,
                                                           tiles_per_batch, alpha)

        cc = _mla_attn(qf, kf, vt, batch, t_len, l_len, need_ctx)
        a, b_f, b_b, d_f, d_b = _mixers(
            [_swa_part(pa, swa_sink[l], cos_swa, sin_swa, batch, t_len, l_len, need_ctx),
             _retention_part(pb, ret_decay_exp[l], batch, t_len, l_len),
             _hgrn_part(pd, lower_bounds[l], mats, batch, t_len, l_len)], n)

        w_r = _pad_cols(jnp.concatenate([router_group[l], router_expert[l]], axis=1), LANES).astype(BF16)
        x_new, h2, route, counts = _out_proj(
            stream, a, b_f, b_b, cc, d_f, d_b, pb, pd, g1, sh2, sc2,
            ln1_g[l].reshape(1, -1), ln1_b[l].reshape(1, -1),
            ret_gn[l].reshape(1, -1), hgrn_gn[l].reshape(1, -1),
            w_out[l].astype(BF16), w_r, tiles_per_batch, alpha, need_ctx)

        n_active = n if need_ctx else batch * t_len
        n_blocks = (n_active * TOP_K + N_EXPERTS * (MOE_ROWS - 1) + MOE_ROWS - 1) // MOE_ROWS
        pos, blk_e, blk_valid, blk_fill, blk_slot, blk_next = _route_plan(route, counts, n_blocks)
        xbuf = _dispatch(h2, pos, blk_fill, tiles_per_batch, not need_ctx, batch)
        ybuf = _moe_experts(xbuf, blk_e, blk_valid, blk_slot, blk_next, moe_w1, moe_w3, moe_w2, l)
        combine_args = (x_new, route, ybuf, pos, g2, ln2_g[l].reshape(1, -1), ln2_b[l].reshape(1, -1))
        if need_ctx:
            pending = combine_args
        else:
            out = _combine(*combine_args, tiles_per_batch, alpha, True, batch).reshape(batch, t_len, dm)
    return out
```

```python
import functools
import math

import numpy as np
import jax
import jax.numpy as jnp
from jax import lax
from jax.experimental import pallas as pl
from jax.experimental.pallas import tpu as pltpu

F32 = jnp.float32
BF16 = jnp.bfloat16

GRID_W = 64
HEAD_DIM = 128
SWA_HEADS = 4
SWA_KV_HEADS = 2
SWA_BLOCK = 128
RET_HEADS = 4
RET_DK = 128
MLA_HEADS = 4
MLA_Q_RANK = 384
MLA_KV_RANK = 128
MLA_NOPE = 128
MLA_ROPE = 64
MLA_V = 128
HGRN_HEADS = 4
N_GROUPS = 4
EXPERTS_PER_GROUP = 8
N_EXPERTS = N_GROUPS * EXPERTS_PER_GROUP
TOP_K = 2
ROPE_BASE = 10000.0
EPS = 1e-6

LANES = 128
SUBLANES = 8
VMEM_LIMIT_BYTES = 56 * 1024 * 1024
PROJ_VMEM_LIMIT_BYTES = 60 * 1024 * 1024

CHUNK = 128
ROW_TILE = 256
MOE_ROWS = 256
MLA_KEY_CHUNK = 256
MLA_HEADS_PER_STEP = 4
WIDE = 512

SEG_A = 1024
SEG_B = 2048
SEG_C = 640
SEG_D = 2560


def _cparams(sem, vmem=VMEM_LIMIT_BYTES):
    return pltpu.CompilerParams(dimension_semantics=sem, vmem_limit_bytes=vmem)


def _ln_rows(x):
    mu = jnp.mean(x, axis=-1, keepdims=True)
    xc = x - mu
    var = jnp.mean(xc * xc, axis=-1, keepdims=True)
    return xc * lax.rsqrt(var + EPS)


def _silu(x):
    return x * (1.0 / (1.0 + jnp.exp(-x)))


def _dot(a, b):
    return jnp.dot(a, b, preferred_element_type=F32)


def _pack_bf16_pairs(x):
    k = x.shape[1] // 2
    hi = lax.bitcast_convert_type(x[:, :k].astype(BF16).astype(F32), jnp.uint32)
    lo = lax.bitcast_convert_type(x[:, k:].astype(BF16).astype(F32), jnp.uint32)
    return hi | (lo >> 16)


def _unpack_bf16_pairs(w):
    hi = lax.bitcast_convert_type(w & jnp.uint32(0xFFFF0000), F32)
    lo = lax.bitcast_convert_type(w << 16, F32)
    return jnp.concatenate([hi, lo], axis=1)


def _dot_nt(a, b):
    return lax.dot_general(a, b, (((1,), (1,)), ((), ())), preferred_element_type=F32)


def _dot_tn(a, b):
    return lax.dot_general(a, b, (((0,), (0,)), ((), ())), preferred_element_type=F32)


def _ada_kernel(c_ref, w_ref, b_ref, o_ref):
    s = _silu(c_ref[...]).astype(BF16)
    o_ref[0] = _dot(s, w_ref[0].astype(BF16)) + b_ref[0]


def _ada(cvec, w_ada, b_ada):
    depth, d, n6 = w_ada.shape
    tn = n6 // 8
    return pl.pallas_call(
        _ada_kernel,
        out_shape=jax.ShapeDtypeStruct((depth, SUBLANES, n6), F32),
        grid=(depth, n6 // tn),
        in_specs=[
            pl.BlockSpec((SUBLANES, d), lambda l, j: (0, 0)),
            pl.BlockSpec((1, d, tn), lambda l, j: (l, 0, j)),
            pl.BlockSpec((1, 1, tn), lambda l, j: (l, 0, j)),
        ],
        out_specs=pl.BlockSpec((1, SUBLANES, tn), lambda l, j: (l, 0, j)),
        compiler_params=_cparams(("arbitrary", "arbitrary")),
        name="ada_mod",
    )(cvec, w_ada, b_ada.reshape(depth, 1, n6))


SEG_WIDTHS = (SEG_A, SEG_B, SEG_C, SEG_D)
SEG_C_TRUE = MLA_Q_RANK + MLA_KV_RANK + MLA_ROPE
IN_COLS = SEG_A + SEG_B + SEG_C_TRUE + SEG_D
IN_COLS_PAD = sum(SEG_WIDTHS)


PREP_ROWS = 256
_C_END = SEG_A + SEG_B + SEG_C_TRUE
_C_PAD = SEG_C - SEG_C_TRUE
assert _C_END % PREP_ROWS + _C_PAD <= PREP_ROWS and IN_COLS_PAD % PREP_ROWS in (0, PREP_ROWS // 2)


def _w_in_prep_kernel(w_ref, o_ref, prev_ref):
    j = pl.program_id(0)
    jc = _C_END // PREP_ROWS
    rem = _C_END % PREP_ROWS
    keep = PREP_ROWS - _C_PAD
    cur = w_ref[0].astype(BF16)

    @pl.when(j < jc)
    def _():
        o_ref[...] = cur

    @pl.when(j == jc)
    def _():
        o_ref[0:rem, :] = cur[0:rem]
        o_ref[rem:rem + _C_PAD, :] = jnp.zeros((_C_PAD, cur.shape[1]), BF16)
        o_ref[rem + _C_PAD:PREP_ROWS, :] = cur[rem:keep]

    @pl.when(j > jc)
    def _():
        o_ref[0:_C_PAD, :] = prev_ref[keep:PREP_ROWS, :]
        o_ref[_C_PAD:PREP_ROWS, :] = cur[0:keep]

    prev_ref[...] = cur


def _w_in_prep(w_in, layer):
    _, d, cols = w_in.shape
    assert cols == IN_COLS
    w_t = jnp.swapaxes(w_in, 1, 2)
    return pl.pallas_call(
        _w_in_prep_kernel,
        out_shape=jax.ShapeDtypeStruct((IN_COLS_PAD, d), BF16),
        grid=(pl.cdiv(IN_COLS_PAD, PREP_ROWS),),
        in_specs=[pl.BlockSpec((1, PREP_ROWS, d), lambda j: (layer, j, 0))],
        out_specs=pl.BlockSpec((PREP_ROWS, d), lambda j: (j, 0)),
        scratch_shapes=[pltpu.VMEM((PREP_ROWS, d), BF16)],
        compiler_params=_cparams(("arbitrary",)),
        name="w_in_prep",
    )(w_t)


def _column_blocks(outs):
    return [(o_ref, j, min(WIDE, o_ref.shape[1] - j)) for o_ref in outs for j in range(0, o_ref.shape[1], WIDE)]


def _project_rows(x, sh_ref, sc_ref, w_ref, outs, between=None):
    y = _ln_rows(x)
    h = (y * (1.0 + sc_ref[0]) + sh_ref[0]).astype(BF16)
    base = 0
    for idx, (o_ref, j, cw) in enumerate(_column_blocks(outs)):
        o_ref[:, j:j + cw] = _dot_nt(h, w_ref[base:base + cw, :])
        base += cw
        if between is not None:
            between(idx)


def _proj_kernel(x_ref, ctx_ref, sh_ref, sc_ref, w_ref, *rest, tiles_per_batch):
    mla_in, (oa, ob, od, qf, kf, vt, so, pc_buf) = rest[:7], rest[7:]
    is_ctx = pl.program_id(0) % tiles_per_batch == 0
    x = jnp.where(is_ctx, ctx_ref[0], x_ref[0])
    so[...] = x
    outs = (oa, ob, pc_buf, od)
    c_done = max(k for k, blk in enumerate(_column_blocks(outs)) if blk[0] is pc_buf)

    def after_block(idx):
        if idx == c_done:
            _mla_up_rows(pc_buf[...], *mla_in, qf, kf, vt)

    _project_rows(x, sh_ref, sc_ref, w_ref, outs, between=after_block)


def _group_of_tile(i, tiles_per_batch):
    return jnp.where(i % tiles_per_batch == 0, 2, i // tiles_per_batch)


def _proj(x, ctx, shift, scale, w_pad, mla_args, tiles_per_batch):
    tm = ROW_TILE
    batch, t_len, d = x.shape
    n = batch * (t_len + ctx.shape[1])
    tpb = tiles_per_batch
    gmap = lambda i: (_group_of_tile(i, tiles_per_batch), 0, 0)
    row = lambda w: pl.BlockSpec((tm, w), lambda i: (i, 0))
    f32 = lambda w: jax.ShapeDtypeStruct((n, w), F32)
    mla_in, mla_out, mla_shapes = _mla_up_specs(n, mla_args)
    return pl.pallas_call(
        functools.partial(_proj_kernel, tiles_per_batch=tiles_per_batch),
        out_shape=[f32(SEG_A), f32(SEG_B), f32(SEG_D)] + mla_shapes + [f32(d)],
        grid=(n // tm,),
        in_specs=[
            pl.BlockSpec((1, tm, d), lambda i: (i // tpb, jnp.maximum(i % tpb - 1, 0), 0)),
            pl.BlockSpec((1, tm, d), lambda i: (i // tpb, 0, 0)),
            pl.BlockSpec((1, 1, d), gmap),
            pl.BlockSpec((1, 1, d), gmap),
            pl.BlockSpec(memory_space=pltpu.VMEM),
        ] + mla_in,
        out_specs=[row(SEG_A), row(SEG_B), row(SEG_D)] + mla_out + [row(d)],
        scratch_shapes=[pltpu.VMEM((tm, SEG_C), F32)],
        compiler_params=_cparams(("arbitrary",), vmem=PROJ_VMEM_LIMIT_BYTES),
        name="ln_mod_inproj",
    )(x, ctx, shift, scale, w_pad, *mla_args)


def _rope128(x, cos, sin):
    return x * cos + pltpu.roll(x, 64, 1) * sin


def _rope64(x, cos, sin):
    lane = lax.broadcasted_iota(jnp.int32, x.shape, 1)
    rot = jnp.where((lane % 64) < 32, pltpu.roll(x, 96, 1), pltpu.roll(x, 32, 1))
    return x * cos + rot * sin


def _rope_tables(t_len, l_len, batch):
    rows = t_len // GRID_W
    row = np.repeat(np.arange(rows), GRID_W).astype(np.float32)
    col = np.tile(np.arange(GRID_W), rows).astype(np.float32)

    def angles(rot_dim):
        n_freq = rot_dim // 4
        inv = (ROPE_BASE ** (-np.arange(n_freq, dtype=np.float32) / n_freq)).astype(np.float32)
        return np.concatenate([row[:, None] * inv, col[:, None] * inv], -1).astype(np.float32)

    a_swa = angles(HEAD_DIM)
    cos_swa = np.concatenate([np.cos(a_swa), np.cos(a_swa)], -1)
    sin_swa = np.concatenate([-np.sin(a_swa), np.sin(a_swa)], -1)
    a_mla = angles(MLA_ROPE)
    cos_m = np.concatenate([np.cos(a_mla), np.cos(a_mla), np.ones((t_len, 64), np.float32)], -1)
    sin_m = np.concatenate([-np.sin(a_mla), np.sin(a_mla), np.zeros((t_len, 64), np.float32)], -1)
    ones = np.ones((l_len, LANES), np.float32)
    zeros = np.zeros((l_len, LANES), np.float32)
    cos_rows = np.concatenate([np.concatenate([ones, cos_m], 0)] * batch, 0)
    sin_rows = np.concatenate([np.concatenate([zeros, sin_m], 0)] * batch, 0)
    return (jnp.asarray(cos_swa, F32), jnp.asarray(sin_swa, F32),
            jnp.asarray(cos_rows, F32), jnp.asarray(sin_rows, F32))


def _mla_up_rows(pc, qn_ref, kvn_ref, wq_ref, wk_ref, wv_ref, cos_ref, sin_ref, q_out, k_out, v_out):
    scale = float((MLA_NOPE + MLA_ROPE) ** -0.5 * math.log2(math.e))
    cq = pc[:, :MLA_Q_RANK]
    ckv = pc[:, MLA_Q_RANK:MLA_Q_RANK + MLA_KV_RANK]
    kr = pc[:, MLA_Q_RANK + MLA_KV_RANK:]
    cos = cos_ref[...]
    sin = sin_ref[...]

    def rms(x, g):
        return x * lax.rsqrt(jnp.mean(x * x, axis=-1, keepdims=True) + EPS) * g

    qh = _dot(rms(cq, qn_ref[...]).astype(BF16), wq_ref[...])
    ckn = rms(ckv, kvn_ref[...]).astype(BF16)
    kh = _dot(ckn, wk_ref[...])
    v_out[...] = _dot_nt(wv_ref[...], ckn).astype(BF16)
    kr_rot = _rope64(kr, cos, sin).astype(BF16)
    for h in range(MLA_HEADS):
        base = h * 2 * LANES
        q_out[:, base:base + LANES] = (qh[:, base:base + LANES] * scale).astype(BF16)
        q_out[:, base + LANES:base + 2 * LANES] = (
            _rope64(qh[:, base + LANES:base + 2 * LANES], cos, sin) * scale).astype(BF16)
        k_out[:, base:base + LANES] = kh[:, h * LANES:(h + 1) * LANES].astype(BF16)
        k_out[:, base + LANES:base + 2 * LANES] = kr_rot


def _mla_up_specs(n, mla_args):
    tm = ROW_TILE
    const2 = lambda i, *_: (0, 0)
    rows = lambda i, *_: (i, 0)
    in_specs = [pl.BlockSpec(a.shape, const2) for a in mla_args[:5]]
    in_specs += [pl.BlockSpec((tm, LANES), rows), pl.BlockSpec((tm, LANES), rows)]
    qk_w = MLA_HEADS * 2 * LANES
    out_specs = [pl.BlockSpec((tm, qk_w), rows), pl.BlockSpec((tm, qk_w), rows),
                 pl.BlockSpec((MLA_HEADS * MLA_V, tm), lambda i, *_: (0, i))]
    out_shapes = [jax.ShapeDtypeStruct((n, qk_w), BF16), jax.ShapeDtypeStruct((n, qk_w), BF16),
                  jax.ShapeDtypeStruct((MLA_HEADS * MLA_V, n), BF16)]
    return in_specs, out_specs, out_shapes


def _mla_attn_kernel(q_ref, k_ref, vt_ref, o_ref, s_ref, p_ref, *, kc, l_len, need_ctx):
    p_len = k_ref.shape[0]
    tq = q_ref.shape[0]
    dqk = 2 * LANES
    heads = q_ref.shape[1] // dqk
    half = p_len // 2
    is_ctx = pl.program_id(2) == 0

    @pl.when(is_ctx)
    def _():
        if not need_ctx:
            o_ref[...] = jnp.zeros_like(o_ref)
            return
        for h in range(heads):
            hq = slice(h * dqk, (h + 1) * dqk)
            hv = slice(h * MLA_V, (h + 1) * MLA_V)
            s = _dot_nt(k_ref[0:l_len, hq], q_ref[:, hq])
            p = jnp.exp2(s - jnp.max(s, axis=0, keepdims=True))
            acc = _dot(vt_ref[hv, 0:l_len], p.astype(BF16))
            o_ref[:, hv] = (acc / jnp.sum(p, axis=0, keepdims=True)).T

    @pl.when(jnp.logical_not(is_ctx))
    def _():
        fold = lambda a: a.reshape(kc // SUBLANES, SUBLANES, tq)
        m8s = []
        for h in range(heads):
            q = q_ref[:, h * dqk:(h + 1) * dqk]
            m8 = None
            for off in range(0, p_len, kc):
                s = _dot_nt(k_ref[off:off + kc, h * dqk:(h + 1) * dqk], q)
                s_ref[h, off:off + kc, :] = s
                cm = jnp.max(fold(s), axis=0)
                m8 = cm if m8 is None else jnp.maximum(m8, cm)
            m8s.append(m8)
        for h in range(heads):
            m = jnp.max(m8s[h], axis=0, keepdims=True)
            l8 = jnp.zeros((SUBLANES, tq), F32)
            for off in range(0, p_len, kc):
                p = jnp.exp2(s_ref[h, off:off + kc, :] - m)
                l8 = l8 + jnp.sum(fold(p), axis=0)
                p_ref[h, off:off + kc, :] = p.astype(BF16)
            l = jnp.sum(l8, axis=0, keepdims=True)
            hv = slice(h * MLA_V, (h + 1) * MLA_V)
            acc = (_dot(vt_ref[hv, 0:half], p_ref[h, 0:half, :])
                   + _dot(vt_ref[hv, half:p_len], p_ref[h, half:p_len, :]))
            o_ref[:, hv] = (acc / l).T


def _mla_attn(qf, kf, vt, batch, t_len, l_len, need_ctx):
    n = qf.shape[0]
    p_len = t_len + l_len
    tq = ROW_TILE
    assert l_len == tq
    kc = MLA_KEY_CHUNK
    hp = MLA_HEADS_PER_STEP
    tpb = p_len // tq
    return pl.pallas_call(
        functools.partial(_mla_attn_kernel, kc=kc, l_len=l_len, need_ctx=need_ctx),
        out_shape=jax.ShapeDtypeStruct((n, MLA_HEADS * MLA_V), F32),
        grid=(batch, MLA_HEADS // hp, tpb),
        in_specs=[
            pl.BlockSpec((tq, hp * 2 * LANES), lambda b, h, i: (b * tpb + i, h)),
            pl.BlockSpec((p_len, hp * 2 * LANES), lambda b, h, i: (b, h), pipeline_mode=pl.Buffered(1)),
            pl.BlockSpec((hp * MLA_V, p_len), lambda b, h, i: (h, b), pipeline_mode=pl.Buffered(1)),
        ],
        out_specs=pl.BlockSpec((tq, hp * MLA_V), lambda b, h, i: (b * tpb + i, h)),
        scratch_shapes=[pltpu.VMEM((hp, p_len, tq), F32), pltpu.VMEM((hp, p_len, tq), BF16)],
        compiler_params=_cparams(("arbitrary", "arbitrary", "arbitrary")),
        name="mla_attn",
    )(qf, kf, vt)


def _swa_kernel(sink_ref, q_ref, kp_ref, kc_ref, kn_ref, vp_ref, vc_ref, vn_ref, kx_ref, vx_ref,
                cq_ref, sq_ref, cp_ref, sp_ref, cn_ref, sn_ref, o_ref, *, nb, n_ctx, scale, need_ctx):
    j = pl.program_id(0)
    n = j - n_ctx
    blk = SWA_BLOCK
    batch = q_ref.shape[0]
    rr = lax.broadcasted_iota(jnp.int32, (2 * blk, 1), 0)
    mx = lambda a: jnp.max(a, axis=-1, keepdims=True)
    sm = lambda a: jnp.sum(a, axis=-1, keepdims=True)

    def heads_of(b, h):
        q0 = q_ref[b, :, 2 * h * LANES:(2 * h + 1) * LANES]
        q1 = q_ref[b, :, (2 * h + 1) * LANES:(2 * h + 2) * LANES]
        sk = jnp.where(rr < blk, sink_ref[2 * h], sink_ref[2 * h + 1]).astype(F32)
        return q0, q1, sk

    def store(b, h, o):
        o_ref[b, :, 2 * h * LANES:(2 * h + 1) * LANES] = o[:blk]
        o_ref[b, :, (2 * h + 1) * LANES:(2 * h + 2) * LANES] = o[blk:]

    @pl.when(j < n_ctx)
    def _():
        if not need_ctx:
            o_ref[...] = jnp.zeros_like(o_ref)
            return
        for b in range(batch):
            for h in range(SWA_KV_HEADS):
                ks = slice(h * LANES, (h + 1) * LANES)
                q0, q1, sk = heads_of(b, h)
                qq = jnp.concatenate([q0, q1], axis=0).astype(BF16)
                s_x = _dot_nt(qq, kx_ref[b, :, ks].astype(BF16)) * scale
                m = jnp.maximum(mx(s_x), sk)
                p_x = jnp.exp(s_x - m)
                den = sm(p_x) + jnp.exp(sk - m)
                store(b, h, _dot(p_x.astype(BF16), vx_ref[b, :, ks].astype(BF16)) / den)

    @pl.when(j >= n_ctx)
    def _():
        cq, sq = cq_ref[...], sq_ref[...]
        cp, sp = cp_ref[...], sp_ref[...]
        cn, sn = cn_ref[...], sn_ref[...]
        ri = lax.broadcasted_iota(jnp.int32, (2 * blk, blk), 0) % blk
        ci = lax.broadcasted_iota(jnp.int32, (2 * blk, blk), 1)
        mask_p = (ci >= ri) & (n > 0)
        mask_n = (ci <= ri) & (n < nb - 1)
        neg = jnp.float32(-jnp.inf)
        for b in range(batch):
            for h in range(SWA_KV_HEADS):
                ks = slice(h * LANES, (h + 1) * LANES)
                q0, q1, sk = heads_of(b, h)
                qq = jnp.concatenate([_rope128(q0, cq, sq), _rope128(q1, cq, sq)], axis=0).astype(BF16)
                kc = _rope128(kc_ref[b, :, ks], cq, sq).astype(BF16)
                kp = _rope128(kp_ref[b, :, ks], cp, sp).astype(BF16)
                kn = _rope128(kn_ref[b, :, ks], cn, sn).astype(BF16)
                s_c = _dot_nt(qq, kc) * scale
                s_p = jnp.where(mask_p, _dot_nt(qq, kp) * scale, neg)
                s_n = jnp.where(mask_n, _dot_nt(qq, kn) * scale, neg)
                s_x = _dot_nt(qq, kx_ref[b, :, ks].astype(BF16)) * scale
                m = jnp.maximum(jnp.maximum(mx(s_c), mx(s_p)), jnp.maximum(mx(s_n), mx(s_x)))
                m = jnp.maximum(m, sk)
                p_c, p_p, p_n, p_x = (jnp.exp(a - m) for a in (s_c, s_p, s_n, s_x))
                den = sm(p_c) + sm(p_p) + sm(p_n) + sm(p_x) + jnp.exp(sk - m)
                store(b, h, (_dot(p_c.astype(BF16), vc_ref[b, :, ks].astype(BF16))
                             + _dot(p_p.astype(BF16), vp_ref[b, :, ks].astype(BF16))
                             + _dot(p_n.astype(BF16), vn_ref[b, :, ks].astype(BF16))
                             + _dot(p_x.astype(BF16), vx_ref[b, :, ks].astype(BF16))) / den)


def _swa_part(pa, sink, cos_t, sin_t, batch, t_len, l_len, need_ctx):
    p_len = t_len + l_len
    blk = SWA_BLOCK
    nb = t_len // blk
    n_ctx = l_len // blk
    scale = float(HEAD_DIM ** -0.5)
    kv_w = SWA_KV_HEADS * HEAD_DIM
    q_w = SWA_HEADS * HEAD_DIM
    kcol, vcol = q_w // kv_w, q_w // kv_w + 1
    pa3 = pa.reshape(batch, p_len, pa.shape[1])
    lat = lambda j: jnp.maximum(j - n_ctx, 0)
    same = lambda j: lat(j)
    prv = lambda j: jnp.maximum(lat(j) - 1, 0)
    nxt = lambda j: jnp.minimum(lat(j) + 1, nb - 1)
    kv = lambda col, rowf: pl.BlockSpec((batch, blk, kv_w), lambda j: (0, n_ctx + rowf(j), col))
    tab = lambda rowf: pl.BlockSpec((blk, LANES), lambda j: (rowf(j), 0))
    in_specs = [
        pl.BlockSpec(memory_space=pltpu.SMEM),
        pl.BlockSpec((batch, blk, q_w), lambda j: (0, j, 0)),
        kv(kcol, prv), kv(kcol, same), kv(kcol, nxt),
        kv(vcol, prv), kv(vcol, same), kv(vcol, nxt),
        pl.BlockSpec((batch, l_len, kv_w), lambda j: (0, 0, kcol)),
        pl.BlockSpec((batch, l_len, kv_w), lambda j: (0, 0, vcol)),
        tab(same), tab(same), tab(prv), tab(prv), tab(nxt), tab(nxt),
    ]
    return dict(
        kernel=functools.partial(_swa_kernel, nb=nb, n_ctx=n_ctx, scale=scale, need_ctx=need_ctx),
        in_specs=in_specs,
        args=[sink] + [pa3] * 9 + [cos_t, sin_t] * 3,
        out_specs=[pl.BlockSpec((batch, blk, q_w), lambda j: (0, j, 0))],
        out_shapes=[jax.ShapeDtypeStruct((batch, p_len, q_w), F32)],
        scratch=[],
        steps=n_ctx + nb)


def _chunk_index(d, c, n_l, n_t):
    bwd = jnp.where(c < n_l, n_l - 1 - c, n_l + (n_t - 1) - (c - n_l))
    return jnp.where(d == 0, c, bwd)


def _flip_iotas(d):
    row = lax.broadcasted_iota(jnp.int32, (CHUNK, CHUNK), 0)
    col = lax.broadcasted_iota(jnp.int32, (CHUNK, CHUNK), 1)
    rf = jnp.where(d == 0, row, CHUNK - 1 - row)
    cf = jnp.where(d == 0, col, CHUNK - 1 - col)
    return rf, cf


def _ret_kernel(s_ref, qf_ref, kf_ref, vf_ref, qb_ref, kb_ref, vb_ref, of_ref, ob_ref, st_ref):
    c = pl.program_id(0)
    batch = qf_ref.shape[0]

    @pl.when(c == 0)
    def _():
        st_ref[...] = jnp.zeros_like(st_ref)

    ks = float(RET_DK ** -0.5)
    dirs = ((qf_ref, kf_ref, vf_ref, of_ref), (qb_ref, kb_ref, vb_ref, ob_ref))
    for d, (q_ref, k_ref, v_ref, o_ref) in enumerate(dirs):
        rf, cf = _flip_iotas(d)
        rff = rf.astype(F32)
        dn = (rf - cf).astype(F32)
        for h in range(RET_HEADS):
            hs = slice(h * LANES, (h + 1) * LANES)
            sv = jnp.full((CHUNK, CHUNK), s_ref[d * RET_HEADS + h], F32)
            lg = jnp.log1p(-jnp.exp2(-sv))
            dec = jnp.where(dn >= 0, jnp.exp(dn * lg), 0.0)
            eq = jnp.exp((rff + 1.0) * lg)
            ek = jnp.exp((CHUNK - 1.0 - rff) * lg)
            a_chunk = jnp.exp(float(CHUNK) * lg)
            for b in range(batch):
                q = q_ref[b, :, hs]
                k = k_ref[b, :, hs] * ks
                v = v_ref[b, :, hs].astype(BF16)
                st = st_ref[d, b, h]
                a = _dot_nt(q.astype(BF16), k.astype(BF16)) * dec
                o = _dot(a.astype(BF16), v) + _dot_nt((q * eq).astype(BF16), st.astype(BF16))
                o_ref[b, :, hs] = o
                st_ref[d, b, h] = a_chunk * st + _dot_tn(v, (k * ek).astype(BF16))


def _hgrn_level_mats():
    c = CHUNK
    t = np.arange(c)[:, None]
    u = np.arange(c)[None, :]
    mats = [u <= t, u > t]
    m = c // 2
    while m >= 1:
        mid = (t // (2 * m)) * 2 * m + m
        second = (t % (2 * m)) >= m
        qrole = (u >= mid) & (u <= t)
        krole = (u > t) & (u <= mid - 1)
        mats.append(np.where(second, qrole, krole))
        m //= 2
    fwd = np.concatenate(mats, 0).astype(np.float32)
    bwd = np.concatenate([mm[::-1, ::-1] for mm in mats], 0).astype(np.float32)
    return np.stack([fwd, bwd], 0)


N_LEVELS = int(math.log2(CHUNK))


def _hgrn_kernel(m_ref, lb_ref, qf_ref, ff_ref, vf_ref, qb_ref, fb_ref, vb_ref, of_ref, ob_ref, st_ref):
    c = pl.program_id(0)
    batch = qf_ref.shape[0]

    @pl.when(c == 0)
    def _():
        st_ref[...] = jnp.zeros_like(st_ref)

    dirs = ((qf_ref, ff_ref, vf_ref, of_ref), (qb_ref, fb_ref, vb_ref, ob_ref))
    for d, (q_ref, f_ref, v_ref, o_ref) in enumerate(dirs):
        rf, cf = _flip_iotas(d)
        mst = m_ref[d]
        last = CHUNK - 1 if d == 0 else 0
        lvl_masks = []
        for lvl in range(N_LEVELS):
            m = CHUNK >> (lvl + 1)
            sh = N_LEVELS - lvl
            lvl_masks.append(((rf >> sh) == (cf >> sh)) & ((rf & m) != 0) & ((cf & m) == 0))
        diag = rf == cf
        for h in range(HGRN_HEADS):
            hs = slice(h * LANES, (h + 1) * LANES)
            lb = lb_ref[d, :, hs]
            for b in range(batch):
                f = lb + (1.0 - lb) * (1.0 / (1.0 + jnp.exp(-f_ref[b, :, hs])))
                k = 1.0 - f
                g = jnp.log(f)
                q = _silu(q_ref[b, :, hs])
                v = v_ref[b, :, hs].astype(BF16)
                e = jnp.exp(_dot(mst, g.astype(BF16)))
                e_q = e[0:CHUNK]
                q_in = q * e_q
                k_st = k * e[CHUNK:2 * CHUNK]
                scores = jnp.where(diag, _dot_nt(q.astype(BF16), k.astype(BF16)), 0.0)
                for lvl in range(N_LEVELS):
                    el = e[(2 + lvl) * CHUNK:(3 + lvl) * CHUNK]
                    sl = _dot_nt((q * el).astype(BF16), (k * el).astype(BF16))
                    scores = scores + jnp.where(lvl_masks[lvl], sl, 0.0)
                st = st_ref[d, b, h]
                o = _dot(scores.astype(BF16), v) + _dot_nt(q_in.astype(BF16), st.astype(BF16))
                o_ref[b, :, hs] = o
                st_ref[d, b, h] = st * e_q[last:last + 1, :] + _dot_tn(v, k_st.astype(BF16))


def _scan_specs(batch, t_len, l_len):
    n_l, n_t = l_len // CHUNK, t_len // CHUNK

    def spec(d, col):
        return pl.BlockSpec((batch, CHUNK, WIDE), lambda c: (0, _chunk_index(d, c, n_l, n_t), col))

    return spec, n_l + n_t


def _retention_part(pb, ret_s, batch, t_len, l_len):
    p_len = t_len + l_len
    pb3 = pb.reshape(batch, p_len, pb.shape[1])
    spec, n_chunks = _scan_specs(batch, t_len, l_len)
    return dict(
        kernel=_ret_kernel,
        in_specs=[pl.BlockSpec(memory_space=pltpu.SMEM)]
                 + [spec(0, col) for col in (0, 1, 2)] + [spec(1, col) for col in (0, 1, 2)],
        args=[ret_s.reshape(-1)] + [pb3] * 6,
        out_specs=[spec(0, 0), spec(1, 0)],
        out_shapes=[jax.ShapeDtypeStruct((batch, p_len, WIDE), F32)] * 2,
        scratch=[pltpu.VMEM((2, batch, RET_HEADS, CHUNK, CHUNK), F32)],
        steps=n_chunks)


def _hgrn_part(pd, lower_bounds, mats, batch, t_len, l_len):
    p_len = t_len + l_len
    pd3 = pd.reshape(batch, p_len, pd.shape[1])
    lb3 = lower_bounds.reshape(2, 1, WIDE)
    spec, n_chunks = _scan_specs(batch, t_len, l_len)
    return dict(
        kernel=_hgrn_kernel,
        in_specs=[pl.BlockSpec(mats.shape, lambda c: (0, 0, 0)),
                  pl.BlockSpec(lb3.shape, lambda c: (0, 0, 0)),
                  spec(0, 0), spec(0, 1), spec(0, 3), spec(1, 0), spec(1, 2), spec(1, 3)],
        args=[mats, lb3] + [pd3] * 6,
        out_specs=[spec(0, 0), spec(1, 0)],
        out_shapes=[jax.ShapeDtypeStruct((batch, p_len, WIDE), F32)] * 2,
        scratch=[pltpu.VMEM((2, batch, HGRN_HEADS, CHUNK, CHUNK), F32)],
        steps=n_chunks)


def _mixers(parts, n):
    steps = parts[0]["steps"]
    assert all(p["steps"] == steps for p in parts)
    n_in = [len(p["args"]) for p in parts]
    n_out = [len(p["out_shapes"]) for p in parts]
    n_scr = [len(p["scratch"]) for p in parts]

    def kern(*refs):
        ins, outs, scr = refs[:sum(n_in)], refs[sum(n_in):sum(n_in) + sum(n_out)], refs[sum(n_in) + sum(n_out):]
        a = b = c = 0
        for p, ni, no, ns in zip(parts, n_in, n_out, n_scr):
            p["kernel"](*ins[a:a + ni], *outs[b:b + no], *scr[c:c + ns])
            a, b, c = a + ni, b + no, c + ns

    outs = pl.pallas_call(
        kern,
        out_shape=[s for p in parts for s in p["out_shapes"]],
        grid=(steps,),
        in_specs=[s for p in parts for s in p["in_specs"]],
        out_specs=[s for p in parts for s in p["out_specs"]],
        scratch_shapes=[s for p in parts for s in p["scratch"]],
        compiler_params=_cparams(("arbitrary",)),
        name="swa_retention_hgrn",
    )(*[a for p in parts for a in p["args"]])
    return [o.reshape(n, o.shape[-1]) for o in outs]


R_E1, R_E2, R_G1, R_G2, R_S1, R_S2 = 0, 1, 2, 3, 4, 5


def _route(logits, count_ref, active):
    tm = logits.shape[0]
    lane = lax.broadcasted_iota(jnp.int32, logits.shape, 1)
    lanef = lane.astype(F32)
    big = jnp.float32(1e9)
    neg = jnp.float32(-jnp.inf)
    mx = lambda a: jnp.max(a, axis=-1, keepdims=True)
    mn = lambda a: jnp.min(a, axis=-1, keepdims=True)
    sm = lambda a: jnp.sum(a, axis=-1, keepdims=True)
    gl = jnp.where(lane < N_GROUPS, logits, neg)
    gm = mx(gl)
    p_grp = 1.0 / sm(jnp.exp(gl - gm))
    grp = mn(jnp.where(gl == gm, lanef, big))
    lo = N_GROUPS + grp * EXPERTS_PER_GROUP
    ing = (lanef >= lo) & (lanef < lo + EXPERTS_PER_GROUP)
    el = jnp.where(ing, logits, neg)
    l1 = mx(el)
    i1 = mn(jnp.where(el == l1, lanef, big))
    el2 = jnp.where(lanef == i1, neg, el)
    l2 = mx(el2)
    i2 = mn(jnp.where(el2 == l2, lanef, big))
    r = jnp.exp(l2 - l1)
    g1 = p_grp / (1.0 + r)
    g2 = p_grp * r / (1.0 + r)
    e1 = i1 - N_GROUPS
    e2 = i2 - N_GROUPS
    oh1 = lanef == e1
    oh2 = lanef == e2
    both = jnp.where(oh1 | oh2, active, 0.0)
    ri = lax.broadcasted_iota(jnp.int32, (tm, tm), 0)
    ci = lax.broadcasted_iota(jnp.int32, (tm, tm), 1)
    earlier = jnp.where(ci < ri, 1.0, 0.0).astype(BF16)
    before = _dot(earlier, both.astype(BF16)) + count_ref[0:1, :]
    s1 = sm(jnp.where(oh1, before, 0.0))
    s2 = sm(jnp.where(oh2, before, 0.0))
    count_ref[...] = count_ref[...] + jnp.sum(both, axis=0, keepdims=True)
    out = jnp.zeros(logits.shape, F32)
    for ln, val in ((R_E1, e1), (R_E2, e2), (R_G1, g1), (R_G2, g2), (R_S1, s1), (R_S2, s2)):
        out = jnp.where(lane == ln, val, out)
    return out


def _out_kernel(x_ref, a_ref, bf_ref, bb_ref, c_ref, df_ref, db_ref, rg_ref, hg_ref, g1_ref, sh2_ref, sc2_ref,
                lng_ref, lnb_ref, rgn_ref, hgn_ref, wo_ref, wr_ref,
                xo_ref, h2_ref, r_ref, cnt_ref, *, alpha, tiles_per_batch, route_ctx):
    i = pl.program_id(0)

    @pl.when(i == 0)
    def _():
        cnt_ref[...] = jnp.zeros_like(cnt_ref)

    bsum = bf_ref[...] + bb_ref[...]
    dsum = df_ref[...] + db_ref[...]
    rgn = rgn_ref[...]
    hgn = hgn_ref[...]
    parts_b, parts_d = [], []
    for h in range(4):
        hs = slice(h * LANES, (h + 1) * LANES)
        parts_b.append(_ln_rows(bsum[:, hs]))
        dh = dsum[:, hs]
        parts_d.append(dh * lax.rsqrt(jnp.mean(dh * dh, axis=-1, keepdims=True) + EPS))
    bo = _silu(rg_ref[...]) * (jnp.concatenate(parts_b, axis=1) * rgn)
    do = _silu(hg_ref[...]) * (jnp.concatenate(parts_d, axis=1) * hgn)
    o = (_dot(a_ref[...].astype(BF16), wo_ref[0:WIDE, :])
         + _dot(c_ref[...].astype(BF16), wo_ref[2 * WIDE:3 * WIDE, :])
         + _dot(bo.astype(BF16), wo_ref[WIDE:2 * WIDE, :])
         + _dot(do.astype(BF16), wo_ref[3 * WIDE:4 * WIDE, :]))
    y = alpha * x_ref[...] + g1_ref[0] * o
    xn = _ln_rows(y) * lng_ref[...] + lnb_ref[...]
    xo_ref[...] = xn
    h2 = _ln_rows(xn) * (1.0 + sc2_ref[0]) + sh2_ref[0]
    h2_ref[...] = _pack_bf16_pairs(h2)
    if route_ctx:
        active = jnp.float32(1.0)
    else:
        active = jnp.where(i % tiles_per_batch == 0, 0.0, 1.0).astype(F32)
    r_ref[...] = _route(_dot(h2.astype(BF16), wr_ref[...]), cnt_ref, active)


def _out_proj(stream, a, b_f, b_b, c, d_f, d_b, pb, pd, g1, sh2, sc2, ln_g, ln_b, ret_gn, hgrn_gn, w_out, w_r,
              tiles_per_batch, alpha, route_ctx):
    n, dm = stream.shape
    tm = ROW_TILE
    gmap = lambda i: (_group_of_tile(i, tiles_per_batch), 0, 0)
    row = lambda w: pl.BlockSpec((tm, w), lambda i: (i, 0))
    const = lambda w: pl.BlockSpec((1, w), lambda i: (0, 0))
    resident = pl.BlockSpec(memory_space=pltpu.VMEM)
    return pl.pallas_call(
        functools.partial(_out_kernel, alpha=alpha, tiles_per_batch=tiles_per_batch, route_ctx=route_ctx),
        out_shape=[jax.ShapeDtypeStruct((n, dm), F32), jax.ShapeDtypeStruct((n, dm // 2), jnp.uint32),
                   jax.ShapeDtypeStruct((n, LANES), F32), jax.ShapeDtypeStruct((SUBLANES, LANES), F32)],
        grid=(n // tm,),
        in_specs=[row(dm), row(WIDE), row(WIDE), row(WIDE), row(WIDE), row(WIDE), row(WIDE),
                  pl.BlockSpec((tm, WIDE), lambda i: (i, 3)),
                  pl.BlockSpec((tm, WIDE), lambda i: (i, 4)),
                  pl.BlockSpec((1, 1, dm), gmap), pl.BlockSpec((1, 1, dm), gmap),
                  pl.BlockSpec((1, 1, dm), gmap),
                  const(dm), const(dm), const(WIDE), const(WIDE), resident, resident],
        out_specs=[row(dm), row(dm // 2), row(LANES),
                   pl.BlockSpec((SUBLANES, LANES), lambda i: (0, 0))],
        compiler_params=_cparams(("arbitrary",)),
        name="mix_outproj_norm_route",
    )(stream, a, b_f, b_b, c, d_f, d_b, pb, pd, g1, sh2, sc2, ln_g, ln_b, ret_gn, hgrn_gn, w_out, w_r)


def _route_plan(route, counts_f, n_blocks):
    rows = MOE_ROWS
    counts = counts_f[0, :N_EXPERTS].astype(jnp.int32)
    padded = (counts + rows - 1) // rows * rows
    pend = jnp.cumsum(padded)
    pstart = pend - padded
    eid = route[:, R_E1:R_E2 + 1].astype(jnp.int32)
    rank = route[:, R_S1:R_S2 + 1].astype(jnp.int32)
    onehot = eid[:, :, None] == jnp.arange(N_EXPERTS, dtype=jnp.int32)
    pos = jnp.sum(jnp.where(onehot, pstart, 0), axis=-1) + rank
    blk_start = jnp.arange(n_blocks, dtype=jnp.int32) * rows
    blk_e = jnp.minimum(jnp.sum(blk_start[:, None] >= pend[None, :], axis=-1), N_EXPERTS - 1)
    blk_valid = (blk_start < pend[-1]).astype(jnp.int32)
    blk_ids = jnp.arange(n_blocks, dtype=jnp.int32)
    has_pad = padded > counts
    last_blk = pend // rows - 1
    blk_fill = (blk_valid == 0) | jnp.any((blk_ids[:, None] == last_blk[None, :]) & has_pad[None, :], axis=-1)
    ids = jnp.arange(N_EXPERTS, dtype=jnp.int32)
    nonempty = counts > 0
    seq = jnp.cumsum(nonempty.astype(jnp.int32)) - nonempty.astype(jnp.int32)
    cand = jnp.where(nonempty, ids, N_EXPERTS)
    later = jnp.where(ids[None, :] > ids[:, None], cand[None, :], N_EXPERTS)
    nxt = jnp.min(later, axis=-1)
    nxt = jnp.where(nxt >= N_EXPERTS, -1, nxt)
    blk_oh = blk_e[:, None] == ids[None, :]
    blk_slot = jnp.sum(jnp.where(blk_oh, seq % 2, 0), axis=-1)
    blk_next = jnp.sum(jnp.where(blk_oh, nxt, 0), axis=-1)
    i32 = lambda a: a.astype(jnp.int32)
    return (i32(pos.reshape(-1)), i32(blk_e), blk_valid, i32(blk_fill), i32(blk_slot), i32(blk_next))


def _dispatch_kernel(pos_ref, fill_ref, h_ref, x_hbm, zbuf, sem, *, tile_of_step, n_blocks):
    i = pl.program_id(0)
    tm = ROW_TILE
    rows = MOE_ROWS

    def zero_block_copy(j):
        start = pl.multiple_of(j * rows, rows)
        return pltpu.make_async_copy(zbuf, x_hbm.at[pl.ds(start, rows)], sem.at[1])

    @pl.when(i == 0)
    def _():
        zbuf[...] = jnp.zeros_like(zbuf)

        def fill_start(j, carry):
            @pl.when(fill_ref[j] == 1)
            def _():
                zero_block_copy(j).start()
            return carry

        def fill_wait(j, carry):
            @pl.when(fill_ref[j] == 1)
            def _():
                zero_block_copy(j).wait()
            return carry

        lax.fori_loop(0, n_blocks, fill_start, 0)
        lax.fori_loop(0, n_blocks, fill_wait, 0)

    def row_copy(r, k):
        p = pos_ref[(tile_of_step(i) * tm + r) * TOP_K + k]
        return pltpu.make_async_copy(h_ref.at[r], x_hbm.at[p], sem.at[0])

    for r in range(tm):
        for k in range(TOP_K):
            row_copy(r, k).start(priority=k)
    for r in range(tm):
        for k in range(TOP_K):
            row_copy(r, k).wait()


def _tile_schedule(n, tiles_per_batch, latent_only, batch):
    if latent_only:
        lat = tiles_per_batch - 1
        return batch * lat, (lambda s: (s // lat) * tiles_per_batch + 1 + s % lat)
    return n // ROW_TILE, (lambda s: s)


def _dispatch(h2, pos, blk_fill, tiles_per_batch, latent_only, batch):
    n, dm = h2.shape
    tm = ROW_TILE
    n_blocks = blk_fill.shape[0]
    n_tiles, tile_of_step = _tile_schedule(n, tiles_per_batch, latent_only, batch)
    grid_spec = pltpu.PrefetchScalarGridSpec(
        num_scalar_prefetch=2,
        grid=(n_tiles,),
        in_specs=[pl.BlockSpec((tm, dm), lambda i, p, f: (tile_of_step(i), 0))],
        out_specs=pl.BlockSpec(memory_space=pl.ANY),
        scratch_shapes=[pltpu.VMEM((MOE_ROWS, dm), h2.dtype), pltpu.SemaphoreType.DMA((2,))],
    )
    return pl.pallas_call(
        functools.partial(_dispatch_kernel, tile_of_step=tile_of_step, n_blocks=n_blocks),
        out_shape=jax.ShapeDtypeStruct((n_blocks * MOE_ROWS, dm), h2.dtype),
        grid_spec=grid_spec,
        compiler_params=_cparams(("arbitrary",)),
        name="moe_dispatch",
    )(pos, blk_fill, h2)


def _moe_kernel(be_ref, bv_ref, slot_ref, next_ref, x_ref, w1_hbm, w3_hbm, w2_hbm, y_ref,
                s1, s3, s2, w1b, w3b, w2b, sem, *, layer):
    j = pl.program_id(0)
    valid = bv_ref[j] == 1
    first = valid & ((j == 0) | (be_ref[j] != be_ref[jnp.maximum(j - 1, 0)]))

    def weight_copies(e, slot):
        return (pltpu.make_async_copy(w1_hbm.at[layer, e], s1.at[slot], sem.at[slot]),
                pltpu.make_async_copy(w3_hbm.at[layer, e], s3.at[slot], sem.at[slot]),
                pltpu.make_async_copy(w2_hbm.at[layer, e], s2.at[slot], sem.at[slot]))

    @pl.when(valid & (j == 0))
    def _():
        for cp in weight_copies(be_ref[0], slot_ref[0]):
            cp.start()

    @pl.when(first)
    def _():
        slot = slot_ref[j]
        for cp in weight_copies(be_ref[j], slot):
            cp.wait()
        nxt = next_ref[j]

        @pl.when(nxt >= 0)
        def _():
            for cp in weight_copies(nxt, 1 - slot):
                cp.start(priority=1)

        w1b[...] = s1[slot].astype(BF16)
        w3b[...] = s3[slot].astype(BF16)
        w2b[...] = s2[slot].astype(BF16)

    @pl.when(valid)
    def _():
        x = _unpack_bf16_pairs(x_ref[...]).astype(BF16)
        hmid = _silu(_dot(x, w1b[...])) * _dot(x, w3b[...])
        y_ref[...] = _pack_bf16_pairs(_dot(hmid.astype(BF16), w2b[...]))

    @pl.when(bv_ref[j] == 0)
    def _():
        y_ref[...] = jnp.zeros_like(y_ref)


def _moe_experts(xbuf, blk_e, blk_valid, blk_slot, blk_next, w1, w3, w2, layer):
    dm = w1.shape[2]
    ff = w1.shape[-1]
    rows = MOE_ROWS
    n_blocks = blk_e.shape[0]
    hbm = pl.BlockSpec(memory_space=pl.ANY)
    grid_spec = pltpu.PrefetchScalarGridSpec(
        num_scalar_prefetch=4,
        grid=(n_blocks,),
        in_specs=[
            pl.BlockSpec((rows, dm // 2), lambda j, be, bv, sl, nx: (j * bv[j], 0)),
            hbm, hbm, hbm,
        ],
        out_specs=pl.BlockSpec((rows, dm // 2), lambda j, be, bv, sl, nx: (j, 0)),
        scratch_shapes=[
            pltpu.VMEM((2, dm, ff), F32), pltpu.VMEM((2, dm, ff), F32), pltpu.VMEM((2, ff, dm), F32),
            pltpu.VMEM((dm, ff), BF16), pltpu.VMEM((dm, ff), BF16), pltpu.VMEM((ff, dm), BF16),
            pltpu.SemaphoreType.DMA((2,)),
        ],
    )
    return pl.pallas_call(
        functools.partial(_moe_kernel, layer=layer),
        out_shape=jax.ShapeDtypeStruct((n_blocks * rows, dm // 2), jnp.uint32),
        grid_spec=grid_spec,
        compiler_params=_cparams(("arbitrary",)),
        name="moe_experts",
    )(blk_e, blk_valid, blk_slot, blk_next, xbuf, w1, w3, w2)


def _expert_row_gather(pos_ref, y_hbm, ybuf, sem, tile_of_step):
    tm = ROW_TILE

    def row_copy(step, sl, r, k):
        p = pos_ref[(tile_of_step(step) * tm + r) * TOP_K + k]
        return pltpu.make_async_copy(y_hbm.at[p], ybuf.at[sl, k, r], sem.at[sl])

    def start_rows(step, sl, r0, r1):
        for r in range(r0, r1):
            for k in range(TOP_K):
                row_copy(step, sl, r, k).start(priority=k)

    def wait_rows(step, sl):
        for r in range(tm):
            for k in range(TOP_K):
                row_copy(step, sl, r, k).wait()

    return start_rows, wait_rows


def _gate_weighted(ybuf, slot, r_ref):
    route = r_ref[...]
    return (_unpack_bf16_pairs(ybuf[slot, 0]) * route[:, R_G1:R_G1 + 1]
            + _unpack_bf16_pairs(ybuf[slot, 1]) * route[:, R_G2:R_G2 + 1])


def _gathered_expert_rows(pos_ref, r_ref, y_hbm, ybuf, sem, *, n_tiles, tile_of_step):
    i = pl.program_id(0)
    slot = i % 2
    start_rows, wait_rows = _expert_row_gather(pos_ref, y_hbm, ybuf, sem, tile_of_step)

    @pl.when(i == 0)
    def _():
        start_rows(0, 0, 0, ROW_TILE)

    @pl.when(i + 1 < n_tiles)
    def _():
        start_rows(jnp.minimum(i + 1, n_tiles - 1), 1 - slot, 0, ROW_TILE)

    wait_rows(i, slot)
    return _gate_weighted(ybuf, slot, r_ref)


def _combine_kernel(pos_ref, x_ref, r_ref, g2_ref, lng_ref, lnb_ref, y_hbm, o_ref, ybuf, sem,
                    *, n_tiles, tile_of_step, alpha):
    y = _gathered_expert_rows(pos_ref, r_ref, y_hbm, ybuf, sem, n_tiles=n_tiles, tile_of_step=tile_of_step)
    z = alpha * x_ref[...] + g2_ref[0] * y
    o_ref[...] = _ln_rows(z) * lng_ref[...] + lnb_ref[...]


def _combine(x_new, route, ybuf, pos, g2, ln_g, ln_b, tiles_per_batch, alpha, batch):
    n, dm = x_new.shape
    tm = ROW_TILE
    n_tiles, tile_of_step = _tile_schedule(n, tiles_per_batch, True, batch)
    gmap = lambda i, p: (_group_of_tile(tile_of_step(i), tiles_per_batch), 0, 0)
    grid_spec = pltpu.PrefetchScalarGridSpec(
        num_scalar_prefetch=1,
        grid=(n_tiles,),
        in_specs=[
            pl.BlockSpec((tm, dm), lambda i, p: (tile_of_step(i), 0)),
            pl.BlockSpec((tm, LANES), lambda i, p: (tile_of_step(i), 0)),
            pl.BlockSpec((1, 1, dm), gmap),
            pl.BlockSpec((1, dm), lambda i, p: (0, 0)),
            pl.BlockSpec((1, dm), lambda i, p: (0, 0)),
            pl.BlockSpec(memory_space=pl.ANY),
        ],
        out_specs=pl.BlockSpec((tm, dm), lambda i, p: (i, 0)),
        scratch_shapes=[pltpu.VMEM((2, TOP_K, tm, dm // 2), jnp.uint32), pltpu.SemaphoreType.DMA((2,))],
    )
    return pl.pallas_call(
        functools.partial(_combine_kernel, n_tiles=n_tiles, tile_of_step=tile_of_step, alpha=alpha),
        out_shape=jax.ShapeDtypeStruct((n_tiles * tm, dm), F32),
        grid_spec=grid_spec,
        compiler_params=_cparams(("arbitrary",)),
        name="moe_combine_norm",
    )(pos, x_new, route, g2, ln_g, ln_b, ybuf)


def _combine_proj_kernel(pos_ref, x_ref, r_ref, g2_ref, lng_ref, lnb_ref, y_hbm, sh_ref, sc_ref, w_ref,
                         *rest, n_tiles, alpha):
    mla_in, (oa, ob, od, qf, kf, vt, so, ybuf, sem, oc) = rest[:7], rest[7:]
    i = pl.program_id(0)
    slot = i % 2
    tm = ROW_TILE
    start_rows, wait_rows = _expert_row_gather(pos_ref, y_hbm, ybuf, sem, lambda s: s)

    @pl.when(i == 0)
    def _():
        start_rows(0, 0, 0, tm)

    wait_rows(i, slot)
    z = alpha * x_ref[...] + g2_ref[0] * _gate_weighted(ybuf, slot, r_ref)
    stream = _ln_rows(z) * lng_ref[...] + lnb_ref[...]
    so[...] = stream
    outs = (oa, ob, oc, od)
    n_blk = len(_column_blocks(outs))
    per = -(-tm // n_blk)
    nxt = jnp.minimum(i + 1, n_tiles - 1)

    c_done = max(k for k, blk in enumerate(_column_blocks(outs)) if blk[0] is oc)

    def after_block(idx):
        start_rows(nxt, 1 - slot, min(idx * per, tm), min((idx + 1) * per, tm))
        if idx == c_done:
            _mla_up_rows(oc[...], *mla_in, qf, kf, vt)

    _project_rows(stream, sh_ref, sc_ref, w_ref, outs, between=after_block)

    @pl.when(i == n_tiles - 1)
    def _():
        wait_rows(nxt, 1 - slot)


def _combine_proj(x_new, route, ybuf, pos, g2, ln_g, ln_b, shift, scale, w_pad, mla_args,
                  tiles_per_batch, alpha):
    n, dm = x_new.shape
    tm = ROW_TILE
    n_tiles = n // tm
    gmap = lambda i, p: (_group_of_tile(i, tiles_per_batch), 0, 0)
    row = lambda w: pl.BlockSpec((tm, w), lambda i, p: (i, 0))
    f32 = lambda w: jax.ShapeDtypeStruct((n, w), F32)
    const = pl.BlockSpec((1, dm), lambda i, p: (0, 0))
    mod = pl.BlockSpec((1, 1, dm), gmap)
    mla_in, mla_out, mla_shapes = _mla_up_specs(n, mla_args)
    grid_spec = pltpu.PrefetchScalarGridSpec(
        num_scalar_prefetch=1,
        grid=(n_tiles,),
        in_specs=[row(dm), row(LANES), mod, const, const, pl.BlockSpec(memory_space=pl.ANY),
                  mod, mod, pl.BlockSpec(memory_space=pltpu.VMEM)] + mla_in,
        out_specs=[row(SEG_A), row(SEG_B), row(SEG_D)] + mla_out + [row(dm)],
        scratch_shapes=[pltpu.VMEM((2, TOP_K, tm, dm // 2), jnp.uint32), pltpu.SemaphoreType.DMA((2,)),
                        pltpu.VMEM((tm, SEG_C), F32)],
    )
    return pl.pallas_call(
        functools.partial(_combine_proj_kernel, n_tiles=n_tiles, alpha=alpha),
        out_shape=[f32(SEG_A), f32(SEG_B), f32(SEG_D)] + mla_shapes + [f32(dm)],
        grid_spec=grid_spec,
        compiler_params=_cparams(("arbitrary",), vmem=PROJ_VMEM_LIMIT_BYTES),
        name="moe_combine_ln_mod_inproj",
    )(pos, x_new, route, g2, ln_g, ln_b, ybuf, shift, scale, w_pad, *mla_args)


def _pad_cols(w, width):
    return jnp.pad(w, ((0, 0), (0, width - w.shape[1])))


def kernel(x, c, ctx, c_ctx, w_ada, b_ada, w_in, swa_sink, ret_decay_exp, ret_gn, mla_q_norm, mla_kv_norm,
           mla_w_uq, mla_w_ukv, hgrn_lb_logits, hgrn_gn, w_out, ln1_g, ln1_b, router_group, router_expert,
           moe_w1, moe_w3, moe_w2, ln2_g, ln2_b):
    batch, t_len, dm = x.shape
    l_len = ctx.shape[1]
    depth = w_in.shape[0]
    p_len = t_len + l_len
    n = batch * p_len
    assert batch == 2 and l_len == ROW_TILE and t_len % 256 == 0 and t_len % GRID_W == 0
    tiles_per_batch = p_len // ROW_TILE
    alpha = float((2 * depth) ** 0.25)

    cos_swa, sin_swa, cos_rows, sin_rows = _rope_tables(t_len, l_len, batch)
    mats = jnp.asarray(_hgrn_level_mats(), BF16)

    cvec = jnp.zeros((SUBLANES, dm), F32).at[0:batch].set(c).at[batch].set(c_ctx)
    mod = _ada(cvec, w_ada, b_ada)

    lbp = jax.nn.softmax(hgrn_lb_logits.astype(F32), axis=0)
    lower_bounds = jnp.cumsum(lbp, axis=0) - lbp[0]

    pending = None

    out = None
    for l in range(depth):
        need_ctx = l < depth - 1
        m6 = mod[l, :3].reshape(3, 6, dm)
        sh1, sc1, g1, sh2, sc2, g2 = (m6[:, k].reshape(3, 1, dm) for k in range(6))

        wq = mla_w_uq[l].reshape(MLA_Q_RANK, MLA_HEADS, MLA_NOPE + MLA_ROPE)
        wq = jnp.pad(wq, ((0, 0), (0, 0), (0, 2 * LANES - (MLA_NOPE + MLA_ROPE))))
        wq = wq.reshape(MLA_Q_RANK, MLA_HEADS * 2 * LANES).astype(BF16)
        wkv = mla_w_ukv[l].reshape(MLA_KV_RANK, MLA_HEADS, MLA_NOPE + MLA_V)
        wk = wkv[:, :, :MLA_NOPE].reshape(MLA_KV_RANK, -1).astype(BF16)
        wv_t = wkv[:, :, MLA_NOPE:].reshape(MLA_KV_RANK, -1).T.astype(BF16)
        mla_args = (mla_q_norm[l].reshape(1, -1), mla_kv_norm[l].reshape(1, -1), wq, wk, wv_t,
                    cos_rows, sin_rows)

        w_pad = _w_in_prep(w_in, l)
        if pending is None:
            pa, pb, pd, qf, kf, vt, stream = _proj(x, ctx, sh1, sc1, w_pad, mla_args, tiles_per_batch)
        else:
            pa, pb, pd, qf, kf, vt, stream = _combine_proj(*pending, sh1, sc1, w_pad, mla_args,
                                                           tiles_per_batch, alpha)

        cc = _mla_attn(qf, kf, vt, batch, t_len, l_len, need_ctx)
        a, b_f, b_b, d_f, d_b = _mixers(
            [_swa_part(pa, swa_sink[l], cos_swa, sin_swa, batch, t_len, l_len, need_ctx),
             _retention_part(pb, ret_decay_exp[l], batch, t_len, l_len),
             _hgrn_part(pd, lower_bounds[l], mats, batch, t_len, l_len)], n)

        w_r = _pad_cols(jnp.concatenate([router_group[l], router_expert[l]], axis=1), LANES).astype(BF16)
        x_new, h2, route, counts = _out_proj(
            stream, a, b_f, b_b, cc, d_f, d_b, pb, pd, g1, sh2, sc2,
            ln1_g[l].reshape(1, -1), ln1_b[l].reshape(1, -1),
            ret_gn[l].reshape(1, -1), hgrn_gn[l].reshape(1, -1),
            w_out[l].astype(BF16), w_r, tiles_per_batch, alpha, need_ctx)

        n_active = n if need_ctx else batch * t_len
        n_blocks = (n_active * TOP_K + N_EXPERTS * (MOE_ROWS - 1) + MOE_ROWS - 1) // MOE_ROWS
        pos, blk_e, blk_valid, blk_fill, blk_slot, blk_next = _route_plan(route, counts, n_blocks)
        xbuf = _dispatch(h2, pos, blk_fill, tiles_per_batch, not need_ctx, batch)
        ybuf = _moe_experts(xbuf, blk_e, blk_valid, blk_slot, blk_next, moe_w1, moe_w3, moe_w2, l)
        combine_args = (x_new, route, ybuf, pos, g2, ln2_g[l].reshape(1, -1), ln2_b[l].reshape(1, -1))
        if need_ctx:
            pending = combine_args
        else:
            out = _combine(*combine_args, tiles_per_batch, alpha, batch).reshape(batch, t_len, dm)
    return out
```

```python
import functools
import math

import numpy as np
import jax
import jax.numpy as jnp
from jax import lax
from jax.experimental import pallas as pl
from jax.experimental.pallas import tpu as pltpu

F32 = jnp.float32
BF16 = jnp.bfloat16

GRID_W = 64
HEAD_DIM = 128
SWA_HEADS = 4
SWA_KV_HEADS = 2
SWA_BLOCK = 128
RET_HEADS = 4
RET_DK = 128
MLA_HEADS = 4
MLA_Q_RANK = 384
MLA_KV_RANK = 128
MLA_NOPE = 128
MLA_ROPE = 64
MLA_V = 128
HGRN_HEADS = 4
N_GROUPS = 4
EXPERTS_PER_GROUP = 8
N_EXPERTS = N_GROUPS * EXPERTS_PER_GROUP
TOP_K = 2
ROPE_BASE = 10000.0
EPS = 1e-6

LANES = 128
SUBLANES = 8
VMEM_LIMIT_BYTES = 56 * 1024 * 1024
PROJ_VMEM_LIMIT_BYTES = 60 * 1024 * 1024

CHUNK = 128
ROW_TILE = 256
MOE_ROWS = 256
MLA_KEY_CHUNK = 256
MLA_HEADS_PER_STEP = 4
WIDE = 512

SEG_A = 1024
SEG_B = 2048
SEG_C = 640
SEG_D = 2560


def _cparams(sem, vmem=VMEM_LIMIT_BYTES):
    return pltpu.CompilerParams(dimension_semantics=sem, vmem_limit_bytes=vmem)


def _ln_rows(x):
    mu = jnp.mean(x, axis=-1, keepdims=True)
    xc = x - mu
    var = jnp.mean(xc * xc, axis=-1, keepdims=True)
    return xc * lax.rsqrt(var + EPS)


def _silu(x):
    return x * (1.0 / (1.0 + jnp.exp(-x)))


def _dot(a, b):
    return jnp.dot(a, b, preferred_element_type=F32)


def _pack_bf16_pairs(x):
    k = x.shape[1] // 2
    hi = lax.bitcast_convert_type(x[:, :k].astype(BF16).astype(F32), jnp.uint32)
    lo = lax.bitcast_convert_type(x[:, k:].astype(BF16).astype(F32), jnp.uint32)
    return hi | (lo >> 16)


def _unpack_bf16_pairs(w):
    hi = lax.bitcast_convert_type(w & jnp.uint32(0xFFFF0000), F32)
    lo = lax.bitcast_convert_type(w << 16, F32)
    return jnp.concatenate([hi, lo], axis=1)


def _dot_nt(a, b):
    return lax.dot_general(a, b, (((1,), (1,)), ((), ())), preferred_element_type=F32)


def _dot_tn(a, b):
    return lax.dot_general(a, b, (((0,), (0,)), ((), ())), preferred_element_type=F32)


def _ada_kernel(c_ref, w_ref, b_ref, o_ref):
    s = _silu(c_ref[...]).astype(BF16)
    o_ref[0] = _dot(s, w_ref[0].astype(BF16)) + b_ref[0]


def _ada(cvec, w_ada, b_ada):
    depth, d, n6 = w_ada.shape
    tn = n6 // 8
    return pl.pallas_call(
        _ada_kernel,
        out_shape=jax.ShapeDtypeStruct((depth, SUBLANES, n6), F32),
        grid=(depth, n6 // tn),
        in_specs=[
            pl.BlockSpec((SUBLANES, d), lambda l, j: (0, 0)),
            pl.BlockSpec((1, d, tn), lambda l, j: (l, 0, j)),
            pl.BlockSpec((1, 1, tn), lambda l, j: (l, 0, j)),
        ],
        out_specs=pl.BlockSpec((1, SUBLANES, tn), lambda l, j: (l, 0, j)),
        compiler_params=_cparams(("arbitrary", "arbitrary")),
        name="ada_mod",
    )(cvec, w_ada, b_ada.reshape(depth, 1, n6))


SEG_WIDTHS = (SEG_A, SEG_B, SEG_C, SEG_D)
SEG_C_TRUE = MLA_Q_RANK + MLA_KV_RANK + MLA_ROPE
IN_COLS = SEG_A + SEG_B + SEG_C_TRUE + SEG_D
IN_COLS_PAD = sum(SEG_WIDTHS)


PREP_ROWS = 256
_C_END = SEG_A + SEG_B + SEG_C_TRUE
_C_PAD = SEG_C - SEG_C_TRUE
assert _C_END % PREP_ROWS + _C_PAD <= PREP_ROWS and IN_COLS_PAD % PREP_ROWS in (0, PREP_ROWS // 2)


def _w_in_prep_kernel(w_ref, o_ref, prev_ref):
    j = pl.program_id(0)
    jc = _C_END // PREP_ROWS
    rem = _C_END % PREP_ROWS
    keep = PREP_ROWS - _C_PAD
    cur = w_ref[0].astype(BF16)

    @pl.when(j < jc)
    def _():
        o_ref[...] = cur

    @pl.when(j == jc)
    def _():
        o_ref[0:rem, :] = cur[0:rem]
        o_ref[rem:rem + _C_PAD, :] = jnp.zeros((_C_PAD, cur.shape[1]), BF16)
        o_ref[rem + _C_PAD:PREP_ROWS, :] = cur[rem:keep]

    @pl.when(j > jc)
    def _():
        o_ref[0:_C_PAD, :] = prev_ref[keep:PREP_ROWS, :]
        o_ref[_C_PAD:PREP_ROWS, :] = cur[0:keep]

    prev_ref[...] = cur


def _w_in_prep(w_in, layer):
    _, d, cols = w_in.shape
    assert cols == IN_COLS
    w_t = jnp.swapaxes(w_in, 1, 2)
    return pl.pallas_call(
        _w_in_prep_kernel,
        out_shape=jax.ShapeDtypeStruct((IN_COLS_PAD, d), BF16),
        grid=(pl.cdiv(IN_COLS_PAD, PREP_ROWS),),
        in_specs=[pl.BlockSpec((1, PREP_ROWS, d), lambda j: (layer, j, 0))],
        out_specs=pl.BlockSpec((PREP_ROWS, d), lambda j: (j, 0)),
        scratch_shapes=[pltpu.VMEM((PREP_ROWS, d), BF16)],
        compiler_params=_cparams(("arbitrary",)),
        name="w_in_prep",
    )(w_t)


def _column_blocks(outs):
    return [(o_ref, j, min(WIDE, o_ref.shape[1] - j)) for o_ref in outs for j in range(0, o_ref.shape[1], WIDE)]


def _project_rows(x, sh_ref, sc_ref, w_ref, outs, between=None):
    y = _ln_rows(x)
    h = (y * (1.0 + sc_ref[0]) + sh_ref[0]).astype(BF16)
    base = 0
    for idx, (o_ref, j, cw) in enumerate(_column_blocks(outs)):
        o_ref[:, j:j + cw] = _dot_nt(h, w_ref[base:base + cw, :])
        base += cw
        if between is not None:
            between(idx)


def _proj_kernel(x_ref, ctx_ref, sh_ref, sc_ref, w_ref, *rest, tiles_per_batch):
    mla_in, (oa, ob, od, qf, kf, vt, so, pc_buf) = rest[:7], rest[7:]
    is_ctx = pl.program_id(0) % tiles_per_batch == 0
    x = jnp.where(is_ctx, ctx_ref[0], x_ref[0])
    so[...] = x
    outs = (oa, ob, pc_buf, od)
    c_done = max(k for k, blk in enumerate(_column_blocks(outs)) if blk[0] is pc_buf)

    def after_block(idx):
        if idx == c_done:
            _mla_up_rows(pc_buf[...], *mla_in, qf, kf, vt)

    _project_rows(x, sh_ref, sc_ref, w_ref, outs, between=after_block)


def _group_of_tile(i, tiles_per_batch):
    return jnp.where(i % tiles_per_batch == 0, 2, i // tiles_per_batch)


def _proj(x, ctx, shift, scale, w_pad, mla_args, tiles_per_batch):
    tm = ROW_TILE
    batch, t_len, d = x.shape
    n = batch * (t_len + ctx.shape[1])
    tpb = tiles_per_batch
    gmap = lambda i: (_group_of_tile(i, tiles_per_batch), 0, 0)
    row = lambda w: pl.BlockSpec((tm, w), lambda i: (i, 0))
    f32 = lambda w: jax.ShapeDtypeStruct((n, w), F32)
    mla_in, mla_out, mla_shapes = _mla_up_specs(n, mla_args)
    return pl.pallas_call(
        functools.partial(_proj_kernel, tiles_per_batch=tiles_per_batch),
        out_shape=[f32(SEG_A), f32(SEG_B), f32(SEG_D)] + mla_shapes + [f32(d)],
        grid=(n // tm,),
        in_specs=[
            pl.BlockSpec((1, tm, d), lambda i: (i // tpb, jnp.maximum(i % tpb - 1, 0), 0)),
            pl.BlockSpec((1, tm, d), lambda i: (i // tpb, 0, 0)),
            pl.BlockSpec((1, 1, d), gmap),
            pl.BlockSpec((1, 1, d), gmap),
            pl.BlockSpec(memory_space=pltpu.VMEM),
        ] + mla_in,
        out_specs=[row(SEG_A), row(SEG_B), row(SEG_D)] + mla_out + [row(d)],
        scratch_shapes=[pltpu.VMEM((tm, SEG_C), F32)],
        compiler_params=_cparams(("arbitrary",), vmem=PROJ_VMEM_LIMIT_BYTES),
        name="ln_mod_inproj",
    )(x, ctx, shift, scale, w_pad, *mla_args)


def _rope128(x, cos, sin):
    return x * cos + pltpu.roll(x, 64, 1) * sin


def _rope64(x, cos, sin):
    lane = lax.broadcasted_iota(jnp.int32, x.shape, 1)
    rot = jnp.where((lane % 64) < 32, pltpu.roll(x, 96, 1), pltpu.roll(x, 32, 1))
    return x * cos + rot * sin


def _rope_tables(t_len, l_len, batch):
    rows = t_len // GRID_W
    row = np.repeat(np.arange(rows), GRID_W).astype(np.float32)
    col = np.tile(np.arange(GRID_W), rows).astype(np.float32)

    def angles(rot_dim):
        n_freq = rot_dim // 4
        inv = (ROPE_BASE ** (-np.arange(n_freq, dtype=np.float32) / n_freq)).astype(np.float32)
        return np.concatenate([row[:, None] * inv, col[:, None] * inv], -1).astype(np.float32)

    a_swa = angles(HEAD_DIM)
    cos_swa = np.concatenate([np.cos(a_swa), np.cos(a_swa)], -1)
    sin_swa = np.concatenate([-np.sin(a_swa), np.sin(a_swa)], -1)
    a_mla = angles(MLA_ROPE)
    cos_m = np.concatenate([np.cos(a_mla), np.cos(a_mla), np.ones((t_len, 64), np.float32)], -1)
    sin_m = np.concatenate([-np.sin(a_mla), np.sin(a_mla), np.zeros((t_len, 64), np.float32)], -1)
    ones = np.ones((l_len, LANES), np.float32)
    zeros = np.zeros((l_len, LANES), np.float32)
    cos_rows = np.concatenate([np.concatenate([ones, cos_m], 0)] * batch, 0)
    sin_rows = np.concatenate([np.concatenate([zeros, sin_m], 0)] * batch, 0)
    return (jnp.asarray(cos_swa, F32), jnp.asarray(sin_swa, F32),
            jnp.asarray(cos_rows, F32), jnp.asarray(sin_rows, F32))


def _mla_up_rows(pc, qn_ref, kvn_ref, wq_ref, wk_ref, wv_ref, cos_ref, sin_ref, q_out, k_out, v_out):
    scale = float((MLA_NOPE + MLA_ROPE) ** -0.5 * math.log2(math.e))
    cq = pc[:, :MLA_Q_RANK]
    ckv = pc[:, MLA_Q_RANK:MLA_Q_RANK + MLA_KV_RANK]
    kr = pc[:, MLA_Q_RANK + MLA_KV_RANK:]
    cos = cos_ref[...]
    sin = sin_ref[...]

    def rms(x, g):
        return x * lax.rsqrt(jnp.mean(x * x, axis=-1, keepdims=True) + EPS) * g

    qh = _dot(rms(cq, qn_ref[...]).astype(BF16), wq_ref[...])
    ckn = rms(ckv, kvn_ref[...]).astype(BF16)
    kh = _dot(ckn, wk_ref[...])
    v_out[...] = _dot_nt(wv_ref[...], ckn).astype(BF16)
    kr_rot = _rope64(kr, cos, sin).astype(BF16)
    for h in range(MLA_HEADS):
        base = h * 2 * LANES
        q_out[:, base:base + LANES] = (qh[:, base:base + LANES] * scale).astype(BF16)
        q_out[:, base + LANES:base + 2 * LANES] = (
            _rope64(qh[:, base + LANES:base + 2 * LANES], cos, sin) * scale).astype(BF16)
        k_out[:, base:base + LANES] = kh[:, h * LANES:(h + 1) * LANES].astype(BF16)
        k_out[:, base + LANES:base + 2 * LANES] = kr_rot


def _mla_up_specs(n, mla_args):
    tm = ROW_TILE
    const2 = lambda i, *_: (0, 0)
    rows = lambda i, *_: (i, 0)
    in_specs = [pl.BlockSpec(a.shape, const2) for a in mla_args[:5]]
    in_specs += [pl.BlockSpec((tm, LANES), rows), pl.BlockSpec((tm, LANES), rows)]
    qk_w = MLA_HEADS * 2 * LANES
    out_specs = [pl.BlockSpec((tm, qk_w), rows), pl.BlockSpec((tm, qk_w), rows),
                 pl.BlockSpec((MLA_HEADS * MLA_V, tm), lambda i, *_: (0, i))]
    out_shapes = [jax.ShapeDtypeStruct((n, qk_w), BF16), jax.ShapeDtypeStruct((n, qk_w), BF16),
                  jax.ShapeDtypeStruct((MLA_HEADS * MLA_V, n), BF16)]
    return in_specs, out_specs, out_shapes


def _mla_attn_kernel(q_ref, k_ref, vt_ref, o_ref, s_ref, p_ref, *, kc, l_len, need_ctx):
    p_len = k_ref.shape[0]
    tq = q_ref.shape[0]
    dqk = 2 * LANES
    heads = q_ref.shape[1] // dqk
    half = p_len // 2
    is_ctx = pl.program_id(2) == 0

    @pl.when(is_ctx)
    def _():
        if not need_ctx:
            o_ref[...] = jnp.zeros_like(o_ref)
            return
        for h in range(heads):
            hq = slice(h * dqk, (h + 1) * dqk)
            hv = slice(h * MLA_V, (h + 1) * MLA_V)
            s = _dot_nt(k_ref[0:l_len, hq], q_ref[:, hq])
            p = jnp.exp2(s - jnp.max(s, axis=0, keepdims=True))
            acc = _dot(vt_ref[hv, 0:l_len], p.astype(BF16))
            o_ref[:, hv] = (acc / jnp.sum(p, axis=0, keepdims=True)).T

    @pl.when(jnp.logical_not(is_ctx))
    def _():
        fold = lambda a: a.reshape(kc // SUBLANES, SUBLANES, tq)
        m8s = []
        for h in range(heads):
            q = q_ref[:, h * dqk:(h + 1) * dqk]
            m8 = None
            for off in range(0, p_len, kc):
                s = _dot_nt(k_ref[off:off + kc, h * dqk:(h + 1) * dqk], q)
                s_ref[h, off:off + kc, :] = s
                cm = jnp.max(fold(s), axis=0)
                m8 = cm if m8 is None else jnp.maximum(m8, cm)
            m8s.append(m8)
        for h in range(heads):
            m = jnp.max(m8s[h], axis=0, keepdims=True)
            l8 = jnp.zeros((SUBLANES, tq), F32)
            for off in range(0, p_len, kc):
                p = jnp.exp2(s_ref[h, off:off + kc, :] - m)
                l8 = l8 + jnp.sum(fold(p), axis=0)
                p_ref[h, off:off + kc, :] = p.astype(BF16)
            l = jnp.sum(l8, axis=0, keepdims=True)
            hv = slice(h * MLA_V, (h + 1) * MLA_V)
            acc = (_dot(vt_ref[hv, 0:half], p_ref[h, 0:half, :])
                   + _dot(vt_ref[hv, half:p_len], p_ref[h, half:p_len, :]))
            o_ref[:, hv] = (acc / l).T


def _mla_attn(qf, kf, vt, batch, t_len, l_len, need_ctx):
    n = qf.shape[0]
    p_len = t_len + l_len
    tq = ROW_TILE
    assert l_len == tq
    kc = MLA_KEY_CHUNK
    hp = MLA_HEADS_PER_STEP
    tpb = p_len // tq
    return pl.pallas_call(
        functools.partial(_mla_attn_kernel, kc=kc, l_len=l_len, need_ctx=need_ctx),
        out_shape=jax.ShapeDtypeStruct((n, MLA_HEADS * MLA_V), F32),
        grid=(batch, MLA_HEADS // hp, tpb),
        in_specs=[
            pl.BlockSpec((tq, hp * 2 * LANES), lambda b, h, i: (b * tpb + i, h)),
            pl.BlockSpec((p_len, hp * 2 * LANES), lambda b, h, i: (b, h), pipeline_mode=pl.Buffered(1)),
            pl.BlockSpec((hp * MLA_V, p_len), lambda b, h, i: (h, b), pipeline_mode=pl.Buffered(1)),
        ],
        out_specs=pl.BlockSpec((tq, hp * MLA_V), lambda b, h, i: (b * tpb + i, h)),
        scratch_shapes=[pltpu.VMEM((hp, p_len, tq), F32), pltpu.VMEM((hp, p_len, tq), BF16)],
        compiler_params=_cparams(("arbitrary", "arbitrary", "arbitrary")),
        name="mla_attn",
    )(qf, kf, vt)


def _swa_kernel(sink_ref, q_ref, kp_ref, kc_ref, kn_ref, vp_ref, vc_ref, vn_ref, kx_ref, vx_ref,
                cq_ref, sq_ref, cp_ref, sp_ref, cn_ref, sn_ref, o_ref, *, nb, n_ctx, scale, need_ctx):
    j = pl.program_id(0)
    n = j - n_ctx
    blk = SWA_BLOCK
    batch = q_ref.shape[0]
    rr = lax.broadcasted_iota(jnp.int32, (2 * blk, 1), 0)
    mx = lambda a: jnp.max(a, axis=-1, keepdims=True)
    sm = lambda a: jnp.sum(a, axis=-1, keepdims=True)

    def heads_of(b, h):
        q0 = q_ref[b, :, 2 * h * LANES:(2 * h + 1) * LANES]
        q1 = q_ref[b, :, (2 * h + 1) * LANES:(2 * h + 2) * LANES]
        sk = jnp.where(rr < blk, sink_ref[2 * h], sink_ref[2 * h + 1]).astype(F32)
        return q0, q1, sk

    def store(b, h, o):
        o_ref[b, :, 2 * h * LANES:(2 * h + 1) * LANES] = o[:blk]
        o_ref[b, :, (2 * h + 1) * LANES:(2 * h + 2) * LANES] = o[blk:]

    @pl.when(j < n_ctx)
    def _():
        if not need_ctx:
            o_ref[...] = jnp.zeros_like(o_ref)
            return
        for b in range(batch):
            for h in range(SWA_KV_HEADS):
                ks = slice(h * LANES, (h + 1) * LANES)
                q0, q1, sk = heads_of(b, h)
                qq = jnp.concatenate([q0, q1], axis=0).astype(BF16)
                s_x = _dot_nt(qq, kx_ref[b, :, ks].astype(BF16)) * scale
                m = jnp.maximum(mx(s_x), sk)
                p_x = jnp.exp(s_x - m)
                den = sm(p_x) + jnp.exp(sk - m)
                store(b, h, _dot(p_x.astype(BF16), vx_ref[b, :, ks].astype(BF16)) / den)

    @pl.when(j >= n_ctx)
    def _():
        cq, sq = cq_ref[...], sq_ref[...]
        cp, sp = cp_ref[...], sp_ref[...]
        cn, sn = cn_ref[...], sn_ref[...]
        ri = lax.broadcasted_iota(jnp.int32, (2 * blk, blk), 0) % blk
        ci = lax.broadcasted_iota(jnp.int32, (2 * blk, blk), 1)
        mask_p = (ci >= ri) & (n > 0)
        mask_n = (ci <= ri) & (n < nb - 1)
        neg = jnp.float32(-jnp.inf)
        for b in range(batch):
            for h in range(SWA_KV_HEADS):
                ks = slice(h * LANES, (h + 1) * LANES)
                q0, q1, sk = heads_of(b, h)
                qq = jnp.concatenate([_rope128(q0, cq, sq), _rope128(q1, cq, sq)], axis=0).astype(BF16)
                kc = _rope128(kc_ref[b, :, ks], cq, sq).astype(BF16)
                kp = _rope128(kp_ref[b, :, ks], cp, sp).astype(BF16)
                kn = _rope128(kn_ref[b, :, ks], cn, sn).astype(BF16)
                s_c = _dot_nt(qq, kc) * scale
                s_p = jnp.where(mask_p, _dot_nt(qq, kp) * scale, neg)
                s_n = jnp.where(mask_n, _dot_nt(qq, kn) * scale, neg)
                s_x = _dot_nt(qq, kx_ref[b, :, ks].astype(BF16)) * scale
                m = jnp.maximum(jnp.maximum(mx(s_c), mx(s_p)), jnp.maximum(mx(s_n), mx(s_x)))
                m = jnp.maximum(m, sk)
                p_c, p_p, p_n, p_x = (jnp.exp(a - m) for a in (s_c, s_p, s_n, s_x))
                den = sm(p_c) + sm(p_p) + sm(p_n) + sm(p_x) + jnp.exp(sk - m)
                store(b, h, (_dot(p_c.astype(BF16), vc_ref[b, :, ks].astype(BF16))
                             + _dot(p_p.astype(BF16), vp_ref[b, :, ks].astype(BF16))
                             + _dot(p_n.astype(BF16), vn_ref[b, :, ks].astype(BF16))
                             + _dot(p_x.astype(BF16), vx_ref[b, :, ks].astype(BF16))) / den)


def _swa_part(pa, sink, cos_t, sin_t, batch, t_len, l_len, need_ctx):
    p_len = t_len + l_len
    blk = SWA_BLOCK
    nb = t_len // blk
    n_ctx = l_len // blk
    scale = float(HEAD_DIM ** -0.5)
    kv_w = SWA_KV_HEADS * HEAD_DIM
    q_w = SWA_HEADS * HEAD_DIM
    kcol, vcol = q_w // kv_w, q_w // kv_w + 1
    pa3 = pa.reshape(batch, p_len, pa.shape[1])
    lat = lambda j: jnp.maximum(j - n_ctx, 0)
    same = lambda j: lat(j)
    prv = lambda j: jnp.maximum(lat(j) - 1, 0)
    nxt = lambda j: jnp.minimum(lat(j) + 1, nb - 1)
    kv = lambda col, rowf: pl.BlockSpec((batch, blk, kv_w), lambda j: (0, n_ctx + rowf(j), col))
    tab = lambda rowf: pl.BlockSpec((blk, LANES), lambda j: (rowf(j), 0))
    in_specs = [
        pl.BlockSpec(memory_space=pltpu.SMEM),
        pl.BlockSpec((batch, blk, q_w), lambda j: (0, j, 0)),
        kv(kcol, prv), kv(kcol, same), kv(kcol, nxt),
        kv(vcol, prv), kv(vcol, same), kv(vcol, nxt),
        pl.BlockSpec((batch, l_len, kv_w), lambda j: (0, 0, kcol)),
        pl.BlockSpec((batch, l_len, kv_w), lambda j: (0, 0, vcol)),
        tab(same), tab(same), tab(prv), tab(prv), tab(nxt), tab(nxt),
    ]
    return dict(
        kernel=functools.partial(_swa_kernel, nb=nb, n_ctx=n_ctx, scale=scale, need_ctx=need_ctx),
        in_specs=in_specs,
        args=[sink] + [pa3] * 9 + [cos_t, sin_t] * 3,
        out_specs=[pl.BlockSpec((batch, blk, q_w), lambda j: (0, j, 0))],
        out_shapes=[jax.ShapeDtypeStruct((batch, p_len, q_w), F32)],
        scratch=[],
        steps=n_ctx + nb)


def _chunk_index(d, c, n_l, n_t):
    bwd = jnp.where(c < n_l, n_l - 1 - c, n_l + (n_t - 1) - (c - n_l))
    return jnp.where(d == 0, c, bwd)


def _flip_iotas(d):
    row = lax.broadcasted_iota(jnp.int32, (CHUNK, CHUNK), 0)
    col = lax.broadcasted_iota(jnp.int32, (CHUNK, CHUNK), 1)
    rf = jnp.where(d == 0, row, CHUNK - 1 - row)
    cf = jnp.where(d == 0, col, CHUNK - 1 - col)
    return rf, cf


def _ret_kernel(s_ref, qf_ref, kf_ref, vf_ref, qb_ref, kb_ref, vb_ref, of_ref, ob_ref, st_ref):
    c = pl.program_id(0)
    batch = qf_ref.shape[0]

    @pl.when(c == 0)
    def _():
        st_ref[...] = jnp.zeros_like(st_ref)

    ks = float(RET_DK ** -0.5)
    dirs = ((qf_ref, kf_ref, vf_ref, of_ref), (qb_ref, kb_ref, vb_ref, ob_ref))
    for d, (q_ref, k_ref, v_ref, o_ref) in enumerate(dirs):
        rf, cf = _flip_iotas(d)
        rff = rf.astype(F32)
        dn = (rf - cf).astype(F32)
        for h in range(RET_HEADS):
            hs = slice(h * LANES, (h + 1) * LANES)
            sv = jnp.full((CHUNK, CHUNK), s_ref[d * RET_HEADS + h], F32)
            lg = jnp.log1p(-jnp.exp2(-sv))
            dec = jnp.where(dn >= 0, jnp.exp(dn * lg), 0.0)
            eq = jnp.exp((rff + 1.0) * lg)
            ek = jnp.exp((CHUNK - 1.0 - rff) * lg)
            a_chunk = jnp.exp(float(CHUNK) * lg)
            for b in range(batch):
                q = q_ref[b, :, hs]
                k = k_ref[b, :, hs] * ks
                v = v_ref[b, :, hs].astype(BF16)
                st = st_ref[d, b, h]
                a = _dot_nt(q.astype(BF16), k.astype(BF16)) * dec
                o = _dot(a.astype(BF16), v) + _dot_nt((q * eq).astype(BF16), st.astype(BF16))
                o_ref[b, :, hs] = o
                st_ref[d, b, h] = a_chunk * st + _dot_tn(v, (k * ek).astype(BF16))


def _hgrn_level_mats():
    c = CHUNK
    t = np.arange(c)[:, None]
    u = np.arange(c)[None, :]
    mats = [u <= t, u > t]
    m = c // 2
    while m >= 1:
        mid = (t // (2 * m)) * 2 * m + m
        second = (t % (2 * m)) >= m
        qrole = (u >= mid) & (u <= t)
        krole = (u > t) & (u <= mid - 1)
        mats.append(np.where(second, qrole, krole))
        m //= 2
    fwd = np.concatenate(mats, 0).astype(np.float32)
    bwd = np.concatenate([mm[::-1, ::-1] for mm in mats], 0).astype(np.float32)
    return np.stack([fwd, bwd], 0)


N_LEVELS = int(math.log2(CHUNK))


def _hgrn_kernel(m_ref, lb_ref, qf_ref, ff_ref, vf_ref, qb_ref, fb_ref, vb_ref, of_ref, ob_ref, st_ref):
    c = pl.program_id(0)
    batch = qf_ref.shape[0]

    @pl.when(c == 0)
    def _():
        st_ref[...] = jnp.zeros_like(st_ref)

    dirs = ((qf_ref, ff_ref, vf_ref, of_ref), (qb_ref, fb_ref, vb_ref, ob_ref))
    for d, (q_ref, f_ref, v_ref, o_ref) in enumerate(dirs):
        rf, cf = _flip_iotas(d)
        mst = m_ref[d]
        last = CHUNK - 1 if d == 0 else 0
        lvl_masks = []
        for lvl in range(N_LEVELS):
            m = CHUNK >> (lvl + 1)
            sh = N_LEVELS - lvl
            lvl_masks.append(((rf >> sh) == (cf >> sh)) & ((rf & m) != 0) & ((cf & m) == 0))
        for h in range(HGRN_HEADS):
            hs = slice(h * LANES, (h + 1) * LANES)
            lb = lb_ref[d, :, hs]
            for b in range(batch):
                f = lb + (1.0 - lb) * (1.0 / (1.0 + jnp.exp(-f_ref[b, :, hs])))
                k = 1.0 - f
                g = jnp.log(f)
                q = _silu(q_ref[b, :, hs])
                v = v_ref[b, :, hs].astype(BF16)
                e = jnp.exp(_dot(mst, g.astype(BF16)))
                e_q = e[0:CHUNK]
                q_in = q * e_q
                k_st = k * e[CHUNK:2 * CHUNK]
                qk_diag = jnp.sum(q * k, axis=-1, keepdims=True)
                scores = jnp.zeros((CHUNK, CHUNK), F32)
                for lvl in range(N_LEVELS):
                    el = e[(2 + lvl) * CHUNK:(3 + lvl) * CHUNK]
                    sl = _dot_nt((q * el).astype(BF16), (k * el).astype(BF16))
                    scores = scores + jnp.where(lvl_masks[lvl], sl, 0.0)
                st = st_ref[d, b, h]
                o = (_dot(scores.astype(BF16), v) + _dot_nt(q_in.astype(BF16), st.astype(BF16))
                     + qk_diag * v_ref[b, :, hs])
                o_ref[b, :, hs] = o
                st_ref[d, b, h] = st * e_q[last:last + 1, :] + _dot_tn(v, k_st.astype(BF16))


def _scan_specs(batch, t_len, l_len):
    n_l, n_t = l_len // CHUNK, t_len // CHUNK

    def spec(d, col):
        return pl.BlockSpec((batch, CHUNK, WIDE), lambda c: (0, _chunk_index(d, c, n_l, n_t), col))

    return spec, n_l + n_t


def _retention_part(pb, ret_s, batch, t_len, l_len):
    p_len = t_len + l_len
    pb3 = pb.reshape(batch, p_len, pb.shape[1])
    spec, n_chunks = _scan_specs(batch, t_len, l_len)
    return dict(
        kernel=_ret_kernel,
        in_specs=[pl.BlockSpec(memory_space=pltpu.SMEM)]
                 + [spec(0, col) for col in (0, 1, 2)] + [spec(1, col) for col in (0, 1, 2)],
        args=[ret_s.reshape(-1)] + [pb3] * 6,
        out_specs=[spec(0, 0), spec(1, 0)],
        out_shapes=[jax.ShapeDtypeStruct((batch, p_len, WIDE), F32)] * 2,
        scratch=[pltpu.VMEM((2, batch, RET_HEADS, CHUNK, CHUNK), F32)],
        steps=n_chunks)


def _hgrn_part(pd, lower_bounds, mats, batch, t_len, l_len):
    p_len = t_len + l_len
    pd3 = pd.reshape(batch, p_len, pd.shape[1])
    lb3 = lower_bounds.reshape(2, 1, WIDE)
    spec, n_chunks = _scan_specs(batch, t_len, l_len)
    return dict(
        kernel=_hgrn_kernel,
        in_specs=[pl.BlockSpec(mats.shape, lambda c: (0, 0, 0)),
                  pl.BlockSpec(lb3.shape, lambda c: (0, 0, 0)),
                  spec(0, 0), spec(0, 1), spec(0, 3), spec(1, 0), spec(1, 2), spec(1, 3)],
        args=[mats, lb3] + [pd3] * 6,
        out_specs=[spec(0, 0), spec(1, 0)],
        out_shapes=[jax.ShapeDtypeStruct((batch, p_len, WIDE), F32)] * 2,
        scratch=[pltpu.VMEM((2, batch, HGRN_HEADS, CHUNK, CHUNK), F32)],
        steps=n_chunks)


def _mixers(parts, n):
    steps = parts[0]["steps"]
    assert all(p["steps"] == steps for p in parts)
    n_in = [len(p["args"]) for p in parts]
    n_out = [len(p["out_shapes"]) for p in parts]
    n_scr = [len(p["scratch"]) for p in parts]

    def kern(*refs):
        ins, outs, scr = refs[:sum(n_in)], refs[sum(n_in):sum(n_in) + sum(n_out)], refs[sum(n_in) + sum(n_out):]
        a = b = c = 0
        for p, ni, no, ns in zip(parts, n_in, n_out, n_scr):
            p["kernel"](*ins[a:a + ni], *outs[b:b + no], *scr[c:c + ns])
            a, b, c = a + ni, b + no, c + ns

    outs = pl.pallas_call(
        kern,
        out_shape=[s for p in parts for s in p["out_shapes"]],
        grid=(steps,),
        in_specs=[s for p in parts for s in p["in_specs"]],
        out_specs=[s for p in parts for s in p["out_specs"]],
        scratch_shapes=[s for p in parts for s in p["scratch"]],
        compiler_params=_cparams(("arbitrary",)),
        name="swa_retention_hgrn",
    )(*[a for p in parts for a in p["args"]])
    return [o.reshape(n, o.shape[-1]) for o in outs]


R_E1, R_E2, R_G1, R_G2, R_S1, R_S2 = 0, 1, 2, 3, 4, 5


def _route(logits, count_ref, active):
    tm = logits.shape[0]
    lane = lax.broadcasted_iota(jnp.int32, logits.shape, 1)
    lanef = lane.astype(F32)
    big = jnp.float32(1e9)
    neg = jnp.float32(-jnp.inf)
    mx = lambda a: jnp.max(a, axis=-1, keepdims=True)
    mn = lambda a: jnp.min(a, axis=-1, keepdims=True)
    sm = lambda a: jnp.sum(a, axis=-1, keepdims=True)
    gl = jnp.where(lane < N_GROUPS, logits, neg)
    gm = mx(gl)
    p_grp = 1.0 / sm(jnp.exp(gl - gm))
    grp = mn(jnp.where(gl == gm, lanef, big))
    lo = N_GROUPS + grp * EXPERTS_PER_GROUP
    ing = (lanef >= lo) & (lanef < lo + EXPERTS_PER_GROUP)
    el = jnp.where(ing, logits, neg)
    l1 = mx(el)
    i1 = mn(jnp.where(el == l1, lanef, big))
    el2 = jnp.where(lanef == i1, neg, el)
    l2 = mx(el2)
    i2 = mn(jnp.where(el2 == l2, lanef, big))
    r = jnp.exp(l2 - l1)
    g1 = p_grp / (1.0 + r)
    g2 = p_grp * r / (1.0 + r)
    e1 = i1 - N_GROUPS
    e2 = i2 - N_GROUPS
    oh1 = lanef == e1
    oh2 = lanef == e2
    both = jnp.where(oh1 | oh2, active, 0.0)
    ri = lax.broadcasted_iota(jnp.int32, (tm, tm), 0)
    ci = lax.broadcasted_iota(jnp.int32, (tm, tm), 1)
    earlier = jnp.where(ci < ri, 1.0, 0.0).astype(BF16)
    before = _dot(earlier, both.astype(BF16)) + count_ref[0:1, :]
    s1 = sm(jnp.where(oh1, before, 0.0))
    s2 = sm(jnp.where(oh2, before, 0.0))
    count_ref[...] = count_ref[...] + jnp.sum(both, axis=0, keepdims=True)
    out = jnp.zeros(logits.shape, F32)
    for ln, val in ((R_E1, e1), (R_E2, e2), (R_G1, g1), (R_G2, g2), (R_S1, s1), (R_S2, s2)):
        out = jnp.where(lane == ln, val, out)
    return out


def _out_kernel(x_ref, a_ref, bf_ref, bb_ref, c_ref, df_ref, db_ref, rg_ref, hg_ref, g1_ref, sh2_ref, sc2_ref,
                lng_ref, lnb_ref, rgn_ref, hgn_ref, wo_ref, wr_ref,
                xo_ref, h2_ref, r_ref, cnt_ref, *, alpha, tiles_per_batch, route_ctx):
    i = pl.program_id(0)

    @pl.when(i == 0)
    def _():
        cnt_ref[...] = jnp.zeros_like(cnt_ref)

    bsum = bf_ref[...] + bb_ref[...]
    dsum = df_ref[...] + db_ref[...]
    rgn = rgn_ref[...]
    hgn = hgn_ref[...]
    parts_b, parts_d = [], []
    for h in range(4):
        hs = slice(h * LANES, (h + 1) * LANES)
        parts_b.append(_ln_rows(bsum[:, hs]))
        dh = dsum[:, hs]
        parts_d.append(dh * lax.rsqrt(jnp.mean(dh * dh, axis=-1, keepdims=True) + EPS))
    bo = _silu(rg_ref[...]) * (jnp.concatenate(parts_b, axis=1) * rgn)
    do = _silu(hg_ref[...]) * (jnp.concatenate(parts_d, axis=1) * hgn)
    o = (_dot(a_ref[...].astype(BF16), wo_ref[0:WIDE, :])
         + _dot(c_ref[...].astype(BF16), wo_ref[2 * WIDE:3 * WIDE, :])
         + _dot(bo.astype(BF16), wo_ref[WIDE:2 * WIDE, :])
         + _dot(do.astype(BF16), wo_ref[3 * WIDE:4 * WIDE, :]))
    y = alpha * x_ref[...] + g1_ref[0] * o
    xn = _ln_rows(y) * lng_ref[...] + lnb_ref[...]
    xo_ref[...] = xn
    h2 = _ln_rows(xn) * (1.0 + sc2_ref[0]) + sh2_ref[0]
    h2_ref[...] = _pack_bf16_pairs(h2)
    if route_ctx:
        active = jnp.float32(1.0)
    else:
        active = jnp.where(i % tiles_per_batch == 0, 0.0, 1.0).astype(F32)
    r_ref[...] = _route(_dot(h2.astype(BF16), wr_ref[...]), cnt_ref, active)


def _out_proj(stream, a, b_f, b_b, c, d_f, d_b, pb, pd, g1, sh2, sc2, ln_g, ln_b, ret_gn, hgrn_gn, w_out, w_r,
              tiles_per_batch, alpha, route_ctx):
    n, dm = stream.shape
    tm = ROW_TILE
    gmap = lambda i: (_group_of_tile(i, tiles_per_batch), 0, 0)
    row = lambda w: pl.BlockSpec((tm, w), lambda i: (i, 0))
    const = lambda w: pl.BlockSpec((1, w), lambda i: (0, 0))
    resident = pl.BlockSpec(memory_space=pltpu.VMEM)
    return pl.pallas_call(
        functools.partial(_out_kernel, alpha=alpha, tiles_per_batch=tiles_per_batch, route_ctx=route_ctx),
        out_shape=[jax.ShapeDtypeStruct((n, dm), F32), jax.ShapeDtypeStruct((n, dm // 2), jnp.uint32),
                   jax.ShapeDtypeStruct((n, LANES), F32), jax.ShapeDtypeStruct((SUBLANES, LANES), F32)],
        grid=(n // tm,),
        in_specs=[row(dm), row(WIDE), row(WIDE), row(WIDE), row(WIDE), row(WIDE), row(WIDE),
                  pl.BlockSpec((tm, WIDE), lambda i: (i, 3)),
                  pl.BlockSpec((tm, WIDE), lambda i: (i, 4)),
                  pl.BlockSpec((1, 1, dm), gmap), pl.BlockSpec((1, 1, dm), gmap),
                  pl.BlockSpec((1, 1, dm), gmap),
                  const(dm), const(dm), const(WIDE), const(WIDE), resident, resident],
        out_specs=[row(dm), row(dm // 2), row(LANES),
                   pl.BlockSpec((SUBLANES, LANES), lambda i: (0, 0))],
        compiler_params=_cparams(("arbitrary",)),
        name="mix_outproj_norm_route",
    )(stream, a, b_f, b_b, c, d_f, d_b, pb, pd, g1, sh2, sc2, ln_g, ln_b, ret_gn, hgrn_gn, w_out, w_r)


def _route_plan(route, counts_f, n_blocks):
    rows = MOE_ROWS
    counts = counts_f[0, :N_EXPERTS].astype(jnp.int32)
    padded = (counts + rows - 1) // rows * rows
    pend = jnp.cumsum(padded)
    pstart = pend - padded
    eid = route[:, R_E1:R_E2 + 1].astype(jnp.int32)
    rank = route[:, R_S1:R_S2 + 1].astype(jnp.int32)
    onehot = eid[:, :, None] == jnp.arange(N_EXPERTS, dtype=jnp.int32)
    pos = jnp.sum(jnp.where(onehot, pstart, 0), axis=-1) + rank
    blk_start = jnp.arange(n_blocks, dtype=jnp.int32) * rows
    blk_e = jnp.minimum(jnp.sum(blk_start[:, None] >= pend[None, :], axis=-1), N_EXPERTS - 1)
    blk_valid = (blk_start < pend[-1]).astype(jnp.int32)
    blk_ids = jnp.arange(n_blocks, dtype=jnp.int32)
    has_pad = padded > counts
    last_blk = pend // rows - 1
    blk_fill = (blk_valid == 0) | jnp.any((blk_ids[:, None] == last_blk[None, :]) & has_pad[None, :], axis=-1)
    ids = jnp.arange(N_EXPERTS, dtype=jnp.int32)
    nonempty = counts > 0
    seq = jnp.cumsum(nonempty.astype(jnp.int32)) - nonempty.astype(jnp.int32)
    cand = jnp.where(nonempty, ids, N_EXPERTS)
    later = jnp.where(ids[None, :] > ids[:, None], cand[None, :], N_EXPERTS)
    nxt = jnp.min(later, axis=-1)
    nxt = jnp.where(nxt >= N_EXPERTS, -1, nxt)
    blk_oh = blk_e[:, None] == ids[None, :]
    blk_slot = jnp.sum(jnp.where(blk_oh, seq % 2, 0), axis=-1)
    blk_next = jnp.sum(jnp.where(blk_oh, nxt, 0), axis=-1)
    i32 = lambda a: a.astype(jnp.int32)
    return (i32(pos.reshape(-1)), i32(blk_e), blk_valid, i32(blk_fill), i32(blk_slot), i32(blk_next))


def _dispatch_kernel(pos_ref, fill_ref, h_ref, x_hbm, zbuf, sem, *, tile_of_step, n_blocks):
    i = pl.program_id(0)
    tm = ROW_TILE
    rows = MOE_ROWS

    def zero_block_copy(j):
        start = pl.multiple_of(j * rows, rows)
        return pltpu.make_async_copy(zbuf, x_hbm.at[pl.ds(start, rows)], sem.at[1])

    @pl.when(i == 0)
    def _():
        zbuf[...] = jnp.zeros_like(zbuf)

        def fill_start(j, carry):
            @pl.when(fill_ref[j] == 1)
            def _():
                zero_block_copy(j).start()
            return carry

        def fill_wait(j, carry):
            @pl.when(fill_ref[j] == 1)
            def _():
                zero_block_copy(j).wait()
            return carry

        lax.fori_loop(0, n_blocks, fill_start, 0)
        lax.fori_loop(0, n_blocks, fill_wait, 0)

    def row_copy(r, k):
        p = pos_ref[(tile_of_step(i) * tm + r) * TOP_K + k]
        return pltpu.make_async_copy(h_ref.at[r], x_hbm.at[p], sem.at[0])

    for r in range(tm):
        for k in range(TOP_K):
            row_copy(r, k).start(priority=k)
    for r in range(tm):
        for k in range(TOP_K):
            row_copy(r, k).wait()


def _tile_schedule(n, tiles_per_batch, latent_only, batch):
    if latent_only:
        lat = tiles_per_batch - 1
        return batch * lat, (lambda s: (s // lat) * tiles_per_batch + 1 + s % lat)
    return n // ROW_TILE, (lambda s: s)


def _dispatch(h2, pos, blk_fill, tiles_per_batch, latent_only, batch):
    n, dm = h2.shape
    tm = ROW_TILE
    n_blocks = blk_fill.shape[0]
    n_tiles, tile_of_step = _tile_schedule(n, tiles_per_batch, latent_only, batch)
    grid_spec = pltpu.PrefetchScalarGridSpec(
        num_scalar_prefetch=2,
        grid=(n_tiles,),
        in_specs=[pl.BlockSpec((tm, dm), lambda i, p, f: (tile_of_step(i), 0))],
        out_specs=pl.BlockSpec(memory_space=pl.ANY),
        scratch_shapes=[pltpu.VMEM((MOE_ROWS, dm), h2.dtype), pltpu.SemaphoreType.DMA((2,))],
    )
    return pl.pallas_call(
        functools.partial(_dispatch_kernel, tile_of_step=tile_of_step, n_blocks=n_blocks),
        out_shape=jax.ShapeDtypeStruct((n_blocks * MOE_ROWS, dm), h2.dtype),
        grid_spec=grid_spec,
        compiler_params=_cparams(("arbitrary",)),
        name="moe_dispatch",
    )(pos, blk_fill, h2)


def _moe_kernel(be_ref, bv_ref, slot_ref, next_ref, x_ref, w1_hbm, w3_hbm, w2_hbm, y_ref,
                s1, s3, s2, w1b, w3b, w2b, sem, *, layer):
    j = pl.program_id(0)
    valid = bv_ref[j] == 1
    first = valid & ((j == 0) | (be_ref[j] != be_ref[jnp.maximum(j - 1, 0)]))

    def weight_copies(e, slot):
        return (pltpu.make_async_copy(w1_hbm.at[layer, e], s1.at[slot], sem.at[slot]),
                pltpu.make_async_copy(w3_hbm.at[layer, e], s3.at[slot], sem.at[slot]),
                pltpu.make_async_copy(w2_hbm.at[layer, e], s2.at[slot], sem.at[slot]))

    @pl.when(valid & (j == 0))
    def _():
        for cp in weight_copies(be_ref[0], slot_ref[0]):
            cp.start()

    @pl.when(first)
    def _():
        slot = slot_ref[j]
        for cp in weight_copies(be_ref[j], slot):
            cp.wait()
        nxt = next_ref[j]

        @pl.when(nxt >= 0)
        def _():
            for cp in weight_copies(nxt, 1 - slot):
                cp.start(priority=1)

        w1b[...] = s1[slot].astype(BF16)
        w3b[...] = s3[slot].astype(BF16)
        w2b[...] = s2[slot].astype(BF16)

    @pl.when(valid)
    def _():
        x = _unpack_bf16_pairs(x_ref[...]).astype(BF16)
        hmid = _silu(_dot(x, w1b[...])) * _dot(x, w3b[...])
        y_ref[...] = _pack_bf16_pairs(_dot(hmid.astype(BF16), w2b[...]))

    @pl.when(bv_ref[j] == 0)
    def _():
        y_ref[...] = jnp.zeros_like(y_ref)


def _moe_experts(xbuf, blk_e, blk_valid, blk_slot, blk_next, w1, w3, w2, layer):
    dm = w1.shape[2]
    ff = w1.shape[-1]
    rows = MOE_ROWS
    n_blocks = blk_e.shape[0]
    hbm = pl.BlockSpec(memory_space=pl.ANY)
    grid_spec = pltpu.PrefetchScalarGridSpec(
        num_scalar_prefetch=4,
        grid=(n_blocks,),
        in_specs=[
            pl.BlockSpec((rows, dm // 2), lambda j, be, bv, sl, nx: (j * bv[j], 0)),
            hbm, hbm, hbm,
        ],
        out_specs=pl.BlockSpec((rows, dm // 2), lambda j, be, bv, sl, nx: (j, 0)),
        scratch_shapes=[
            pltpu.VMEM((2, dm, ff), F32), pltpu.VMEM((2, dm, ff), F32), pltpu.VMEM((2, ff, dm), F32),
            pltpu.VMEM((dm, ff), BF16), pltpu.VMEM((dm, ff), BF16), pltpu.VMEM((ff, dm), BF16),
            pltpu.SemaphoreType.DMA((2,)),
        ],
    )
    return pl.pallas_call(
        functools.partial(_moe_kernel, layer=layer),
        out_shape=jax.ShapeDtypeStruct((n_blocks * rows, dm // 2), jnp.uint32),
        grid_spec=grid_spec,
        compiler_params=_cparams(("arbitrary",)),
        name="moe_experts",
    )(blk_e, blk_valid, blk_slot, blk_next, xbuf, w1, w3, w2)


def _expert_row_gather(pos_ref, y_hbm, ybuf, sem, tile_of_step):
    tm = ROW_TILE

    def row_copy(step, sl, r, k):
        p = pos_ref[(tile_of_step(step) * tm + r) * TOP_K + k]
        return pltpu.make_async_copy(y_hbm.at[p], ybuf.at[sl, k, r], sem.at[sl])

    def start_rows(step, sl, r0, r1):
        for r in range(r0, r1):
            for k in range(TOP_K):
                row_copy(step, sl, r, k).start(priority=k)

    def wait_rows(step, sl):
        for r in range(tm):
            for k in range(TOP_K):
                row_copy(step, sl, r, k).wait()

    return start_rows, wait_rows


def _gate_weighted(ybuf, slot, r_ref):
    route = r_ref[...]
    return (_unpack_bf16_pairs(ybuf[slot, 0]) * route[:, R_G1:R_G1 + 1]
            + _unpack_bf16_pairs(ybuf[slot, 1]) * route[:, R_G2:R_G2 + 1])


def _gathered_expert_rows(pos_ref, r_ref, y_hbm, ybuf, sem, *, n_tiles, tile_of_step):
    i = pl.program_id(0)
    slot = i % 2
    start_rows, wait_rows = _expert_row_gather(pos_ref, y_hbm, ybuf, sem, tile_of_step)

    @pl.when(i == 0)
    def _():
        start_rows(0, 0, 0, ROW_TILE)

    @pl.when(i + 1 < n_tiles)
    def _():
        start_rows(jnp.minimum(i + 1, n_tiles - 1), 1 - slot, 0, ROW_TILE)

    wait_rows(i, slot)
    return _gate_weighted(ybuf, slot, r_ref)


def _combine_kernel(pos_ref, x_ref, r_ref, g2_ref, lng_ref, lnb_ref, y_hbm, o_ref, ybuf, sem,
                    *, n_tiles, tile_of_step, alpha):
    y = _gathered_expert_rows(pos_ref, r_ref, y_hbm, ybuf, sem, n_tiles=n_tiles, tile_of_step=tile_of_step)
    z = alpha * x_ref[...] + g2_ref[0] * y
    o_ref[...] = _ln_rows(z) * lng_ref[...] + lnb_ref[...]


def _combine(x_new, route, ybuf, pos, g2, ln_g, ln_b, tiles_per_batch, alpha, batch):
    n, dm = x_new.shape
    tm = ROW_TILE
    n_tiles, tile_of_step = _tile_schedule(n, tiles_per_batch, True, batch)
    gmap = lambda i, p: (_group_of_tile(tile_of_step(i), tiles_per_batch), 0, 0)
    grid_spec = pltpu.PrefetchScalarGridSpec(
        num_scalar_prefetch=1,
        grid=(n_tiles,),
        in_specs=[
            pl.BlockSpec((tm, dm), lambda i, p: (tile_of_step(i), 0)),
            pl.BlockSpec((tm, LANES), lambda i, p: (tile_of_step(i), 0)),
            pl.BlockSpec((1, 1, dm), gmap),
            pl.BlockSpec((1, dm), lambda i, p: (0, 0)),
            pl.BlockSpec((1, dm), lambda i, p: (0, 0)),
            pl.BlockSpec(memory_space=pl.ANY),
        ],
        out_specs=pl.BlockSpec((tm, dm), lambda i, p: (i, 0)),
        scratch_shapes=[pltpu.VMEM((2, TOP_K, tm, dm // 2), jnp.uint32), pltpu.SemaphoreType.DMA((2,))],
    )
    return pl.pallas_call(
        functools.partial(_combine_kernel, n_tiles=n_tiles, tile_of_step=tile_of_step, alpha=alpha),
        out_shape=jax.ShapeDtypeStruct((n_tiles * tm, dm), F32),
        grid_spec=grid_spec,
        compiler_params=_cparams(("arbitrary",)),
        name="moe_combine_norm",
    )(pos, x_new, route, g2, ln_g, ln_b, ybuf)


def _combine_proj_kernel(pos_ref, x_ref, r_ref, g2_ref, lng_ref, lnb_ref, y_hbm, sh_ref, sc_ref, w_ref,
                         *rest, n_tiles, alpha):
    mla_in, (oa, ob, od, qf, kf, vt, so, ybuf, sem, oc) = rest[:7], rest[7:]
    i = pl.program_id(0)
    slot = i % 2
    tm = ROW_TILE
    start_rows, wait_rows = _expert_row_gather(pos_ref, y_hbm, ybuf, sem, lambda s: s)

    @pl.when(i == 0)
    def _():
        start_rows(0, 0, 0, tm)

    wait_rows(i, slot)
    z = alpha * x_ref[...] + g2_ref[0] * _gate_weighted(ybuf, slot, r_ref)
    stream = _ln_rows(z) * lng_ref[...] + lnb_ref[...]
    so[...] = stream
    outs = (oa, ob, oc, od)
    n_blk = len(_column_blocks(outs))
    per = -(-tm // n_blk)
    nxt = jnp.minimum(i + 1, n_tiles - 1)

    c_done = max(k for k, blk in enumerate(_column_blocks(outs)) if blk[0] is oc)

    def after_block(idx):
        start_rows(nxt, 1 - slot, min(idx * per, tm), min((idx + 1) * per, tm))
        if idx == c_done:
            _mla_up_rows(oc[...], *mla_in, qf, kf, vt)

    _project_rows(stream, sh_ref, sc_ref, w_ref, outs, between=after_block)

    @pl.when(i == n_tiles - 1)
    def _():
        wait_rows(nxt, 1 - slot)


def _combine_proj(x_new, route, ybuf, pos, g2, ln_g, ln_b, shift, scale, w_pad, mla_args,
                  tiles_per_batch, alpha):
    n, dm = x_new.shape
    tm = ROW_TILE
    n_tiles = n // tm
    gmap = lambda i, p: (_group_of_tile(i, tiles_per_batch), 0, 0)
    row = lambda w: pl.BlockSpec((tm, w), lambda i, p: (i, 0))
    f32 = lambda w: jax.ShapeDtypeStruct((n, w), F32)
    const = pl.BlockSpec((1, dm), lambda i, p: (0, 0))
    mod = pl.BlockSpec((1, 1, dm), gmap)
    mla_in, mla_out, mla_shapes = _mla_up_specs(n, mla_args)
    grid_spec = pltpu.PrefetchScalarGridSpec(
        num_scalar_prefetch=1,
        grid=(n_tiles,),
        in_specs=[row(dm), row(LANES), mod, const, const, pl.BlockSpec(memory_space=pl.ANY),
                  mod, mod, pl.BlockSpec(memory_space=pltpu.VMEM)] + mla_in,
        out_specs=[row(SEG_A), row(SEG_B), row(SEG_D)] + mla_out + [row(dm)],
        scratch_shapes=[pltpu.VMEM((2, TOP_K, tm, dm // 2), jnp.uint32), pltpu.SemaphoreType.DMA((2,)),
                        pltpu.VMEM((tm, SEG_C), F32)],
    )
    return pl.pallas_call(
        functools.partial(_combine_proj_kernel, n_tiles=n_tiles, alpha=alpha),
        out_shape=[f32(SEG_A), f32(SEG_B), f32(SEG_D)] + mla_shapes + [f32(dm)],
        grid_spec=grid_spec,
        compiler_params=_cparams(("arbitrary",), vmem=PROJ_VMEM_LIMIT_BYTES),
        name="moe_combine_ln_mod_inproj",
    )(pos, x_new, route, g2, ln_g, ln_b, ybuf, shift, scale, w_pad, *mla_args)


def _pad_cols(w, width):
    return jnp.pad(w, ((0, 0), (0, width - w.shape[1])))


def kernel(x, c, ctx, c_ctx, w_ada, b_ada, w_in, swa_sink, ret_decay_exp, ret_gn, mla_q_norm, mla_kv_norm,
           mla_w_uq, mla_w_ukv, hgrn_lb_logits, hgrn_gn, w_out, ln1_g, ln1_b, router_group, router_expert,
           moe_w1, moe_w3, moe_w2, ln2_g, ln2_b):
    batch, t_len, dm = x.shape
    l_len = ctx.shape[1]
    depth = w_in.shape[0]
    p_len = t_len + l_len
    n = batch * p_len
    assert batch == 2 and l_len == ROW_TILE and t_len % 256 == 0 and t_len % GRID_W == 0
    tiles_per_batch = p_len // ROW_TILE
    alpha = float((2 * depth) ** 0.25)

    cos_swa, sin_swa, cos_rows, sin_rows = _rope_tables(t_len, l_len, batch)
    mats = jnp.asarray(_hgrn_level_mats(), BF16)

    cvec = jnp.zeros((SUBLANES, dm), F32).at[0:batch].set(c).at[batch].set(c_ctx)
    mod = _ada(cvec, w_ada, b_ada)

    lbp = jax.nn.softmax(hgrn_lb_logits.astype(F32), axis=0)
    lower_bounds = jnp.cumsum(lbp, axis=0) - lbp[0]

    pending = None

    out = None
    for l in range(depth):
        need_ctx = l < depth - 1
        m6 = mod[l, :3].reshape(3, 6, dm)
        sh1, sc1, g1, sh2, sc2, g2 = (m6[:, k].reshape(3, 1, dm) for k in range(6))

        wq = mla_w_uq[l].reshape(MLA_Q_RANK, MLA_HEADS, MLA_NOPE + MLA_ROPE)
        wq = jnp.pad(wq, ((0, 0), (0, 0), (0, 2 * LANES - (MLA_NOPE + MLA_ROPE))))
        wq = wq.reshape(MLA_Q_RANK, MLA_HEADS * 2 * LANES).astype(BF16)
        wkv = mla_w_ukv[l].reshape(MLA_KV_RANK, MLA_HEADS, MLA_NOPE + MLA_V)
        wk = wkv[:, :, :MLA_NOPE].reshape(MLA_KV_RANK, -1).astype(BF16)
        wv_t = wkv[:, :, MLA_NOPE:].reshape(MLA_KV_RANK, -1).T.astype(BF16)
        mla_args = (mla_q_norm[l].reshape(1, -1), mla_kv_norm[l].reshape(1, -1), wq, wk, wv_t,
                    cos_rows, sin_rows)

        w_pad = _w_in_prep(w_in, l)
        if pending is None:
            pa, pb, pd, qf, kf, vt, stream = _proj(x, ctx, sh1, sc1, w_pad, mla_args, tiles_per_batch)
        else:
            pa, pb, pd, qf, kf, vt, stream = _combine_proj(*pending, sh1, sc1, w_pad, mla_args,
                                                           tiles_per_batch, alpha)

        cc = _mla_attn(qf, kf, vt, batch, t_len, l_len, need_ctx)
        a, b_f, b_b, d_f, d_b = _mixers(
            [_swa_part(pa, swa_sink[l], cos_swa, sin_swa, batch, t_len, l_len, need_ctx),
             _retention_part(pb, ret_decay_exp[l], batch, t_len, l_len),
             _hgrn_part(pd, lower_bounds[l], mats, batch, t_len, l_len)], n)

        w_r = _pad_cols(jnp.concatenate([router_group[l], router_expert[l]], axis=1), LANES).astype(BF16)
        x_new, h2, route, counts = _out_proj(
            stream, a, b_f, b_b, cc, d_f, d_b, pb, pd, g1, sh2, sc2,
            ln1_g[l].reshape(1, -1), ln1_b[l].reshape(1, -1),
            ret_gn[l].reshape(1, -1), hgrn_gn[l].reshape(1, -1),
            w_out[l].astype(BF16), w_r, tiles_per_batch, alpha, need_ctx)

        n_active = n if need_ctx else batch * t_len
        n_blocks = (n_active * TOP_K + N_EXPERTS * (MOE_ROWS - 1) + MOE_ROWS - 1) // MOE_ROWS
        pos, blk_e, blk_valid, blk_fill, blk_slot, blk_next = _route_plan(route, counts, n_blocks)
        xbuf = _dispatch(h2, pos, blk_fill, tiles_per_batch, not need_ctx, batch)
        ybuf = _moe_experts(xbuf, blk_e, blk_valid, blk_slot, blk_next, moe_w1, moe_w3, moe_w2, l)
        combine_args = (x_new, route, ybuf, pos, g2, ln2_g[l].reshape(1, -1), ln2_b[l].reshape(1, -1))
        if need_ctx:
            pending = combine_args
        else:
            out = _combine(*combine_args, tiles_per_batch, alpha, batch).reshape(batch, t_len, dm)
    return out
```
